```python
import math
import jax
import jax.numpy as jnp
from jax import lax
import numpy as np

D_MODEL = 2048
BATCH = 1
SEQ = 16384
DEPTH = 2

GRID_W = 64
CTX_LEN = 256
HEAD_DIM = 64
MIX_WIDTH = D_MODEL // 4
N_BRANCH = 4
A_HEADS = MIX_WIDTH // HEAD_DIM
A_KV_HEADS = A_HEADS // 4
A_WINDOW = 128
A_BLOCK = 128
ROPE_BASE = 10000.0
ROPE_PAIRS = HEAD_DIM // 4
S5_GROUP_CH = 16
S5_GROUPS = MIX_WIDTH // S5_GROUP_CH
S5_STATE = 64
S5_DT_MIN = 0.001
S5_DT_MAX = 0.1
C_HEADS = MIX_WIDTH // HEAD_DIM
C_WIN_R = 8
C_WIN_C = 16
C_QBLOCK = 16
C_KBLOCK = C_QBLOCK + C_WIN_C
CONV_WIDTH = MIX_WIDTH
CONV_K = 31
D_FF = 4 * D_MODEL
EPS = 1e-6
NEG_INF = -1e30

kernel_name = 'hybrid_gated_diffusion_block'


def _in_widths():
    return (A_HEADS * HEAD_DIM, A_KV_HEADS * HEAD_DIM, A_KV_HEADS * HEAD_DIM,
            MIX_WIDTH,
            C_HEADS * HEAD_DIM, C_HEADS * HEAD_DIM, C_HEADS * HEAD_DIM,
            CONV_WIDTH, CONV_WIDTH,
            D_MODEL, D_MODEL, D_MODEL, D_MODEL)


def _split_points():
    return np.cumsum(np.array(_in_widths()))[:-1].tolist()


def rms_norm(x, g):
    xf = x.astype(jnp.float32)
    y = xf * lax.rsqrt(jnp.mean(xf * xf, axis=-1, keepdims=True) + EPS)
    return (y * g.astype(jnp.float32)).astype(x.dtype)


def modulate(h, shift, scale):
    return h * (1.0 + scale) + shift


def heads(t, n):
    return t.reshape(t.shape[0], t.shape[1], n, HEAD_DIM)


def axial_rope(x, cos, sin):
    B, L, H, Dh = x.shape
    xf = x.astype(jnp.float32).reshape(B, L, H, 2, 2, ROPE_PAIRS)
    x1, x2 = xf[..., 0, :], xf[..., 1, :]
    cs, sn = cos[None, :, None], sin[None, :, None]
    out = jnp.stack([x1 * cs - x2 * sn, x1 * sn + x2 * cs], axis=-2)
    return out.reshape(B, L, H, Dh).astype(x.dtype)


def window_attention(q, k, v, k_ctx, v_ctx, sink):
    B, L, Hq, Dh = q.shape
    Hkv = k.shape[2]
    G = Hq // Hkv
    Lc = k_ctx.shape[1]
    nb = L // A_BLOCK
    nw = 3 * A_BLOCK
    scale = Dh ** -0.5
    qb = q.reshape(B, nb, A_BLOCK, Hkv, G, Dh)
    pad = ((0, 0), (A_BLOCK, A_BLOCK), (0, 0), (0, 0))
    kp = jnp.pad(k, pad).reshape(B, nb + 2, A_BLOCK, Hkv, Dh)
    vp = jnp.pad(v, pad).reshape(B, nb + 2, A_BLOCK, Hkv, Dh)
    kw = jnp.concatenate([kp[:, :-2], kp[:, 1:-1], kp[:, 2:]], axis=2)
    vw = jnp.concatenate([vp[:, :-2], vp[:, 1:-1], vp[:, 2:]], axis=2)
    qi = np.arange(A_BLOCK)[:, None]
    kj = np.arange(nw)[None, :] - A_BLOCK
    kabs = np.arange(nb)[:, None, None] * A_BLOCK + kj[None]
    mask = (np.abs(kj - qi) <= A_WINDOW)[None] & (kabs >= 0) & (kabs < L)
    s_win = jnp.einsum('bnqhgd,bnkhd->bnhgqk', qb, kw).astype(jnp.float32) * scale
    s_win = jnp.where(jnp.asarray(mask)[None, :, None, None], s_win, NEG_INF)
    s_ctx = jnp.einsum('bnqhgd,bchd->bnhgqc', qb, k_ctx).astype(jnp.float32) * scale
    s_sink = jnp.broadcast_to(sink.astype(jnp.float32).reshape(1, 1, Hkv, G, 1, 1), s_win.shape[:-1] + (1,))
    p = jax.nn.softmax(jnp.concatenate([s_win, s_ctx, s_sink], axis=-1), axis=-1).astype(v.dtype)
    o = (jnp.einsum('bnhgqk,bnkhd->bnqhgd', p[..., :nw], vw)
         + jnp.einsum('bnhgqc,bchd->bnqhgd', p[..., nw:nw + Lc], v_ctx))
    return o.reshape(B, L, Hq * Dh)


def context_attention(q, k, v, sink):
    B, Lc, Hq, Dh = q.shape
    Hkv = k.shape[2]
    G = Hq // Hkv
    qg = q.reshape(B, Lc, Hkv, G, Dh)
    s = jnp.einsum('bqhgd,bkhd->bhgqk', qg, k).astype(jnp.float32) * (Dh ** -0.5)
    if sink is not None:
        s_sink = jnp.broadcast_to(sink.astype(jnp.float32).reshape(1, Hkv, G, 1, 1), s.shape[:-1] + (1,))
        s = jnp.concatenate([s, s_sink], axis=-1)
    p = jax.nn.softmax(s, axis=-1)[..., :Lc].astype(v.dtype)
    o = jnp.einsum('bhgqk,bkhd->bqhgd', p, v)
    return o.reshape(B, Lc, Hq * Dh)


def neighborhood_attention(q, k, v, k_ctx, v_ctx, rpb, rows):
    B, L, H, Dh = q.shape
    wr = min(C_WIN_R, rows)
    ncb = GRID_W // C_QBLOCK
    nk = wr * C_KBLOCK
    scale = Dh ** -0.5
    r = np.arange(rows)
    krow = np.clip(r - C_WIN_R // 2, 0, rows - wr)[:, None] + np.arange(wr)[None]
    qcol = np.arange(GRID_W).reshape(ncb, C_QBLOCK)
    kcol = (np.clip(np.arange(ncb) * C_QBLOCK - C_WIN_C // 2, 0, GRID_W - C_KBLOCK)[:, None]
            + np.arange(C_KBLOCK)[None])
    cstart = np.clip(qcol - C_WIN_C // 2, 0, GRID_W - C_WIN_C)
    col_ok = (kcol[:, None, :] >= cstart[..., None]) & (kcol[:, None, :] < cstart[..., None] + C_WIN_C)
    mask = np.broadcast_to(col_ok[:, :, None, :], (ncb, C_QBLOCK, wr, C_KBLOCK)).reshape(ncb, C_QBLOCK, nk)
    idx = jnp.asarray((krow[:, None, :, None] * GRID_W + kcol[None, :, None, :]).reshape(-1))
    kg = jnp.take(k, idx, axis=1).reshape(B, rows, ncb, nk, H, Dh)
    vg = jnp.take(v, idx, axis=1).reshape(B, rows, ncb, nk, H, Dh)
    qb = q.reshape(B, rows, ncb, C_QBLOCK, H, Dh)
    dr = (krow - r[:, None] + C_WIN_R - 1)[:, None, None, :, None]
    dc = np.clip(kcol[:, None, :] - qcol[:, :, None] + C_WIN_C - 1, 0, 2 * C_WIN_C - 2)[None, :, :, None, :]
    bias = jnp.moveaxis(rpb.astype(jnp.float32)[:, dr, dc], 0, 2).reshape(rows, ncb, H, C_QBLOCK, nk)
    s = jnp.einsum('brjqhd,brjkhd->brjhqk', qb, kg).astype(jnp.float32) * scale + bias
    s = jnp.where(jnp.asarray(mask)[None, None, :, None], s, NEG_INF)
    s_ctx = jnp.einsum('brjqhd,bchd->brjhqc', qb, k_ctx).astype(jnp.float32) * scale
    p = jax.nn.softmax(jnp.concatenate([s, s_ctx], axis=-1), axis=-1).astype(v.dtype)
    o = (jnp.einsum('brjhqk,brjkhd->brjqhd', p[..., :nk], vg)
         + jnp.einsum('brjhqc,bchd->brjqhd', p[..., nk:], v_ctx))
    return o.reshape(B, L, H * Dh)


def _zoh(a_re, a_im, log_step, b_re, b_im):
    f32 = jnp.float32
    ar, ai = a_re.astype(f32), a_im.astype(f32)
    step = jnp.exp(log_step.astype(f32))[:, None]
    mag = jnp.exp(ar * step)
    ab_re, ab_im = mag * jnp.cos(ai * step), mag * jnp.sin(ai * step)
    den = ar * ar + ai * ai
    nr = ab_re - 1.0
    f_re = (nr * ar + ab_im * ai) / den
    f_im = (ab_im * ar - nr * ai) / den
    br, bi = b_re.astype(f32), b_im.astype(f32)
    bb_re = f_re[..., None] * br - f_im[..., None] * bi
    bb_im = f_re[..., None] * bi + f_im[..., None] * br
    return ab_re, ab_im, bb_re, bb_im


def _diag_scan(ab_re, ab_im, bu_re, bu_im):
    a_re = jnp.broadcast_to(ab_re, bu_re.shape)
    a_im = jnp.broadcast_to(ab_im, bu_im.shape)

    def combine(e1, e2):
        a1r, a1i, b1r, b1i = e1
        a2r, a2i, b2r, b2i = e2
        return (a2r * a1r - a2i * a1i, a2r * a1i + a2i * a1r,
                a2r * b1r - a2i * b1i + b2r, a2r * b1i + a2i * b1r + b2i)

    _, _, h_re, h_im = lax.associative_scan(combine, (a_re, a_im, bu_re, bu_im), axis=1)
    return h_re, h_im


def _chained_scan(ab_re, ab_im, bu_ctx, bu_lat):
    hc_re, hc_im = _diag_scan(ab_re, ab_im, bu_ctx[0], bu_ctx[1])
    s_re, s_im = hc_re[:, -1], hc_im[:, -1]
    lr = bu_lat[0].at[:, 0].add(ab_re * s_re - ab_im * s_im)
    li = bu_lat[1].at[:, 0].add(ab_re * s_im + ab_im * s_re)
    hl_re, hl_im = _diag_scan(ab_re, ab_im, lr, li)
    return (hc_re, hc_im), (hl_re, hl_im)


def s5_mixer(u_ctx, u_lat, a_re, a_im, log_step, b_re, b_im, c_re, c_im, d_skip, glu_w, glu_b, need_ctx):
    f32 = jnp.float32
    B, L, W = u_lat.shape
    Lc = u_ctx.shape[1]
    uc = u_ctx.astype(f32).reshape(B, Lc, S5_GROUPS, S5_GROUP_CH)
    ul = u_lat.astype(f32).reshape(B, L, S5_GROUPS, S5_GROUP_CH)
    dsk = d_skip.astype(f32).reshape(S5_GROUPS, S5_GROUP_CH)
    yl = ul * dsk
    yc = uc * dsk
    for d in range(2):
        ab_re, ab_im, bb_re, bb_im = _zoh(a_re[d], a_im[d], log_step[d], b_re[d], b_im[d])
        cr, ci = c_re[d].astype(f32), c_im[d].astype(f32)
        uc_d = uc if d == 0 else jnp.flip(uc, axis=1)
        ul_d = ul if d == 0 else jnp.flip(ul, axis=1)
        drive_c = (jnp.einsum('blgh,gph->blgp', uc_d, bb_re), jnp.einsum('blgh,gph->blgp', uc_d, bb_im))
        drive_l = (jnp.einsum('blgh,gph->blgp', ul_d, bb_re), jnp.einsum('blgh,gph->blgp', ul_d, bb_im))
        (hcr, hci), (hlr, hli) = _chained_scan(ab_re, ab_im, drive_c, drive_l)
        out_l = jnp.einsum('blgp,ghp->blgh', hlr, cr) - jnp.einsum('blgp,ghp->blgh', hli, ci)
        yl = yl + (out_l if d == 0 else jnp.flip(out_l, axis=1))
        if need_ctx:
            out_c = jnp.einsum('blgp,ghp->blgh', hcr, cr) - jnp.einsum('blgp,ghp->blgh', hci, ci)
            yc = yc + (out_c if d == 0 else jnp.flip(out_c, axis=1))

    def glu(y):
        g = jax.nn.gelu(y)
        return g * jax.nn.sigmoid(g @ glu_w.astype(f32) + glu_b.astype(f32))

    y_lat = glu(yl.reshape(B, L, W)).astype(u_lat.dtype)
    y_ctx = glu(yc.reshape(B, Lc, W)).astype(u_ctx.dtype) if need_ctx else None
    return y_ctx, y_lat


def conv_module(val, gate, w, b, g, beta):
    h = val * jax.nn.sigmoid(gate)
    ch = h.shape[-1]
    y = lax.conv_general_dilated(h, w[:, None, :].astype(h.dtype), window_strides=(1,),
                                 padding=[(CONV_K // 2, CONV_K // 2)],
                                 dimension_numbers=('NWC', 'WIO', 'NWC'),
                                 feature_group_count=ch) + b
    yf = y.astype(jnp.float32)
    mu = jnp.mean(yf, axis=-1, keepdims=True)
    var = jnp.mean(jnp.square(yf - mu), axis=-1, keepdims=True)
    yn = (yf - mu) * lax.rsqrt(var + EPS) * g.astype(jnp.float32) + beta.astype(jnp.float32)
    return jax.nn.silu(yn).astype(val.dtype)


def gated_merge(ys, gate_logits, w_branch, w_out):
    m = jax.nn.sigmoid(gate_logits[0]) * (ys[0] @ w_branch[0])
    for i in range(1, N_BRANCH):
        m = m + jax.nn.sigmoid(gate_logits[i]) * (ys[i] @ w_branch[i])
    return m @ w_out


def hybrid_mixer(hx, hz, cos, sin, rows, w_in, a_sink, s5_a_re, s5_a_im, s5_log_step, s5_b_re, s5_b_im,
                 s5_c_re, s5_c_im, s5_d, s5_glu_w, s5_glu_b, c_rpb, d_conv_w, d_conv_b, d_ln_g, d_ln_b,
                 w_branch, w_out, need_ctx):
    sp = _split_points()
    px = jnp.split(hx @ w_in, sp, axis=-1)
    pz = jnp.split(hz @ w_in, sp, axis=-1)
    qa = axial_rope(heads(px[0], A_HEADS), cos, sin)
    ka = axial_rope(heads(px[1], A_KV_HEADS), cos, sin)
    va = heads(px[2], A_KV_HEADS)
    kza, vza = heads(pz[1], A_KV_HEADS), heads(pz[2], A_KV_HEADS)
    ya_x = window_attention(qa, ka, va, kza, vza, a_sink)
    yb_z, yb_x = s5_mixer(pz[3], px[3], s5_a_re, s5_a_im, s5_log_step, s5_b_re, s5_b_im,
                          s5_c_re, s5_c_im, s5_d, s5_glu_w, s5_glu_b, need_ctx)
    kzc, vzc = heads(pz[5], C_HEADS), heads(pz[6], C_HEADS)
    yc_x = neighborhood_attention(heads(px[4], C_HEADS), heads(px[5], C_HEADS), heads(px[6], C_HEADS),
                                  kzc, vzc, c_rpb, rows)
    yd_x = conv_module(px[7], px[8], d_conv_w, d_conv_b, d_ln_g, d_ln_b)
    out_x = gated_merge((ya_x, yb_x, yc_x, yd_x), px[9:13], w_branch, w_out)
    if not need_ctx:
        return out_x, None
    ya_z = context_attention(heads(pz[0], A_HEADS), kza, vza, a_sink)
    yc_z = context_attention(heads(pz[4], C_HEADS), kzc, vzc, None)
    yd_z = conv_module(pz[7], pz[8], d_conv_w, d_conv_b, d_ln_g, d_ln_b)
    out_z = gated_merge((ya_z, yb_z, yc_z, yd_z), pz[9:13], w_branch, w_out)
    return out_x, out_z


def squared_relu_mlp(h, w1, w2):
    return jnp.square(jax.nn.relu(h @ w1)) @ w2


def setup_inputs(seed: int = 0) -> dict:
    key = jax.random.key(seed)
    ks = iter(jax.random.split(key, 40))
    f32 = jnp.float32

    def nrm(shape, std):
        return jax.random.normal(next(ks), shape, f32) * std

    Dm = D_MODEL
    n_in = int(sum(_in_widths()))
    s5_shape = (DEPTH, 2, S5_GROUPS, S5_STATE)
    n_idx = jnp.arange(S5_STATE, dtype=f32)
    return {
        'x': nrm((BATCH, SEQ, Dm), 1.0),
        'c': nrm((BATCH, Dm), 1.0),
        'ctx': nrm((BATCH, CTX_LEN, Dm), 1.0),
        'c_ctx': nrm((Dm,), 1.0),
        'ada_w': nrm((DEPTH, Dm, 6 * Dm), 0.5 * Dm ** -0.5),
        'ada_b': nrm((DEPTH, 6 * Dm), 0.01),
        'norm1_g': 1.0 + nrm((DEPTH, Dm), 0.05),
        'norm2_g': 1.0 + nrm((DEPTH, Dm), 0.05),
        'w_in': nrm((DEPTH, Dm, n_in), Dm ** -0.5),
        'a_sink': nrm((DEPTH, A_HEADS), 1.0),
        's5_a_re': -0.5 + nrm(s5_shape, 0.01),
        's5_a_im': jnp.broadcast_to(math.pi * n_idx, s5_shape) + nrm(s5_shape, 0.01),
        's5_log_step': jax.random.uniform(next(ks), (DEPTH, 2, S5_GROUPS), f32,
                                          math.log(S5_DT_MIN), math.log(S5_DT_MAX)),
        's5_b_re': nrm((DEPTH, 2, S5_GROUPS, S5_STATE, S5_GROUP_CH), (0.5 / S5_GROUP_CH) ** 0.5),
        's5_b_im': nrm((DEPTH, 2, S5_GROUPS, S5_STATE, S5_GROUP_CH), (0.5 / S5_GROUP_CH) ** 0.5),
        's5_c_re': nrm((DEPTH, 2, S5_GROUPS, S5_GROUP_CH, S5_STATE), (0.5 / S5_STATE) ** 0.5),
        's5_c_im': nrm((DEPTH, 2, S5_GROUPS, S5_GROUP_CH, S5_STATE), (0.5 / S5_STATE) ** 0.5),
        's5_d': nrm((DEPTH, MIX_WIDTH), 1.0),
        's5_glu_w': nrm((DEPTH, MIX_WIDTH, MIX_WIDTH), MIX_WIDTH ** -0.5),
        's5_glu_b': nrm((DEPTH, MIX_WIDTH), 0.01),
        'c_rpb': nrm((DEPTH, C_HEADS, 2 * C_WIN_R - 1, 2 * C_WIN_C - 1), 0.1),
        'd_conv_w': nrm((DEPTH, CONV_K, CONV_WIDTH), CONV_K ** -0.5),
        'd_conv_b': nrm((DEPTH, CONV_WIDTH), 0.01),
        'd_ln_g': 1.0 + nrm((DEPTH, CONV_WIDTH), 0.05),
        'd_ln_b': nrm((DEPTH, CONV_WIDTH), 0.01),
        'w_branch': nrm((DEPTH, N_BRANCH, MIX_WIDTH, Dm), MIX_WIDTH ** -0.5),
        'w_out': nrm((DEPTH, Dm, Dm), Dm ** -0.5),
        'mlp_w1': nrm((DEPTH, Dm, D_FF), Dm ** -0.5),
        'mlp_w2': nrm((DEPTH, D_FF, Dm), D_FF ** -0.5),
        'final_g': 1.0 + nrm((Dm,), 0.05),
    }


def reference(x, c, ctx, c_ctx, ada_w, ada_b, norm1_g, norm2_g, w_in, a_sink, s5_a_re, s5_a_im, s5_log_step,
              s5_b_re, s5_b_im, s5_c_re, s5_c_im, s5_d, s5_glu_w, s5_glu_b, c_rpb, d_conv_w, d_conv_b,
              d_ln_g, d_ln_b, w_branch, w_out, mlp_w1, mlp_w2, final_g):
    L = x.shape[1]
    rows = L // GRID_W
    t = jnp.arange(L)
    freqs = ROPE_BASE ** (-jnp.arange(ROPE_PAIRS, dtype=jnp.float32) / ROPE_PAIRS)
    ang = jnp.stack([(t // GRID_W).astype(jnp.float32)[:, None] * freqs,
                     (t % GRID_W).astype(jnp.float32)[:, None] * freqs], axis=1)
    cos, sin = jnp.cos(ang), jnp.sin(ang)
    z = ctx
    for l in range(DEPTH):
        need_ctx = l < DEPTH - 1
        mod_x = jnp.split((jax.nn.silu(c) @ ada_w[l] + ada_b[l])[:, None, :], 6, axis=-1)
        mod_z = jnp.split(jax.nn.silu(c_ctx) @ ada_w[l] + ada_b[l], 6, axis=-1)
        hx = modulate(rms_norm(x, norm1_g[l]), mod_x[0], mod_x[1])
        hz = modulate(rms_norm(z, norm1_g[l]), mod_z[0], mod_z[1])
        yx, yz = hybrid_mixer(hx, hz, cos, sin, rows, w_in[l], a_sink[l], s5_a_re[l], s5_a_im[l], s5_log_step[l],
                              s5_b_re[l], s5_b_im[l], s5_c_re[l], s5_c_im[l], s5_d[l], s5_glu_w[l], s5_glu_b[l],
                              c_rpb[l], d_conv_w[l], d_conv_b[l], d_ln_g[l], d_ln_b[l], w_branch[l], w_out[l],
                              need_ctx)
        x = x + mod_x[2] * yx
        x = x + mod_x[5] * squared_relu_mlp(modulate(rms_norm(x, norm2_g[l]), mod_x[3], mod_x[4]),
                                            mlp_w1[l], mlp_w2[l])
        if need_ctx:
            z = z + mod_z[2] * yz
            z = z + mod_z[5] * squared_relu_mlp(modulate(rms_norm(z, norm2_g[l]), mod_z[3], mod_z[4]),
                                                mlp_w1[l], mlp_w2[l])
    return rms_norm(x, final_g)
```

```python
import functools
import math

import numpy as np
import jax
import jax.numpy as jnp
from jax import lax
from jax.experimental import pallas as pl
from jax.experimental.pallas import tpu as pltpu

F32 = jnp.float32
BF16 = jnp.bfloat16

LANES = 128
SUBLANES = 8
VMEM_LIMIT = 56 * 1024 * 1024

GRID_W = 64
HEAD_DIM = 64
N_BRANCH = 4
A_HEADS = 8
A_KV_HEADS = 2
A_WINDOW = 128
A_BLOCK = 128
ROPE_BASE = 10000.0
ROPE_PAIRS = HEAD_DIM // 4
S5_GROUP_CH = 16
S5_STATE = 64
C_HEADS = 8
C_WIN_R = 8
C_WIN_C = 16
C_QROWS = 2
C_KROWS = C_QROWS + C_WIN_R
CONV_K = 31
CONV_HALO = 16
EPS = 1e-6
NEG_INF = -1e30


def _cparams(*sem):
    return pltpu.CompilerParams(dimension_semantics=sem, vmem_limit_bytes=VMEM_LIMIT)


def _sigmoid(x):
    return 1.0 / (1.0 + jnp.exp(-x))


def _silu(x):
    return x * _sigmoid(x)


def _gelu_tanh(x):
    return 0.5 * x * (1.0 + jnp.tanh(math.sqrt(2.0 / math.pi) * (x + 0.044715 * (x * x * x))))


def _ada_kernel(c_ref, w_ref, b_ref, o_ref):
    a = _silu(c_ref[...]).astype(BF16)
    o_ref[...] = jnp.dot(a, w_ref[...].astype(BF16), preferred_element_type=F32) + b_ref[...]


def ada_modulation(cc, ada_w, ada_b, tn=1024):
    depth, d, n = ada_w.shape
    return pl.pallas_call(
        _ada_kernel,
        grid=(depth, n // tn),
        in_specs=[pl.BlockSpec((SUBLANES, d), lambda l, j: (0, 0)),
                  pl.BlockSpec((None, d, tn), lambda l, j: (l, 0, j)),
                  pl.BlockSpec((None, 1, tn), lambda l, j: (l, 0, j))],
        out_specs=pl.BlockSpec((None, SUBLANES, tn), lambda l, j: (l, 0, j)),
        out_shape=jax.ShapeDtypeStruct((depth, SUBLANES, n), F32),
        compiler_params=_cparams("arbitrary", "arbitrary"),
        name="ada_modulation",
    )(cc, ada_w, ada_b.reshape(depth, 1, n))


def _norm_mod_kernel(x_ref, g_ref, sc_ref, sh_ref, o_ref):
    x = x_ref[...]
    y = x * lax.rsqrt(jnp.mean(x * x, axis=-1, keepdims=True) + EPS) * g_ref[...]
    o_ref[...] = (y * (1.0 + sc_ref[...]) + sh_ref[...]).astype(o_ref.dtype)


def norm_modulate(x, g, scale, shift, out_dtype, tm=256):
    m, d = x.shape
    vec = pl.BlockSpec((1, d), lambda i: (0, 0))
    return pl.pallas_call(
        _norm_mod_kernel,
        grid=(m // tm,),
        in_specs=[pl.BlockSpec((tm, d), lambda i: (i, 0)), vec, vec, vec],
        out_specs=pl.BlockSpec((tm, d), lambda i: (i, 0)),
        out_shape=jax.ShapeDtypeStruct((m, d), out_dtype),
        compiler_params=_cparams("arbitrary"),
        name="norm_modulate",
    )(x, g.reshape(1, d), scale.reshape(1, d), shift.reshape(1, d))


def _mm_kernel(a_ref, w_ref, o_ref):
    o_ref[...] = jnp.dot(a_ref[...], w_ref[...], preferred_element_type=F32).astype(o_ref.dtype)


def _rope_block(y, cos, sa, sb):
    return y * cos + pltpu.roll(y, 16, 1) * sa + pltpu.roll(y, LANES - 16, 1) * sb


def _mm_rope_kernel(a_ref, w_ref, cos_ref, sa_ref, sb_ref, o_ref):
    acc = jnp.dot(a_ref[...], w_ref[...], preferred_element_type=F32)
    cos, sa, sb = cos_ref[...], sa_ref[...], sb_ref[...]
    for b in range(acc.shape[1] // LANES):
        sl = slice(b * LANES, (b + 1) * LANES)
        o_ref[:, sl] = _rope_block(acc[:, sl], cos, sa, sb).astype(o_ref.dtype)


def _mm_residual_kernel(a_ref, w_ref, x_ref, g_ref, o_ref):
    acc = jnp.dot(a_ref[...], w_ref[...], preferred_element_type=F32)
    o_ref[...] = x_ref[...] + g_ref[...] * acc


def matmul(a, w, tm, tn, rope=None, residual=None, out_dtype=BF16):
    m, k = a.shape
    n = w.shape[1]
    tm, tn = min(tm, m), min(tn, n)
    in_specs = [pl.BlockSpec((tm, k), lambda i, j: (i, 0)), pl.BlockSpec((k, tn), lambda i, j: (0, j))]
    args = [a, w]
    if rope is not None:
        kern = _mm_rope_kernel
        in_specs += [pl.BlockSpec((tm, LANES), lambda i, j: (i, 0))] * 3
        args += list(rope)
    elif residual is not None:
        kern = _mm_residual_kernel
        x, gate = residual
        in_specs += [pl.BlockSpec((tm, tn), lambda i, j: (i, j)), pl.BlockSpec((1, tn), lambda i, j: (0, j))]
        args += [x, gate.reshape(1, n)]
        out_dtype = F32
    else:
        kern = _mm_kernel
    return pl.pallas_call(
        kern,
        grid=(m // tm, n // tn),
        in_specs=in_specs,
        out_specs=pl.BlockSpec((tm, tn), lambda i, j: (i, j)),
        out_shape=jax.ShapeDtypeStruct((m, n), out_dtype),
        compiler_params=_cparams("arbitrary", "arbitrary"),
        name="matmul",
    )(*args)


def _lane_lo():
    return lax.broadcasted_iota(jnp.int32, (1, LANES), 1) < HEAD_DIM


def _half_select(x, lo, half):
    zero = jnp.zeros_like(x)
    return jnp.where(lo, x, zero) if half == 0 else jnp.where(lo, zero, x)


def _dot_nt(a, b):
    return lax.dot_general(a, b, (((1,), (1,)), ((), ())), preferred_element_type=F32)


def _softmax_pv(parts, values, sink=None):
    m = parts[0].max(axis=-1, keepdims=True)
    for s in parts[1:]:
        m = jnp.maximum(m, s.max(axis=-1, keepdims=True))
    if sink is not None:
        m = jnp.maximum(m, sink)
    den = jnp.exp(sink - m) if sink is not None else 0.0
    out = 0.0
    for s, v in zip(parts, values):
        p = jnp.exp(s - m)
        den = den + p.sum(axis=-1, keepdims=True)
        out = out + jnp.dot(p.astype(BF16), v, preferred_element_type=F32)
    return out / den


def _attn_a_kernel(sink_ref, q_ref, kp_ref, kc_ref, kn_ref, vp_ref, vc_ref, vn_ref, kz_ref, vz_ref, o_ref,
                   *, seq):
    i = pl.program_id(0)
    lo = _lane_lo()
    r = lax.broadcasted_iota(jnp.int32, (A_BLOCK, 3 * A_BLOCK), 0)
    c = lax.broadcasted_iota(jnp.int32, (A_BLOCK, 3 * A_BLOCK), 1)
    kabs = c + (i - 1) * A_BLOCK
    valid = ((c - r) >= 0) & ((c - r) <= 2 * A_WINDOW) & (kabs >= 0) & (kabs < seq)
    for g in range(A_KV_HEADS):
        gs = slice(g * LANES, (g + 1) * LANES)
        kw = jnp.concatenate([kp_ref[:, gs], kc_ref[:, gs], kn_ref[:, gs]], axis=0)
        vw = jnp.concatenate([vp_ref[:, gs], vc_ref[:, gs], vn_ref[:, gs]], axis=0)
        kz, vz = kz_ref[:, gs], vz_ref[:, gs]
        for bb in range(2):
            b = 2 * g + bb
            bs = slice(b * LANES, (b + 1) * LANES)
            qb = q_ref[:, bs]
            acc = jnp.zeros((A_BLOCK, LANES), F32)
            for half in range(2):
                ql = _half_select(qb, lo, half) * (HEAD_DIM ** -0.5)
                s = jnp.where(valid, _dot_nt(ql, kw), NEG_INF)
                sz = _dot_nt(ql, kz)
                acc = acc + _softmax_pv([s, sz], [_half_select(vw, lo, half), _half_select(vz, lo, half)],
                                        sink=sink_ref[2 * b + half])
            o_ref[:, bs] = acc.astype(o_ref.dtype)


def window_attention(sink, qk, rest, qk_ctx, rest_ctx, vv_col):
    seq = qk.shape[0]
    lc = qk_ctx.shape[0]
    nb = seq // A_BLOCK
    kw = 2 * LANES
    prev = lambda i: jnp.maximum(i - 1, 0)
    nxt = lambda i: jnp.minimum(i + 1, nb - 1)
    return pl.pallas_call(
        functools.partial(_attn_a_kernel, seq=seq),
        grid=(nb,),
        in_specs=[pl.BlockSpec(memory_space=pltpu.SMEM),
                  pl.BlockSpec((A_BLOCK, 4 * LANES), lambda i: (i, 0)),
                  pl.BlockSpec((A_BLOCK, kw), lambda i: (prev(i), 2)),
                  pl.BlockSpec((A_BLOCK, kw), lambda i: (i, 2)),
                  pl.BlockSpec((A_BLOCK, kw), lambda i: (nxt(i), 2)),
                  pl.BlockSpec((A_BLOCK, kw), lambda i: (prev(i), vv_col)),
                  pl.BlockSpec((A_BLOCK, kw), lambda i: (i, vv_col)),
                  pl.BlockSpec((A_BLOCK, kw), lambda i: (nxt(i), vv_col)),
                  pl.BlockSpec((lc, kw), lambda i: (0, 2)),
                  pl.BlockSpec((lc, kw), lambda i: (0, vv_col))],
        out_specs=pl.BlockSpec((A_BLOCK, 4 * LANES), lambda i: (i, 0)),
        out_shape=jax.ShapeDtypeStruct((seq, 4 * LANES), BF16),
        compiler_params=_cparams("arbitrary"),
        name="window_attention",
    )(sink, qk, qk, qk, qk, rest, rest, rest, qk_ctx, rest_ctx)


def _ctx_attn_kernel(sink_ref, q_ref, k_ref, v_ref, o_ref, *, kv_shared, has_sink):
    lo = _lane_lo()
    for b in range(q_ref.shape[1] // LANES):
        bs = slice(b * LANES, (b + 1) * LANES)
        kb = b // 2 if kv_shared else b
        ks = slice(kb * LANES, (kb + 1) * LANES)
        qb, k, v = q_ref[:, bs], k_ref[:, ks], v_ref[:, ks]
        acc = jnp.zeros((q_ref.shape[0], LANES), F32)
        for half in range(2):
            ql = _half_select(qb, lo, half) * (HEAD_DIM ** -0.5)
            sink = sink_ref[2 * b + half] if has_sink else None
            acc = acc + _softmax_pv([_dot_nt(ql, k)], [_half_select(v, lo, half)], sink=sink)
        o_ref[:, bs] = acc.astype(o_ref.dtype)


def context_attention(sink, q_arr, q_col, k_arr, k_col, k_w, v_arr, v_col, kv_shared, has_sink):
    lc = q_arr.shape[0]
    return pl.pallas_call(
        functools.partial(_ctx_attn_kernel, kv_shared=kv_shared, has_sink=has_sink),
        grid=(1,),
        in_specs=[pl.BlockSpec(memory_space=pltpu.SMEM),
                  pl.BlockSpec((lc, 4 * LANES), lambda i: (0, q_col)),
                  pl.BlockSpec((lc, k_w), lambda i: (0, k_col)),
                  pl.BlockSpec((lc, k_w), lambda i: (0, v_col))],
        out_specs=pl.BlockSpec((lc, 4 * LANES), lambda i: (0, 0)),
        out_shape=jax.ShapeDtypeStruct((lc, 4 * LANES), BF16),
        compiler_params=_cparams("arbitrary"),
        name="context_attention",
    )(sink, q_arr, k_arr, v_arr)


def _nbr_tables(rows):
    nsteps = rows // C_QROWS
    nblk = C_KROWS * GRID_W // LANES
    wr = min(C_WIN_R, rows)
    reps = [0, 1, 2, nsteps - 2, nsteps - 1]
    dr = np.zeros((len(reps), C_QROWS * GRID_W, C_KROWS * GRID_W), np.int32)
    dc = np.zeros_like(dr)
    ok = np.zeros(dr.shape, bool)
    qc = np.arange(GRID_W)[:, None]
    kc = np.arange(GRID_W)[None, :]
    cstart = np.clip(qc - C_WIN_C // 2, 0, GRID_W - C_WIN_C)
    col_ok = (kc >= cstart) & (kc < cstart + C_WIN_C)
    dcol = np.clip(kc - qc + C_WIN_C - 1, 0, 2 * C_WIN_C - 2)
    for v, i in enumerate(reps):
        w0 = C_QROWS * int(np.clip(i - 2, 0, nsteps - nblk))
        for a in range(C_QROWS):
            rq = C_QROWS * i + a
            k0 = int(np.clip(rq - C_WIN_R // 2, 0, rows - wr))
            for j in range(C_KROWS):
                rk = w0 + j
                qs = slice(a * GRID_W, (a + 1) * GRID_W)
                ks = slice(j * GRID_W, (j + 1) * GRID_W)
                if k0 <= rk < k0 + wr:
                    ok[v, qs, ks] = col_ok
                    dr[v, qs, ks] = rk - rq + C_WIN_R - 1
                    dc[v, qs, ks] = dcol
    return dr, dc, ok


def _attn_c_kernel(q_ref, *refs):
    nblk = C_KROWS * GRID_W // LANES
    k_refs, v_refs = refs[:nblk], refs[nblk:2 * nblk]
    kz_ref, vz_ref, bias_ref, o_ref = refs[2 * nblk:]
    lo = _lane_lo()
    for b in range(C_HEADS // 2):
        bs = slice(b * LANES, (b + 1) * LANES)
        kw = jnp.concatenate([kr[:, bs] for kr in k_refs], axis=0)
        vw = jnp.concatenate([vr[:, bs] for vr in v_refs], axis=0)
        kz, vz = kz_ref[:, bs], vz_ref[:, bs]
        qb = q_ref[:, bs]
        acc = jnp.zeros((q_ref.shape[0], LANES), F32)
        for half in range(2):
            ql = _half_select(qb, lo, half) * (HEAD_DIM ** -0.5)
            s = _dot_nt(ql, kw) + bias_ref[2 * b + half]
            sz = _dot_nt(ql, kz)
            acc = acc + _softmax_pv([s, sz], [_half_select(vw, lo, half), _half_select(vz, lo, half)])
        o_ref[:, bs] = acc.astype(o_ref.dtype)


def neighborhood_attention(rest, rest_ctx, rpb, q_col, k_col, v_col):
    seq = rest.shape[0]
    lc = rest_ctx.shape[0]
    rows = seq // GRID_W
    nsteps = rows // C_QROWS
    nblk = C_KROWS * GRID_W // LANES
    assert nsteps > nblk and C_QROWS * GRID_W == LANES
    dr, dc, ok = _nbr_tables(rows)
    bias = jnp.where(jnp.asarray(ok)[None], rpb.astype(F32)[:, dr, dc], NEG_INF)
    bias = jnp.moveaxis(bias, 1, 0)

    def first_blk(i):
        return jnp.clip(i - 2, 0, nsteps - nblk)

    def variant(i):
        return jnp.where(i < 2, i, jnp.where(i >= nsteps - 2, i - (nsteps - 2) + 3, 2))

    w = 4 * LANES
    kspecs = [pl.BlockSpec((LANES, w), functools.partial(lambda i, t: (first_blk(i) + t, k_col), t=t))
              for t in range(nblk)]
    vspecs = [pl.BlockSpec((LANES, w), functools.partial(lambda i, t: (first_blk(i) + t, v_col), t=t))
              for t in range(nblk)]
    return pl.pallas_call(
        _attn_c_kernel,
        grid=(nsteps,),
        in_specs=[pl.BlockSpec((LANES, w), lambda i: (i, q_col))] + kspecs + vspecs + [
            pl.BlockSpec((lc, w), lambda i: (0, k_col)),
            pl.BlockSpec((lc, w), lambda i: (0, v_col)),
            pl.BlockSpec((None, C_HEADS, LANES, nblk * LANES), lambda i: (variant(i), 0, 0, 0))],
        out_specs=pl.BlockSpec((LANES, w), lambda i: (i, 0)),
        out_shape=jax.ShapeDtypeStruct((seq, w), BF16),
        compiler_params=_cparams("arbitrary"),
        name="neighborhood_attention",
    )(rest, *([rest] * (2 * nblk)), rest_ctx, rest_ctx, bias)


def _s5_scan(h_ref, tab_ref, carry_ref, n_tiles, reverse):
    p = h_ref.shape[1] // 2

    def body(step, carry):
        cr, ci = carry
        tile = (n_tiles - 1 - step) if reverse else step
        off = pl.multiple_of(tile * SUBLANES, SUBLANES)
        xr = h_ref[pl.ds(off, SUBLANES), 0:p]
        xi = h_ref[pl.ds(off, SUBLANES), p:2 * p]
        for lvl, k in enumerate((1, 2, 4)):
            shift = (SUBLANES - k) if reverse else k
            sr, si = pltpu.roll(xr, shift, 0), pltpu.roll(xi, shift, 0)
            mr, mi = tab_ref[2 * lvl], tab_ref[2 * lvl + 1]
            xr, xi = xr + mr * sr - mi * si, xi + mr * si + mi * sr
        pr, pi = tab_ref[6], tab_ref[7]
        xr, xi = xr + pr * cr - pi * ci, xi + pr * ci + pi * cr
        h_ref[pl.ds(off, SUBLANES), 0:p] = xr
        h_ref[pl.ds(off, SUBLANES), p:2 * p] = xi
        edge = 0 if reverse else SUBLANES - 1
        return (jnp.broadcast_to(xr[edge:edge + 1], xr.shape), jnp.broadcast_to(xi[edge:edge + 1], xi.shape))

    cr, ci = lax.fori_loop(0, n_tiles, body, (carry_ref[0], carry_ref[1]))
    carry_ref[0] = cr
    carry_ref[1] = ci


def _s5_kernel(uz_ref, ul_ref, b_ref, c_ref, tab_ref, d_ref, *rest, reverse, final):
    if final:
        yfz_ref, yfl_ref, gw_ref, gb_ref, oz_ref, ol_ref, h_ref, carry_ref = rest
    else:
        oz_ref, ol_ref, h_ref, carry_ref = rest
    s = pl.program_id(0)
    t = h_ref.shape[0]

    @pl.when(s == 0)
    def _():
        carry_ref[...] = jnp.zeros_like(carry_ref)

    def chunk(u_ref, yf_ref, o_ref):
        u = u_ref[...]
        h_ref[...] = jnp.dot(u, b_ref[...], preferred_element_type=F32)
        _s5_scan(h_ref, tab_ref, carry_ref, t // SUBLANES, reverse)
        out = jnp.dot(h_ref[...].astype(BF16), c_ref[...], preferred_element_type=F32)
        if not final:
            o_ref[...] = u.astype(F32) * d_ref[...] + out
        else:
            g = _gelu_tanh(yf_ref[...] + out)
            z = jnp.dot(g.astype(BF16), gw_ref[...], preferred_element_type=F32) + gb_ref[...]
            o_ref[...] = (g * _sigmoid(z)).astype(o_ref.dtype)

    @pl.when(s == 0)
    def _():
        chunk(uz_ref, yfz_ref if final else None, oz_ref)

    @pl.when(s > 0)
    def _():
        chunk(ul_ref, yfl_ref if final else None, ol_ref)


def s5_direction(rest, rest_ctx, u_col, bmat, cmat, tabs, dskip, reverse, fwd=None, glu=None):
    seq = rest.shape[0]
    t = rest_ctx.shape[0]
    nl = seq // t
    w = 4 * LANES
    p2 = bmat.shape[1]
    final = fwd is not None
    if reverse:
        lat = lambda s: (jnp.where(s == 0, nl - 1, nl - s), 0)
    else:
        lat = lambda s: (jnp.maximum(s - 1, 0), 0)
    lat_u = lambda s: (lat(s)[0], u_col)
    const = lambda s: (0, 0)
    in_specs = [pl.BlockSpec((t, w), lambda s: (0, u_col)),
                pl.BlockSpec((t, w), lat_u),
                pl.BlockSpec((w, p2), const),
                pl.BlockSpec((p2, w), const),
                pl.BlockSpec((8, SUBLANES, p2 // 2), lambda s: (0, 0, 0)),
                pl.BlockSpec((1, w), const)]
    args = [rest_ctx, rest, bmat, cmat, tabs, dskip.reshape(1, w)]
    if final:
        in_specs += [pl.BlockSpec((t, w), const), pl.BlockSpec((t, w), lat),
                     pl.BlockSpec((w, w), const), pl.BlockSpec((1, w), const)]
        args += [fwd[0], fwd[1], glu[0], glu[1].reshape(1, w)]
    odt = BF16 if final else F32
    return pl.pallas_call(
        functools.partial(_s5_kernel, reverse=reverse, final=final),
        grid=(nl + 1,),
        in_specs=in_specs,
        out_specs=[pl.BlockSpec((t, w), const), pl.BlockSpec((t, w), lat)],
        out_shape=[jax.ShapeDtypeStruct((t, w), odt), jax.ShapeDtypeStruct((seq, w), odt)],
        scratch_shapes=[pltpu.VMEM((t, p2), F32), pltpu.VMEM((2, SUBLANES, p2 // 2), F32)],
        compiler_params=_cparams("arbitrary"),
        name="s5_direction",
    )(*args)


def _s5_params(a_re, a_im, log_step, b_re, b_im, c_re, c_im, reverse):
    ar, ai = a_re.astype(F32), a_im.astype(F32)
    step = jnp.exp(log_step.astype(F32))[:, None]
    mag = jnp.exp(ar * step)
    ab_re, ab_im = mag * jnp.cos(ai * step), mag * jnp.sin(ai * step)
    den = ar * ar + ai * ai
    nr = ab_re - 1.0
    f_re = (nr * ar + ab_im * ai) / den
    f_im = (ab_im * ar - nr * ai) / den
    br, bi = b_re.astype(F32), b_im.astype(F32)
    bb_re = f_re[..., None] * br - f_im[..., None] * bi
    bb_im = f_re[..., None] * bi + f_im[..., None] * br
    g, p, h = bb_re.shape
    eye = jnp.eye(g, dtype=F32)

    def drive(bb):
        return jnp.einsum('gph,gk->ghkp', bb, eye).reshape(g * h, g * p)

    def readout(c):
        return jnp.einsum('ghp,gk->gpkh', c, eye).reshape(g * p, g * h)

    bmat = jnp.concatenate([drive(bb_re), drive(bb_im)], axis=1).astype(BF16)
    cmat = jnp.concatenate([readout(c_re.astype(F32)), -readout(c_im.astype(F32))], axis=0).astype(BF16)
    pr, pi = [ab_re.reshape(-1)], [ab_im.reshape(-1)]
    for _ in range(SUBLANES - 1):
        pr.append(pr[-1] * pr[0] - pi[-1] * pi[0])
        pi.append(pr[-2] * pi[0] + pi[-1] * pr[0])
    row = np.arange(SUBLANES)[:, None]
    tabs = []
    for k in (1, 2, 4):
        keep = jnp.asarray((row < SUBLANES - k) if reverse else (row >= k), F32)
        tabs += [keep * pr[k - 1][None], keep * pi[k - 1][None]]
    order = list(range(SUBLANES - 1, -1, -1)) if reverse else list(range(SUBLANES))
    tabs += [jnp.stack([pr[j] for j in order]), jnp.stack([pi[j] for j in order])]
    return bmat, cmat, jnp.stack(tabs)


def _conv_kernel(val_ref, gate_ref, vp_ref, gp_ref, vn_ref, gn_ref, w_ref, b_ref, g_ref, beta_ref, o_ref, hs_ref,
                 *, sub):
    i = pl.program_id(0)
    n = pl.num_programs(0)
    t = val_ref.shape[0]

    def glu(v, g):
        return v[...].astype(F32) * _sigmoid(g[...].astype(F32))

    hs_ref[0:CONV_HALO] = jnp.where(i > 0, glu(vp_ref, gp_ref), 0.0)
    hs_ref[CONV_HALO:CONV_HALO + t] = glu(val_ref, gate_ref)
    hs_ref[CONV_HALO + t:2 * CONV_HALO + t] = jnp.where(i < n - 1, glu(vn_ref, gn_ref), 0.0)
    base = CONV_HALO - CONV_K // 2
    for r in range(t // sub):
        acc = jnp.zeros((sub, val_ref.shape[1]), F32) + b_ref[...]
        for k in range(CONV_K):
            acc = acc + hs_ref[r * sub + base + k:r * sub + base + k + sub] * w_ref[k:k + 1]
        mu = jnp.mean(acc, axis=-1, keepdims=True)
        cen = acc - mu
        var = jnp.mean(cen * cen, axis=-1, keepdims=True)
        yn = cen * lax.rsqrt(var + EPS) * g_ref[...] + beta_ref[...]
        o_ref[r * sub:(r + 1) * sub] = _silu(yn).astype(o_ref.dtype)


def conv_module(rest, val_col, gate_col, w, b, g, beta, t=256, sub=32):
    seq = rest.shape[0]
    t = min(t, seq)
    cw = 4 * LANES
    hb = t // CONV_HALO
    nh = seq // CONV_HALO
    prev = lambda i: jnp.maximum(i * hb - 1, 0)
    nxt = lambda i: jnp.minimum((i + 1) * hb, nh - 1)
    vec = pl.BlockSpec((1, cw), lambda i: (0, 0))
    return pl.pallas_call(
        functools.partial(_conv_kernel, sub=sub),
        grid=(seq // t,),
        in_specs=[pl.BlockSpec((t, cw), lambda i: (i, val_col)),
                  pl.BlockSpec((t, cw), lambda i: (i, gate_col)),
                  pl.BlockSpec((CONV_HALO, cw), lambda i: (prev(i), val_col)),
                  pl.BlockSpec((CONV_HALO, cw), lambda i: (prev(i), gate_col)),
                  pl.BlockSpec((CONV_HALO, cw), lambda i: (nxt(i), val_col)),
                  pl.BlockSpec((CONV_HALO, cw), lambda i: (nxt(i), gate_col)),
                  pl.BlockSpec((CONV_K, cw), lambda i: (0, 0)), vec, vec, vec],
        out_specs=pl.BlockSpec((t, cw), lambda i: (i, 0)),
        out_shape=jax.ShapeDtypeStruct((seq, cw), BF16),
        scratch_shapes=[pltpu.VMEM((t + 2 * CONV_HALO, cw), F32)],
        compiler_params=_cparams("arbitrary"),
        name="conv_module",
    )(rest, rest, rest, rest, rest, rest, w.astype(F32), b.reshape(1, cw).astype(F32),
      g.reshape(1, cw).astype(F32), beta.reshape(1, cw).astype(F32))


def _merge_kernel(ya_ref, yb_ref, yc_ref, yd_ref, g0_ref, g1_ref, g2_ref, g3_ref, w_ref, o_ref):
    acc = None
    for y_ref, gl_ref, k in zip((ya_ref, yb_ref, yc_ref, yd_ref), (g0_ref, g1_ref, g2_ref, g3_ref), range(N_BRANCH)):
        term = _sigmoid(gl_ref[...].astype(F32)) * jnp.dot(y_ref[...], w_ref[k], preferred_element_type=F32)
        acc = term if acc is None else acc + term
    o_ref[...] = acc.astype(o_ref.dtype)


def gated_merge(ys, rest, w_branch, tm=512, tn=512):
    m, yw = ys[0].shape
    n = w_branch.shape[2]
    tm = min(tm, m)
    nj = n // tn
    yspec = pl.BlockSpec((tm, yw), lambda i, j: (i, 0))
    gspecs = [pl.BlockSpec((tm, tn), functools.partial(lambda i, j, k: (i, k * nj + j), k=k)) for k in range(N_BRANCH)]
    return pl.pallas_call(
        _merge_kernel,
        grid=(m // tm, nj),
        in_specs=[yspec] * N_BRANCH + gspecs + [pl.BlockSpec((N_BRANCH, yw, tn), lambda i, j: (0, 0, j))],
        out_specs=pl.BlockSpec((tm, tn), lambda i, j: (i, j)),
        out_shape=jax.ShapeDtypeStruct((m, n), BF16),
        compiler_params=_cparams("arbitrary", "arbitrary"),
        name="gated_merge",
    )(*ys, rest, rest, rest, rest, w_branch)


def _mlp_kernel(x_ref, g_ref, sc_ref, sh_ref, gate_ref, w1_ref, w2_ref, o_ref, h_ref, acc_ref):
    f = pl.program_id(1)

    @pl.when(f == 0)
    def _():
        x = x_ref[...]
        y = x * lax.rsqrt(jnp.mean(x * x, axis=-1, keepdims=True) + EPS) * g_ref[...]
        h_ref[...] = (y * (1.0 + sc_ref[...]) + sh_ref[...]).astype(BF16)
        acc_ref[...] = jnp.zeros_like(acc_ref)

    t = jnp.dot(h_ref[...], w1_ref[...], preferred_element_type=F32)
    t = jnp.square(jnp.maximum(t, 0.0)).astype(BF16)
    acc_ref[...] += jnp.dot(t, w2_ref[...], preferred_element_type=F32)

    @pl.when(f == pl.num_programs(1) - 1)
    def _():
        o_ref[...] = x_ref[...] + gate_ref[...] * acc_ref[...]


def mlp_block(x, g, scale, shift, gate, w1, w2, tm=512, tf=512):
    m, d = x.shape
    ff = w1.shape[1]
    tm = min(tm, m)
    vec = pl.BlockSpec((1, d), lambda i, f: (0, 0))
    return pl.pallas_call(
        _mlp_kernel,
        grid=(m // tm, ff // tf),
        in_specs=[pl.BlockSpec((tm, d), lambda i, f: (i, 0)), vec, vec, vec, vec,
                  pl.BlockSpec((d, tf), lambda i, f: (0, f)),
                  pl.BlockSpec((tf, d), lambda i, f: (f, 0))],
        out_specs=pl.BlockSpec((tm, d), lambda i, f: (i, 0)),
        out_shape=jax.ShapeDtypeStruct((m, d), F32),
        scratch_shapes=[pltpu.VMEM((tm, d), BF16), pltpu.VMEM((tm, d), F32)],
        compiler_params=_cparams("arbitrary", "arbitrary"),
        name="mlp_block",
    )(x, g.reshape(1, d), scale.reshape(1, d), shift.reshape(1, d), gate.reshape(1, d), w1, w2)


_COL_QC, _COL_KC, _COL_VC, _COL_U, _COL_VAL, _COL_GATE = 16, 17, 18, 19, 20, 21
_COL_VV = 44


def _split_w_in(w_in):
    d = w_in.shape[0]
    mw = d // 4
    hd = HEAD_DIM
    o = 0
    qa = w_in[:, o:o + mw]; o += mw
    ka = w_in[:, o:o + A_KV_HEADS * hd]; o += A_KV_HEADS * hd
    va = w_in[:, o:o + A_KV_HEADS * hd]; o += A_KV_HEADS * hd
    mid = w_in[:, o:o + 6 * mw]; o += 6 * mw
    gates = w_in[:, o:]
    u, qc, kc, vc, val, gate = (mid[:, k * mw:(k + 1) * mw] for k in range(6))

    def dup(wk):
        return jnp.concatenate([wk[:, g * hd:(g + 1) * hd] for g in range(A_KV_HEADS) for _ in range(2)], axis=1)

    w_rope = jnp.concatenate([qa, dup(ka)], axis=1).astype(BF16)
    w_rest = jnp.concatenate([gates, qc, kc, vc, u, val, gate, dup(va)], axis=1).astype(BF16)
    return w_rope, w_rest


def _rope_tables(seq):
    t = jnp.arange(seq)
    freqs = ROPE_BASE ** (-jnp.arange(ROPE_PAIRS, dtype=F32) / ROPE_PAIRS)
    lane = np.arange(LANES) % HEAD_DIM
    axis, half, pair = lane // 32, (lane % 32) // 16, lane % 16
    pos = jnp.stack([(t // GRID_W).astype(F32), (t % GRID_W).astype(F32)], axis=1)
    ang = pos[:, axis] * freqs[pair][None, :]
    cos, sin = jnp.cos(ang), jnp.sin(ang)
    sa = jnp.where(jnp.asarray(half == 1)[None], sin, 0.0)
    sb = jnp.where(jnp.asarray(half == 0)[None], -sin, 0.0)
    return cos, sa, sb


def kernel(x, c, ctx, c_ctx, ada_w, ada_b, norm1_g, norm2_g, w_in, a_sink, s5_a_re, s5_a_im, s5_log_step, s5_b_re, s5_b_im, s5_c_re, s5_c_im, s5_d, s5_glu_w, s5_glu_b, c_rpb, d_conv_w, d_conv_b, d_ln_g, d_ln_b, w_branch, w_out, mlp_w1, mlp_w2, final_g):
    depth = ada_w.shape[0]
    _, seq, d = x.shape
    lc = ctx.shape[1]
    xs = x.reshape(seq, d)
    zs = ctx.reshape(lc, d)
    rope = _rope_tables(seq)

    cc = jnp.zeros((SUBLANES, d), F32).at[0].set(c[0]).at[1].set(c_ctx)
    mods = ada_modulation(cc, ada_w, ada_b)

    for l in range(depth):
        need_ctx = l < depth - 1
        mx = [mods[l, 0, k * d:(k + 1) * d] for k in range(6)]
        mz = [mods[l, 1, k * d:(k + 1) * d] for k in range(6)]
        w_rope, w_rest = _split_w_in(w_in[l])
        wb = w_branch[l].astype(BF16)
        wo = w_out[l].astype(BF16)
        w1 = mlp_w1[l].astype(BF16)
        w2 = mlp_w2[l].astype(BF16)

        hx = norm_modulate(xs, norm1_g[l], mx[1], mx[0], BF16)
        hz = norm_modulate(zs, norm1_g[l], mz[1], mz[0], BF16)
        qk_x = matmul(hx, w_rope, 512, 768, rope=rope)
        qk_z = matmul(hz, w_rope, 256, 768)
        r_x = matmul(hx, w_rest, 512, 1280)
        r_z = matmul(hz, w_rest, 256, 1280)

        ya_x = window_attention(a_sink[l], qk_x, r_x, qk_z, r_z, _COL_VV)
        yc_x = neighborhood_attention(r_x, r_z, c_rpb[l], _COL_QC, _COL_KC, _COL_VC)
        yd_x = conv_module(r_x, _COL_VAL, _COL_GATE, d_conv_w[l], d_conv_b[l], d_ln_g[l], d_ln_b[l])

        pf = _s5_params(s5_a_re[l, 0], s5_a_im[l, 0], s5_log_step[l, 0], s5_b_re[l, 0], s5_b_im[l, 0],
                        s5_c_re[l, 0], s5_c_im[l, 0], reverse=False)
        pb = _s5_params(s5_a_re[l, 1], s5_a_im[l, 1], s5_log_step[l, 1], s5_b_re[l, 1], s5_b_im[l, 1],
                        s5_c_re[l, 1], s5_c_im[l, 1], reverse=True)
        yf = s5_direction(r_x, r_z, _COL_U, *pf, s5_d[l].astype(F32), reverse=False)
        yb_z, yb_x = s5_direction(r_x, r_z, _COL_U, *pb, s5_d[l].astype(F32), reverse=True, fwd=yf,
                                  glu=(s5_glu_w[l].astype(BF16), s5_glu_b[l].astype(F32)))

        m_x = gated_merge((ya_x, yb_x, yc_x, yd_x), r_x, wb)
        xs = matmul(m_x, wo, 512, 512, residual=(xs, mx[2]))
        xs = mlp_block(xs, norm2_g[l], mx[4], mx[3], mx[5], w1, w2)

        if need_ctx:
            ya_z = context_attention(a_sink[l], qk_z, 0, qk_z, 2, 2 * LANES, r_z, _COL_VV, True, True)
            yc_z = context_attention(a_sink[l], r_z, _COL_QC, r_z, _COL_KC, 4 * LANES, r_z, _COL_VC, False, False)
            yd_z = conv_module(r_z, _COL_VAL, _COL_GATE, d_conv_w[l], d_conv_b[l], d_ln_g[l], d_ln_b[l])
            m_z = gated_merge((ya_z, yb_z, yc_z, yd_z), r_z, wb)
            zs = matmul(m_z, wo, 256, 512, residual=(zs, mz[2]))
            zs = mlp_block(zs, norm2_g[l], mz[4], mz[3], mz[5], w1, w2)

    zero = jnp.zeros((d,), F32)
    out = norm_modulate(xs, final_g, zero, zero, F32)
    return out.reshape(x.shape)
```

```python
import functools
import math

import numpy as np
import jax
import jax.numpy as jnp
from jax import lax
from jax.experimental import pallas as pl
from jax.experimental.pallas import tpu as pltpu

F32 = jnp.float32
BF16 = jnp.bfloat16

LANES = 128
SUBLANES = 8
VMEM_LIMIT = 56 * 1024 * 1024

GRID_W = 64
HEAD_DIM = 64
N_BRANCH = 4
A_HEADS = 8
A_KV_HEADS = 2
A_WINDOW = 128
A_BLOCK = 128
ROPE_BASE = 10000.0
ROPE_PAIRS = HEAD_DIM // 4
S5_GROUP_CH = 16
S5_STATE = 64
C_HEADS = 8
C_WIN_R = 8
C_WIN_C = 16
C_QROWS = 2
C_KROWS = C_QROWS + C_WIN_R
CONV_K = 31
CONV_HALO = 16
EPS = 1e-6
NEG_INF = -1e30


def _cparams(*sem):
    return pltpu.CompilerParams(dimension_semantics=sem, vmem_limit_bytes=VMEM_LIMIT)


def _sigmoid(x):
    return 0.5 * jnp.tanh(0.5 * x) + 0.5


def _silu(x):
    return x * _sigmoid(x)


def _gelu_tanh(x):
    return 0.5 * x * (1.0 + jnp.tanh(math.sqrt(2.0 / math.pi) * (x + 0.044715 * (x * x * x))))


def _ada_kernel(c_ref, w_ref, b_ref, o_ref):
    a = _silu(c_ref[...]).astype(BF16)
    o_ref[...] = jnp.dot(a, w_ref[...].astype(BF16), preferred_element_type=F32) + b_ref[...]


def ada_modulation(cc, ada_w, ada_b, tn=1024):
    depth, d, n = ada_w.shape
    return pl.pallas_call(
        _ada_kernel,
        grid=(depth, n // tn),
        in_specs=[pl.BlockSpec((SUBLANES, d), lambda l, j: (0, 0)),
                  pl.BlockSpec((None, d, tn), lambda l, j: (l, 0, j)),
                  pl.BlockSpec((None, 1, tn), lambda l, j: (l, 0, j))],
        out_specs=pl.BlockSpec((None, SUBLANES, tn), lambda l, j: (l, 0, j)),
        out_shape=jax.ShapeDtypeStruct((depth, SUBLANES, n), F32),
        compiler_params=_cparams("arbitrary", "arbitrary"),
        name="ada_modulation",
    )(cc, ada_w, ada_b.reshape(depth, 1, n))


def _norm_mod_kernel(x_ref, g_ref, sc_ref, sh_ref, o_ref):
    x = x_ref[...]
    y = x * lax.rsqrt(jnp.mean(x * x, axis=-1, keepdims=True) + EPS) * g_ref[...]
    o_ref[...] = (y * (1.0 + sc_ref[...]) + sh_ref[...]).astype(o_ref.dtype)


def norm_modulate(x, g, scale, shift, out_dtype, tm=256):
    m, d = x.shape
    vec = pl.BlockSpec((1, d), lambda i: (0, 0))
    return pl.pallas_call(
        _norm_mod_kernel,
        grid=(m // tm,),
        in_specs=[pl.BlockSpec((tm, d), lambda i: (i, 0)), vec, vec, vec],
        out_specs=pl.BlockSpec((tm, d), lambda i: (i, 0)),
        out_shape=jax.ShapeDtypeStruct((m, d), out_dtype),
        compiler_params=_cparams("arbitrary"),
        name="norm_modulate",
    )(x, g.reshape(1, d), scale.reshape(1, d), shift.reshape(1, d))


def _mm_kernel(a_ref, w_ref, o_ref):
    o_ref[...] = jnp.dot(a_ref[...], w_ref[...], preferred_element_type=F32).astype(o_ref.dtype)


def _rope_block(y, cos, sa, sb):
    return y * cos + pltpu.roll(y, 16, 1) * sa + pltpu.roll(y, LANES - 16, 1) * sb


def _mm_rope_kernel(a_ref, w_ref, cos_ref, sa_ref, sb_ref, o_ref):
    acc = jnp.dot(a_ref[...], w_ref[...], preferred_element_type=F32)
    cos, sa, sb = cos_ref[...], sa_ref[...], sb_ref[...]
    for b in range(acc.shape[1] // LANES):
        sl = slice(b * LANES, (b + 1) * LANES)
        o_ref[:, sl] = _rope_block(acc[:, sl], cos, sa, sb).astype(o_ref.dtype)


def _mm_residual_kernel(a_ref, w_ref, x_ref, g_ref, o_ref):
    acc = jnp.dot(a_ref[...], w_ref[...], preferred_element_type=F32)
    o_ref[...] = x_ref[...] + g_ref[...] * acc


def matmul(a, w, tm, tn, rope=None, residual=None, out_dtype=BF16):
    m, k = a.shape
    n = w.shape[1]
    tm, tn = min(tm, m), min(tn, n)
    in_specs = [pl.BlockSpec((tm, k), lambda i, j: (i, 0)), pl.BlockSpec((k, tn), lambda i, j: (0, j))]
    args = [a, w]
    if rope is not None:
        kern = _mm_rope_kernel
        in_specs += [pl.BlockSpec((tm, LANES), lambda i, j: (i, 0))] * 3
        args += list(rope)
    elif residual is not None:
        kern = _mm_residual_kernel
        x, gate = residual
        in_specs += [pl.BlockSpec((tm, tn), lambda i, j: (i, j)), pl.BlockSpec((1, tn), lambda i, j: (0, j))]
        args += [x, gate.reshape(1, n)]
        out_dtype = F32
    else:
        kern = _mm_kernel
    return pl.pallas_call(
        kern,
        grid=(m // tm, n // tn),
        in_specs=in_specs,
        out_specs=pl.BlockSpec((tm, tn), lambda i, j: (i, j)),
        out_shape=jax.ShapeDtypeStruct((m, n), out_dtype),
        compiler_params=_cparams("arbitrary", "arbitrary"),
        name="matmul",
    )(*args)


def _lane_lo():
    return lax.broadcasted_iota(jnp.int32, (1, LANES), 1) < HEAD_DIM


def _half_select(x, lo, half):
    zero = jnp.zeros_like(x)
    return jnp.where(lo, x, zero) if half == 0 else jnp.where(lo, zero, x)


def _dot_nt(a, b):
    return lax.dot_general(a, b, (((1,), (1,)), ((), ())), preferred_element_type=F32)


def _softmax_pv(parts, values, sink=None):
    m = parts[0].max(axis=-1, keepdims=True)
    for s in parts[1:]:
        m = jnp.maximum(m, s.max(axis=-1, keepdims=True))
    if sink is not None:
        m = jnp.maximum(m, sink)
    den = jnp.exp(sink - m) if sink is not None else 0.0
    out = 0.0
    for s, v in zip(parts, values):
        p = jnp.exp(s - m)
        den = den + p.sum(axis=-1, keepdims=True)
        out = out + jnp.dot(p.astype(BF16), v, preferred_element_type=F32)
    return out / den


def _attn_a_kernel(sink_ref, q_ref, kp_ref, kc_ref, kn_ref, vp_ref, vc_ref, vn_ref, kz_ref, vz_ref, o_ref,
                   *, seq):
    i = pl.program_id(0)
    lo = _lane_lo()
    nq = 4 * A_BLOCK
    r = lax.broadcasted_iota(jnp.int32, (nq, 3 * A_BLOCK), 0) & (A_BLOCK - 1)
    c = lax.broadcasted_iota(jnp.int32, (nq, 3 * A_BLOCK), 1)
    kabs = c + (i - 1) * A_BLOCK
    valid = ((c - r) >= 0) & ((c - r) <= 2 * A_WINDOW) & (kabs >= 0) & (kabs < seq)
    rowblk = lax.broadcasted_iota(jnp.int32, (nq, 1), 0) // A_BLOCK
    for g in range(A_KV_HEADS):
        gs = slice(g * LANES, (g + 1) * LANES)
        kw = jnp.concatenate([kp_ref[:, gs], kc_ref[:, gs], kn_ref[:, gs]], axis=0)
        vw = jnp.concatenate([vp_ref[:, gs], vc_ref[:, gs], vn_ref[:, gs]], axis=0)
        kz, vz = kz_ref[:, gs], vz_ref[:, gs]
        blocks = [slice((2 * g + bb) * LANES, (2 * g + bb + 1) * LANES) for bb in range(2)]
        ql = jnp.concatenate([_half_select(q_ref[:, bs], lo, half) * (HEAD_DIM ** -0.5)
                              for half in range(2) for bs in blocks], axis=0)
        heads = [2 * (2 * g + bb) + half for half in range(2) for bb in range(2)]
        sink = jnp.full((nq, 1), sink_ref[heads[3]], F32)
        for k in range(3):
            sink = jnp.where(rowblk == k, sink_ref[heads[k]], sink)
        s = jnp.where(valid, _dot_nt(ql, kw), NEG_INF)
        sz = _dot_nt(ql, kz)
        m = jnp.maximum(jnp.maximum(s.max(axis=-1, keepdims=True), sz.max(axis=-1, keepdims=True)), sink)
        p, pz = jnp.exp(s - m), jnp.exp(sz - m)
        inv = 1.0 / (p.sum(axis=-1, keepdims=True) + pz.sum(axis=-1, keepdims=True) + jnp.exp(sink - m))
        p, pz = p.astype(BF16), pz.astype(BF16)
        out = None
        for half in range(2):
            hs = slice(half * 2 * A_BLOCK, (half + 1) * 2 * A_BLOCK)
            o = (jnp.dot(p[hs], _half_select(vw, lo, half), preferred_element_type=F32)
                 + jnp.dot(pz[hs], _half_select(vz, lo, half), preferred_element_type=F32)) * inv[hs]
            out = o if out is None else out + o
        for bb in range(2):
            o_ref[:, blocks[bb]] = out[bb * A_BLOCK:(bb + 1) * A_BLOCK].astype(o_ref.dtype)


def window_attention(sink, qk, rest, qk_ctx, rest_ctx, vv_col):
    seq = qk.shape[0]
    lc = qk_ctx.shape[0]
    nb = seq // A_BLOCK
    kw = 2 * LANES
    prev = lambda i: jnp.maximum(i - 1, 0)
    nxt = lambda i: jnp.minimum(i + 1, nb - 1)
    return pl.pallas_call(
        functools.partial(_attn_a_kernel, seq=seq),
        grid=(nb,),
        in_specs=[pl.BlockSpec(memory_space=pltpu.SMEM),
                  pl.BlockSpec((A_BLOCK, 4 * LANES), lambda i: (i, 0)),
                  pl.BlockSpec((A_BLOCK, kw), lambda i: (prev(i), 2)),
                  pl.BlockSpec((A_BLOCK, kw), lambda i: (i, 2)),
                  pl.BlockSpec((A_BLOCK, kw), lambda i: (nxt(i), 2)),
                  pl.BlockSpec((A_BLOCK, kw), lambda i: (prev(i), vv_col)),
                  pl.BlockSpec((A_BLOCK, kw), lambda i: (i, vv_col)),
                  pl.BlockSpec((A_BLOCK, kw), lambda i: (nxt(i), vv_col)),
                  pl.BlockSpec((lc, kw), lambda i: (0, 2)),
                  pl.BlockSpec((lc, kw), lambda i: (0, vv_col))],
        out_specs=pl.BlockSpec((A_BLOCK, 4 * LANES), lambda i: (i, 0)),
        out_shape=jax.ShapeDtypeStruct((seq, 4 * LANES), BF16),
        compiler_params=_cparams("arbitrary"),
        name="window_attention",
    )(sink, qk, qk, qk, qk, rest, rest, rest, qk_ctx, rest_ctx)


def _ctx_attn_kernel(sink_ref, q_ref, k_ref, v_ref, o_ref, *, kv_shared, has_sink):
    lo = _lane_lo()
    for b in range(q_ref.shape[1] // LANES):
        bs = slice(b * LANES, (b + 1) * LANES)
        kb = b // 2 if kv_shared else b
        ks = slice(kb * LANES, (kb + 1) * LANES)
        qb, k, v = q_ref[:, bs], k_ref[:, ks], v_ref[:, ks]
        acc = jnp.zeros((q_ref.shape[0], LANES), F32)
        for half in range(2):
            ql = _half_select(qb, lo, half) * (HEAD_DIM ** -0.5)
            sink = sink_ref[2 * b + half] if has_sink else None
            acc = acc + _softmax_pv([_dot_nt(ql, k)], [_half_select(v, lo, half)], sink=sink)
        o_ref[:, bs] = acc.astype(o_ref.dtype)


def context_attention(sink, q_arr, q_col, k_arr, k_col, k_w, v_arr, v_col, kv_shared, has_sink):
    lc = q_arr.shape[0]
    return pl.pallas_call(
        functools.partial(_ctx_attn_kernel, kv_shared=kv_shared, has_sink=has_sink),
        grid=(1,),
        in_specs=[pl.BlockSpec(memory_space=pltpu.SMEM),
                  pl.BlockSpec((lc, 4 * LANES), lambda i: (0, q_col)),
                  pl.BlockSpec((lc, k_w), lambda i: (0, k_col)),
                  pl.BlockSpec((lc, k_w), lambda i: (0, v_col))],
        out_specs=pl.BlockSpec((lc, 4 * LANES), lambda i: (0, 0)),
        out_shape=jax.ShapeDtypeStruct((lc, 4 * LANES), BF16),
        compiler_params=_cparams("arbitrary"),
        name="context_attention",
    )(sink, q_arr, k_arr, v_arr)


def _nbr_row_offsets(rows):
    nsteps = rows // C_QROWS
    nblk = C_KROWS * GRID_W // LANES
    wr = min(C_WIN_R, rows)
    reps = [0, 1, 2, nsteps - 2, nsteps - 1]
    dr = -np.ones((len(reps), C_QROWS, C_KROWS), np.int32)
    for v, i in enumerate(reps):
        w0 = C_QROWS * int(np.clip(i - 2, 0, nsteps - nblk))
        for a in range(C_QROWS):
            rq = C_QROWS * i + a
            k0 = int(np.clip(rq - C_WIN_R // 2, 0, rows - wr))
            for j in range(C_KROWS):
                if k0 <= w0 + j < k0 + wr:
                    dr[v, a, j] = w0 + j - rq + C_WIN_R - 1
    return dr


def _nbr_bias(rpb, rows):
    h = rpb.shape[0]
    qc = np.arange(GRID_W)[:, None]
    kc = np.arange(GRID_W)[None, :]
    cstart = np.clip(qc - C_WIN_C // 2, 0, GRID_W - C_WIN_C)
    col_ok = jnp.asarray((kc >= cstart) & (kc < cstart + C_WIN_C))
    pad = GRID_W - C_WIN_C
    ext = jnp.pad(rpb.astype(F32), ((0, 0), (0, 0), (pad, pad)), mode='edge')
    toep = jnp.stack([ext[:, :, GRID_W - 1 - q:2 * GRID_W - 1 - q] for q in range(GRID_W)], axis=2)
    toep = jnp.where(col_ok, toep, NEG_INF)
    masked = jnp.full((h, GRID_W, GRID_W), NEG_INF, F32)
    dr = _nbr_row_offsets(rows)
    variants = []
    for v in range(dr.shape[0]):
        qrows = [jnp.concatenate([toep[:, dr[v, a, j]] if dr[v, a, j] >= 0 else masked for j in range(C_KROWS)],
                                 axis=-1) for a in range(C_QROWS)]
        variants.append(jnp.concatenate(qrows, axis=-2).reshape(h * C_QROWS * GRID_W, C_KROWS * GRID_W))
    return jnp.stack(variants)


def _attn_c_kernel(q_ref, *refs):
    nblk = C_KROWS * GRID_W // LANES
    k_refs, v_refs = refs[:nblk], refs[nblk:2 * nblk]
    kz_ref, vz_ref, bias_ref, o_ref = refs[2 * nblk:]
    lo = _lane_lo()
    nq = q_ref.shape[0]
    nb = C_HEADS // 2
    blocks = [slice(b * LANES, (b + 1) * LANES) for b in range(nb)]
    s_parts, sz_parts = [], []
    for bs in blocks:
        ql = jnp.concatenate([_half_select(q_ref[:, bs], lo, half) * (HEAD_DIM ** -0.5) for half in range(2)], axis=0)
        s_parts.append(_dot_nt(ql, jnp.concatenate([kr[:, bs] for kr in k_refs], axis=0)))
        sz_parts.append(_dot_nt(ql, kz_ref[:, bs]))
    s = jnp.concatenate(s_parts, axis=0) + bias_ref[...]
    sz = jnp.concatenate(sz_parts, axis=0)
    m = jnp.maximum(s.max(axis=-1, keepdims=True), sz.max(axis=-1, keepdims=True))
    p, pz = jnp.exp(s - m), jnp.exp(sz - m)
    inv = 1.0 / (p.sum(axis=-1, keepdims=True) + pz.sum(axis=-1, keepdims=True))
    p, pz = p.astype(BF16), pz.astype(BF16)
    for b, bs in enumerate(blocks):
        vw = jnp.concatenate([vr[:, bs] for vr in v_refs], axis=0)
        vz = vz_ref[:, bs]
        out = None
        for half in range(2):
            hs = slice((2 * b + half) * nq, (2 * b + half + 1) * nq)
            o = (jnp.dot(p[hs], _half_select(vw, lo, half), preferred_element_type=F32)
                 + jnp.dot(pz[hs], _half_select(vz, lo, half), preferred_element_type=F32)) * inv[hs]
            out = o if out is None else out + o
        o_ref[:, bs] = out.astype(o_ref.dtype)


def neighborhood_attention(rest, rest_ctx, rpb, q_col, k_col, v_col):
    seq = rest.shape[0]
    lc = rest_ctx.shape[0]
    rows = seq // GRID_W
    nsteps = rows // C_QROWS
    nblk = C_KROWS * GRID_W // LANES
    assert nsteps > nblk and C_QROWS * GRID_W == LANES
    bias = _nbr_bias(rpb, rows)

    def first_blk(i):
        return jnp.clip(i - 2, 0, nsteps - nblk)

    def variant(i):
        return jnp.where(i < 2, i, jnp.where(i >= nsteps - 2, i - (nsteps - 2) + 3, 2))

    w = 4 * LANES
    kspecs = [pl.BlockSpec((LANES, w), functools.partial(lambda i, t: (first_blk(i) + t, k_col), t=t))
              for t in range(nblk)]
    vspecs = [pl.BlockSpec((LANES, w), functools.partial(lambda i, t: (first_blk(i) + t, v_col), t=t))
              for t in range(nblk)]
    return pl.pallas_call(
        _attn_c_kernel,
        grid=(nsteps,),
        in_specs=[pl.BlockSpec((LANES, w), lambda i: (i, q_col))] + kspecs + vspecs + [
            pl.BlockSpec((lc, w), lambda i: (0, k_col)),
            pl.BlockSpec((lc, w), lambda i: (0, v_col)),
            pl.BlockSpec((None, C_HEADS * LANES, nblk * LANES), lambda i: (variant(i), 0, 0))],
        out_specs=pl.BlockSpec((LANES, w), lambda i: (i, 0)),
        out_shape=jax.ShapeDtypeStruct((seq, w), BF16),
        compiler_params=_cparams("arbitrary"),
        name="neighborhood_attention",
    )(rest, *([rest] * (2 * nblk)), rest_ctx, rest_ctx, bias)


def _s5_carry_scan(h_ref, tab_ref, carry_ref, n_tiles, reverse):
    p = h_ref.shape[1] // 2

    def body(step, carry):
        cr, ci = carry
        tile = (n_tiles - 1 - step) if reverse else step
        off = pl.multiple_of(tile * SUBLANES, SUBLANES)
        xr = h_ref[pl.ds(off, SUBLANES), 0:p]
        xi = h_ref[pl.ds(off, SUBLANES), p:2 * p]
        pr, pi = tab_ref[0], tab_ref[1]
        xr, xi = xr + pr * cr - pi * ci, xi + pr * ci + pi * cr
        h_ref[pl.ds(off, SUBLANES), 0:p] = xr
        h_ref[pl.ds(off, SUBLANES), p:2 * p] = xi
        edge = 0 if reverse else SUBLANES - 1
        return (jnp.broadcast_to(xr[edge:edge + 1], xr.shape), jnp.broadcast_to(xi[edge:edge + 1], xi.shape))

    cr, ci = lax.fori_loop(0, n_tiles, body, (carry_ref[0], carry_ref[1]))
    carry_ref[0] = cr
    carry_ref[1] = ci


def _s5_kernel(uz_ref, ul_ref, b_ref, c_ref, tab_ref, d_ref, *rest, reverse, final):
    if final:
        yfz_ref, yfl_ref, gw_ref, gb_ref, oz_ref, ol_ref, h_ref, carry_ref = rest
    else:
        oz_ref, ol_ref, h_ref, carry_ref = rest
    s = pl.program_id(0)

    @pl.when(s == 0)
    def _():
        carry_ref[...] = jnp.zeros_like(carry_ref)

    nblk = b_ref.shape[0]
    ps = b_ref.shape[2] // 2
    p = nblk * ps

    def chunk(u_ref, yf_ref, o_ref):
        t = u_ref.shape[0]
        row = lax.broadcasted_iota(jnp.int32, (t, 1), 0) & (SUBLANES - 1)
        u = u_ref[...]
        uf = u.astype(F32)
        lagged = [u]
        for k in range(1, SUBLANES):
            if reverse:
                lag = jnp.where(row < SUBLANES - k, pltpu.roll(uf, t - k, 0), 0.0)
            else:
                lag = jnp.where(row >= k, pltpu.roll(uf, k, 0), 0.0)
            lagged.append(lag.astype(BF16))
        for q in range(nblk):
            lhs = jnp.concatenate([x[:, q * LANES:(q + 1) * LANES] for x in lagged], axis=1)
            bu = jnp.dot(lhs, b_ref[q], preferred_element_type=F32)
            h_ref[0:t, q * ps:(q + 1) * ps] = bu[:, :ps]
            h_ref[0:t, p + q * ps:p + (q + 1) * ps] = bu[:, ps:]
        _s5_carry_scan(h_ref, tab_ref, carry_ref, t // SUBLANES, reverse)
        outs = []
        for q in range(nblk):
            hq = jnp.concatenate([h_ref[0:t, q * ps:(q + 1) * ps], h_ref[0:t, p + q * ps:p + (q + 1) * ps]], axis=1)
            outs.append(jnp.dot(hq.astype(BF16), c_ref[q], preferred_element_type=F32))
        out = jnp.concatenate(outs, axis=1)
        if not final:
            o_ref[...] = u.astype(F32) * d_ref[...] + out
        else:
            g = _gelu_tanh(yf_ref[...] + out)
            z = jnp.dot(g.astype(BF16), gw_ref[...], preferred_element_type=F32) + gb_ref[...]
            o_ref[...] = (g * _sigmoid(z)).astype(o_ref.dtype)

    @pl.when(s == 0)
    def _():
        chunk(uz_ref, yfz_ref if final else None, oz_ref)

    @pl.when(s > 0)
    def _():
        chunk(ul_ref, yfl_ref if final else None, ol_ref)


def s5_direction(rest, rest_ctx, u_col, bmat, cmat, tabs, dskip, reverse, fwd=None, glu=None, t=512):
    seq = rest.shape[0]
    tz = rest_ctx.shape[0]
    t = min(t, seq)
    nl = seq // t
    w = 4 * LANES
    nq, kq, ps2 = bmat.shape
    p2 = nq * ps2
    final = fwd is not None
    if reverse:
        lat = lambda s: (jnp.where(s == 0, nl - 1, nl - s), 0)
    else:
        lat = lambda s: (jnp.maximum(s - 1, 0), 0)
    lat_u = lambda s: (lat(s)[0], u_col)
    const = lambda s: (0, 0)
    in_specs = [pl.BlockSpec((tz, w), lambda s: (0, u_col)),
                pl.BlockSpec((t, w), lat_u),
                pl.BlockSpec((nq, kq, ps2), lambda s: (0, 0, 0)),
                pl.BlockSpec((nq, ps2, LANES), lambda s: (0, 0, 0)),
                pl.BlockSpec((2, SUBLANES, p2 // 2), lambda s: (0, 0, 0)),
                pl.BlockSpec((1, w), const)]
    args = [rest_ctx, rest, bmat, cmat, tabs, dskip.reshape(1, w)]
    if final:
        in_specs += [pl.BlockSpec((tz, w), const), pl.BlockSpec((t, w), lat),
                     pl.BlockSpec((w, w), const), pl.BlockSpec((1, w), const)]
        args += [fwd[0], fwd[1], glu[0], glu[1].reshape(1, w)]
    odt = BF16 if final else F32
    return pl.pallas_call(
        functools.partial(_s5_kernel, reverse=reverse, final=final),
        grid=(nl + 1,),
        in_specs=in_specs,
        out_specs=[pl.BlockSpec((tz, w), const), pl.BlockSpec((t, w), lat)],
        out_shape=[jax.ShapeDtypeStruct((tz, w), odt), jax.ShapeDtypeStruct((seq, w), odt)],
        scratch_shapes=[pltpu.VMEM((max(t, tz), p2), F32), pltpu.VMEM((2, SUBLANES, p2 // 2), F32)],
        compiler_params=_cparams("arbitrary"),
        name="s5_direction",
    )(*args)


def _s5_params(a_re, a_im, log_step, b_re, b_im, c_re, c_im, reverse):
    ar, ai = a_re.astype(F32), a_im.astype(F32)
    step = jnp.exp(log_step.astype(F32))[:, None]
    mag = jnp.exp(ar * step)
    ab_re, ab_im = mag * jnp.cos(ai * step), mag * jnp.sin(ai * step)
    den = ar * ar + ai * ai
    nr = ab_re - 1.0
    f_re = (nr * ar + ab_im * ai) / den
    f_im = (ab_im * ar - nr * ai) / den
    br, bi = b_re.astype(F32), b_im.astype(F32)
    bb_re = f_re[..., None] * br - f_im[..., None] * bi
    bb_im = f_re[..., None] * bi + f_im[..., None] * br
    g, p, h = bb_re.shape
    gq = LANES // h
    nq = g // gq
    eye = jnp.eye(gq, dtype=F32)
    pr, pi = [jnp.ones_like(ab_re), ab_re], [jnp.zeros_like(ab_im), ab_im]
    for _ in range(SUBLANES - 1):
        pr.append(pr[-1] * ab_re - pi[-1] * ab_im)
        pi.append(pr[-2] * ab_im + pi[-1] * ab_re)
    w_re = jnp.stack([pr[k][..., None] * bb_re - pi[k][..., None] * bb_im for k in range(SUBLANES)])
    w_im = jnp.stack([pr[k][..., None] * bb_im + pi[k][..., None] * bb_re for k in range(SUBLANES)])

    def drive(w):
        w = w.reshape(SUBLANES, nq, gq, p, h)
        return jnp.einsum('kqgph,gj->qkghjp', w, eye).reshape(nq, SUBLANES * gq * h, gq * p)

    def readout(c):
        c = c.astype(F32).reshape(nq, gq, h, p)
        return jnp.einsum('qghp,gj->qgpjh', c, eye).reshape(nq, gq * p, gq * h)

    bmat = jnp.concatenate([drive(w_re), drive(w_im)], axis=2).astype(BF16)
    cmat = jnp.concatenate([readout(c_re), -readout(c_im)], axis=1).astype(BF16)
    order = list(range(SUBLANES, 0, -1)) if reverse else list(range(1, SUBLANES + 1))
    tabs = jnp.stack([jnp.stack([pr[j].reshape(-1) for j in order]), jnp.stack([pi[j].reshape(-1) for j in order])])
    return bmat, cmat, tabs


def _conv_kernel(val_ref, gate_ref, vp_ref, gp_ref, vn_ref, gn_ref, w_ref, b_ref, g_ref, beta_ref, o_ref, hs_ref,
                 *, sub):
    i = pl.program_id(0)
    n = pl.num_programs(0)
    t = val_ref.shape[0]

    def glu(v, g):
        return v[...].astype(F32) * _sigmoid(g[...].astype(F32))

    hs_ref[0:CONV_HALO] = jnp.where(i > 0, glu(vp_ref, gp_ref), 0.0)
    hs_ref[CONV_HALO:CONV_HALO + t] = glu(val_ref, gate_ref)
    hs_ref[CONV_HALO + t:2 * CONV_HALO + t] = jnp.where(i < n - 1, glu(vn_ref, gn_ref), 0.0)
    base = CONV_HALO - CONV_K // 2
    for r0 in range(t // sub):
        acc = jnp.zeros((sub, val_ref.shape[1]), F32) + b_ref[...]
        for r in range(SUBLANES):
            part = None
            for m in range((base + CONV_K - 1) // SUBLANES + 1):
                k = SUBLANES * m + r - base
                if 0 <= k < CONV_K:
                    lo_row = r0 * sub + SUBLANES * m
                    term = hs_ref[lo_row:lo_row + sub + SUBLANES] * w_ref[k:k + 1]
                    part = term if part is None else part + term
            acc = acc + part[r:r + sub]
        mu = jnp.mean(acc, axis=-1, keepdims=True)
        cen = acc - mu
        var = jnp.mean(cen * cen, axis=-1, keepdims=True)
        yn = cen * lax.rsqrt(var + EPS) * g_ref[...] + beta_ref[...]
        o_ref[r0 * sub:(r0 + 1) * sub] = _silu(yn).astype(o_ref.dtype)


def conv_module(rest, val_col, gate_col, w, b, g, beta, t=256, sub=64):
    seq = rest.shape[0]
    t = min(t, seq)
    cw = 4 * LANES
    hb = t // CONV_HALO
    nh = seq // CONV_HALO
    prev = lambda i: jnp.maximum(i * hb - 1, 0)
    nxt = lambda i: jnp.minimum((i + 1) * hb, nh - 1)
    vec = pl.BlockSpec((1, cw), lambda i: (0, 0))
    return pl.pallas_call(
        functools.partial(_conv_kernel, sub=sub),
        grid=(seq // t,),
        in_specs=[pl.BlockSpec((t, cw), lambda i: (i, val_col)),
                  pl.BlockSpec((t, cw), lambda i: (i, gate_col)),
                  pl.BlockSpec((CONV_HALO, cw), lambda i: (prev(i), val_col)),
                  pl.BlockSpec((CONV_HALO, cw), lambda i: (prev(i), gate_col)),
                  pl.BlockSpec((CONV_HALO, cw), lambda i: (nxt(i), val_col)),
                  pl.BlockSpec((CONV_HALO, cw), lambda i: (nxt(i), gate_col)),
                  pl.BlockSpec((CONV_K, cw), lambda i: (0, 0)), vec, vec, vec],
        out_specs=pl.BlockSpec((t, cw), lambda i: (i, 0)),
        out_shape=jax.ShapeDtypeStruct((seq, cw), BF16),
        scratch_shapes=[pltpu.VMEM((t + 2 * CONV_HALO, cw), F32)],
        compiler_params=_cparams("arbitrary"),
        name="conv_module",
    )(rest, rest, rest, rest, rest, rest, w.astype(F32), b.reshape(1, cw).astype(F32),
      g.reshape(1, cw).astype(F32), beta.reshape(1, cw).astype(F32))


def _merge_kernel(ya_ref, yb_ref, yc_ref, yd_ref, g0_ref, g1_ref, g2_ref, g3_ref, w_ref, o_ref):
    acc = None
    for y_ref, gl_ref, k in zip((ya_ref, yb_ref, yc_ref, yd_ref), (g0_ref, g1_ref, g2_ref, g3_ref), range(N_BRANCH)):
        term = _sigmoid(gl_ref[...].astype(F32)) * jnp.dot(y_ref[...], w_ref[k], preferred_element_type=F32)
        acc = term if acc is None else acc + term
    o_ref[...] = acc.astype(o_ref.dtype)


def gated_merge(ys, rest, w_branch, tm=512, tn=512):
    m, yw = ys[0].shape
    n = w_branch.shape[2]
    tm = min(tm, m)
    nj = n // tn
    yspec = pl.BlockSpec((tm, yw), lambda i, j: (i, 0))
    gspecs = [pl.BlockSpec((tm, tn), functools.partial(lambda i, j, k: (i, k * nj + j), k=k)) for k in range(N_BRANCH)]
    return pl.pallas_call(
        _merge_kernel,
        grid=(m // tm, nj),
        in_specs=[yspec] * N_BRANCH + gspecs + [pl.BlockSpec((N_BRANCH, yw, tn), lambda i, j: (0, 0, j))],
        out_specs=pl.BlockSpec((tm, tn), lambda i, j: (i, j)),
        out_shape=jax.ShapeDtypeStruct((m, n), BF16),
        compiler_params=_cparams("arbitrary", "arbitrary"),
        name="gated_merge",
    )(*ys, rest, rest, rest, rest, w_branch)


def _mlp_kernel(x_ref, g_ref, sc_ref, sh_ref, gate_ref, w1_ref, w2_ref, o_ref, h_ref, acc_ref):
    f = pl.program_id(1)

    @pl.when(f == 0)
    def _():
        x = x_ref[...]
        y = x * lax.rsqrt(jnp.mean(x * x, axis=-1, keepdims=True) + EPS) * g_ref[...]
        h_ref[...] = (y * (1.0 + sc_ref[...]) + sh_ref[...]).astype(BF16)
        acc_ref[...] = jnp.zeros_like(acc_ref)

    t = jnp.dot(h_ref[...], w1_ref[...], preferred_element_type=F32)
    t = jnp.square(jnp.maximum(t, 0.0)).astype(BF16)
    acc_ref[...] += jnp.dot(t, w2_ref[...], preferred_element_type=F32)

    @pl.when(f == pl.num_programs(1) - 1)
    def _():
        o_ref[...] = x_ref[...] + gate_ref[...] * acc_ref[...]


def mlp_block(x, g, scale, shift, gate, w1, w2, tm=512, tf=512):
    m, d = x.shape
    ff = w1.shape[1]
    tm = min(tm, m)
    vec = pl.BlockSpec((1, d), lambda i, f: (0, 0))
    return pl.pallas_call(
        _mlp_kernel,
        grid=(m // tm, ff // tf),
        in_specs=[pl.BlockSpec((tm, d), lambda i, f: (i, 0)), vec, vec, vec, vec,
                  pl.BlockSpec((d, tf), lambda i, f: (0, f)),
                  pl.BlockSpec((tf, d), lambda i, f: (f, 0))],
        out_specs=pl.BlockSpec((tm, d), lambda i, f: (i, 0)),
        out_shape=jax.ShapeDtypeStruct((m, d), F32),
        scratch_shapes=[pltpu.VMEM((tm, d), BF16), pltpu.VMEM((tm, d), F32)],
        compiler_params=_cparams("arbitrary", "arbitrary"),
        name="mlp_block",
    )(x, g.reshape(1, d), scale.reshape(1, d), shift.reshape(1, d), gate.reshape(1, d), w1, w2)


_COL_QC, _COL_KC, _COL_VC, _COL_U, _COL_VAL, _COL_GATE = 16, 17, 18, 19, 20, 21
_COL_VV = 44


def _split_w_in(w_in):
    d = w_in.shape[0]
    mw = d // 4
    hd = HEAD_DIM
    o = 0
    qa = w_in[:, o:o + mw]; o += mw
    ka = w_in[:, o:o + A_KV_HEADS * hd]; o += A_KV_HEADS * hd
    va = w_in[:, o:o + A_KV_HEADS * hd]; o += A_KV_HEADS * hd
    mid = w_in[:, o:o + 6 * mw]; o += 6 * mw
    gates = w_in[:, o:]
    u, qc, kc, vc, val, gate = (mid[:, k * mw:(k + 1) * mw] for k in range(6))

    def dup(wk):
        return jnp.concatenate([wk[:, g * hd:(g + 1) * hd] for g in range(A_KV_HEADS) for _ in range(2)], axis=1)

    w_rope = jnp.concatenate([qa, dup(ka)], axis=1).astype(BF16)
    w_rest = jnp.concatenate([gates, qc, kc, vc, u, val, gate, dup(va)], axis=1).astype(BF16)
    return w_rope, w_rest


def _rope_tables(seq):
    t = jnp.arange(seq)
    freqs = ROPE_BASE ** (-jnp.arange(ROPE_PAIRS, dtype=F32) / ROPE_PAIRS)
    lane = np.arange(LANES) % HEAD_DIM
    axis, half, pair = lane // 32, (lane % 32) // 16, lane % 16
    pos = jnp.stack([(t // GRID_W).astype(F32), (t % GRID_W).astype(F32)], axis=1)
    ang = pos[:, axis] * freqs[pair][None, :]
    cos, sin = jnp.cos(ang), jnp.sin(ang)
    sa = jnp.where(jnp.asarray(half == 1)[None], sin, 0.0)
    sb = jnp.where(jnp.asarray(half == 0)[None], -sin, 0.0)
    return cos, sa, sb


def kernel(x, c, ctx, c_ctx, ada_w, ada_b, norm1_g, norm2_g, w_in, a_sink, s5_a_re, s5_a_im, s5_log_step, s5_b_re, s5_b_im, s5_c_re, s5_c_im, s5_d, s5_glu_w, s5_glu_b, c_rpb, d_conv_w, d_conv_b, d_ln_g, d_ln_b, w_branch, w_out, mlp_w1, mlp_w2, final_g):
    depth = ada_w.shape[0]
    _, seq, d = x.shape
    lc = ctx.shape[1]
    xs = x.reshape(seq, d)
    zs = ctx.reshape(lc, d)
    rope = _rope_tables(seq)

    cc = jnp.zeros((SUBLANES, d), F32).at[0].set(c[0]).at[1].set(c_ctx)
    mods = ada_modulation(cc, ada_w, ada_b)

    for l in range(depth):
        need_ctx = l < depth - 1
        mx = [mods[l, 0, k * d:(k + 1) * d] for k in range(6)]
        mz = [mods[l, 1, k * d:(k + 1) * d] for k in range(6)]
        w_rope, w_rest = _split_w_in(w_in[l])
        wb = w_branch[l].astype(BF16)
        wo = w_out[l].astype(BF16)
        w1 = mlp_w1[l].astype(BF16)
        w2 = mlp_w2[l].astype(BF16)

        hx = norm_modulate(xs, norm1_g[l], mx[1], mx[0], BF16)
        hz = norm_modulate(zs, norm1_g[l], mz[1], mz[0], BF16)
        qk_x = matmul(hx, w_rope, 512, 768, rope=rope)
        qk_z = matmul(hz, w_rope, 256, 768)
        r_x = matmul(hx, w_rest, 512, 1280)
        r_z = matmul(hz, w_rest, 256, 1280)

        ya_x = window_attention(a_sink[l], qk_x, r_x, qk_z, r_z, _COL_VV)
        yc_x = neighborhood_attention(r_x, r_z, c_rpb[l], _COL_QC, _COL_KC, _COL_VC)
        yd_x = conv_module(r_x, _COL_VAL, _COL_GATE, d_conv_w[l], d_conv_b[l], d_ln_g[l], d_ln_b[l])

        pf = _s5_params(s5_a_re[l, 0], s5_a_im[l, 0], s5_log_step[l, 0], s5_b_re[l, 0], s5_b_im[l, 0],
                        s5_c_re[l, 0], s5_c_im[l, 0], reverse=False)
        pb = _s5_params(s5_a_re[l, 1], s5_a_im[l, 1], s5_log_step[l, 1], s5_b_re[l, 1], s5_b_im[l, 1],
                        s5_c_re[l, 1], s5_c_im[l, 1], reverse=True)
        yf = s5_direction(r_x, r_z, _COL_U, *pf, s5_d[l].astype(F32), reverse=False)
        yb_z, yb_x = s5_direction(r_x, r_z, _COL_U, *pb, s5_d[l].astype(F32), reverse=True, fwd=yf,
                                  glu=(s5_glu_w[l].astype(BF16), s5_glu_b[l].astype(F32)))

        m_x = gated_merge((ya_x, yb_x, yc_x, yd_x), r_x, wb)
        xs = matmul(m_x, wo, 512, 512, residual=(xs, mx[2]))
        xs = mlp_block(xs, norm2_g[l], mx[4], mx[3], mx[5], w1, w2)

        if need_ctx:
            ya_z = context_attention(a_sink[l], qk_z, 0, qk_z, 2, 2 * LANES, r_z, _COL_VV, True, True)
            yc_z = context_attention(a_sink[l], r_z, _COL_QC, r_z, _COL_KC, 4 * LANES, r_z, _COL_VC, False, False)
            yd_z = conv_module(r_z, _COL_VAL, _COL_GATE, d_conv_w[l], d_conv_b[l], d_ln_g[l], d_ln_b[l])
            m_z = gated_merge((ya_z, yb_z, yc_z, yd_z), r_z, wb)
            zs = matmul(m_z, wo, 256, 512, residual=(zs, mz[2]))
            zs = mlp_block(zs, norm2_g[l], mz[4], mz[3], mz[5], w1, w2)

    zero = jnp.zeros((d,), F32)
    out = norm_modulate(xs, final_g, zero, zero, F32)
    return out.reshape(x.shape)
```

```python
import functools
import math

import numpy as np
import jax
import jax.numpy as jnp
from jax import lax
from jax.experimental import pallas as pl
from jax.experimental.pallas import tpu as pltpu

F32 = jnp.float32
BF16 = jnp.bfloat16

LANES = 128
SUBLANES = 8
VMEM_LIMIT = 56 * 1024 * 1024

GRID_W = 64
HEAD_DIM = 64
N_BRANCH = 4
A_HEADS = 8
A_KV_HEADS = 2
A_WINDOW = 128
A_BLOCK = 128
ROPE_BASE = 10000.0
ROPE_PAIRS = HEAD_DIM // 4
S5_GROUP_CH = 16
S5_STATE = 64
C_HEADS = 8
C_WIN_R = 8
C_WIN_C = 16
C_QROWS = 2
C_KROWS = C_QROWS + C_WIN_R
CONV_K = 31
CONV_HALO = 16
EPS = 1e-6
NEG_INF = -1e30


def _cparams(*sem):
    return pltpu.CompilerParams(dimension_semantics=sem, vmem_limit_bytes=VMEM_LIMIT)


def _sigmoid(x):
    return 0.5 * jnp.tanh(0.5 * x) + 0.5


def _silu(x):
    return x * _sigmoid(x)


def _gelu_tanh(x):
    return 0.5 * x * (1.0 + jnp.tanh(math.sqrt(2.0 / math.pi) * (x + 0.044715 * (x * x * x))))


def _ada_kernel(c_ref, w_ref, b_ref, o_ref):
    a = _silu(c_ref[...]).astype(BF16)
    o_ref[...] = jnp.dot(a, w_ref[...].astype(BF16), preferred_element_type=F32) + b_ref[...]


def ada_modulation(cc, ada_w, ada_b, tn=1024):
    depth, d, n = ada_w.shape
    return pl.pallas_call(
        _ada_kernel,
        grid=(depth, n // tn),
        in_specs=[pl.BlockSpec((SUBLANES, d), lambda l, j: (0, 0)),
                  pl.BlockSpec((None, d, tn), lambda l, j: (l, 0, j)),
                  pl.BlockSpec((None, 1, tn), lambda l, j: (l, 0, j))],
        out_specs=pl.BlockSpec((None, SUBLANES, tn), lambda l, j: (l, 0, j)),
        out_shape=jax.ShapeDtypeStruct((depth, SUBLANES, n), F32),
        compiler_params=_cparams("arbitrary", "arbitrary"),
        name="ada_modulation",
    )(cc, ada_w, ada_b.reshape(depth, 1, n))


def _norm_mod_kernel(x_ref, g_ref, sc_ref, sh_ref, o_ref):
    x = x_ref[...]
    y = x * lax.rsqrt(jnp.mean(x * x, axis=-1, keepdims=True) + EPS) * g_ref[...]
    o_ref[...] = (y * (1.0 + sc_ref[...]) + sh_ref[...]).astype(o_ref.dtype)


def norm_modulate(x, g, scale, shift, out_dtype, tm=256):
    m, d = x.shape
    vec = pl.BlockSpec((1, d), lambda i: (0, 0))
    return pl.pallas_call(
        _norm_mod_kernel,
        grid=(m // tm,),
        in_specs=[pl.BlockSpec((tm, d), lambda i: (i, 0)), vec, vec, vec],
        out_specs=pl.BlockSpec((tm, d), lambda i: (i, 0)),
        out_shape=jax.ShapeDtypeStruct((m, d), out_dtype),
        compiler_params=_cparams("arbitrary"),
        name="norm_modulate",
    )(x, g.reshape(1, d), scale.reshape(1, d), shift.reshape(1, d))


def _rope_block(y, cos, sa, sb):
    return y * cos + pltpu.roll(y, 16, 1) * sa + pltpu.roll(y, LANES - 16, 1) * sb


def _head_proj_kernel(a_ref, w_ref, *rest, n_rope):
    if n_rope:
        cos_ref, sa_ref, sb_ref, o_ref = rest
        cos, sa, sb = cos_ref[...], sa_ref[...], sb_ref[...]
    else:
        o_ref, = rest
    acc = jnp.dot(a_ref[...], w_ref[...], preferred_element_type=F32)
    for b in range(acc.shape[1] // LANES):
        sl = slice(b * LANES, (b + 1) * LANES)
        y = acc[:, sl]
        o_ref[:, sl] = (_rope_block(y, cos, sa, sb) if b < n_rope else y).astype(o_ref.dtype)


def head_projection(a, w, n_rope, rope=None, tm=512):
    m, k = a.shape
    n = w.shape[1]
    tm = min(tm, m)
    in_specs = [pl.BlockSpec((tm, k), lambda i: (i, 0)), pl.BlockSpec((k, n), lambda i: (0, 0))]
    args = [a, w]
    if rope is not None:
        in_specs += [pl.BlockSpec((tm, LANES), lambda i: (i, 0))] * 3
        args += list(rope)
    return pl.pallas_call(
        functools.partial(_head_proj_kernel, n_rope=n_rope if rope is not None else 0),
        grid=(m // tm,),
        in_specs=in_specs,
        out_specs=pl.BlockSpec((tm, n), lambda i: (i, 0)),
        out_shape=jax.ShapeDtypeStruct((m, n), BF16),
        compiler_params=_cparams("arbitrary"),
        name="head_projection",
    )(*args)


def _wide_proj_kernel(a_ref, w_ref, o_ref, wb_ref):
    @pl.when(pl.program_id(1) == 0)
    def _():
        wb_ref[...] = w_ref[...].astype(BF16)

    o_ref[...] = jnp.dot(a_ref[...], wb_ref[...], preferred_element_type=F32).astype(o_ref.dtype)


def wide_projection(a, w, col0, n, tm=1024, tn=1024):
    m, k = a.shape
    tm = min(tm, m)
    assert n % tn == 0 and col0 % LANES == 0 and tn % LANES == 0
    return pl.pallas_call(
        _wide_proj_kernel,
        grid=(n // tn, m // tm),
        in_specs=[pl.BlockSpec((tm, k), lambda j, i: (i, 0)),
                  pl.BlockSpec((pl.Element(k), pl.Element(tn)),
                               lambda j, i: (0, pl.multiple_of(col0 + j * tn, LANES)))],
        out_specs=pl.BlockSpec((tm, tn), lambda j, i: (i, j)),
        out_shape=jax.ShapeDtypeStruct((m, n), BF16),
        scratch_shapes=[pltpu.VMEM((k, tn), BF16)],
        compiler_params=_cparams("arbitrary", "arbitrary"),
        name="wide_projection",
    )(a, w)


def _lane_lo():
    return lax.broadcasted_iota(jnp.int32, (1, LANES), 1) < HEAD_DIM


def _half_select(x, lo, half):
    zero = jnp.zeros_like(x)
    return jnp.where(lo, x, zero) if half == 0 else jnp.where(lo, zero, x)


def _dot_nt(a, b):
    return lax.dot_general(a, b, (((1,), (1,)), ((), ())), preferred_element_type=F32)


def _softmax_pv(parts, values, sink=None):
    m = parts[0].max(axis=-1, keepdims=True)
    for s in parts[1:]:
        m = jnp.maximum(m, s.max(axis=-1, keepdims=True))
    if sink is not None:
        m = jnp.maximum(m, sink)
    den = jnp.exp(sink - m) if sink is not None else 0.0
    out = 0.0
    for s, v in zip(parts, values):
        p = jnp.exp(s - m)
        den = den + p.sum(axis=-1, keepdims=True)
        out = out + jnp.dot(p.astype(BF16), v, preferred_element_type=F32)
    return out / den


def _attn_a_kernel(sink_ref, q_ref, kp_ref, kc_ref, kn_ref, vp_ref, vc_ref, vn_ref, kz_ref, vz_ref, o_ref,
                   *, seq):
    i = pl.program_id(0)
    lo = _lane_lo()
    nq = 4 * A_BLOCK
    r = lax.broadcasted_iota(jnp.int32, (nq, 3 * A_BLOCK), 0) & (A_BLOCK - 1)
    c = lax.broadcasted_iota(jnp.int32, (nq, 3 * A_BLOCK), 1)
    kabs = c + (i - 1) * A_BLOCK
    valid = ((c - r) >= 0) & ((c - r) <= 2 * A_WINDOW) & (kabs >= 0) & (kabs < seq)
    rowblk = lax.broadcasted_iota(jnp.int32, (nq, 1), 0) // A_BLOCK
    for g in range(A_KV_HEADS):
        gs = slice(g * LANES, (g + 1) * LANES)
        kw = jnp.concatenate([kp_ref[:, gs], kc_ref[:, gs], kn_ref[:, gs]], axis=0)
        vw = jnp.concatenate([vp_ref[:, gs], vc_ref[:, gs], vn_ref[:, gs]], axis=0)
        kz, vz = kz_ref[:, gs], vz_ref[:, gs]
        blocks = [slice((2 * g + bb) * LANES, (2 * g + bb + 1) * LANES) for bb in range(2)]
        ql = jnp.concatenate([_half_select(q_ref[:, bs], lo, half) * (HEAD_DIM ** -0.5)
                              for half in range(2) for bs in blocks], axis=0)
        heads = [2 * (2 * g + bb) + half for half in range(2) for bb in range(2)]
        sink = jnp.full((nq, 1), sink_ref[heads[3]], F32)
        for k in range(3):
            sink = jnp.where(rowblk == k, sink_ref[heads[k]], sink)
        s = jnp.where(valid, _dot_nt(ql, kw), NEG_INF)
        sz = _dot_nt(ql, kz)
        m = jnp.maximum(jnp.maximum(s.max(axis=-1, keepdims=True), sz.max(axis=-1, keepdims=True)), sink)
        p, pz = jnp.exp(s - m), jnp.exp(sz - m)
        inv = 1.0 / (p.sum(axis=-1, keepdims=True) + pz.sum(axis=-1, keepdims=True) + jnp.exp(sink - m))
        p, pz = p.astype(BF16), pz.astype(BF16)
        out = None
        for half in range(2):
            hs = slice(half * 2 * A_BLOCK, (half + 1) * 2 * A_BLOCK)
            o = (jnp.dot(p[hs], _half_select(vw, lo, half), preferred_element_type=F32)
                 + jnp.dot(pz[hs], _half_select(vz, lo, half), preferred_element_type=F32)) * inv[hs]
            out = o if out is None else out + o
        for bb in range(2):
            o_ref[:, blocks[bb]] = out[bb * A_BLOCK:(bb + 1) * A_BLOCK].astype(o_ref.dtype)


def window_attention(sink, qk, rest, qk_ctx, rest_ctx, vv_col):
    seq = qk.shape[0]
    lc = qk_ctx.shape[0]
    nb = seq // A_BLOCK
    kw = 2 * LANES
    prev = lambda i: jnp.maximum(i - 1, 0)
    nxt = lambda i: jnp.minimum(i + 1, nb - 1)
    return pl.pallas_call(
        functools.partial(_attn_a_kernel, seq=seq),
        grid=(nb,),
        in_specs=[pl.BlockSpec(memory_space=pltpu.SMEM),
                  pl.BlockSpec((A_BLOCK, 4 * LANES), lambda i: (i, 0)),
                  pl.BlockSpec((A_BLOCK, kw), lambda i: (prev(i), 2)),
                  pl.BlockSpec((A_BLOCK, kw), lambda i: (i, 2)),
                  pl.BlockSpec((A_BLOCK, kw), lambda i: (nxt(i), 2)),
                  pl.BlockSpec((A_BLOCK, kw), lambda i: (prev(i), vv_col)),
                  pl.BlockSpec((A_BLOCK, kw), lambda i: (i, vv_col)),
                  pl.BlockSpec((A_BLOCK, kw), lambda i: (nxt(i), vv_col)),
                  pl.BlockSpec((lc, kw), lambda i: (0, 2)),
                  pl.BlockSpec((lc, kw), lambda i: (0, vv_col))],
        out_specs=pl.BlockSpec((A_BLOCK, 4 * LANES), lambda i: (i, 0)),
        out_shape=jax.ShapeDtypeStruct((seq, 4 * LANES), BF16),
        compiler_params=_cparams("arbitrary"),
        name="window_attention",
    )(sink, qk, qk, qk, qk, rest, rest, rest, qk_ctx, rest_ctx)


def _ctx_attn_kernel(sink_ref, q_ref, k_ref, v_ref, o_ref, *, kv_shared, has_sink):
    lo = _lane_lo()
    for b in range(q_ref.shape[1] // LANES):
        bs = slice(b * LANES, (b + 1) * LANES)
        kb = b // 2 if kv_shared else b
        ks = slice(kb * LANES, (kb + 1) * LANES)
        qb, k, v = q_ref[:, bs], k_ref[:, ks], v_ref[:, ks]
        acc = jnp.zeros((q_ref.shape[0], LANES), F32)
        for half in range(2):
            ql = _half_select(qb, lo, half) * (HEAD_DIM ** -0.5)
            sink = sink_ref[2 * b + half] if has_sink else None
            acc = acc + _softmax_pv([_dot_nt(ql, k)], [_half_select(v, lo, half)], sink=sink)
        o_ref[:, bs] = acc.astype(o_ref.dtype)


def context_attention(sink, q_arr, q_col, k_arr, k_col, k_w, v_arr, v_col, kv_shared, has_sink):
    lc = q_arr.shape[0]
    return pl.pallas_call(
        functools.partial(_ctx_attn_kernel, kv_shared=kv_shared, has_sink=has_sink),
        grid=(1,),
        in_specs=[pl.BlockSpec(memory_space=pltpu.SMEM),
                  pl.BlockSpec((lc, 4 * LANES), lambda i: (0, q_col)),
                  pl.BlockSpec((lc, k_w), lambda i: (0, k_col)),
                  pl.BlockSpec((lc, k_w), lambda i: (0, v_col))],
        out_specs=pl.BlockSpec((lc, 4 * LANES), lambda i: (0, 0)),
        out_shape=jax.ShapeDtypeStruct((lc, 4 * LANES), BF16),
        compiler_params=_cparams("arbitrary"),
        name="context_attention",
    )(sink, q_arr, k_arr, v_arr)


def _nbr_row_offsets(rows):
    nsteps = rows // C_QROWS
    nblk = C_KROWS * GRID_W // LANES
    wr = min(C_WIN_R, rows)
    reps = [0, 1, 2, nsteps - 2, nsteps - 1]
    dr = -np.ones((len(reps), C_QROWS, C_KROWS), np.int32)
    for v, i in enumerate(reps):
        w0 = C_QROWS * int(np.clip(i - 2, 0, nsteps - nblk))
        for a in range(C_QROWS):
            rq = C_QROWS * i + a
            k0 = int(np.clip(rq - C_WIN_R // 2, 0, rows - wr))
            for j in range(C_KROWS):
                if k0 <= w0 + j < k0 + wr:
                    dr[v, a, j] = w0 + j - rq + C_WIN_R - 1
    return dr


def _nbr_bias(rpb, rows):
    h = rpb.shape[0]
    qc = np.arange(GRID_W)[:, None]
    kc = np.arange(GRID_W)[None, :]
    cstart = np.clip(qc - C_WIN_C // 2, 0, GRID_W - C_WIN_C)
    col_ok = jnp.asarray((kc >= cstart) & (kc < cstart + C_WIN_C))
    pad = GRID_W - C_WIN_C
    ext = jnp.pad(rpb.astype(F32), ((0, 0), (0, 0), (pad, pad)), mode='edge')
    toep = jnp.stack([ext[:, :, GRID_W - 1 - q:2 * GRID_W - 1 - q] for q in range(GRID_W)], axis=2)
    toep = jnp.where(col_ok, toep, NEG_INF)
    masked = jnp.full((h, GRID_W, GRID_W), NEG_INF, F32)
    dr = _nbr_row_offsets(rows)
    variants = []
    for v in range(dr.shape[0]):
        qrows = [jnp.concatenate([toep[:, dr[v, a, j]] if dr[v, a, j] >= 0 else masked for j in range(C_KROWS)],
                                 axis=-1) for a in range(C_QROWS)]
        variants.append(jnp.concatenate(qrows, axis=-2).reshape(h * C_QROWS * GRID_W, C_KROWS * GRID_W))
    return jnp.stack(variants)


def _attn_c_kernel(q_ref, *refs):
    nblk = C_KROWS * GRID_W // LANES
    k_refs, v_refs = refs[:nblk], refs[nblk:2 * nblk]
    kz_ref, vz_ref, bias_ref, o_ref = refs[2 * nblk:]
    lo = _lane_lo()
    nq = q_ref.shape[0]
    nb = C_HEADS // 2
    blocks = [slice(b * LANES, (b + 1) * LANES) for b in range(nb)]
    s_parts, sz_parts = [], []
    for bs in blocks:
        ql = jnp.concatenate([_half_select(q_ref[:, bs], lo, half) * (HEAD_DIM ** -0.5) for half in range(2)], axis=0)
        s_parts.append(_dot_nt(ql, jnp.concatenate([kr[:, bs] for kr in k_refs], axis=0)))
        sz_parts.append(_dot_nt(ql, kz_ref[:, bs]))
    s = jnp.concatenate(s_parts, axis=0) + bias_ref[...]
    sz = jnp.concatenate(sz_parts, axis=0)
    m = jnp.maximum(s.max(axis=-1, keepdims=True), sz.max(axis=-1, keepdims=True))
    p, pz = jnp.exp(s - m), jnp.exp(sz - m)
    inv = 1.0 / (p.sum(axis=-1, keepdims=True) + pz.sum(axis=-1, keepdims=True))
    p, pz = p.astype(BF16), pz.astype(BF16)
    for b, bs in enumerate(blocks):
        vw = jnp.concatenate([vr[:, bs] for vr in v_refs], axis=0)
        vz = vz_ref[:, bs]
        out = None
        for half in range(2):
            hs = slice((2 * b + half) * nq, (2 * b + half + 1) * nq)
            o = (jnp.dot(p[hs], _half_select(vw, lo, half), preferred_element_type=F32)
                 + jnp.dot(pz[hs], _half_select(vz, lo, half), preferred_element_type=F32)) * inv[hs]
            out = o if out is None else out + o
        o_ref[:, bs] = out.astype(o_ref.dtype)


def neighborhood_attention(rest, rest_ctx, rpb, q_col, k_col, v_col):
    seq = rest.shape[0]
    lc = rest_ctx.shape[0]
    rows = seq // GRID_W
    nsteps = rows // C_QROWS
    nblk = C_KROWS * GRID_W // LANES
    assert nsteps > nblk and C_QROWS * GRID_W == LANES
    bias = _nbr_bias(rpb, rows)

    def first_blk(i):
        return jnp.clip(i - 2, 0, nsteps - nblk)

    def variant(i):
        return jnp.where(i < 2, i, jnp.where(i >= nsteps - 2, i - (nsteps - 2) + 3, 2))

    w = 4 * LANES
    kspecs = [pl.BlockSpec((LANES, w), functools.partial(lambda i, t: (first_blk(i) + t, k_col), t=t))
              for t in range(nblk)]
    vspecs = [pl.BlockSpec((LANES, w), functools.partial(lambda i, t: (first_blk(i) + t, v_col), t=t))
              for t in range(nblk)]
    return pl.pallas_call(
        _attn_c_kernel,
        grid=(nsteps,),
        in_specs=[pl.BlockSpec((LANES, w), lambda i: (i, q_col))] + kspecs + vspecs + [
            pl.BlockSpec((lc, w), lambda i: (0, k_col)),
            pl.BlockSpec((lc, w), lambda i: (0, v_col)),
            pl.BlockSpec((None, C_HEADS * LANES, nblk * LANES), lambda i: (variant(i), 0, 0))],
        out_specs=pl.BlockSpec((LANES, w), lambda i: (i, 0)),
        out_shape=jax.ShapeDtypeStruct((seq, w), BF16),
        compiler_params=_cparams("arbitrary"),
        name="neighborhood_attention",
    )(rest, *([rest] * (2 * nblk)), rest_ctx, rest_ctx, bias)


def _s5_carry_scan(bu, pr, pi, cr, ci, reverse):
    p = bu.shape[1] // 2
    n_tiles = bu.shape[0] // SUBLANES
    edge = 0 if reverse else SUBLANES - 1
    rows_r, rows_i = [None] * n_tiles, [None] * n_tiles
    for step in range(n_tiles):
        tile = (n_tiles - 1 - step) if reverse else step
        xr = bu[tile * SUBLANES:(tile + 1) * SUBLANES, 0:p]
        xi = bu[tile * SUBLANES:(tile + 1) * SUBLANES, p:2 * p]
        xr, xi = xr + pr * cr - pi * ci, xi + pr * ci + pi * cr
        cr, ci = jnp.broadcast_to(xr[edge:edge + 1], xr.shape), jnp.broadcast_to(xi[edge:edge + 1], xi.shape)
        rows_r[tile], rows_i[tile] = xr, xi
    h = jnp.concatenate([jnp.concatenate(rows_r, axis=0), jnp.concatenate(rows_i, axis=0)], axis=1)
    return h, cr, ci


def _s5_kernel(uz_ref, ul_ref, b_ref, c_ref, tab_ref, d_ref, *rest, reverse, final):
    if final:
        yfz_ref, yfl_ref, gw_ref, gb_ref, oz_ref, ol_ref, carry_ref = rest
    else:
        oz_ref, ol_ref, carry_ref = rest
    s = pl.program_id(0)

    @pl.when(s == 0)
    def _():
        carry_ref[...] = jnp.zeros_like(carry_ref)

    nblk = b_ref.shape[0]
    ps = b_ref.shape[2] // 2
    p = nblk * ps

    def chunk(u_ref, yf_ref, o_ref):
        t = u_ref.shape[0]
        row = lax.broadcasted_iota(jnp.int32, (t, 1), 0) & (SUBLANES - 1)
        u = u_ref[...]
        uf = u.astype(F32)
        lagged = [u]
        for k in range(1, SUBLANES):
            if reverse:
                lag = jnp.where(row < SUBLANES - k, pltpu.roll(uf, t - k, 0), 0.0)
            else:
                lag = jnp.where(row >= k, pltpu.roll(uf, k, 0), 0.0)
            lagged.append(lag.astype(BF16))
        outs = []
        for q in range(nblk):
            qs = slice(q * ps, (q + 1) * ps)
            lhs = jnp.concatenate([x[:, q * LANES:(q + 1) * LANES] for x in lagged], axis=1)
            bu = jnp.dot(lhs, b_ref[q], preferred_element_type=F32)
            hq, cr, ci = _s5_carry_scan(bu, tab_ref[0, :, qs], tab_ref[1, :, qs], carry_ref[0, :, qs],
                                        carry_ref[1, :, qs], reverse)
            carry_ref[0, :, qs] = cr
            carry_ref[1, :, qs] = ci
            outs.append(jnp.dot(hq.astype(BF16), c_ref[q], preferred_element_type=F32))
        out = jnp.concatenate(outs, axis=1)
        if not final:
            o_ref[...] = u.astype(F32) * d_ref[...] + out
        else:
            g = _gelu_tanh(yf_ref[...] + out)
            z = jnp.dot(g.astype(BF16), gw_ref[...], preferred_element_type=F32) + gb_ref[...]
            o_ref[...] = (g * _sigmoid(z)).astype(o_ref.dtype)

    @pl.when(s == 0)
    def _():
        chunk(uz_ref, yfz_ref if final else None, oz_ref)

    @pl.when(s > 0)
    def _():
        chunk(ul_ref, yfl_ref if final else None, ol_ref)


def s5_direction(rest, rest_ctx, u_col, bmat, cmat, tabs, dskip, reverse, fwd=None, glu=None, t=512):
    seq = rest.shape[0]
    tz = rest_ctx.shape[0]
    t = min(t, seq)
    nl = seq // t
    w = 4 * LANES
    nq, kq, ps2 = bmat.shape
    p2 = nq * ps2
    final = fwd is not None
    if reverse:
        lat = lambda s: (jnp.where(s == 0, nl - 1, nl - s), 0)
    else:
        lat = lambda s: (jnp.maximum(s - 1, 0), 0)
    lat_u = lambda s: (lat(s)[0], u_col)
    const = lambda s: (0, 0)
    in_specs = [pl.BlockSpec((tz, w), lambda s: (0, u_col)),
                pl.BlockSpec((t, w), lat_u),
                pl.BlockSpec((nq, kq, ps2), lambda s: (0, 0, 0)),
                pl.BlockSpec((nq, ps2, LANES), lambda s: (0, 0, 0)),
                pl.BlockSpec((2, SUBLANES, p2 // 2), lambda s: (0, 0, 0)),
                pl.BlockSpec((1, w), const)]
    args = [rest_ctx, rest, bmat, cmat, tabs, dskip.reshape(1, w)]
    if final:
        in_specs += [pl.BlockSpec((tz, w), const), pl.BlockSpec((t, w), lat),
                     pl.BlockSpec((w, w), const), pl.BlockSpec((1, w), const)]
        args += [fwd[0], fwd[1], glu[0], glu[1].reshape(1, w)]
    odt = BF16 if final else F32
    return pl.pallas_call(
        functools.partial(_s5_kernel, reverse=reverse, final=final),
        grid=(nl + 1,),
        in_specs=in_specs,
        out_specs=[pl.BlockSpec((tz, w), const), pl.BlockSpec((t, w), lat)],
        out_shape=[jax.ShapeDtypeStruct((tz, w), odt), jax.ShapeDtypeStruct((seq, w), odt)],
        scratch_shapes=[pltpu.VMEM((2, SUBLANES, p2 // 2), F32)],
        compiler_params=_cparams("arbitrary"),
        name="s5_direction",
    )(*args)


def _s5_params(a_re, a_im, log_step, b_re, b_im, c_re, c_im, reverse):
    ar, ai = a_re.astype(F32), a_im.astype(F32)
    step = jnp.exp(log_step.astype(F32))[:, None]
    mag = jnp.exp(ar * step)
    ab_re, ab_im = mag * jnp.cos(ai * step), mag * jnp.sin(ai * step)
    den = ar * ar + ai * ai
    nr = ab_re - 1.0
    f_re = (nr * ar + ab_im * ai) / den
    f_im = (ab_im * ar - nr * ai) / den
    br, bi = b_re.astype(F32), b_im.astype(F32)
    bb_re = f_re[..., None] * br - f_im[..., None] * bi
    bb_im = f_re[..., None] * bi + f_im[..., None] * br
    g, p, h = bb_re.shape
    gq = LANES // h
    nq = g // gq
    eye = jnp.eye(gq, dtype=F32)
    pr, pi = [jnp.ones_like(ab_re), ab_re], [jnp.zeros_like(ab_im), ab_im]
    for _ in range(SUBLANES - 1):
        pr.append(pr[-1] * ab_re - pi[-1] * ab_im)
        pi.append(pr[-2] * ab_im + pi[-1] * ab_re)
    w_re = jnp.stack([pr[k][..., None] * bb_re - pi[k][..., None] * bb_im for k in range(SUBLANES)])
    w_im = jnp.stack([pr[k][..., None] * bb_im + pi[k][..., None] * bb_re for k in range(SUBLANES)])

    def drive(w):
        w = w.reshape(SUBLANES, nq, gq, p, h)
        return jnp.einsum('kqgph,gj->qkghjp', w, eye).reshape(nq, SUBLANES * gq * h, gq * p)

    def readout(c):
        c = c.astype(F32).reshape(nq, gq, h, p)
        return jnp.einsum('qghp,gj->qgpjh', c, eye).reshape(nq, gq * p, gq * h)

    bmat = jnp.concatenate([drive(w_re), drive(w_im)], axis=2).astype(BF16)
    cmat = jnp.concatenate([readout(c_re), -readout(c_im)], axis=1).astype(BF16)
    order = list(range(SUBLANES, 0, -1)) if reverse else list(range(1, SUBLANES + 1))
    tabs = jnp.stack([jnp.stack([pr[j].reshape(-1) for j in order]), jnp.stack([pi[j].reshape(-1) for j in order])])
    return bmat, cmat, tabs


def _conv_kernel(val_ref, gate_ref, vp_ref, gp_ref, vn_ref, gn_ref, w_ref, b_ref, g_ref, beta_ref, o_ref, hs_ref,
                 *, sub):
    i = pl.program_id(0)
    n = pl.num_programs(0)
    t = val_ref.shape[0]

    def glu(v, g):
        return v[...].astype(F32) * _sigmoid(g[...].astype(F32))

    hs_ref[0:CONV_HALO] = jnp.where(i > 0, glu(vp_ref, gp_ref), 0.0)
    hs_ref[CONV_HALO:CONV_HALO + t] = glu(val_ref, gate_ref)
    hs_ref[CONV_HALO + t:2 * CONV_HALO + t] = jnp.where(i < n - 1, glu(vn_ref, gn_ref), 0.0)
    base = CONV_HALO - CONV_K // 2
    for r0 in range(t // sub):
        acc = jnp.zeros((sub, val_ref.shape[1]), F32) + b_ref[...]
        for r in range(SUBLANES):
            part = None
            for m in range((base + CONV_K - 1) // SUBLANES + 1):
                k = SUBLANES * m + r - base
                if 0 <= k < CONV_K:
                    lo_row = r0 * sub + SUBLANES * m
                    term = hs_ref[lo_row:lo_row + sub + SUBLANES] * w_ref[k:k + 1]
                    part = term if part is None else part + term
            acc = acc + part[r:r + sub]
        mu = jnp.mean(acc, axis=-1, keepdims=True)
        cen = acc - mu
        var = jnp.mean(cen * cen, axis=-1, keepdims=True)
        yn = cen * lax.rsqrt(var + EPS) * g_ref[...] + beta_ref[...]
        o_ref[r0 * sub:(r0 + 1) * sub] = _silu(yn).astype(o_ref.dtype)


def conv_module(rest, val_col, gate_col, w, b, g, beta, t=256, sub=64):
    seq = rest.shape[0]
    t = min(t, seq)
    cw = 4 * LANES
    hb = t // CONV_HALO
    nh = seq // CONV_HALO
    prev = lambda i: jnp.maximum(i * hb - 1, 0)
    nxt = lambda i: jnp.minimum((i + 1) * hb, nh - 1)
    vec = pl.BlockSpec((1, cw), lambda i: (0, 0))
    return pl.pallas_call(
        functools.partial(_conv_kernel, sub=sub),
        grid=(seq // t,),
        in_specs=[pl.BlockSpec((t, cw), lambda i: (i, val_col)),
                  pl.BlockSpec((t, cw), lambda i: (i, gate_col)),
                  pl.BlockSpec((CONV_HALO, cw), lambda i: (prev(i), val_col)),
                  pl.BlockSpec((CONV_HALO, cw), lambda i: (prev(i), gate_col)),
                  pl.BlockSpec((CONV_HALO, cw), lambda i: (nxt(i), val_col)),
                  pl.BlockSpec((CONV_HALO, cw), lambda i: (nxt(i), gate_col)),
                  pl.BlockSpec((CONV_K, cw), lambda i: (0, 0)), vec, vec, vec],
        out_specs=pl.BlockSpec((t, cw), lambda i: (i, 0)),
        out_shape=jax.ShapeDtypeStruct((seq, cw), BF16),
        scratch_shapes=[pltpu.VMEM((t + 2 * CONV_HALO, cw), F32)],
        compiler_params=_cparams("arbitrary"),
        name="conv_module",
    )(rest, rest, rest, rest, rest, rest, w.astype(F32), b.reshape(1, cw).astype(F32),
      g.reshape(1, cw).astype(F32), beta.reshape(1, cw).astype(F32))


def _merge_kernel(ya_ref, yb_ref, yc_ref, yd_ref, g0_ref, g1_ref, g2_ref, g3_ref, wb_ref, wo_ref, x_ref, gate_ref,
                  o_ref, acc_ref):
    j = pl.program_id(1)
    m = None
    for y_ref, gl_ref, k in zip((ya_ref, yb_ref, yc_ref, yd_ref), (g0_ref, g1_ref, g2_ref, g3_ref), range(N_BRANCH)):
        term = _sigmoid(gl_ref[...].astype(F32)) * jnp.dot(y_ref[...], wb_ref[k], preferred_element_type=F32)
        m = term if m is None else m + term
    part = jnp.dot(m.astype(BF16), wo_ref[...], preferred_element_type=F32)

    @pl.when(j == 0)
    def _():
        acc_ref[...] = part

    @pl.when(j > 0)
    def _():
        acc_ref[...] += part

    @pl.when(j == pl.num_programs(1) - 1)
    def _():
        o_ref[...] = x_ref[...] + gate_ref[...] * acc_ref[...]


def gated_merge_residual(ys, rest, logit_col, w_branch, w_out, x, gate, tm=512, tn=512):
    m, yw = ys[0].shape
    n = w_branch.shape[2]
    tm = min(tm, m)
    nj = n // tn
    yspec = pl.BlockSpec((tm, yw), lambda i, j: (i, 0))
    gspecs = [pl.BlockSpec((tm, tn), functools.partial(lambda i, j, k: (i, logit_col + k * nj + j), k=k))
              for k in range(N_BRANCH)]
    return pl.pallas_call(
        _merge_kernel,
        grid=(m // tm, nj),
        in_specs=[yspec] * N_BRANCH + gspecs + [
            pl.BlockSpec((N_BRANCH, yw, tn), lambda i, j: (0, 0, j)),
            pl.BlockSpec((tn, n), lambda i, j: (j, 0)),
            pl.BlockSpec((tm, n), lambda i, j: (i, 0)),
            pl.BlockSpec((1, n), lambda i, j: (0, 0))],
        out_specs=pl.BlockSpec((tm, n), lambda i, j: (i, 0)),
        out_shape=jax.ShapeDtypeStruct((m, n), F32),
        scratch_shapes=[pltpu.VMEM((tm, n), F32)],
        compiler_params=_cparams("arbitrary", "arbitrary"),
        name="gated_merge_residual",
    )(*ys, rest, rest, rest, rest, w_branch, w_out, x, gate.reshape(1, n))


def _mlp_kernel(x_ref, g_ref, sc_ref, sh_ref, gate_ref, w1_ref, w2_ref, *rest, final_norm):
    if final_norm:
        gf_ref, o_ref, h_ref, acc_ref = rest
    else:
        o_ref, h_ref, acc_ref = rest
    f = pl.program_id(1)

    @pl.when(f == 0)
    def _():
        x = x_ref[...]
        y = x * lax.rsqrt(jnp.mean(x * x, axis=-1, keepdims=True) + EPS) * g_ref[...]
        h_ref[...] = (y * (1.0 + sc_ref[...]) + sh_ref[...]).astype(BF16)
        acc_ref[...] = jnp.zeros_like(acc_ref)

    t = jnp.dot(h_ref[...], w1_ref[...], preferred_element_type=F32)
    t = jnp.square(jnp.maximum(t, 0.0)).astype(BF16)
    acc_ref[...] += jnp.dot(t, w2_ref[...], preferred_element_type=F32)

    @pl.when(f == pl.num_programs(1) - 1)
    def _():
        y = x_ref[...] + gate_ref[...] * acc_ref[...]
        if final_norm:
            y = y * lax.rsqrt(jnp.mean(y * y, axis=-1, keepdims=True) + EPS) * gf_ref[...]
        o_ref[...] = y


def mlp_block(x, g, scale, shift, gate, w1, w2, final_g=None, tm=512, tf=512):
    m, d = x.shape
    ff = w1.shape[1]
    tm = min(tm, m)
    vec = pl.BlockSpec((1, d), lambda i, f: (0, 0))
    final_norm = final_g is not None
    args = [x, g.reshape(1, d), scale.reshape(1, d), shift.reshape(1, d), gate.reshape(1, d), w1, w2]
    in_specs = [pl.BlockSpec((tm, d), lambda i, f: (i, 0)), vec, vec, vec, vec,
                pl.BlockSpec((d, tf), lambda i, f: (0, f)),
                pl.BlockSpec((tf, d), lambda i, f: (f, 0))]
    if final_norm:
        args.append(final_g.reshape(1, d))
        in_specs.append(vec)
    return pl.pallas_call(
        functools.partial(_mlp_kernel, final_norm=final_norm),
        grid=(m // tm, ff // tf),
        in_specs=in_specs,
        out_specs=pl.BlockSpec((tm, d), lambda i, f: (i, 0)),
        out_shape=jax.ShapeDtypeStruct((m, d), F32),
        scratch_shapes=[pltpu.VMEM((tm, d), BF16), pltpu.VMEM((tm, d), F32)],
        compiler_params=_cparams("arbitrary", "arbitrary"),
        name="mlp_block",
    )(*args)


_A_WIDTH = A_HEADS * HEAD_DIM + 4 * A_KV_HEADS * HEAD_DIM
_A_ROPE_BLOCKS = (A_HEADS * HEAD_DIM + 2 * A_KV_HEADS * HEAD_DIM) // LANES
_A_COL_K, _A_COL_V = 2, 3
_REST_COL0 = (A_HEADS + 2 * A_KV_HEADS) * HEAD_DIM
_COL_U, _COL_QC, _COL_KC, _COL_VC, _COL_VAL, _COL_GATE, _COL_MERGE = 0, 1, 2, 3, 4, 5, 6


def _mixer_a_weights(w_in):
    hd = HEAD_DIM
    nq, nkv = A_HEADS * hd, A_KV_HEADS * hd
    qa, ka, va = w_in[:, :nq], w_in[:, nq:nq + nkv], w_in[:, nq + nkv:nq + 2 * nkv]

    def dup(wk):
        return jnp.concatenate([wk[:, g * hd:(g + 1) * hd] for g in range(A_KV_HEADS) for _ in range(2)], axis=1)

    return jnp.concatenate([qa, dup(ka), dup(va)], axis=1).astype(BF16)


def _rope_tables(seq):
    freqs = ROPE_BASE ** (-jnp.arange(ROPE_PAIRS, dtype=F32) / ROPE_PAIRS)
    rows = seq // GRID_W
    ang_r = jnp.arange(rows).astype(F32)[:, None] * freqs
    ang_c = jnp.arange(GRID_W).astype(F32)[:, None] * freqs

    def expand(tab_r, tab_c, first, second):
        r = jnp.repeat(tab_r, GRID_W, axis=0)
        cc = jnp.tile(tab_c, (rows, 1))
        head = jnp.concatenate([first * r, second * r, first * cc, second * cc], axis=1)
        return jnp.tile(head, (1, LANES // HEAD_DIM))

    cos = expand(jnp.cos(ang_r), jnp.cos(ang_c), 1.0, 1.0)
    sa = expand(jnp.sin(ang_r), jnp.sin(ang_c), 0.0, 1.0)
    sb = expand(jnp.sin(ang_r), jnp.sin(ang_c), -1.0, 0.0)
    return cos, sa, sb


def kernel(x, c, ctx, c_ctx, ada_w, ada_b, norm1_g, norm2_g, w_in, a_sink, s5_a_re, s5_a_im, s5_log_step, s5_b_re, s5_b_im, s5_c_re, s5_c_im, s5_d, s5_glu_w, s5_glu_b, c_rpb, d_conv_w, d_conv_b, d_ln_g, d_ln_b, w_branch, w_out, mlp_w1, mlp_w2, final_g):
    depth = ada_w.shape[0]
    _, seq, d = x.shape
    lc = ctx.shape[1]
    xs = x.reshape(seq, d)
    zs = ctx.reshape(lc, d)
    rope = _rope_tables(seq)

    cc = jnp.zeros((SUBLANES, d), F32).at[0].set(c[0]).at[1].set(c_ctx)
    mods = ada_modulation(cc, ada_w, ada_b)

    for l in range(depth):
        need_ctx = l < depth - 1
        mx = [mods[l, 0, k * d:(k + 1) * d] for k in range(6)]
        mz = [mods[l, 1, k * d:(k + 1) * d] for k in range(6)]
        w_a = _mixer_a_weights(w_in[l])
        n_rest = w_in.shape[2] - _REST_COL0
        wb = w_branch[l].astype(BF16)
        wo = w_out[l].astype(BF16)
        w1 = mlp_w1[l].astype(BF16)
        w2 = mlp_w2[l].astype(BF16)

        hx = norm_modulate(xs, norm1_g[l], mx[1], mx[0], BF16)
        hz = norm_modulate(zs, norm1_g[l], mz[1], mz[0], BF16)
        qk_x = head_projection(hx, w_a, _A_ROPE_BLOCKS, rope=rope)
        qk_z = head_projection(hz, w_a, _A_ROPE_BLOCKS)
        r_x = wide_projection(hx, w_in[l], _REST_COL0, n_rest)
        r_z = wide_projection(hz, w_in[l], _REST_COL0, n_rest)

        ya_x = window_attention(a_sink[l], qk_x, qk_x, qk_z, qk_z, _A_COL_V)
        yc_x = neighborhood_attention(r_x, r_z, c_rpb[l], _COL_QC, _COL_KC, _COL_VC)
        yd_x = conv_module(r_x, _COL_VAL, _COL_GATE, d_conv_w[l], d_conv_b[l], d_ln_g[l], d_ln_b[l])

        pf = _s5_params(s5_a_re[l, 0], s5_a_im[l, 0], s5_log_step[l, 0], s5_b_re[l, 0], s5_b_im[l, 0],
                        s5_c_re[l, 0], s5_c_im[l, 0], reverse=False)
        pb = _s5_params(s5_a_re[l, 1], s5_a_im[l, 1], s5_log_step[l, 1], s5_b_re[l, 1], s5_b_im[l, 1],
                        s5_c_re[l, 1], s5_c_im[l, 1], reverse=True)
        yf = s5_direction(r_x, r_z, _COL_U, *pf, s5_d[l].astype(F32), reverse=False)
        yb_z, yb_x = s5_direction(r_x, r_z, _COL_U, *pb, s5_d[l].astype(F32), reverse=True, fwd=yf,
                                  glu=(s5_glu_w[l].astype(BF16), s5_glu_b[l].astype(F32)))

        xs = gated_merge_residual((ya_x, yb_x, yc_x, yd_x), r_x, _COL_MERGE, wb, wo, xs, mx[2])
        xs = mlp_block(xs, norm2_g[l], mx[4], mx[3], mx[5], w1, w2, final_g=None if need_ctx else final_g)

        if need_ctx:
            ya_z = context_attention(a_sink[l], qk_z, 0, qk_z, _A_COL_K, 2 * LANES, qk_z, _A_COL_V, True, True)
            yc_z = context_attention(a_sink[l], r_z, _COL_QC, r_z, _COL_KC, 4 * LANES, r_z, _COL_VC, False, False)
            yd_z = conv_module(r_z, _COL_VAL, _COL_GATE, d_conv_w[l], d_conv_b[l], d_ln_g[l], d_ln_b[l])
            zs = gated_merge_residual((ya_z, yb_z, yc_z, yd_z), r_z, _COL_MERGE, wb, wo, zs, mz[2])
            zs = mlp_block(zs, norm2_g[l], mz[4], mz[3], mz[5], w1, w2)

    return xs.reshape(x.shape)
```

```python
import functools
import math

import numpy as np
import jax
import jax.numpy as jnp
from jax import lax
from jax.experimental import pallas as pl
from jax.experimental.pallas import tpu as pltpu

F32 = jnp.float32
BF16 = jnp.bfloat16

LANES = 128
SUBLANES = 8
VMEM_LIMIT = 56 * 1024 * 1024

GRID_W = 64
HEAD_DIM = 64
N_BRANCH = 4
A_HEADS = 8
A_KV_HEADS = 2
A_WINDOW = 128
A_BLOCK = 128
ROPE_BASE = 10000.0
ROPE_PAIRS = HEAD_DIM // 4
S5_GROUP_CH = 16
S5_STATE = 64
C_HEADS = 8
C_WIN_R = 8
C_WIN_C = 16
C_QROWS = 2
C_KROWS = C_QROWS + C_WIN_R
CONV_K = 31
CONV_HALO = 16
EPS = 1e-6
NEG_INF = -1e30


def _cparams(*sem):
    return pltpu.CompilerParams(dimension_semantics=sem, vmem_limit_bytes=VMEM_LIMIT)


def _sigmoid(x):
    return 0.5 * jnp.tanh(0.5 * x) + 0.5


def _silu(x):
    return x * _sigmoid(x)


def _gelu_tanh(x):
    return 0.5 * x * (1.0 + jnp.tanh(math.sqrt(2.0 / math.pi) * (x + 0.044715 * (x * x * x))))


def _ada_kernel(c_ref, w_ref, b_ref, o_ref):
    a = _silu(c_ref[...]).astype(BF16)
    o_ref[...] = jnp.dot(a, w_ref[...].astype(BF16), preferred_element_type=F32) + b_ref[...]


def ada_modulation(cc, ada_w, ada_b, tn=1024):
    depth, d, n = ada_w.shape
    return pl.pallas_call(
        _ada_kernel,
        grid=(depth, n // tn),
        in_specs=[pl.BlockSpec((SUBLANES, d), lambda l, j: (0, 0)),
                  pl.BlockSpec((None, d, tn), lambda l, j: (l, 0, j)),
                  pl.BlockSpec((None, 1, tn), lambda l, j: (l, 0, j))],
        out_specs=pl.BlockSpec((None, SUBLANES, tn), lambda l, j: (l, 0, j)),
        out_shape=jax.ShapeDtypeStruct((depth, SUBLANES, n), F32),
        compiler_params=_cparams("arbitrary", "arbitrary"),
        name="ada_modulation",
    )(cc, ada_w, ada_b.reshape(depth, 1, n))


def _norm_mod_kernel(x_ref, g_ref, sc_ref, sh_ref, o_ref):
    x = x_ref[...]
    y = x * lax.rsqrt(jnp.mean(x * x, axis=-1, keepdims=True) + EPS) * g_ref[...]
    o_ref[...] = (y * (1.0 + sc_ref[...]) + sh_ref[...]).astype(o_ref.dtype)


def norm_modulate(x, g, scale, shift, out_dtype, tm=256):
    m, d = x.shape
    vec = pl.BlockSpec((1, d), lambda i: (0, 0))
    return pl.pallas_call(
        _norm_mod_kernel,
        grid=(m // tm,),
        in_specs=[pl.BlockSpec((tm, d), lambda i: (i, 0)), vec, vec, vec],
        out_specs=pl.BlockSpec((tm, d), lambda i: (i, 0)),
        out_shape=jax.ShapeDtypeStruct((m, d), out_dtype),
        compiler_params=_cparams("arbitrary"),
        name="norm_modulate",
    )(x, g.reshape(1, d), scale.reshape(1, d), shift.reshape(1, d))


def _rope_block(y, cos, sa, sb):
    return y * cos + pltpu.roll(y, 16, 1) * sa + pltpu.roll(y, LANES - 16, 1) * sb


def _head_proj_kernel(a_ref, w_ref, *rest, n_q, with_rope):
    if with_rope:
        cos_ref, sa_ref, sb_ref, o_ref, wb_ref = rest
        cos, sa, sb = cos_ref[...], sa_ref[...], sb_ref[...]
    else:
        o_ref, wb_ref = rest

    @pl.when(pl.program_id(0) == 0)
    def _():
        wb_ref[...] = w_ref[0].astype(BF16)

    acc = jnp.dot(a_ref[...], wb_ref[...], preferred_element_type=F32)
    lo = _lane_lo()
    out = 0
    for b in range(acc.shape[1] // LANES):
        y = acc[:, b * LANES:(b + 1) * LANES]
        if with_rope and b <= n_q:
            y = _rope_block(y, cos, sa, sb)
        if b < n_q:
            blocks = [y]
        else:
            swapped = pltpu.roll(y, HEAD_DIM, 1)
            blocks = [jnp.where(lo, y, swapped), jnp.where(lo, swapped, y)]
        for blk in blocks:
            o_ref[:, out * LANES:(out + 1) * LANES] = blk.astype(o_ref.dtype)
            out += 1


def head_projection(a, w_in, layer, rope=None, tm=512):
    m, k = a.shape
    n_in = (A_HEADS + 2 * A_KV_HEADS) * HEAD_DIM
    n_q = A_HEADS * HEAD_DIM // LANES
    assert A_KV_HEADS * HEAD_DIM == LANES
    n_out = n_in + 2 * LANES
    tm = min(tm, m)
    in_specs = [pl.BlockSpec((tm, k), lambda i: (i, 0)),
                pl.BlockSpec((pl.Element(1), pl.Element(k), pl.Element(n_in)), lambda i: (layer, 0, 0))]
    args = [a, w_in]
    if rope is not None:
        in_specs += [pl.BlockSpec((tm, LANES), lambda i: (i, 0))] * 3
        args += list(rope)
    return pl.pallas_call(
        functools.partial(_head_proj_kernel, n_q=n_q, with_rope=rope is not None),
        grid=(m // tm,),
        in_specs=in_specs,
        out_specs=pl.BlockSpec((tm, n_out), lambda i: (i, 0)),
        out_shape=jax.ShapeDtypeStruct((m, n_out), BF16),
        scratch_shapes=[pltpu.VMEM((k, n_in), BF16)],
        compiler_params=_cparams("arbitrary"),
        name="head_projection",
    )(*args)


def _wide_proj_kernel(a_ref, w_ref, o_ref, wb_ref):
    @pl.when(pl.program_id(1) == 0)
    def _():
        wb_ref[...] = w_ref[0].astype(BF16)

    o_ref[...] = jnp.dot(a_ref[...], wb_ref[...], preferred_element_type=F32).astype(o_ref.dtype)


def wide_projection(a, w, layer, col0, n, tm=1024, tn=1024):
    m, k = a.shape
    tm = min(tm, m)
    assert n % tn == 0 and col0 % LANES == 0 and tn % LANES == 0
    return pl.pallas_call(
        _wide_proj_kernel,
        grid=(n // tn, m // tm),
        in_specs=[pl.BlockSpec((tm, k), lambda j, i: (i, 0)),
                  pl.BlockSpec((pl.Element(1), pl.Element(k), pl.Element(tn)),
                               lambda j, i: (layer, 0, pl.multiple_of(col0 + j * tn, LANES)))],
        out_specs=pl.BlockSpec((tm, tn), lambda j, i: (i, j)),
        out_shape=jax.ShapeDtypeStruct((m, n), BF16),
        scratch_shapes=[pltpu.VMEM((k, tn), BF16)],
        compiler_params=_cparams("arbitrary", "arbitrary"),
        name="wide_projection",
    )(a, w)


def _lane_lo():
    return lax.broadcasted_iota(jnp.int32, (1, LANES), 1) < HEAD_DIM


def _half_select(x, lo, half):
    zero = jnp.zeros_like(x)
    return jnp.where(lo, x, zero) if half == 0 else jnp.where(lo, zero, x)


def _dot_nt(a, b):
    return lax.dot_general(a, b, (((1,), (1,)), ((), ())), preferred_element_type=F32)


def _softmax_pv(parts, values, sink=None):
    m = parts[0].max(axis=-1, keepdims=True)
    for s in parts[1:]:
        m = jnp.maximum(m, s.max(axis=-1, keepdims=True))
    if sink is not None:
        m = jnp.maximum(m, sink)
    den = jnp.exp(sink - m) if sink is not None else 0.0
    out = 0.0
    for s, v in zip(parts, values):
        p = jnp.exp(s - m)
        den = den + p.sum(axis=-1, keepdims=True)
        out = out + jnp.dot(p.astype(BF16), v, preferred_element_type=F32)
    return out / den


def _attn_a_kernel(sink_ref, q_ref, kp_ref, kc_ref, kn_ref, vp_ref, vc_ref, vn_ref, kz_ref, vz_ref, o_ref,
                   *, seq):
    i = pl.program_id(0)
    lo = _lane_lo()
    nq = 4 * A_BLOCK
    r = lax.broadcasted_iota(jnp.int32, (nq, 3 * A_BLOCK), 0) & (A_BLOCK - 1)
    c = lax.broadcasted_iota(jnp.int32, (nq, 3 * A_BLOCK), 1)
    kabs = c + (i - 1) * A_BLOCK
    valid = ((c - r) >= 0) & ((c - r) <= 2 * A_WINDOW) & (kabs >= 0) & (kabs < seq)
    rowblk = lax.broadcasted_iota(jnp.int32, (nq, 1), 0) // A_BLOCK
    for g in range(A_KV_HEADS):
        gs = slice(g * LANES, (g + 1) * LANES)
        kw = jnp.concatenate([kp_ref[:, gs], kc_ref[:, gs], kn_ref[:, gs]], axis=0)
        vw = jnp.concatenate([vp_ref[:, gs], vc_ref[:, gs], vn_ref[:, gs]], axis=0)
        kz, vz = kz_ref[:, gs], vz_ref[:, gs]
        blocks = [slice((2 * g + bb) * LANES, (2 * g + bb + 1) * LANES) for bb in range(2)]
        ql = jnp.concatenate([_half_select(q_ref[:, bs], lo, half) * (HEAD_DIM ** -0.5)
                              for half in range(2) for bs in blocks], axis=0)
        heads = [2 * (2 * g + bb) + half for half in range(2) for bb in range(2)]
        sink = jnp.full((nq, 1), sink_ref[heads[3]], F32)
        for k in range(3):
            sink = jnp.where(rowblk == k, sink_ref[heads[k]], sink)
        s = jnp.where(valid, _dot_nt(ql, kw), NEG_INF)
        sz = _dot_nt(ql, kz)
        m = jnp.maximum(jnp.maximum(s.max(axis=-1, keepdims=True), sz.max(axis=-1, keepdims=True)), sink)
        p, pz = jnp.exp(s - m), jnp.exp(sz - m)
        inv = 1.0 / (p.sum(axis=-1, keepdims=True) + pz.sum(axis=-1, keepdims=True) + jnp.exp(sink - m))
        p, pz = p.astype(BF16), pz.astype(BF16)
        out = None
        for half in range(2):
            hs = slice(half * 2 * A_BLOCK, (half + 1) * 2 * A_BLOCK)
            o = (jnp.dot(p[hs], _half_select(vw, lo, half), preferred_element_type=F32)
                 + jnp.dot(pz[hs], _half_select(vz, lo, half), preferred_element_type=F32)) * inv[hs]
            out = o if out is None else out + o
        for bb in range(2):
            o_ref[:, blocks[bb]] = out[bb * A_BLOCK:(bb + 1) * A_BLOCK].astype(o_ref.dtype)


def window_attention(sink, qk, rest, qk_ctx, rest_ctx, vv_col):
    seq = qk.shape[0]
    lc = qk_ctx.shape[0]
    nb = seq // A_BLOCK
    kw = 2 * LANES
    prev = lambda i: jnp.maximum(i - 1, 0)
    nxt = lambda i: jnp.minimum(i + 1, nb - 1)
    return pl.pallas_call(
        functools.partial(_attn_a_kernel, seq=seq),
        grid=(nb,),
        in_specs=[pl.BlockSpec(memory_space=pltpu.SMEM),
                  pl.BlockSpec((A_BLOCK, 4 * LANES), lambda i: (i, 0)),
                  pl.BlockSpec((A_BLOCK, kw), lambda i: (prev(i), 2)),
                  pl.BlockSpec((A_BLOCK, kw), lambda i: (i, 2)),
                  pl.BlockSpec((A_BLOCK, kw), lambda i: (nxt(i), 2)),
                  pl.BlockSpec((A_BLOCK, kw), lambda i: (prev(i), vv_col)),
                  pl.BlockSpec((A_BLOCK, kw), lambda i: (i, vv_col)),
                  pl.BlockSpec((A_BLOCK, kw), lambda i: (nxt(i), vv_col)),
                  pl.BlockSpec((lc, kw), lambda i: (0, 2)),
                  pl.BlockSpec((lc, kw), lambda i: (0, vv_col))],
        out_specs=pl.BlockSpec((A_BLOCK, 4 * LANES), lambda i: (i, 0)),
        out_shape=jax.ShapeDtypeStruct((seq, 4 * LANES), BF16),
        compiler_params=_cparams("arbitrary"),
        name="window_attention",
    )(sink, qk, qk, qk, qk, rest, rest, rest, qk_ctx, rest_ctx)


def _ctx_attn_kernel(sink_ref, q_ref, k_ref, v_ref, o_ref, *, kv_shared, has_sink):
    lo = _lane_lo()
    for b in range(q_ref.shape[1] // LANES):
        bs = slice(b * LANES, (b + 1) * LANES)
        kb = b // 2 if kv_shared else b
        ks = slice(kb * LANES, (kb + 1) * LANES)
        qb, k, v = q_ref[:, bs], k_ref[:, ks], v_ref[:, ks]
        acc = jnp.zeros((q_ref.shape[0], LANES), F32)
        for half in range(2):
            ql = _half_select(qb, lo, half) * (HEAD_DIM ** -0.5)
            sink = sink_ref[2 * b + half] if has_sink else None
            acc = acc + _softmax_pv([_dot_nt(ql, k)], [_half_select(v, lo, half)], sink=sink)
        o_ref[:, bs] = acc.astype(o_ref.dtype)


def context_attention(sink, q_arr, q_col, k_arr, k_col, k_w, v_arr, v_col, kv_shared, has_sink):
    lc = q_arr.shape[0]
    return pl.pallas_call(
        functools.partial(_ctx_attn_kernel, kv_shared=kv_shared, has_sink=has_sink),
        grid=(1,),
        in_specs=[pl.BlockSpec(memory_space=pltpu.SMEM),
                  pl.BlockSpec((lc, 4 * LANES), lambda i: (0, q_col)),
                  pl.BlockSpec((lc, k_w), lambda i: (0, k_col)),
                  pl.BlockSpec((lc, k_w), lambda i: (0, v_col))],
        out_specs=pl.BlockSpec((lc, 4 * LANES), lambda i: (0, 0)),
        out_shape=jax.ShapeDtypeStruct((lc, 4 * LANES), BF16),
        compiler_params=_cparams("arbitrary"),
        name="context_attention",
    )(sink, q_arr, k_arr, v_arr)


def _nbr_row_offsets(rows):
    nsteps = rows // C_QROWS
    nblk = C_KROWS * GRID_W // LANES
    wr = min(C_WIN_R, rows)
    reps = [0, 1, 2, nsteps - 2, nsteps - 1]
    dr = -np.ones((len(reps), C_QROWS, C_KROWS), np.int32)
    for v, i in enumerate(reps):
        w0 = C_QROWS * int(np.clip(i - 2, 0, nsteps - nblk))
        for a in range(C_QROWS):
            rq = C_QROWS * i + a
            k0 = int(np.clip(rq - C_WIN_R // 2, 0, rows - wr))
            for j in range(C_KROWS):
                if k0 <= w0 + j < k0 + wr:
                    dr[v, a, j] = w0 + j - rq + C_WIN_R - 1
    return dr


def _nbr_bias(rpb, rows):
    h = rpb.shape[0]
    qc = np.arange(GRID_W)[:, None]
    kc = np.arange(GRID_W)[None, :]
    cstart = np.clip(qc - C_WIN_C // 2, 0, GRID_W - C_WIN_C)
    col_ok = jnp.asarray((kc >= cstart) & (kc < cstart + C_WIN_C))
    pad = GRID_W - C_WIN_C
    ext = jnp.pad(rpb.astype(F32), ((0, 0), (0, 0), (pad, pad)), mode='edge')
    toep = jnp.stack([ext[:, :, GRID_W - 1 - q:2 * GRID_W - 1 - q] for q in range(GRID_W)], axis=2)
    toep = jnp.where(col_ok, toep, NEG_INF)
    masked = jnp.full((h, GRID_W, GRID_W), NEG_INF, F32)
    dr = _nbr_row_offsets(rows)
    variants = []
    for v in range(dr.shape[0]):
        qrows = [jnp.concatenate([toep[:, dr[v, a, j]] if dr[v, a, j] >= 0 else masked for j in range(C_KROWS)],
                                 axis=-1) for a in range(C_QROWS)]
        variants.append(jnp.concatenate(qrows, axis=-2).reshape(h * C_QROWS * GRID_W, C_KROWS * GRID_W))
    return jnp.stack(variants)


def _attn_c_kernel(q_ref, *refs):
    nblk = C_KROWS * GRID_W // LANES
    k_refs, v_refs = refs[:nblk], refs[nblk:2 * nblk]
    kz_ref, vz_ref, bias_ref, o_ref = refs[2 * nblk:]
    lo = _lane_lo()
    nq = q_ref.shape[0]
    nb = C_HEADS // 2
    blocks = [slice(b * LANES, (b + 1) * LANES) for b in range(nb)]
    s_parts, sz_parts = [], []
    for bs in blocks:
        ql = jnp.concatenate([_half_select(q_ref[:, bs], lo, half) * (HEAD_DIM ** -0.5) for half in range(2)], axis=0)
        s_parts.append(_dot_nt(ql, jnp.concatenate([kr[:, bs] for kr in k_refs], axis=0)))
        sz_parts.append(_dot_nt(ql, kz_ref[:, bs]))
    s = jnp.concatenate(s_parts, axis=0) + bias_ref[...]
    sz = jnp.concatenate(sz_parts, axis=0)
    m = jnp.maximum(s.max(axis=-1, keepdims=True), sz.max(axis=-1, keepdims=True))
    p, pz = jnp.exp(s - m), jnp.exp(sz - m)
    inv = 1.0 / (p.sum(axis=-1, keepdims=True) + pz.sum(axis=-1, keepdims=True))
    p, pz = p.astype(BF16), pz.astype(BF16)
    for b, bs in enumerate(blocks):
        vw = jnp.concatenate([vr[:, bs] for vr in v_refs], axis=0)
        vz = vz_ref[:, bs]
        out = None
        for half in range(2):
            hs = slice((2 * b + half) * nq, (2 * b + half + 1) * nq)
            o = (jnp.dot(p[hs], _half_select(vw, lo, half), preferred_element_type=F32)
                 + jnp.dot(pz[hs], _half_select(vz, lo, half), preferred_element_type=F32)) * inv[hs]
            out = o if out is None else out + o
        o_ref[:, bs] = out.astype(o_ref.dtype)


def neighborhood_attention(rest, rest_ctx, rpb, q_col, k_col, v_col):
    seq = rest.shape[0]
    lc = rest_ctx.shape[0]
    rows = seq // GRID_W
    nsteps = rows // C_QROWS
    nblk = C_KROWS * GRID_W // LANES
    assert nsteps > nblk and C_QROWS * GRID_W == LANES
    bias = _nbr_bias(rpb, rows)

    def first_blk(i):
        return jnp.clip(i - 2, 0, nsteps - nblk)

    def variant(i):
        return jnp.where(i < 2, i, jnp.where(i >= nsteps - 2, i - (nsteps - 2) + 3, 2))

    w = 4 * LANES
    kspecs = [pl.BlockSpec((LANES, w), functools.partial(lambda i, t: (first_blk(i) + t, k_col), t=t))
              for t in range(nblk)]
    vspecs = [pl.BlockSpec((LANES, w), functools.partial(lambda i, t: (first_blk(i) + t, v_col), t=t))
              for t in range(nblk)]
    return pl.pallas_call(
        _attn_c_kernel,
        grid=(nsteps,),
        in_specs=[pl.BlockSpec((LANES, w), lambda i: (i, q_col))] + kspecs + vspecs + [
            pl.BlockSpec((lc, w), lambda i: (0, k_col)),
            pl.BlockSpec((lc, w), lambda i: (0, v_col)),
            pl.BlockSpec((None, C_HEADS * LANES, nblk * LANES), lambda i: (variant(i), 0, 0))],
        out_specs=pl.BlockSpec((LANES, w), lambda i: (i, 0)),
        out_shape=jax.ShapeDtypeStruct((seq, w), BF16),
        compiler_params=_cparams("arbitrary"),
        name="neighborhood_attention",
    )(rest, *([rest] * (2 * nblk)), rest_ctx, rest_ctx, bias)


def _s5_carry_scan(bu, pr, pi, cr, ci, reverse):
    p = bu.shape[1] // 2
    n_tiles = bu.shape[0] // SUBLANES
    edge = 0 if reverse else SUBLANES - 1
    rows_r, rows_i = [None] * n_tiles, [None] * n_tiles
    for step in range(n_tiles):
        tile = (n_tiles - 1 - step) if reverse else step
        xr = bu[tile * SUBLANES:(tile + 1) * SUBLANES, 0:p]
        xi = bu[tile * SUBLANES:(tile + 1) * SUBLANES, p:2 * p]
        xr, xi = xr + pr * cr - pi * ci, xi + pr * ci + pi * cr
        cr, ci = jnp.broadcast_to(xr[edge:edge + 1], xr.shape), jnp.broadcast_to(xi[edge:edge + 1], xi.shape)
        rows_r[tile], rows_i[tile] = xr, xi
    h = jnp.concatenate([jnp.concatenate(rows_r, axis=0), jnp.concatenate(rows_i, axis=0)], axis=1)
    return h, cr, ci


def _s5_kernel(uz_ref, ul_ref, b_ref, c_ref, tab_ref, d_ref, *rest, reverse, final):
    if final:
        yfz_ref, yfl_ref, gw_ref, gb_ref, oz_ref, ol_ref, carry_ref = rest
    else:
        oz_ref, ol_ref, carry_ref = rest
    s = pl.program_id(0)

    @pl.when(s == 0)
    def _():
        carry_ref[...] = jnp.zeros_like(carry_ref)

    nblk = b_ref.shape[0]
    ps = b_ref.shape[2] // 2
    p = nblk * ps

    def chunk(u_ref, yf_ref, o_ref):
        t = u_ref.shape[0]
        row = lax.broadcasted_iota(jnp.int32, (t, 1), 0) & (SUBLANES - 1)
        u = u_ref[...]
        uf = u.astype(F32)
        lagged = [u]
        for k in range(1, SUBLANES):
            if reverse:
                lag = jnp.where(row < SUBLANES - k, pltpu.roll(uf, t - k, 0), 0.0)
            else:
                lag = jnp.where(row >= k, pltpu.roll(uf, k, 0), 0.0)
            lagged.append(lag.astype(BF16))
        outs = []
        for q in range(nblk):
            qs = slice(q * ps, (q + 1) * ps)
            lhs = jnp.concatenate([x[:, q * LANES:(q + 1) * LANES] for x in lagged], axis=1)
            bu = jnp.dot(lhs, b_ref[q], preferred_element_type=F32)
            hq, cr, ci = _s5_carry_scan(bu, tab_ref[0, :, qs], tab_ref[1, :, qs], carry_ref[0, :, qs],
                                        carry_ref[1, :, qs], reverse)
            carry_ref[0, :, qs] = cr
            carry_ref[1, :, qs] = ci
            outs.append(jnp.dot(hq.astype(BF16), c_ref[q], preferred_element_type=F32))
        out = jnp.concatenate(outs, axis=1)
        if not final:
            o_ref[...] = u.astype(F32) * d_ref[...] + out
        else:
            g = _gelu_tanh(yf_ref[...] + out)
            z = jnp.dot(g.astype(BF16), gw_ref[...], preferred_element_type=F32) + gb_ref[...]
            o_ref[...] = (g * _sigmoid(z)).astype(o_ref.dtype)

    @pl.when(s == 0)
    def _():
        chunk(uz_ref, yfz_ref if final else None, oz_ref)

    @pl.when(s > 0)
    def _():
        chunk(ul_ref, yfl_ref if final else None, ol_ref)


def s5_direction(rest, rest_ctx, u_col, bmat, cmat, tabs, dskip, reverse, fwd=None, glu=None, t=512):
    seq = rest.shape[0]
    tz = rest_ctx.shape[0]
    t = min(t, seq)
    nl = seq // t
    w = 4 * LANES
    nq, kq, ps2 = bmat.shape
    p2 = nq * ps2
    final = fwd is not None
    if reverse:
        lat = lambda s: (jnp.where(s == 0, nl - 1, nl - s), 0)
    else:
        lat = lambda s: (jnp.maximum(s - 1, 0), 0)
    lat_u = lambda s: (lat(s)[0], u_col)
    const = lambda s: (0, 0)
    in_specs = [pl.BlockSpec((tz, w), lambda s: (0, u_col)),
                pl.BlockSpec((t, w), lat_u),
                pl.BlockSpec((nq, kq, ps2), lambda s: (0, 0, 0)),
                pl.BlockSpec((nq, ps2, LANES), lambda s: (0, 0, 0)),
                pl.BlockSpec((2, SUBLANES, p2 // 2), lambda s: (0, 0, 0)),
                pl.BlockSpec((1, w), const)]
    args = [rest_ctx, rest, bmat, cmat, tabs, dskip.reshape(1, w)]
    if final:
        in_specs += [pl.BlockSpec((tz, w), const), pl.BlockSpec((t, w), lat),
                     pl.BlockSpec((w, w), const), pl.BlockSpec((1, w), const)]
        args += [fwd[0], fwd[1], glu[0], glu[1].reshape(1, w)]
    odt = BF16 if final else F32
    return pl.pallas_call(
        functools.partial(_s5_kernel, reverse=reverse, final=final),
        grid=(nl + 1,),
        in_specs=in_specs,
        out_specs=[pl.BlockSpec((tz, w), const), pl.BlockSpec((t, w), lat)],
        out_shape=[jax.ShapeDtypeStruct((tz, w), odt), jax.ShapeDtypeStruct((seq, w), odt)],
        scratch_shapes=[pltpu.VMEM((2, SUBLANES, p2 // 2), F32)],
        compiler_params=_cparams("arbitrary"),
        name="s5_direction",
    )(*args)


def _s5_params(a_re, a_im, log_step, b_re, b_im, c_re, c_im, reverse):
    ar, ai = a_re.astype(F32), a_im.astype(F32)
    step = jnp.exp(log_step.astype(F32))[:, None]
    mag = jnp.exp(ar * step)
    ab_re, ab_im = mag * jnp.cos(ai * step), mag * jnp.sin(ai * step)
    den = ar * ar + ai * ai
    nr = ab_re - 1.0
    f_re = (nr * ar + ab_im * ai) / den
    f_im = (ab_im * ar - nr * ai) / den
    br, bi = b_re.astype(F32), b_im.astype(F32)
    bb_re = f_re[..., None] * br - f_im[..., None] * bi
    bb_im = f_re[..., None] * bi + f_im[..., None] * br
    g, p, h = bb_re.shape
    gq = LANES // h
    nq = g // gq
    eye = jnp.eye(gq, dtype=F32)
    pr, pi = [jnp.ones_like(ab_re), ab_re], [jnp.zeros_like(ab_im), ab_im]
    for _ in range(SUBLANES - 1):
        pr.append(pr[-1] * ab_re - pi[-1] * ab_im)
        pi.append(pr[-2] * ab_im + pi[-1] * ab_re)
    w_re = jnp.stack([pr[k][..., None] * bb_re - pi[k][..., None] * bb_im for k in range(SUBLANES)])
    w_im = jnp.stack([pr[k][..., None] * bb_im + pi[k][..., None] * bb_re for k in range(SUBLANES)])

    def drive(w):
        w = w.reshape(SUBLANES, nq, gq, p, h)
        return jnp.einsum('kqgph,gj->qkghjp', w, eye).reshape(nq, SUBLANES * gq * h, gq * p)

    def readout(c):
        c = c.astype(F32).reshape(nq, gq, h, p)
        return jnp.einsum('qghp,gj->qgpjh', c, eye).reshape(nq, gq * p, gq * h)

    bmat = jnp.concatenate([drive(w_re), drive(w_im)], axis=2).astype(BF16)
    cmat = jnp.concatenate([readout(c_re), -readout(c_im)], axis=1).astype(BF16)
    order = list(range(SUBLANES, 0, -1)) if reverse else list(range(1, SUBLANES + 1))
    tabs = jnp.stack([jnp.stack([pr[j].reshape(-1) for j in order]), jnp.stack([pi[j].reshape(-1) for j in order])])
    return bmat, cmat, tabs


def _conv_kernel(val_ref, gate_ref, vp_ref, gp_ref, vn_ref, gn_ref, w_ref, b_ref, g_ref, beta_ref, o_ref, hs_ref,
                 *, sub):
    i = pl.program_id(0)
    n = pl.num_programs(0)
    t = val_ref.shape[0]

    def glu(v, g):
        return v[...].astype(F32) * _sigmoid(g[...].astype(F32))

    hs_ref[0:CONV_HALO] = jnp.where(i > 0, glu(vp_ref, gp_ref), 0.0)
    hs_ref[CONV_HALO:CONV_HALO + t] = glu(val_ref, gate_ref)
    hs_ref[CONV_HALO + t:2 * CONV_HALO + t] = jnp.where(i < n - 1, glu(vn_ref, gn_ref), 0.0)
    base = CONV_HALO - CONV_K // 2
    for r0 in range(t // sub):
        acc = jnp.zeros((sub, val_ref.shape[1]), F32) + b_ref[...]
        for r in range(SUBLANES):
            part = None
            for m in range((base + CONV_K - 1) // SUBLANES + 1):
                k = SUBLANES * m + r - base
                if 0 <= k < CONV_K:
                    lo_row = r0 * sub + SUBLANES * m
                    term = hs_ref[lo_row:lo_row + sub + SUBLANES] * w_ref[k:k + 1]
                    part = term if part is None else part + term
            acc = acc + part[r:r + sub]
        mu = jnp.mean(acc, axis=-1, keepdims=True)
        cen = acc - mu
        var = jnp.mean(cen * cen, axis=-1, keepdims=True)
        yn = cen * lax.rsqrt(var + EPS) * g_ref[...] + beta_ref[...]
        o_ref[r0 * sub:(r0 + 1) * sub] = _silu(yn).astype(o_ref.dtype)


def conv_module(rest, val_col, gate_col, w, b, g, beta, t=256, sub=64):
    seq = rest.shape[0]
    t = min(t, seq)
    cw = 4 * LANES
    hb = t // CONV_HALO
    nh = seq // CONV_HALO
    prev = lambda i: jnp.maximum(i * hb - 1, 0)
    nxt = lambda i: jnp.minimum((i + 1) * hb, nh - 1)
    vec = pl.BlockSpec((1, cw), lambda i: (0, 0))
    return pl.pallas_call(
        functools.partial(_conv_kernel, sub=sub),
        grid=(seq // t,),
        in_specs=[pl.BlockSpec((t, cw), lambda i: (i, val_col)),
                  pl.BlockSpec((t, cw), lambda i: (i, gate_col)),
                  pl.BlockSpec((CONV_HALO, cw), lambda i: (prev(i), val_col)),
                  pl.BlockSpec((CONV_HALO, cw), lambda i: (prev(i), gate_col)),
                  pl.BlockSpec((CONV_HALO, cw), lambda i: (nxt(i), val_col)),
                  pl.BlockSpec((CONV_HALO, cw), lambda i: (nxt(i), gate_col)),
                  pl.BlockSpec((CONV_K, cw), lambda i: (0, 0)), vec, vec, vec],
        out_specs=pl.BlockSpec((t, cw), lambda i: (i, 0)),
        out_shape=jax.ShapeDtypeStruct((seq, cw), BF16),
        scratch_shapes=[pltpu.VMEM((t + 2 * CONV_HALO, cw), F32)],
        compiler_params=_cparams("arbitrary"),
        name="conv_module",
    )(rest, rest, rest, rest, rest, rest, w.astype(F32), b.reshape(1, cw).astype(F32),
      g.reshape(1, cw).astype(F32), beta.reshape(1, cw).astype(F32))


def _merge_kernel(ya_ref, yb_ref, yc_ref, yd_ref, g0_ref, g1_ref, g2_ref, g3_ref, wb_ref, wo_ref, x_ref, gate_ref,
                  o_ref, m_ref):
    j = pl.program_id(1)
    nj = m_ref.shape[0]

    @pl.when(j < nj)
    def _():
        m = None
        for y_ref, gl_ref, k in zip((ya_ref, yb_ref, yc_ref, yd_ref), (g0_ref, g1_ref, g2_ref, g3_ref),
                                    range(N_BRANCH)):
            term = _sigmoid(gl_ref[...].astype(F32)) * jnp.dot(y_ref[...], wb_ref[k], preferred_element_type=F32)
            m = term if m is None else m + term
        m_ref[j] = m.astype(BF16)

    @pl.when(j >= nj)
    def _():
        merged = jnp.concatenate([m_ref[k] for k in range(nj)], axis=1)
        o_ref[...] = x_ref[...] + gate_ref[...] * jnp.dot(merged, wo_ref[...], preferred_element_type=F32)


def gated_merge_residual(ys, rest, logit_col, w_branch, w_out, layer, x, gate, tm=512, tn=512):
    m, yw = ys[0].shape
    n = w_branch.shape[3]
    tm = min(tm, m)
    nj = n // tn
    first = lambda j: jnp.minimum(j, nj - 1)
    second = lambda j: jnp.maximum(j - nj, 0)
    yspec = pl.BlockSpec((tm, yw), lambda i, j: (i, 0))
    gspecs = [pl.BlockSpec((tm, tn), functools.partial(lambda i, j, k: (i, logit_col + k * nj + first(j)), k=k))
              for k in range(N_BRANCH)]
    return pl.pallas_call(
        _merge_kernel,
        grid=(m // tm, 2 * nj),
        in_specs=[yspec] * N_BRANCH + gspecs + [
            pl.BlockSpec((None, N_BRANCH, yw, tn), lambda i, j: (layer, 0, 0, first(j))),
            pl.BlockSpec((None, n, tn), lambda i, j: (layer, 0, second(j))),
            pl.BlockSpec((tm, tn), lambda i, j: (i, second(j))),
            pl.BlockSpec((1, tn), lambda i, j: (0, second(j)))],
        out_specs=pl.BlockSpec((tm, tn), lambda i, j: (i, second(j))),
        out_shape=jax.ShapeDtypeStruct((m, n), F32),
        scratch_shapes=[pltpu.VMEM((nj, tm, tn), BF16)],
        compiler_params=_cparams("arbitrary", "arbitrary"),
        name="gated_merge_residual",
    )(*ys, rest, rest, rest, rest, w_branch, w_out, x, gate.reshape(1, n))


def _mlp_kernel(x_ref, g_ref, sc_ref, sh_ref, gate_ref, w1_ref, w2_ref, *rest, final_norm):
    if final_norm:
        gf_ref, o_ref, h_ref, acc_ref = rest
    else:
        o_ref, h_ref, acc_ref = rest
    f = pl.program_id(1)

    @pl.when(f == 0)
    def _():
        x = x_ref[...]
        y = x * lax.rsqrt(jnp.mean(x * x, axis=-1, keepdims=True) + EPS) * g_ref[...]
        h_ref[...] = (y * (1.0 + sc_ref[...]) + sh_ref[...]).astype(BF16)
        acc_ref[...] = jnp.zeros_like(acc_ref)

    t = jnp.dot(h_ref[...], w1_ref[...], preferred_element_type=F32)
    t = jnp.square(jnp.maximum(t, 0.0)).astype(BF16)
    acc_ref[...] += jnp.dot(t, w2_ref[...], preferred_element_type=F32)

    @pl.when(f == pl.num_programs(1) - 1)
    def _():
        y = x_ref[...] + gate_ref[...] * acc_ref[...]
        if final_norm:
            y = y * lax.rsqrt(jnp.mean(y * y, axis=-1, keepdims=True) + EPS) * gf_ref[...]
        o_ref[...] = y


def mlp_block(x, g, scale, shift, gate, w1, w2, layer, final_g=None, tm=512, tf=1024):
    m, d = x.shape
    ff = w1.shape[2]
    tm = min(tm, m)
    vec = pl.BlockSpec((1, d), lambda i, f: (0, 0))
    final_norm = final_g is not None
    args = [x, g.reshape(1, d), scale.reshape(1, d), shift.reshape(1, d), gate.reshape(1, d), w1, w2]
    in_specs = [pl.BlockSpec((tm, d), lambda i, f: (i, 0)), vec, vec, vec, vec,
                pl.BlockSpec((None, d, tf), lambda i, f: (layer, 0, f)),
                pl.BlockSpec((None, tf, d), lambda i, f: (layer, f, 0))]
    if final_norm:
        args.append(final_g.reshape(1, d))
        in_specs.append(vec)
    return pl.pallas_call(
        functools.partial(_mlp_kernel, final_norm=final_norm),
        grid=(m // tm, ff // tf),
        in_specs=in_specs,
        out_specs=pl.BlockSpec((tm, d), lambda i, f: (i, 0)),
        out_shape=jax.ShapeDtypeStruct((m, d), F32),
        scratch_shapes=[pltpu.VMEM((tm, d), BF16), pltpu.VMEM((tm, d), F32)],
        compiler_params=_cparams("arbitrary", "arbitrary"),
        name="mlp_block",
    )(*args)


_A_COL_K, _A_COL_V = 2, 3
_REST_COL0 = (A_HEADS + 2 * A_KV_HEADS) * HEAD_DIM
_COL_U, _COL_QC, _COL_KC, _COL_VC, _COL_VAL, _COL_GATE, _COL_MERGE = 0, 1, 2, 3, 4, 5, 6


def _rope_tables(seq):
    freqs = ROPE_BASE ** (-jnp.arange(ROPE_PAIRS, dtype=F32) / ROPE_PAIRS)
    rows = seq // GRID_W
    ang_r = jnp.arange(rows).astype(F32)[:, None] * freqs
    ang_c = jnp.arange(GRID_W).astype(F32)[:, None] * freqs

    def expand(tab_r, tab_c, first, second):
        r = jnp.repeat(tab_r, GRID_W, axis=0)
        cc = jnp.tile(tab_c, (rows, 1))
        head = jnp.concatenate([first * r, second * r, first * cc, second * cc], axis=1)
        return jnp.tile(head, (1, LANES // HEAD_DIM))

    cos = expand(jnp.cos(ang_r), jnp.cos(ang_c), 1.0, 1.0)
    sa = expand(jnp.sin(ang_r), jnp.sin(ang_c), 0.0, 1.0)
    sb = expand(jnp.sin(ang_r), jnp.sin(ang_c), -1.0, 0.0)
    return cos, sa, sb


def kernel(x, c, ctx, c_ctx, ada_w, ada_b, norm1_g, norm2_g, w_in, a_sink, s5_a_re, s5_a_im, s5_log_step, s5_b_re, s5_b_im, s5_c_re, s5_c_im, s5_d, s5_glu_w, s5_glu_b, c_rpb, d_conv_w, d_conv_b, d_ln_g, d_ln_b, w_branch, w_out, mlp_w1, mlp_w2, final_g):
    depth = ada_w.shape[0]
    _, seq, d = x.shape
    lc = ctx.shape[1]
    xs = x.reshape(seq, d)
    zs = ctx.reshape(lc, d)
    rope = _rope_tables(seq)

    cc = jnp.zeros((SUBLANES, d), F32).at[0].set(c[0]).at[1].set(c_ctx)
    mods = ada_modulation(cc, ada_w, ada_b)
    n_rest = w_in.shape[2] - _REST_COL0
    wb, wo, w1, w2 = (w.astype(BF16) for w in (w_branch, w_out, mlp_w1, mlp_w2))

    for l in range(depth):
        need_ctx = l < depth - 1
        mx = [mods[l, 0, k * d:(k + 1) * d] for k in range(6)]
        mz = [mods[l, 1, k * d:(k + 1) * d] for k in range(6)]

        hx = norm_modulate(xs, norm1_g[l], mx[1], mx[0], BF16)
        hz = norm_modulate(zs, norm1_g[l], mz[1], mz[0], BF16)
        qk_x = head_projection(hx, w_in, l, rope=rope)
        qk_z = head_projection(hz, w_in, l)
        r_x = wide_projection(hx, w_in, l, _REST_COL0, n_rest)
        r_z = wide_projection(hz, w_in, l, _REST_COL0, n_rest)

        ya_x = window_attention(a_sink[l], qk_x, qk_x, qk_z, qk_z, _A_COL_V)
        yc_x = neighborhood_attention(r_x, r_z, c_rpb[l], _COL_QC, _COL_KC, _COL_VC)
        yd_x = conv_module(r_x, _COL_VAL, _COL_GATE, d_conv_w[l], d_conv_b[l], d_ln_g[l], d_ln_b[l])

        pf = _s5_params(s5_a_re[l, 0], s5_a_im[l, 0], s5_log_step[l, 0], s5_b_re[l, 0], s5_b_im[l, 0],
                        s5_c_re[l, 0], s5_c_im[l, 0], reverse=False)
        pb = _s5_params(s5_a_re[l, 1], s5_a_im[l, 1], s5_log_step[l, 1], s5_b_re[l, 1], s5_b_im[l, 1],
                        s5_c_re[l, 1], s5_c_im[l, 1], reverse=True)
        yf = s5_direction(r_x, r_z, _COL_U, *pf, s5_d[l].astype(F32), reverse=False)
        yb_z, yb_x = s5_direction(r_x, r_z, _COL_U, *pb, s5_d[l].astype(F32), reverse=True, fwd=yf,
                                  glu=(s5_glu_w[l].astype(BF16), s5_glu_b[l].astype(F32)))

        xs = gated_merge_residual((ya_x, yb_x, yc_x, yd_x), r_x, _COL_MERGE, wb, wo, l, xs, mx[2])
        xs = mlp_block(xs, norm2_g[l], mx[4], mx[3], mx[5], w1, w2, l, final_g=None if need_ctx else final_g)

        if need_ctx:
            ya_z = context_attention(a_sink[l], qk_z, 0, qk_z, _A_COL_K, 2 * LANES, qk_z, _A_COL_V, True, True)
            yc_z = context_attention(a_sink[l], r_z, _COL_QC, r_z, _COL_KC, 4 * LANES, r_z, _COL_VC, False, False)
            yd_z = conv_module(r_z, _COL_VAL, _COL_GATE, d_conv_w[l], d_conv_b[l], d_ln_g[l], d_ln_b[l])
            zs = gated_merge_residual((ya_z, yb_z, yc_z, yd_z), r_z, _COL_MERGE, wb, wo, l, zs, mz[2])
            zs = mlp_block(zs, norm2_g[l], mz[4], mz[3], mz[5], w1, w2, l)

    return xs.reshape(x.shape)
```

```python
import functools
import math

import numpy as np
import jax
import jax.numpy as jnp
from jax import lax
from jax.experimental import pallas as pl
from jax.experimental.pallas import tpu as pltpu

F32 = jnp.float32
BF16 = jnp.bfloat16

LANES = 128
SUBLANES = 8
VMEM_LIMIT = 56 * 1024 * 1024

GRID_W = 64
HEAD_DIM = 64
N_BRANCH = 4
A_HEADS = 8
A_KV_HEADS = 2
A_WINDOW = 128
A_BLOCK = 128
ROPE_BASE = 10000.0
ROPE_PAIRS = HEAD_DIM // 4
S5_GROUP_CH = 16
S5_STATE = 64
C_HEADS = 8
C_WIN_R = 8
C_WIN_C = 16
C_QROWS = 2
C_KROWS = C_QROWS + C_WIN_R
CONV_K = 31
CONV_HALO = 16
EPS = 1e-6
NEG_INF = -1e30


def _cparams(*sem):
    return pltpu.CompilerParams(dimension_semantics=sem, vmem_limit_bytes=VMEM_LIMIT)


def _sigmoid(x):
    return 0.5 * jnp.tanh(0.5 * x) + 0.5


def _silu(x):
    return x * _sigmoid(x)


def _gelu_tanh(x):
    return 0.5 * x * (1.0 + jnp.tanh(math.sqrt(2.0 / math.pi) * (x + 0.044715 * (x * x * x))))


def _ada_kernel(c_ref, w_ref, b_ref, o_ref):
    a = _silu(c_ref[...]).astype(BF16)
    o_ref[...] = jnp.dot(a, w_ref[...].astype(BF16), preferred_element_type=F32) + b_ref[...]


def ada_modulation(cc, ada_w, ada_b, tn=1024):
    depth, d, n = ada_w.shape
    return pl.pallas_call(
        _ada_kernel,
        grid=(depth, n // tn),
        in_specs=[pl.BlockSpec((SUBLANES, d), lambda l, j: (0, 0)),
                  pl.BlockSpec((None, d, tn), lambda l, j: (l, 0, j)),
                  pl.BlockSpec((None, 1, tn), lambda l, j: (l, 0, j))],
        out_specs=pl.BlockSpec((None, SUBLANES, tn), lambda l, j: (l, 0, j)),
        out_shape=jax.ShapeDtypeStruct((depth, SUBLANES, n), F32),
        compiler_params=_cparams("arbitrary", "arbitrary"),
        name="ada_modulation",
    )(cc, ada_w, ada_b.reshape(depth, 1, n))


def _rope_block(y, cos, sa, sb):
    return y * cos + pltpu.roll(y, 16, 1) * sa + pltpu.roll(y, LANES - 16, 1) * sb


def _head_proj_kernel(x_ref, g_ref, sc_ref, sh_ref, w_ref, *rest, n_q, with_rope):
    if with_rope:
        cos_ref, sa_ref, sb_ref, h_ref, o_ref, wb_ref = rest
        cos, sa, sb = cos_ref[...], sa_ref[...], sb_ref[...]
    else:
        h_ref, o_ref, wb_ref = rest

    @pl.when(pl.program_id(0) == 0)
    def _():
        wb_ref[...] = w_ref[0].astype(BF16)

    x = x_ref[...]
    y = x * lax.rsqrt(jnp.mean(x * x, axis=-1, keepdims=True) + EPS) * g_ref[...]
    h = (y * (1.0 + sc_ref[...]) + sh_ref[...]).astype(BF16)
    h_ref[...] = h
    acc = jnp.dot(h, wb_ref[...], preferred_element_type=F32)
    lo = _lane_lo()
    out = 0
    for b in range(acc.shape[1] // LANES):
        y = acc[:, b * LANES:(b + 1) * LANES]
        if with_rope and b <= n_q:
            y = _rope_block(y, cos, sa, sb)
        if b < n_q:
            blocks = [y]
        else:
            swapped = pltpu.roll(y, HEAD_DIM, 1)
            blocks = [jnp.where(lo, y, swapped), jnp.where(lo, swapped, y)]
        for blk in blocks:
            o_ref[:, out * LANES:(out + 1) * LANES] = blk.astype(o_ref.dtype)
            out += 1


def norm_head_projection(x, g, scale, shift, w_in, layer, rope=None, tm=256):
    m, k = x.shape
    n_in = (A_HEADS + 2 * A_KV_HEADS) * HEAD_DIM
    n_q = A_HEADS * HEAD_DIM // LANES
    assert A_KV_HEADS * HEAD_DIM == LANES
    n_out = n_in + 2 * LANES
    tm = min(tm, m)
    vec = pl.BlockSpec((1, k), lambda i: (0, 0))
    in_specs = [pl.BlockSpec((tm, k), lambda i: (i, 0)), vec, vec, vec,
                pl.BlockSpec((pl.Element(1), pl.Element(k), pl.Element(n_in)), lambda i: (layer, 0, 0))]
    args = [x, g.reshape(1, k), scale.reshape(1, k), shift.reshape(1, k), w_in]
    if rope is not None:
        in_specs += [pl.BlockSpec((tm, LANES), lambda i: (i, 0))] * 3
        args += list(rope)
    return pl.pallas_call(
        functools.partial(_head_proj_kernel, n_q=n_q, with_rope=rope is not None),
        grid=(m // tm,),
        in_specs=in_specs,
        out_specs=[pl.BlockSpec((tm, k), lambda i: (i, 0)), pl.BlockSpec((tm, n_out), lambda i: (i, 0))],
        out_shape=[jax.ShapeDtypeStruct((m, k), BF16), jax.ShapeDtypeStruct((m, n_out), BF16)],
        scratch_shapes=[pltpu.VMEM((k, n_in), BF16)],
        compiler_params=_cparams("arbitrary"),
        name="norm_head_projection",
    )(*args)


def _wide_proj_kernel(a_ref, w_ref, o_ref, wb_ref):
    @pl.when(pl.program_id(1) == 0)
    def _():
        wb_ref[...] = w_ref[0].astype(BF16)

    o_ref[...] = jnp.dot(a_ref[...], wb_ref[...], preferred_element_type=F32).astype(o_ref.dtype)


def wide_projection(a, w, layer, col0, n, tm=1024, tn=1024):
    m, k = a.shape
    tm = min(tm, m)
    assert n % tn == 0 and col0 % LANES == 0 and tn % LANES == 0
    return pl.pallas_call(
        _wide_proj_kernel,
        grid=(n // tn, m // tm),
        in_specs=[pl.BlockSpec((tm, k), lambda j, i: (i, 0)),
                  pl.BlockSpec((pl.Element(1), pl.Element(k), pl.Element(tn)),
                               lambda j, i: (layer, 0, pl.multiple_of(col0 + j * tn, LANES)))],
        out_specs=pl.BlockSpec((tm, tn), lambda j, i: (i, j)),
        out_shape=jax.ShapeDtypeStruct((m, n), BF16),
        scratch_shapes=[pltpu.VMEM((k, tn), BF16)],
        compiler_params=_cparams("arbitrary", "arbitrary"),
        name="wide_projection",
    )(a, w)


def _lane_lo():
    return lax.broadcasted_iota(jnp.int32, (1, LANES), 1) < HEAD_DIM


def _half_select(x, lo, half):
    zero = jnp.zeros_like(x)
    return jnp.where(lo, x, zero) if half == 0 else jnp.where(lo, zero, x)


def _dot_nt(a, b):
    return lax.dot_general(a, b, (((1,), (1,)), ((), ())), preferred_element_type=F32)


def _softmax_pv(parts, values, sink=None):
    m = parts[0].max(axis=-1, keepdims=True)
    for s in parts[1:]:
        m = jnp.maximum(m, s.max(axis=-1, keepdims=True))
    if sink is not None:
        m = jnp.maximum(m, sink)
    den = jnp.exp(sink - m) if sink is not None else 0.0
    out = 0.0
    for s, v in zip(parts, values):
        p = jnp.exp(s - m)
        den = den + p.sum(axis=-1, keepdims=True)
        out = out + jnp.dot(p.astype(BF16), v, preferred_element_type=F32)
    return out / den


def _attn_a_kernel(sink_ref, q_ref, kp_ref, kc_ref, kn_ref, vp_ref, vc_ref, vn_ref, kz_ref, vz_ref, o_ref,
                   *, seq):
    i = pl.program_id(0)
    lo = _lane_lo()
    nq = 4 * A_BLOCK
    r = lax.broadcasted_iota(jnp.int32, (nq, 3 * A_BLOCK), 0) & (A_BLOCK - 1)
    c = lax.broadcasted_iota(jnp.int32, (nq, 3 * A_BLOCK), 1)
    kabs = c + (i - 1) * A_BLOCK
    valid = ((c - r) >= 0) & ((c - r) <= 2 * A_WINDOW) & (kabs >= 0) & (kabs < seq)
    rowblk = lax.broadcasted_iota(jnp.int32, (nq, 1), 0) // A_BLOCK
    for g in range(A_KV_HEADS):
        gs = slice(g * LANES, (g + 1) * LANES)
        kw = jnp.concatenate([kp_ref[:, gs], kc_ref[:, gs], kn_ref[:, gs]], axis=0)
        vw = jnp.concatenate([vp_ref[:, gs], vc_ref[:, gs], vn_ref[:, gs]], axis=0)
        kz, vz = kz_ref[:, gs], vz_ref[:, gs]
        blocks = [slice((2 * g + bb) * LANES, (2 * g + bb + 1) * LANES) for bb in range(2)]
        ql = jnp.concatenate([_half_select(q_ref[:, bs], lo, half) * (HEAD_DIM ** -0.5)
                              for half in range(2) for bs in blocks], axis=0)
        heads = [2 * (2 * g + bb) + half for half in range(2) for bb in range(2)]
        sink = jnp.full((nq, 1), sink_ref[heads[3]], F32)
        for k in range(3):
            sink = jnp.where(rowblk == k, sink_ref[heads[k]], sink)
        s = jnp.where(valid, _dot_nt(ql, kw), NEG_INF)
        sz = _dot_nt(ql, kz)
        m = jnp.maximum(jnp.maximum(s.max(axis=-1, keepdims=True), sz.max(axis=-1, keepdims=True)), sink)
        p, pz = jnp.exp(s - m), jnp.exp(sz - m)
        inv = 1.0 / (p.sum(axis=-1, keepdims=True) + pz.sum(axis=-1, keepdims=True) + jnp.exp(sink - m))
        p, pz = p.astype(BF16), pz.astype(BF16)
        out = None
        for half in range(2):
            hs = slice(half * 2 * A_BLOCK, (half + 1) * 2 * A_BLOCK)
            o = (jnp.dot(p[hs], _half_select(vw, lo, half), preferred_element_type=F32)
                 + jnp.dot(pz[hs], _half_select(vz, lo, half), preferred_element_type=F32)) * inv[hs]
            out = o if out is None else out + o
        for bb in range(2):
            o_ref[:, blocks[bb]] = out[bb * A_BLOCK:(bb + 1) * A_BLOCK].astype(o_ref.dtype)


def window_attention(sink, qk, rest, qk_ctx, rest_ctx, vv_col):
    seq = qk.shape[0]
    lc = qk_ctx.shape[0]
    nb = seq // A_BLOCK
    kw = 2 * LANES
    prev = lambda i: jnp.maximum(i - 1, 0)
    nxt = lambda i: jnp.minimum(i + 1, nb - 1)
    return pl.pallas_call(
        functools.partial(_attn_a_kernel, seq=seq),
        grid=(nb,),
        in_specs=[pl.BlockSpec(memory_space=pltpu.SMEM),
                  pl.BlockSpec((A_BLOCK, 4 * LANES), lambda i: (i, 0)),
                  pl.BlockSpec((A_BLOCK, kw), lambda i: (prev(i), 2)),
                  pl.BlockSpec((A_BLOCK, kw), lambda i: (i, 2)),
                  pl.BlockSpec((A_BLOCK, kw), lambda i: (nxt(i), 2)),
                  pl.BlockSpec((A_BLOCK, kw), lambda i: (prev(i), vv_col)),
                  pl.BlockSpec((A_BLOCK, kw), lambda i: (i, vv_col)),
                  pl.BlockSpec((A_BLOCK, kw), lambda i: (nxt(i), vv_col)),
                  pl.BlockSpec((lc, kw), lambda i: (0, 2)),
                  pl.BlockSpec((lc, kw), lambda i: (0, vv_col))],
        out_specs=pl.BlockSpec((A_BLOCK, 4 * LANES), lambda i: (i, 0)),
        out_shape=jax.ShapeDtypeStruct((seq, 4 * LANES), BF16),
        compiler_params=_cparams("arbitrary"),
        name="window_attention",
    )(sink, qk, qk, qk, qk, rest, rest, rest, qk_ctx, rest_ctx)


def _ctx_attn_kernel(sink_ref, q_ref, k_ref, v_ref, o_ref, *, kv_shared, has_sink):
    lo = _lane_lo()
    for b in range(q_ref.shape[1] // LANES):
        bs = slice(b * LANES, (b + 1) * LANES)
        kb = b // 2 if kv_shared else b
        ks = slice(kb * LANES, (kb + 1) * LANES)
        qb, k, v = q_ref[:, bs], k_ref[:, ks], v_ref[:, ks]
        acc = jnp.zeros((q_ref.shape[0], LANES), F32)
        for half in range(2):
            ql = _half_select(qb, lo, half) * (HEAD_DIM ** -0.5)
            sink = sink_ref[2 * b + half] if has_sink else None
            acc = acc + _softmax_pv([_dot_nt(ql, k)], [_half_select(v, lo, half)], sink=sink)
        o_ref[:, bs] = acc.astype(o_ref.dtype)


def context_attention(sink, q_arr, q_col, k_arr, k_col, k_w, v_arr, v_col, kv_shared, has_sink):
    lc = q_arr.shape[0]
    return pl.pallas_call(
        functools.partial(_ctx_attn_kernel, kv_shared=kv_shared, has_sink=has_sink),
        grid=(1,),
        in_specs=[pl.BlockSpec(memory_space=pltpu.SMEM),
                  pl.BlockSpec((lc, 4 * LANES), lambda i: (0, q_col)),
                  pl.BlockSpec((lc, k_w), lambda i: (0, k_col)),
                  pl.BlockSpec((lc, k_w), lambda i: (0, v_col))],
        out_specs=pl.BlockSpec((lc, 4 * LANES), lambda i: (0, 0)),
        out_shape=jax.ShapeDtypeStruct((lc, 4 * LANES), BF16),
        compiler_params=_cparams("arbitrary"),
        name="context_attention",
    )(sink, q_arr, k_arr, v_arr)


def _nbr_row_offsets(rows):
    nsteps = rows // C_QROWS
    nblk = C_KROWS * GRID_W // LANES
    wr = min(C_WIN_R, rows)
    reps = [0, 1, 2, nsteps - 2, nsteps - 1]
    dr = -np.ones((len(reps), C_QROWS, C_KROWS), np.int32)
    for v, i in enumerate(reps):
        w0 = C_QROWS * int(np.clip(i - 2, 0, nsteps - nblk))
        for a in range(C_QROWS):
            rq = C_QROWS * i + a
            k0 = int(np.clip(rq - C_WIN_R // 2, 0, rows - wr))
            for j in range(C_KROWS):
                if k0 <= w0 + j < k0 + wr:
                    dr[v, a, j] = w0 + j - rq + C_WIN_R - 1
    return dr


def _nbr_bias(rpb, rows):
    h = rpb.shape[0]
    qc = np.arange(GRID_W)[:, None]
    kc = np.arange(GRID_W)[None, :]
    cstart = np.clip(qc - C_WIN_C // 2, 0, GRID_W - C_WIN_C)
    col_ok = jnp.asarray((kc >= cstart) & (kc < cstart + C_WIN_C))
    pad = GRID_W - C_WIN_C
    ext = jnp.pad(rpb.astype(F32), ((0, 0), (0, 0), (pad, pad)), mode='edge')
    toep = jnp.stack([ext[:, :, GRID_W - 1 - q:2 * GRID_W - 1 - q] for q in range(GRID_W)], axis=2)
    toep = jnp.where(col_ok, toep, NEG_INF)
    masked = jnp.full((h, GRID_W, GRID_W), NEG_INF, F32)
    dr = _nbr_row_offsets(rows)
    variants = []
    for v in range(dr.shape[0]):
        qrows = [jnp.concatenate([toep[:, dr[v, a, j]] if dr[v, a, j] >= 0 else masked for j in range(C_KROWS)],
                                 axis=-1) for a in range(C_QROWS)]
        variants.append(jnp.concatenate(qrows, axis=-2).reshape(h * C_QROWS * GRID_W, C_KROWS * GRID_W))
    return jnp.stack(variants)


def _attn_c_kernel(q_ref, *refs):
    nblk = C_KROWS * GRID_W // LANES
    k_refs, v_refs = refs[:nblk], refs[nblk:2 * nblk]
    kz_ref, vz_ref, bias_ref, o_ref = refs[2 * nblk:]
    lo = _lane_lo()
    nq = q_ref.shape[0]
    nb = C_HEADS // 2
    blocks = [slice(b * LANES, (b + 1) * LANES) for b in range(nb)]
    s_parts, sz_parts = [], []
    for bs in blocks:
        ql = jnp.concatenate([_half_select(q_ref[:, bs], lo, half) * (HEAD_DIM ** -0.5) for half in range(2)], axis=0)
        s_parts.append(_dot_nt(ql, jnp.concatenate([kr[:, bs] for kr in k_refs], axis=0)))
        sz_parts.append(_dot_nt(ql, kz_ref[:, bs]))
    s = jnp.concatenate(s_parts, axis=0) + bias_ref[...]
    sz = jnp.concatenate(sz_parts, axis=0)
    m = jnp.maximum(s.max(axis=-1, keepdims=True), sz.max(axis=-1, keepdims=True))
    p, pz = jnp.exp(s - m), jnp.exp(sz - m)
    inv = 1.0 / (p.sum(axis=-1, keepdims=True) + pz.sum(axis=-1, keepdims=True))
    p, pz = p.astype(BF16), pz.astype(BF16)
    for b, bs in enumerate(blocks):
        vw = jnp.concatenate([vr[:, bs] for vr in v_refs], axis=0)
        vz = vz_ref[:, bs]
        out = None
        for half in range(2):
            hs = slice((2 * b + half) * nq, (2 * b + half + 1) * nq)
            o = (jnp.dot(p[hs], _half_select(vw, lo, half), preferred_element_type=F32)
                 + jnp.dot(pz[hs], _half_select(vz, lo, half), preferred_element_type=F32)) * inv[hs]
            out = o if out is None else out + o
        o_ref[:, bs] = out.astype(o_ref.dtype)


def neighborhood_attention(rest, rest_ctx, rpb, q_col, k_col, v_col):
    seq = rest.shape[0]
    lc = rest_ctx.shape[0]
    rows = seq // GRID_W
    nsteps = rows // C_QROWS
    nblk = C_KROWS * GRID_W // LANES
    assert nsteps > nblk and C_QROWS * GRID_W == LANES
    bias = _nbr_bias(rpb, rows)

    def first_blk(i):
        return jnp.clip(i - 2, 0, nsteps - nblk)

    def variant(i):
        return jnp.where(i < 2, i, jnp.where(i >= nsteps - 2, i - (nsteps - 2) + 3, 2))

    w = 4 * LANES
    kspecs = [pl.BlockSpec((LANES, w), functools.partial(lambda i, t: (first_blk(i) + t, k_col), t=t))
              for t in range(nblk)]
    vspecs = [pl.BlockSpec((LANES, w), functools.partial(lambda i, t: (first_blk(i) + t, v_col), t=t))
              for t in range(nblk)]
    return pl.pallas_call(
        _attn_c_kernel,
        grid=(nsteps,),
        in_specs=[pl.BlockSpec((LANES, w), lambda i: (i, q_col))] + kspecs + vspecs + [
            pl.BlockSpec((lc, w), lambda i: (0, k_col)),
            pl.BlockSpec((lc, w), lambda i: (0, v_col)),
            pl.BlockSpec((None, C_HEADS * LANES, nblk * LANES), lambda i: (variant(i), 0, 0))],
        out_specs=pl.BlockSpec((LANES, w), lambda i: (i, 0)),
        out_shape=jax.ShapeDtypeStruct((seq, w), BF16),
        compiler_params=_cparams("arbitrary"),
        name="neighborhood_attention",
    )(rest, *([rest] * (2 * nblk)), rest_ctx, rest_ctx, bias)


def _s5_carry_scan(bu, pr, pi, cr, ci, reverse):
    p = bu.shape[1] // 2
    n_tiles = bu.shape[0] // SUBLANES
    edge = 0 if reverse else SUBLANES - 1
    rows_r, rows_i = [None] * n_tiles, [None] * n_tiles
    for step in range(n_tiles):
        tile = (n_tiles - 1 - step) if reverse else step
        xr = bu[tile * SUBLANES:(tile + 1) * SUBLANES, 0:p]
        xi = bu[tile * SUBLANES:(tile + 1) * SUBLANES, p:2 * p]
        xr, xi = xr + pr * cr - pi * ci, xi + pr * ci + pi * cr
        cr, ci = jnp.broadcast_to(xr[edge:edge + 1], xr.shape), jnp.broadcast_to(xi[edge:edge + 1], xi.shape)
        rows_r[tile], rows_i[tile] = xr, xi
    h = jnp.concatenate([jnp.concatenate(rows_r, axis=0), jnp.concatenate(rows_i, axis=0)], axis=1)
    return h, cr, ci


def _s5_kernel(uz_ref, ul_ref, b_ref, c_ref, tab_ref, d_ref, *rest, reverse, final):
    if final:
        yfz_ref, yfl_ref, gw_ref, gb_ref, oz_ref, ol_ref, carry_ref = rest
    else:
        oz_ref, ol_ref, carry_ref = rest
    s = pl.program_id(0)

    @pl.when(s == 0)
    def _():
        carry_ref[...] = jnp.zeros_like(carry_ref)

    nblk = b_ref.shape[0]
    ps = b_ref.shape[2] // 2
    p = nblk * ps

    def chunk(u_ref, yf_ref, o_ref):
        t = u_ref.shape[0]
        row = lax.broadcasted_iota(jnp.int32, (t, 1), 0) & (SUBLANES - 1)
        u = u_ref[...]
        uf = u.astype(F32)
        lagged = [u]
        for k in range(1, SUBLANES):
            if reverse:
                lag = jnp.where(row < SUBLANES - k, pltpu.roll(uf, t - k, 0), 0.0)
            else:
                lag = jnp.where(row >= k, pltpu.roll(uf, k, 0), 0.0)
            lagged.append(lag.astype(BF16))
        outs = []
        for q in range(nblk):
            qs = slice(q * ps, (q + 1) * ps)
            lhs = jnp.concatenate([x[:, q * LANES:(q + 1) * LANES] for x in lagged], axis=1)
            bu = jnp.dot(lhs, b_ref[q], preferred_element_type=F32)
            hq, cr, ci = _s5_carry_scan(bu, tab_ref[0, :, qs], tab_ref[1, :, qs], carry_ref[0, :, qs],
                                        carry_ref[1, :, qs], reverse)
            carry_ref[0, :, qs] = cr
            carry_ref[1, :, qs] = ci
            outs.append(jnp.dot(hq.astype(BF16), c_ref[q], preferred_element_type=F32))
        out = jnp.concatenate(outs, axis=1)
        if not final:
            o_ref[...] = u.astype(F32) * d_ref[...] + out
        else:
            g = _gelu_tanh(yf_ref[...] + out)
            z = jnp.dot(g.astype(BF16), gw_ref[...], preferred_element_type=F32) + gb_ref[...]
            o_ref[...] = (g * _sigmoid(z)).astype(o_ref.dtype)

    @pl.when(s == 0)
    def _():
        chunk(uz_ref, yfz_ref if final else None, oz_ref)

    @pl.when(s > 0)
    def _():
        chunk(ul_ref, yfl_ref if final else None, ol_ref)


def s5_direction(rest, rest_ctx, u_col, params, which, dskip, reverse, fwd=None, glu=None, t=512):
    seq = rest.shape[0]
    tz = rest_ctx.shape[0]
    t = min(t, seq)
    nl = seq // t
    w = 4 * LANES
    bmat, cmat = params[0], params[1]
    tabs = params[3] if reverse else params[2]
    _, nq, kq, ps2 = bmat.shape
    p2 = nq * ps2
    final = fwd is not None
    if reverse:
        lat = lambda s: (jnp.where(s == 0, nl - 1, nl - s), 0)
    else:
        lat = lambda s: (jnp.maximum(s - 1, 0), 0)
    lat_u = lambda s: (lat(s)[0], u_col)
    const = lambda s: (0, 0)
    in_specs = [pl.BlockSpec((tz, w), lambda s: (0, u_col)),
                pl.BlockSpec((t, w), lat_u),
                pl.BlockSpec((None, nq, kq, ps2), lambda s: (which, 0, 0, 0)),
                pl.BlockSpec((None, nq, ps2, LANES), lambda s: (which, 0, 0, 0)),
                pl.BlockSpec((None, 2, SUBLANES, p2 // 2), lambda s: (which, 0, 0, 0)),
                pl.BlockSpec((1, w), const)]
    args = [rest_ctx, rest, bmat, cmat, tabs, dskip.reshape(1, w)]
    if final:
        in_specs += [pl.BlockSpec((tz, w), const), pl.BlockSpec((t, w), lat),
                     pl.BlockSpec((w, w), const), pl.BlockSpec((1, w), const)]
        args += [fwd[0], fwd[1], glu[0], glu[1].reshape(1, w)]
    odt = BF16 if final else F32
    return pl.pallas_call(
        functools.partial(_s5_kernel, reverse=reverse, final=final),
        grid=(nl + 1,),
        in_specs=in_specs,
        out_specs=[pl.BlockSpec((tz, w), const), pl.BlockSpec((t, w), lat)],
        out_shape=[jax.ShapeDtypeStruct((tz, w), odt), jax.ShapeDtypeStruct((seq, w), odt)],
        scratch_shapes=[pltpu.VMEM((2, SUBLANES, p2 // 2), F32)],
        compiler_params=_cparams("arbitrary"),
        name="s5_direction",
    )(*args)


def _s5_params(a_re, a_im, log_step, b_re, b_im, c_re, c_im):
    ar, ai = a_re.astype(F32), a_im.astype(F32)
    step = jnp.exp(log_step.astype(F32))[:, None]
    mag = jnp.exp(ar * step)
    ab_re, ab_im = mag * jnp.cos(ai * step), mag * jnp.sin(ai * step)
    den = ar * ar + ai * ai
    nr = ab_re - 1.0
    f_re = (nr * ar + ab_im * ai) / den
    f_im = (ab_im * ar - nr * ai) / den
    br, bi = b_re.astype(F32), b_im.astype(F32)
    bb_re = f_re[..., None] * br - f_im[..., None] * bi
    bb_im = f_re[..., None] * bi + f_im[..., None] * br
    g, p, h = bb_re.shape
    gq = LANES // h
    nq = g // gq
    eye = jnp.eye(gq, dtype=F32)
    pr, pi = [jnp.ones_like(ab_re), ab_re], [jnp.zeros_like(ab_im), ab_im]
    for _ in range(SUBLANES - 1):
        pr.append(pr[-1] * ab_re - pi[-1] * ab_im)
        pi.append(pr[-2] * ab_im + pi[-1] * ab_re)
    w_re = jnp.stack([pr[k][..., None] * bb_re - pi[k][..., None] * bb_im for k in range(SUBLANES)])
    w_im = jnp.stack([pr[k][..., None] * bb_im + pi[k][..., None] * bb_re for k in range(SUBLANES)])

    def drive(w):
        w = w.reshape(SUBLANES, nq, gq, p, h)
        return jnp.einsum('kqgph,gj->qkghjp', w, eye).reshape(nq, SUBLANES * gq * h, gq * p)

    def readout(c):
        c = c.astype(F32).reshape(nq, gq, h, p)
        return jnp.einsum('qghp,gj->qgpjh', c, eye).reshape(nq, gq * p, gq * h)

    bmat = jnp.concatenate([drive(w_re), drive(w_im)], axis=2).astype(BF16)
    cmat = jnp.concatenate([readout(c_re), -readout(c_im)], axis=1).astype(BF16)
    powers = jnp.stack([jnp.stack([pr[j].reshape(-1) for j in range(1, SUBLANES + 1)]),
                        jnp.stack([pi[j].reshape(-1) for j in range(1, SUBLANES + 1)])])
    return bmat, cmat, powers, powers[:, ::-1]


def _conv_kernel(val_ref, gate_ref, vp_ref, gp_ref, vn_ref, gn_ref, w_ref, b_ref, g_ref, beta_ref, o_ref, hs_ref,
                 *, sub):
    i = pl.program_id(0)
    n = pl.num_programs(0)
    t = val_ref.shape[0]

    def glu(v, g):
        return v[...].astype(F32) * _sigmoid(g[...].astype(F32))

    hs_ref[0:CONV_HALO] = jnp.where(i > 0, glu(vp_ref, gp_ref), 0.0)
    hs_ref[CONV_HALO:CONV_HALO + t] = glu(val_ref, gate_ref)
    hs_ref[CONV_HALO + t:2 * CONV_HALO + t] = jnp.where(i < n - 1, glu(vn_ref, gn_ref), 0.0)
    base = CONV_HALO - CONV_K // 2
    for r0 in range(t // sub):
        acc = jnp.zeros((sub, val_ref.shape[1]), F32) + b_ref[...]
        for r in range(SUBLANES):
            part = None
            for m in range((base + CONV_K - 1) // SUBLANES + 1):
                k = SUBLANES * m + r - base
                if 0 <= k < CONV_K:
                    lo_row = r0 * sub + SUBLANES * m
                    term = hs_ref[lo_row:lo_row + sub + SUBLANES] * w_ref[k:k + 1]
                    part = term if part is None else part + term
            acc = acc + part[r:r + sub]
        mu = jnp.mean(acc, axis=-1, keepdims=True)
        cen = acc - mu
        var = jnp.mean(cen * cen, axis=-1, keepdims=True)
        yn = cen * lax.rsqrt(var + EPS) * g_ref[...] + beta_ref[...]
        o_ref[r0 * sub:(r0 + 1) * sub] = _silu(yn).astype(o_ref.dtype)


def conv_module(rest, val_col, gate_col, w, b, g, beta, t=256, sub=64):
    seq = rest.shape[0]
    t = min(t, seq)
    cw = 4 * LANES
    hb = t // CONV_HALO
    nh = seq // CONV_HALO
    prev = lambda i: jnp.maximum(i * hb - 1, 0)
    nxt = lambda i: jnp.minimum((i + 1) * hb, nh - 1)
    vec = pl.BlockSpec((1, cw), lambda i: (0, 0))
    return pl.pallas_call(
        functools.partial(_conv_kernel, sub=sub),
        grid=(seq // t,),
        in_specs=[pl.BlockSpec((t, cw), lambda i: (i, val_col)),
                  pl.BlockSpec((t, cw), lambda i: (i, gate_col)),
                  pl.BlockSpec((CONV_HALO, cw), lambda i: (prev(i), val_col)),
                  pl.BlockSpec((CONV_HALO, cw), lambda i: (prev(i), gate_col)),
                  pl.BlockSpec((CONV_HALO, cw), lambda i: (nxt(i), val_col)),
                  pl.BlockSpec((CONV_HALO, cw), lambda i: (nxt(i), gate_col)),
                  pl.BlockSpec((CONV_K, cw), lambda i: (0, 0)), vec, vec, vec],
        out_specs=pl.BlockSpec((t, cw), lambda i: (i, 0)),
        out_shape=jax.ShapeDtypeStruct((seq, cw), BF16),
        scratch_shapes=[pltpu.VMEM((t + 2 * CONV_HALO, cw), F32)],
        compiler_params=_cparams("arbitrary"),
        name="conv_module",
    )(rest, rest, rest, rest, rest, rest, w.astype(F32), b.reshape(1, cw).astype(F32),
      g.reshape(1, cw).astype(F32), beta.reshape(1, cw).astype(F32))


def _merge_kernel(ya_ref, yb_ref, yc_ref, yd_ref, g0_ref, g1_ref, g2_ref, g3_ref, wb_ref, wo_ref, x_ref, gate_ref,
                  o_ref, m_ref):
    j = pl.program_id(1)
    nj = m_ref.shape[0]

    @pl.when(j < nj)
    def _():
        m = None
        for y_ref, gl_ref, k in zip((ya_ref, yb_ref, yc_ref, yd_ref), (g0_ref, g1_ref, g2_ref, g3_ref),
                                    range(N_BRANCH)):
            term = _sigmoid(gl_ref[...].astype(F32)) * jnp.dot(y_ref[...], wb_ref[j, k],
                                                                  preferred_element_type=F32)
            m = term if m is None else m + term
        m_ref[j] = m.astype(BF16)

    @pl.when(j >= nj)
    def _():
        merged = jnp.concatenate([m_ref[k] for k in range(nj)], axis=1)
        o_ref[...] = x_ref[...] + gate_ref[...] * jnp.dot(merged, wo_ref[j - nj], preferred_element_type=F32)


def merge_weights(w_branch, w_out, tn=512):
    depth, nb, yw, n = w_branch.shape
    nj = n // tn
    wb = jnp.transpose(w_branch.astype(BF16).reshape(depth, nb, yw, nj, tn), (0, 3, 1, 2, 4))
    wo = jnp.transpose(w_out.astype(BF16).reshape(depth, n, nj, tn), (0, 2, 1, 3))
    return wb, wo


def gated_merge_residual(ys, rest, logit_col, wb, wo, layer, x, gate, tm=512):
    m, yw = ys[0].shape
    _, nj, _, _, tn = wb.shape
    n = nj * tn
    tm = min(tm, m)
    first = lambda j: jnp.minimum(j, nj - 1)
    second = lambda j: jnp.maximum(j - nj, 0)
    yspec = pl.BlockSpec((tm, yw), lambda i, j: (i, 0))
    gspecs = [pl.BlockSpec((tm, tn), functools.partial(lambda i, j, k: (i, logit_col + k * nj + first(j)), k=k))
              for k in range(N_BRANCH)]
    return pl.pallas_call(
        _merge_kernel,
        grid=(m // tm, 2 * nj),
        in_specs=[yspec] * N_BRANCH + gspecs + [
            pl.BlockSpec((None, nj, N_BRANCH, yw, tn), lambda i, j: (layer, 0, 0, 0, 0)),
            pl.BlockSpec((None, nj, n, tn), lambda i, j: (layer, 0, 0, 0)),
            pl.BlockSpec((tm, tn), lambda i, j: (i, second(j))),
            pl.BlockSpec((1, tn), lambda i, j: (0, second(j)))],
        out_specs=pl.BlockSpec((tm, tn), lambda i, j: (i, second(j))),
        out_shape=jax.ShapeDtypeStruct((m, n), F32),
        scratch_shapes=[pltpu.VMEM((nj, tm, tn), BF16)],
        compiler_params=_cparams("arbitrary", "arbitrary"),
        name="gated_merge_residual",
    )(*ys, rest, rest, rest, rest, wb, wo, x, gate.reshape(1, n))


def _mlp_kernel(x_ref, g_ref, sc_ref, sh_ref, gate_ref, w1_ref, w2_ref, *rest, final_norm):
    if final_norm:
        gf_ref, o_ref, h_ref, acc_ref = rest
    else:
        o_ref, h_ref, acc_ref = rest
    f = pl.program_id(1)

    @pl.when(f == 0)
    def _():
        x = x_ref[...]
        y = x * lax.rsqrt(jnp.mean(x * x, axis=-1, keepdims=True) + EPS) * g_ref[...]
        h_ref[...] = (y * (1.0 + sc_ref[...]) + sh_ref[...]).astype(BF16)
        acc_ref[...] = jnp.zeros_like(acc_ref)

    t = jnp.dot(h_ref[...], w1_ref[...], preferred_element_type=F32)
    t = jnp.square(jnp.maximum(t, 0.0)).astype(BF16)
    acc_ref[...] += jnp.dot(t, w2_ref[...], preferred_element_type=F32)

    @pl.when(f == pl.num_programs(1) - 1)
    def _():
        y = x_ref[...] + gate_ref[...] * acc_ref[...]
        if final_norm:
            y = y * lax.rsqrt(jnp.mean(y * y, axis=-1, keepdims=True) + EPS) * gf_ref[...]
        o_ref[...] = y


def mlp_block(x, g, scale, shift, gate, w1, w2, layer, final_g=None, tm=512, tf=1024):
    m, d = x.shape
    ff = w1.shape[2]
    tm = min(tm, m)
    vec = pl.BlockSpec((1, d), lambda i, f: (0, 0))
    final_norm = final_g is not None
    args = [x, g.reshape(1, d), scale.reshape(1, d), shift.reshape(1, d), gate.reshape(1, d), w1, w2]
    in_specs = [pl.BlockSpec((tm, d), lambda i, f: (i, 0)), vec, vec, vec, vec,
                pl.BlockSpec((None, d, tf), lambda i, f: (layer, 0, f)),
                pl.BlockSpec((None, tf, d), lambda i, f: (layer, f, 0))]
    if final_norm:
        args.append(final_g.reshape(1, d))
        in_specs.append(vec)
    return pl.pallas_call(
        functools.partial(_mlp_kernel, final_norm=final_norm),
        grid=(m // tm, ff // tf),
        in_specs=in_specs,
        out_specs=pl.BlockSpec((tm, d), lambda i, f: (i, 0)),
        out_shape=jax.ShapeDtypeStruct((m, d), F32),
        scratch_shapes=[pltpu.VMEM((tm, d), BF16), pltpu.VMEM((tm, d), F32)],
        compiler_params=_cparams("arbitrary", "arbitrary"),
        name="mlp_block",
    )(*args)


_A_COL_K, _A_COL_V = 2, 3
_REST_COL0 = (A_HEADS + 2 * A_KV_HEADS) * HEAD_DIM
_COL_U, _COL_QC, _COL_KC, _COL_VC, _COL_VAL, _COL_GATE, _COL_MERGE = 0, 1, 2, 3, 4, 5, 6


def _rope_tables(seq):
    freqs = ROPE_BASE ** (-jnp.arange(ROPE_PAIRS, dtype=F32) / ROPE_PAIRS)
    rows = seq // GRID_W
    ang_r = jnp.arange(rows).astype(F32)[:, None] * freqs
    ang_c = jnp.arange(GRID_W).astype(F32)[:, None] * freqs

    def expand(tab_r, tab_c, first, second):
        r = jnp.repeat(tab_r, GRID_W, axis=0)
        cc = jnp.tile(tab_c, (rows, 1))
        head = jnp.concatenate([first * r, second * r, first * cc, second * cc], axis=1)
        return jnp.tile(head, (1, LANES // HEAD_DIM))

    cos = expand(jnp.cos(ang_r), jnp.cos(ang_c), 1.0, 1.0)
    sa = expand(jnp.sin(ang_r), jnp.sin(ang_c), 0.0, 1.0)
    sb = expand(jnp.sin(ang_r), jnp.sin(ang_c), -1.0, 0.0)
    return cos, sa, sb


def kernel(x, c, ctx, c_ctx, ada_w, ada_b, norm1_g, norm2_g, w_in, a_sink, s5_a_re, s5_a_im, s5_log_step, s5_b_re, s5_b_im, s5_c_re, s5_c_im, s5_d, s5_glu_w, s5_glu_b, c_rpb, d_conv_w, d_conv_b, d_ln_g, d_ln_b, w_branch, w_out, mlp_w1, mlp_w2, final_g):
    depth = ada_w.shape[0]
    _, seq, d = x.shape
    lc = ctx.shape[1]
    xs = x.reshape(seq, d)
    zs = ctx.reshape(lc, d)
    rope = _rope_tables(seq)

    cc = jnp.zeros((SUBLANES, d), F32).at[0].set(c[0]).at[1].set(c_ctx)
    mods = ada_modulation(cc, ada_w, ada_b)
    n_rest = w_in.shape[2] - _REST_COL0
    w1, w2 = mlp_w1.astype(BF16), mlp_w2.astype(BF16)
    wb, wo = merge_weights(w_branch, w_out)
    s5_params = jax.vmap(_s5_params)(*(a.reshape((2 * depth,) + a.shape[2:]) for a in
                                       (s5_a_re, s5_a_im, s5_log_step, s5_b_re, s5_b_im, s5_c_re, s5_c_im)))

    for l in range(depth):
        need_ctx = l < depth - 1
        mx = [mods[l, 0, k * d:(k + 1) * d] for k in range(6)]
        mz = [mods[l, 1, k * d:(k + 1) * d] for k in range(6)]

        hx, qk_x = norm_head_projection(xs, norm1_g[l], mx[1], mx[0], w_in, l, rope=rope)
        hz, qk_z = norm_head_projection(zs, norm1_g[l], mz[1], mz[0], w_in, l)
        r_x = wide_projection(hx, w_in, l, _REST_COL0, n_rest)
        r_z = wide_projection(hz, w_in, l, _REST_COL0, n_rest)

        ya_x = window_attention(a_sink[l], qk_x, qk_x, qk_z, qk_z, _A_COL_V)
        yc_x = neighborhood_attention(r_x, r_z, c_rpb[l], _COL_QC, _COL_KC, _COL_VC)
        yd_x = conv_module(r_x, _COL_VAL, _COL_GATE, d_conv_w[l], d_conv_b[l], d_ln_g[l], d_ln_b[l])

        yf = s5_direction(r_x, r_z, _COL_U, s5_params, 2 * l, s5_d[l].astype(F32), reverse=False)
        yb_z, yb_x = s5_direction(r_x, r_z, _COL_U, s5_params, 2 * l + 1, s5_d[l].astype(F32), reverse=True, fwd=yf,
                                  glu=(s5_glu_w[l].astype(BF16), s5_glu_b[l].astype(F32)))

        xs = gated_merge_residual((ya_x, yb_x, yc_x, yd_x), r_x, _COL_MERGE, wb, wo, l, xs, mx[2])
        xs = mlp_block(xs, norm2_g[l], mx[4], mx[3], mx[5], w1, w2, l, final_g=None if need_ctx else final_g)

        if need_ctx:
            ya_z = context_attention(a_sink[l], qk_z, 0, qk_z, _A_COL_K, 2 * LANES, qk_z, _A_COL_V, True, True)
            yc_z = context_attention(a_sink[l], r_z, _COL_QC, r_z, _COL_KC, 4 * LANES, r_z, _COL_VC, False, False)
            yd_z = conv_module(r_z, _COL_VAL, _COL_GATE, d_conv_w[l], d_conv_b[l], d_ln_g[l], d_ln_b[l])
            zs = gated_merge_residual((ya_z, yb_z, yc_z, yd_z), r_z, _COL_MERGE, wb, wo, l, zs, mz[2])
            zs = mlp_block(zs, norm2_g[l], mz[4], mz[3], mz[5], w1, w2, l)

    return xs.reshape(x.shape)
```

```python
import functools
import math

import numpy as np
import jax
import jax.numpy as jnp
from jax import lax
from jax.experimental import pallas as pl
from jax.experimental.pallas import tpu as pltpu

F32 = jnp.float32
BF16 = jnp.bfloat16

LANES = 128
SUBLANES = 8
VMEM_LIMIT = 56 * 1024 * 1024

GRID_W = 64
HEAD_DIM = 64
N_BRANCH = 4
A_HEADS = 8
A_KV_HEADS = 2
A_WINDOW = 128
A_BLOCK = 128
ROPE_BASE = 10000.0
ROPE_PAIRS = HEAD_DIM // 4
S5_GROUP_CH = 16
S5_STATE = 64
C_HEADS = 8
C_WIN_R = 8
C_WIN_C = 16
C_QROWS = 2
C_KROWS = C_QROWS + C_WIN_R
CONV_K = 31
CONV_HALO = 16
EPS = 1e-6
NEG_INF = -1e30


def _cparams(*sem):
    return pltpu.CompilerParams(dimension_semantics=sem, vmem_limit_bytes=VMEM_LIMIT)


def _sigmoid(x):
    return 0.5 * jnp.tanh(0.5 * x) + 0.5


def _silu(x):
    return x * _sigmoid(x)


def _gelu_tanh(x):
    return 0.5 * x * (1.0 + jnp.tanh(math.sqrt(2.0 / math.pi) * (x + 0.044715 * (x * x * x))))


def _ada_kernel(c_ref, w_ref, b_ref, o_ref):
    a = _silu(c_ref[...]).astype(BF16)
    o_ref[...] = jnp.dot(a, w_ref[...].astype(BF16), preferred_element_type=F32) + b_ref[...]


def ada_modulation(cc, ada_w, ada_b, tn=1024):
    depth, d, n = ada_w.shape
    return pl.pallas_call(
        _ada_kernel,
        grid=(depth, n // tn),
        in_specs=[pl.BlockSpec((SUBLANES, d), lambda l, j: (0, 0)),
                  pl.BlockSpec((None, d, tn), lambda l, j: (l, 0, j)),
                  pl.BlockSpec((None, 1, tn), lambda l, j: (l, 0, j))],
        out_specs=pl.BlockSpec((None, SUBLANES, tn), lambda l, j: (l, 0, j)),
        out_shape=jax.ShapeDtypeStruct((depth, SUBLANES, n), F32),
        compiler_params=_cparams("arbitrary", "arbitrary"),
        name="ada_modulation",
    )(cc, ada_w, ada_b.reshape(depth, 1, n))


def _rope_block(y, cos, sa, sb):
    return y * cos + pltpu.roll(y, 16, 1) * sa + pltpu.roll(y, LANES - 16, 1) * sb


def _head_proj_kernel(x_ref, g_ref, sc_ref, sh_ref, w_ref, *rest, n_q, with_rope):
    if with_rope:
        cos_ref, sa_ref, sb_ref, h_ref, o_ref, wb_ref = rest
        cos, sa, sb = cos_ref[...], sa_ref[...], sb_ref[...]
    else:
        h_ref, o_ref, wb_ref = rest

    @pl.when(pl.program_id(0) == 0)
    def _():
        wb_ref[...] = w_ref[0].astype(BF16)

    x = x_ref[...]
    y = x * lax.rsqrt(jnp.mean(x * x, axis=-1, keepdims=True) + EPS) * g_ref[...]
    h = (y * (1.0 + sc_ref[...]) + sh_ref[...]).astype(BF16)
    h_ref[...] = h
    acc = jnp.dot(h, wb_ref[...], preferred_element_type=F32)
    lo = _lane_lo()
    out = 0
    for b in range(acc.shape[1] // LANES):
        y = acc[:, b * LANES:(b + 1) * LANES]
        if with_rope and b <= n_q:
            y = _rope_block(y, cos, sa, sb)
        if b < n_q:
            blocks = [y]
        else:
            swapped = pltpu.roll(y, HEAD_DIM, 1)
            blocks = [jnp.where(lo, y, swapped), jnp.where(lo, swapped, y)]
        for blk in blocks:
            o_ref[:, out * LANES:(out + 1) * LANES] = blk.astype(o_ref.dtype)
            out += 1


def norm_head_projection(x, g, scale, shift, w_in, layer, rope=None, tm=256):
    m, k = x.shape
    n_in = (A_HEADS + 2 * A_KV_HEADS) * HEAD_DIM
    n_q = A_HEADS * HEAD_DIM // LANES
    assert A_KV_HEADS * HEAD_DIM == LANES
    n_out = n_in + 2 * LANES
    tm = min(tm, m)
    vec = pl.BlockSpec((1, k), lambda i: (0, 0))
    in_specs = [pl.BlockSpec((tm, k), lambda i: (i, 0)), vec, vec, vec,
                pl.BlockSpec((pl.Element(1), pl.Element(k), pl.Element(n_in)), lambda i: (layer, 0, 0))]
    args = [x, g.reshape(1, k), scale.reshape(1, k), shift.reshape(1, k), w_in]
    if rope is not None:
        in_specs += [pl.BlockSpec((tm, LANES), lambda i: (i, 0))] * 3
        args += list(rope)
    return pl.pallas_call(
        functools.partial(_head_proj_kernel, n_q=n_q, with_rope=rope is not None),
        grid=(m // tm,),
        in_specs=in_specs,
        out_specs=[pl.BlockSpec((tm, k), lambda i: (i, 0)), pl.BlockSpec((tm, n_out), lambda i: (i, 0))],
        out_shape=[jax.ShapeDtypeStruct((m, k), BF16), jax.ShapeDtypeStruct((m, n_out), BF16)],
        scratch_shapes=[pltpu.VMEM((k, n_in), BF16)],
        compiler_params=_cparams("arbitrary"),
        name="norm_head_projection",
    )(*args)


def _wide_proj_kernel(a_ref, w_ref, o_ref, wb_ref):
    @pl.when(pl.program_id(1) == 0)
    def _():
        wb_ref[...] = w_ref[0].astype(BF16)

    o_ref[...] = jnp.dot(a_ref[...], wb_ref[...], preferred_element_type=F32).astype(o_ref.dtype)


def wide_projection(a, w, layer, col0, n, tm=1024, tn=1024):
    m, k = a.shape
    tm = min(tm, m)
    assert n % tn == 0 and col0 % LANES == 0 and tn % LANES == 0
    return pl.pallas_call(
        _wide_proj_kernel,
        grid=(n // tn, m // tm),
        in_specs=[pl.BlockSpec((tm, k), lambda j, i: (i, 0)),
                  pl.BlockSpec((pl.Element(1), pl.Element(k), pl.Element(tn)),
                               lambda j, i: (layer, 0, pl.multiple_of(col0 + j * tn, LANES)))],
        out_specs=pl.BlockSpec((tm, tn), lambda j, i: (i, j)),
        out_shape=jax.ShapeDtypeStruct((m, n), BF16),
        scratch_shapes=[pltpu.VMEM((k, tn), BF16)],
        compiler_params=_cparams("arbitrary", "arbitrary"),
        name="wide_projection",
    )(a, w)


def _lane_lo():
    return lax.broadcasted_iota(jnp.int32, (1, LANES), 1) < HEAD_DIM


def _half_select(x, lo, half):
    zero = jnp.zeros_like(x)
    return jnp.where(lo, x, zero) if half == 0 else jnp.where(lo, zero, x)


def _dot_nt(a, b):
    return lax.dot_general(a, b, (((1,), (1,)), ((), ())), preferred_element_type=F32)


def _softmax_pv(parts, values, sink=None):
    m = parts[0].max(axis=-1, keepdims=True)
    for s in parts[1:]:
        m = jnp.maximum(m, s.max(axis=-1, keepdims=True))
    if sink is not None:
        m = jnp.maximum(m, sink)
    den = jnp.exp(sink - m) if sink is not None else 0.0
    out = 0.0
    for s, v in zip(parts, values):
        p = jnp.exp(s - m)
        den = den + p.sum(axis=-1, keepdims=True)
        out = out + jnp.dot(p.astype(BF16), v, preferred_element_type=F32)
    return out / den


def _attn_a_kernel(sink_ref, q_ref, kp_ref, kc_ref, kn_ref, vp_ref, vc_ref, vn_ref, kz_ref, vz_ref, o_ref,
                   *, seq):
    i = pl.program_id(0)
    lo = _lane_lo()
    nq = 4 * A_BLOCK
    r = lax.broadcasted_iota(jnp.int32, (nq, 3 * A_BLOCK), 0) & (A_BLOCK - 1)
    c = lax.broadcasted_iota(jnp.int32, (nq, 3 * A_BLOCK), 1)
    kabs = c + (i - 1) * A_BLOCK
    valid = ((c - r) >= 0) & ((c - r) <= 2 * A_WINDOW) & (kabs >= 0) & (kabs < seq)
    rowblk = lax.broadcasted_iota(jnp.int32, (nq, 1), 0) // A_BLOCK
    for g in range(A_KV_HEADS):
        gs = slice(g * LANES, (g + 1) * LANES)
        kw = jnp.concatenate([kp_ref[:, gs], kc_ref[:, gs], kn_ref[:, gs]], axis=0)
        vw = jnp.concatenate([vp_ref[:, gs], vc_ref[:, gs], vn_ref[:, gs]], axis=0)
        kz, vz = kz_ref[:, gs], vz_ref[:, gs]
        blocks = [slice((2 * g + bb) * LANES, (2 * g + bb + 1) * LANES) for bb in range(2)]
        ql = jnp.concatenate([_half_select(q_ref[:, bs], lo, half) * (HEAD_DIM ** -0.5)
                              for half in range(2) for bs in blocks], axis=0)
        heads = [2 * (2 * g + bb) + half for half in range(2) for bb in range(2)]
        sink = jnp.full((nq, 1), sink_ref[heads[3]], F32)
        for k in range(3):
            sink = jnp.where(rowblk == k, sink_ref[heads[k]], sink)
        s = jnp.where(valid, _dot_nt(ql, kw), NEG_INF)
        sz = _dot_nt(ql, kz)
        m = jnp.maximum(jnp.maximum(s.max(axis=-1, keepdims=True), sz.max(axis=-1, keepdims=True)), sink)
        p, pz = jnp.exp(s - m), jnp.exp(sz - m)
        inv = 1.0 / (p.sum(axis=-1, keepdims=True) + pz.sum(axis=-1, keepdims=True) + jnp.exp(sink - m))
        p, pz = p.astype(BF16), pz.astype(BF16)
        out = None
        for half in range(2):
            hs = slice(half * 2 * A_BLOCK, (half + 1) * 2 * A_BLOCK)
            o = (jnp.dot(p[hs], _half_select(vw, lo, half), preferred_element_type=F32)
                 + jnp.dot(pz[hs], _half_select(vz, lo, half), preferred_element_type=F32)) * inv[hs]
            out = o if out is None else out + o
        for bb in range(2):
            o_ref[:, blocks[bb]] = out[bb * A_BLOCK:(bb + 1) * A_BLOCK].astype(o_ref.dtype)


def window_attention(sink, qk, rest, qk_ctx, rest_ctx, vv_col):
    seq = qk.shape[0]
    lc = qk_ctx.shape[0]
    nb = seq // A_BLOCK
    kw = 2 * LANES
    prev = lambda i: jnp.maximum(i - 1, 0)
    nxt = lambda i: jnp.minimum(i + 1, nb - 1)
    return pl.pallas_call(
        functools.partial(_attn_a_kernel, seq=seq),
        grid=(nb,),
        in_specs=[pl.BlockSpec(memory_space=pltpu.SMEM),
                  pl.BlockSpec((A_BLOCK, 4 * LANES), lambda i: (i, 0)),
                  pl.BlockSpec((A_BLOCK, kw), lambda i: (prev(i), 2)),
                  pl.BlockSpec((A_BLOCK, kw), lambda i: (i, 2)),
                  pl.BlockSpec((A_BLOCK, kw), lambda i: (nxt(i), 2)),
                  pl.BlockSpec((A_BLOCK, kw), lambda i: (prev(i), vv_col)),
                  pl.BlockSpec((A_BLOCK, kw), lambda i: (i, vv_col)),
                  pl.BlockSpec((A_BLOCK, kw), lambda i: (nxt(i), vv_col)),
                  pl.BlockSpec((lc, kw), lambda i: (0, 2)),
                  pl.BlockSpec((lc, kw), lambda i: (0, vv_col))],
        out_specs=pl.BlockSpec((A_BLOCK, 4 * LANES), lambda i: (i, 0)),
        out_shape=jax.ShapeDtypeStruct((seq, 4 * LANES), BF16),
        compiler_params=_cparams("arbitrary"),
        name="window_attention",
    )(sink, qk, qk, qk, qk, rest, rest, rest, qk_ctx, rest_ctx)


def _ctx_attn_kernel(sink_ref, q_ref, k_ref, v_ref, o_ref, *, kv_shared, has_sink):
    lo = _lane_lo()
    for b in range(q_ref.shape[1] // LANES):
        bs = slice(b * LANES, (b + 1) * LANES)
        kb = b // 2 if kv_shared else b
        ks = slice(kb * LANES, (kb + 1) * LANES)
        qb, k, v = q_ref[:, bs], k_ref[:, ks], v_ref[:, ks]
        acc = jnp.zeros((q_ref.shape[0], LANES), F32)
        for half in range(2):
            ql = _half_select(qb, lo, half) * (HEAD_DIM ** -0.5)
            sink = sink_ref[2 * b + half] if has_sink else None
            acc = acc + _softmax_pv([_dot_nt(ql, k)], [_half_select(v, lo, half)], sink=sink)
        o_ref[:, bs] = acc.astype(o_ref.dtype)


def context_attention(sink, q_arr, q_col, k_arr, k_col, k_w, v_arr, v_col, kv_shared, has_sink):
    lc = q_arr.shape[0]
    return pl.pallas_call(
        functools.partial(_ctx_attn_kernel, kv_shared=kv_shared, has_sink=has_sink),
        grid=(1,),
        in_specs=[pl.BlockSpec(memory_space=pltpu.SMEM),
                  pl.BlockSpec((lc, 4 * LANES), lambda i: (0, q_col)),
                  pl.BlockSpec((lc, k_w), lambda i: (0, k_col)),
                  pl.BlockSpec((lc, k_w), lambda i: (0, v_col))],
        out_specs=pl.BlockSpec((lc, 4 * LANES), lambda i: (0, 0)),
        out_shape=jax.ShapeDtypeStruct((lc, 4 * LANES), BF16),
        compiler_params=_cparams("arbitrary"),
        name="context_attention",
    )(sink, q_arr, k_arr, v_arr)


def _nbr_row_offsets(rows):
    nsteps = rows // C_QROWS
    nblk = C_KROWS * GRID_W // LANES
    wr = min(C_WIN_R, rows)
    reps = [0, 1, 2, nsteps - 2, nsteps - 1]
    dr = -np.ones((len(reps), C_QROWS, C_KROWS), np.int32)
    for v, i in enumerate(reps):
        w0 = C_QROWS * int(np.clip(i - 2, 0, nsteps - nblk))
        for a in range(C_QROWS):
            rq = C_QROWS * i + a
            k0 = int(np.clip(rq - C_WIN_R // 2, 0, rows - wr))
            for j in range(C_KROWS):
                if k0 <= w0 + j < k0 + wr:
                    dr[v, a, j] = w0 + j - rq + C_WIN_R - 1
    return dr


def _nbr_bias(rpb, rows):
    h = rpb.shape[0]
    qc = np.arange(GRID_W)[:, None]
    kc = np.arange(GRID_W)[None, :]
    cstart = np.clip(qc - C_WIN_C // 2, 0, GRID_W - C_WIN_C)
    col_ok = jnp.asarray((kc >= cstart) & (kc < cstart + C_WIN_C))
    pad = GRID_W - C_WIN_C
    ext = jnp.pad(rpb.astype(F32), ((0, 0), (0, 0), (pad, pad)), mode='edge')
    toep = jnp.stack([ext[:, :, GRID_W - 1 - q:2 * GRID_W - 1 - q] for q in range(GRID_W)], axis=2)
    toep = jnp.where(col_ok, toep, NEG_INF)
    masked = jnp.full((h, GRID_W, GRID_W), NEG_INF, F32)
    dr = _nbr_row_offsets(rows)
    variants = []
    for v in range(dr.shape[0]):
        qrows = [jnp.concatenate([toep[:, dr[v, a, j]] if dr[v, a, j] >= 0 else masked for j in range(C_KROWS)],
                                 axis=-1) for a in range(C_QROWS)]
        variants.append(jnp.concatenate(qrows, axis=-2).reshape(h * C_QROWS * GRID_W, C_KROWS * GRID_W))
    return jnp.stack(variants)


def _attn_c_kernel(q_ref, *refs):
    nblk = C_KROWS * GRID_W // LANES
    k_refs, v_refs = refs[:nblk], refs[nblk:2 * nblk]
    kz_ref, vz_ref, bias_ref, o_ref = refs[2 * nblk:]
    lo = _lane_lo()
    nq = q_ref.shape[0]
    nb = C_HEADS // 2
    blocks = [slice(b * LANES, (b + 1) * LANES) for b in range(nb)]
    s_parts, sz_parts = [], []
    for bs in blocks:
        ql = jnp.concatenate([_half_select(q_ref[:, bs], lo, half) * (HEAD_DIM ** -0.5) for half in range(2)], axis=0)
        s_parts.append(_dot_nt(ql, jnp.concatenate([kr[:, bs] for kr in k_refs], axis=0)))
        sz_parts.append(_dot_nt(ql, kz_ref[:, bs]))
    s = jnp.concatenate(s_parts, axis=0) + bias_ref[...]
    sz = jnp.concatenate(sz_parts, axis=0)
    m = jnp.maximum(s.max(axis=-1, keepdims=True), sz.max(axis=-1, keepdims=True))
    p, pz = jnp.exp(s - m), jnp.exp(sz - m)
    inv = 1.0 / (p.sum(axis=-1, keepdims=True) + pz.sum(axis=-1, keepdims=True))
    p, pz = p.astype(BF16), pz.astype(BF16)
    for b, bs in enumerate(blocks):
        vw = jnp.concatenate([vr[:, bs] for vr in v_refs], axis=0)
        vz = vz_ref[:, bs]
        out = None
        for half in range(2):
            hs = slice((2 * b + half) * nq, (2 * b + half + 1) * nq)
            o = (jnp.dot(p[hs], _half_select(vw, lo, half), preferred_element_type=F32)
                 + jnp.dot(pz[hs], _half_select(vz, lo, half), preferred_element_type=F32)) * inv[hs]
            out = o if out is None else out + o
        o_ref[:, bs] = out.astype(o_ref.dtype)


def neighborhood_attention(rest, rest_ctx, rpb, q_col, k_col, v_col):
    seq = rest.shape[0]
    lc = rest_ctx.shape[0]
    rows = seq // GRID_W
    nsteps = rows // C_QROWS
    nblk = C_KROWS * GRID_W // LANES
    assert nsteps > nblk and C_QROWS * GRID_W == LANES
    bias = _nbr_bias(rpb, rows)

    def first_blk(i):
        return jnp.clip(i - 2, 0, nsteps - nblk)

    def variant(i):
        return jnp.where(i < 2, i, jnp.where(i >= nsteps - 2, i - (nsteps - 2) + 3, 2))

    w = 4 * LANES
    kspecs = [pl.BlockSpec((LANES, w), functools.partial(lambda i, t: (first_blk(i) + t, k_col), t=t))
              for t in range(nblk)]
    vspecs = [pl.BlockSpec((LANES, w), functools.partial(lambda i, t: (first_blk(i) + t, v_col), t=t))
              for t in range(nblk)]
    return pl.pallas_call(
        _attn_c_kernel,
        grid=(nsteps,),
        in_specs=[pl.BlockSpec((LANES, w), lambda i: (i, q_col))] + kspecs + vspecs + [
            pl.BlockSpec((lc, w), lambda i: (0, k_col)),
            pl.BlockSpec((lc, w), lambda i: (0, v_col)),
            pl.BlockSpec((None, C_HEADS * LANES, nblk * LANES), lambda i: (variant(i), 0, 0))],
        out_specs=pl.BlockSpec((LANES, w), lambda i: (i, 0)),
        out_shape=jax.ShapeDtypeStruct((seq, w), BF16),
        compiler_params=_cparams("arbitrary"),
        name="neighborhood_attention",
    )(rest, *([rest] * (2 * nblk)), rest_ctx, rest_ctx, bias)


def _s5_window_scan(sr, si, tabs, cr, ci, reverse):
    n_tiles = sr.shape[0] // SUBLANES
    edge = 0 if reverse else SUBLANES - 1
    out_r, out_i = [None] * n_tiles, [None] * n_tiles
    for step in range(n_tiles):
        tile = (n_tiles - 1 - step) if reverse else step
        xr = sr[tile * SUBLANES:(tile + 1) * SUBLANES]
        xi = si[tile * SUBLANES:(tile + 1) * SUBLANES]
        for lvl, k in enumerate((1, 2, 4)):
            shift = (SUBLANES - k) if reverse else k
            rr, ri = pltpu.roll(xr, shift, 0), pltpu.roll(xi, shift, 0)
            mr, mi = tabs[2 * lvl], tabs[2 * lvl + 1]
            xr, xi = xr + mr * rr - mi * ri, xi + mr * ri + mi * rr
        pr, pi = tabs[6], tabs[7]
        xr, xi = xr + pr * cr - pi * ci, xi + pr * ci + pi * cr
        cr, ci = jnp.broadcast_to(xr[edge:edge + 1], xr.shape), jnp.broadcast_to(xi[edge:edge + 1], xi.shape)
        out_r[tile], out_i[tile] = xr, xi
    return jnp.concatenate(out_r, axis=0), jnp.concatenate(out_i, axis=0), cr, ci


def _s5_kernel(uz_ref, ul_ref, b_ref, k_ref, c_ref, tab_ref, d_ref, *rest, reverse, final):
    if final:
        yfz_ref, yfl_ref, gw_ref, gb_ref, oz_ref, ol_ref, carry_ref, uf_ref, yc_ref = rest
    else:
        oz_ref, ol_ref, carry_ref, uf_ref, yc_ref = rest
    s = pl.program_id(0)

    @pl.when(s == 0)
    def _():
        carry_ref[...] = jnp.zeros_like(carry_ref)

    nblk = b_ref.shape[0]
    ps = b_ref.shape[2] // 2

    def chunk(u_ref, yf_ref, o_ref):
        t = u_ref.shape[0]
        nw = t // SUBLANES
        row = lax.broadcasted_iota(jnp.int32, (t, 1), 0) & (SUBLANES - 1)
        wrow = lax.broadcasted_iota(jnp.int32, (nw, 1), 0)
        u = u_ref[...]
        uf = u.astype(F32)
        for q in range(nblk):
            uf_ref[q, 0:t] = uf[:, q * LANES:(q + 1) * LANES]
        lagged = [u]
        for k in range(1, SUBLANES):
            if reverse:
                lag = jnp.where(row < SUBLANES - k, pltpu.roll(uf, t - k, 0), 0.0)
            else:
                lag = jnp.where(row >= k, pltpu.roll(uf, k, 0), 0.0)
            lagged.append(lag.astype(BF16))
        outs = []
        for q in range(nblk):
            qs = slice(q * ps, (q + 1) * ps)
            lhs = jnp.concatenate([x[:, q * LANES:(q + 1) * LANES] for x in lagged], axis=1)
            y_loc = jnp.dot(lhs, k_ref[q], preferred_element_type=F32)
            ends = [uf_ref[q, pl.ds(k if reverse else SUBLANES - 1 - k, nw, stride=SUBLANES), :].astype(BF16)
                    for k in range(SUBLANES)]
            sw = jnp.dot(jnp.concatenate(ends, axis=1), b_ref[q], preferred_element_type=F32)
            tabs = [tab_ref[i, :, qs] for i in range(8)]
            cin_r, cin_i = carry_ref[0, :, qs], carry_ref[1, :, qs]
            c_r, c_i, cr, ci = _s5_window_scan(sw[:, :ps], sw[:, ps:], tabs, cin_r, cin_i, reverse)
            carry_ref[0, :, qs] = cr
            carry_ref[1, :, qs] = ci
            if reverse:
                e_r = jnp.where(wrow == nw - 1, cin_r[0:1], pltpu.roll(c_r, nw - 1, 0))
                e_i = jnp.where(wrow == nw - 1, cin_i[0:1], pltpu.roll(c_i, nw - 1, 0))
            else:
                e_r = jnp.where(wrow == 0, cin_r[0:1], pltpu.roll(c_r, 1, 0))
                e_i = jnp.where(wrow == 0, cin_i[0:1], pltpu.roll(c_i, 1, 0))
            y_in = jnp.dot(jnp.concatenate([e_r, e_i], axis=1).astype(BF16), c_ref[q],
                           preferred_element_type=F32)
            for j in range(SUBLANES):
                yc_ref[q, pl.ds(j, nw, stride=SUBLANES), :] = y_in[:, j * LANES:(j + 1) * LANES]
            outs.append(y_loc + yc_ref[q, 0:t])
        out = jnp.concatenate(outs, axis=1)
        if not final:
            o_ref[...] = u.astype(F32) * d_ref[...] + out
        else:
            g = _gelu_tanh(yf_ref[...] + out)
            z = jnp.dot(g.astype(BF16), gw_ref[...], preferred_element_type=F32) + gb_ref[...]
            o_ref[...] = (g * _sigmoid(z)).astype(o_ref.dtype)

    @pl.when(s == 0)
    def _():
        chunk(uz_ref, yfz_ref if final else None, oz_ref)

    @pl.when(s > 0)
    def _():
        chunk(ul_ref, yfl_ref if final else None, ol_ref)


def s5_direction(rest, rest_ctx, u_col, params, which, dskip, reverse, fwd=None, glu=None, t=1024):
    seq = rest.shape[0]
    tz = rest_ctx.shape[0]
    t = min(t, seq)
    nl = seq // t
    w = 4 * LANES
    bmat, kmat, cmat, tabs = params
    _, nq, kq, ps2 = bmat.shape
    p2 = nq * ps2
    final = fwd is not None
    if reverse:
        lat = lambda s: (jnp.where(s == 0, nl - 1, nl - s), 0)
    else:
        lat = lambda s: (jnp.maximum(s - 1, 0), 0)
    lat_u = lambda s: (lat(s)[0], u_col)
    const = lambda s: (0, 0)
    table = lambda shape: pl.BlockSpec((None,) + shape, lambda s: (which,) + (0,) * len(shape),
                                       pipeline_mode=pl.Buffered(1))
    in_specs = [pl.BlockSpec((tz, w), lambda s: (0, u_col)),
                pl.BlockSpec((t, w), lat_u),
                table((nq, kq, ps2)),
                table((nq, kq, LANES)),
                table((nq, ps2, SUBLANES * LANES)),
                table((8, SUBLANES, p2 // 2)),
                pl.BlockSpec((1, w), const)]
    args = [rest_ctx, rest, bmat, kmat, cmat, tabs, dskip.reshape(1, w)]
    if final:
        in_specs += [pl.BlockSpec((tz, w), const), pl.BlockSpec((t, w), lat),
                     pl.BlockSpec((w, w), const), pl.BlockSpec((1, w), const)]
        args += [fwd[0], fwd[1], glu[0], glu[1].reshape(1, w)]
    odt = BF16 if final else F32
    return pl.pallas_call(
        functools.partial(_s5_kernel, reverse=reverse, final=final),
        grid=(nl + 1,),
        in_specs=in_specs,
        out_specs=[pl.BlockSpec((tz, w), const), pl.BlockSpec((t, w), lat)],
        out_shape=[jax.ShapeDtypeStruct((tz, w), odt), jax.ShapeDtypeStruct((seq, w), odt)],
        scratch_shapes=[pltpu.VMEM((2, SUBLANES, p2 // 2), F32),
                        pltpu.VMEM((nq, max(t, tz), LANES), F32),
                        pltpu.VMEM((nq, max(t, tz), LANES), F32)],
        compiler_params=_cparams("arbitrary"),
        name="s5_direction",
    )(*args)


def _s5_params(a_re, a_im, log_step, b_re, b_im, c_re, c_im, rev):
    ar, ai = a_re.astype(F32), a_im.astype(F32)
    step = jnp.exp(log_step.astype(F32))[:, None]
    mag = jnp.exp(ar * step)
    ab_re, ab_im = mag * jnp.cos(ai * step), mag * jnp.sin(ai * step)
    den = ar * ar + ai * ai
    nr = ab_re - 1.0
    f_re = (nr * ar + ab_im * ai) / den
    f_im = (ab_im * ar - nr * ai) / den
    br, bi = b_re.astype(F32), b_im.astype(F32)
    bb_re = f_re[..., None] * br - f_im[..., None] * bi
    bb_im = f_re[..., None] * bi + f_im[..., None] * br
    cr, ci = c_re.astype(F32), c_im.astype(F32)
    g, p, h = bb_re.shape
    gq = LANES // h
    nq = g // gq
    nl = SUBLANES

    def cmul(xr, xi, yr, yi):
        return xr * yr - xi * yi, xr * yi + xi * yr

    def powers(xr, xi, n):
        out = [(jnp.ones_like(xr), jnp.zeros_like(xi)), (xr, xi)]
        for _ in range(n - 1):
            out.append(cmul(*out[-1], xr, xi))
        return jnp.stack([o[0] for o in out]), jnp.stack([o[1] for o in out])

    def spread(x, per):
        tiled = jnp.tile(x, (1,) * (x.ndim - 1) + (gq,))
        block = (np.arange(gq * per)[None, None, :] // per) == np.arange(gq)[:, None, None]
        return jnp.where(jnp.asarray(block), tiled, 0.0)

    pr, pi = powers(ab_re, ab_im, nl)
    w_re = pr[:nl, ..., None] * bb_re - pi[:nl, ..., None] * bb_im
    w_im = pr[:nl, ..., None] * bb_im + pi[:nl, ..., None] * bb_re

    def rows_lag_group_channel(x):
        return jnp.transpose(x, (1, 0, 2, 3, 4)).reshape(nq, nl * gq * h, x.shape[-1])

    def drive(w):
        wt = jnp.swapaxes(w, 2, 3).reshape(nl, nq, gq, h, p)
        return rows_lag_group_channel(spread(wt, p))

    bmat = jnp.concatenate([drive(w_re), drive(w_im)], axis=2).astype(BF16)
    hp = lax.Precision.HIGHEST
    kk = (jnp.einsum('kgph,gop->kgho', w_re, cr, precision=hp)
          - jnp.einsum('kgph,gop->kgho', w_im, ci, precision=hp))
    kmat = rows_lag_group_channel(spread(kk.reshape(nl, nq, gq, h, h), h)).astype(BF16)
    dr = jnp.where(rev, pr[nl:0:-1], pr[1:nl + 1])
    di = jnp.where(rev, pi[nl:0:-1], pi[1:nl + 1])
    cd_re = cr[None] * dr[:, :, None, :] - ci[None] * di[:, :, None, :]
    cd_im = cr[None] * di[:, :, None, :] + ci[None] * dr[:, :, None, :]

    def readout(cd):
        x = jnp.transpose(cd.reshape(nl, nq, gq, h, p), (1, 2, 4, 0, 3))
        tiled = jnp.tile(x, (1, 1, 1, 1, gq))
        block = (np.arange(gq * h)[None, :] // h) == np.arange(gq)[:, None]
        return jnp.where(jnp.asarray(block)[None, :, None, None, :], tiled, 0.0).reshape(nq, gq * p, nl * gq * h)

    cmat = jnp.concatenate([readout(cd_re), -readout(cd_im)], axis=1).astype(BF16)
    qr, qi = powers(pr[nl].reshape(-1), pi[nl].reshape(-1), SUBLANES)
    row = np.arange(SUBLANES)[:, None]
    tabs = []
    for k in (1, 2, 4):
        keep = jnp.where(rev, jnp.asarray(row < SUBLANES - k), jnp.asarray(row >= k))
        tabs += [jnp.where(keep, qr[k][None], 0.0), jnp.where(keep, qi[k][None], 0.0)]
    tabs += [jnp.where(rev, qr[SUBLANES:0:-1], qr[1:SUBLANES + 1]),
             jnp.where(rev, qi[SUBLANES:0:-1], qi[1:SUBLANES + 1])]
    return bmat, kmat, cmat, jnp.stack(tabs)


def _conv_kernel(val_ref, gate_ref, vp_ref, gp_ref, vn_ref, gn_ref, w_ref, b_ref, g_ref, beta_ref, o_ref, hs_ref,
                 *, sub):
    i = pl.program_id(0)
    n = pl.num_programs(0)
    t = val_ref.shape[0]

    def glu(v, g):
        return v[...].astype(F32) * _sigmoid(g[...].astype(F32))

    hs_ref[0:CONV_HALO] = jnp.where(i > 0, glu(vp_ref, gp_ref), 0.0)
    hs_ref[CONV_HALO:CONV_HALO + t] = glu(val_ref, gate_ref)
    hs_ref[CONV_HALO + t:2 * CONV_HALO + t] = jnp.where(i < n - 1, glu(vn_ref, gn_ref), 0.0)
    base = CONV_HALO - CONV_K // 2
    for r0 in range(t // sub):
        acc = jnp.zeros((sub, val_ref.shape[1]), F32) + b_ref[...]
        for r in range(SUBLANES):
            part = None
            for m in range((base + CONV_K - 1) // SUBLANES + 1):
                k = SUBLANES * m + r - base
                if 0 <= k < CONV_K:
                    lo_row = r0 * sub + SUBLANES * m
                    term = hs_ref[lo_row:lo_row + sub + SUBLANES] * w_ref[k:k + 1]
                    part = term if part is None else part + term
            acc = acc + part[r:r + sub]
        mu = jnp.mean(acc, axis=-1, keepdims=True)
        cen = acc - mu
        var = jnp.mean(cen * cen, axis=-1, keepdims=True)
        yn = cen * lax.rsqrt(var + EPS) * g_ref[...] + beta_ref[...]
        o_ref[r0 * sub:(r0 + 1) * sub] = _silu(yn).astype(o_ref.dtype)


def conv_module(rest, val_col, gate_col, w, b, g, beta, t=256, sub=64):
    seq = rest.shape[0]
    t = min(t, seq)
    cw = 4 * LANES
    hb = t // CONV_HALO
    nh = seq // CONV_HALO
    prev = lambda i: jnp.maximum(i * hb - 1, 0)
    nxt = lambda i: jnp.minimum((i + 1) * hb, nh - 1)
    vec = pl.BlockSpec((1, cw), lambda i: (0, 0))
    return pl.pallas_call(
        functools.partial(_conv_kernel, sub=sub),
        grid=(seq // t,),
        in_specs=[pl.BlockSpec((t, cw), lambda i: (i, val_col)),
                  pl.BlockSpec((t, cw), lambda i: (i, gate_col)),
                  pl.BlockSpec((CONV_HALO, cw), lambda i: (prev(i), val_col)),
                  pl.BlockSpec((CONV_HALO, cw), lambda i: (prev(i), gate_col)),
                  pl.BlockSpec((CONV_HALO, cw), lambda i: (nxt(i), val_col)),
                  pl.BlockSpec((CONV_HALO, cw), lambda i: (nxt(i), gate_col)),
                  pl.BlockSpec((CONV_K, cw), lambda i: (0, 0)), vec, vec, vec],
        out_specs=pl.BlockSpec((t, cw), lambda i: (i, 0)),
        out_shape=jax.ShapeDtypeStruct((seq, cw), BF16),
        scratch_shapes=[pltpu.VMEM((t + 2 * CONV_HALO, cw), F32)],
        compiler_params=_cparams("arbitrary"),
        name="conv_module",
    )(rest, rest, rest, rest, rest, rest, w.astype(F32), b.reshape(1, cw).astype(F32),
      g.reshape(1, cw).astype(F32), beta.reshape(1, cw).astype(F32))


def _merge_kernel(ya_ref, yb_ref, yc_ref, yd_ref, g0_ref, g1_ref, g2_ref, g3_ref, wb_ref, wo_ref, x_ref, gate_ref,
                  o_ref, m_ref):
    j = pl.program_id(1)
    nj = m_ref.shape[0]

    @pl.when(j < nj)
    def _():
        m = None
        for y_ref, gl_ref, k in zip((ya_ref, yb_ref, yc_ref, yd_ref), (g0_ref, g1_ref, g2_ref, g3_ref),
                                    range(N_BRANCH)):
            term = _sigmoid(gl_ref[...].astype(F32)) * jnp.dot(y_ref[...], wb_ref[j, k],
                                                                  preferred_element_type=F32)
            m = term if m is None else m + term
        m_ref[j] = m.astype(BF16)

    @pl.when(j >= nj)
    def _():
        merged = jnp.concatenate([m_ref[k] for k in range(nj)], axis=1)
        o_ref[...] = x_ref[...] + gate_ref[...] * jnp.dot(merged, wo_ref[j - nj], preferred_element_type=F32)


def merge_weights(w_branch, w_out, tn=512):
    depth, nb, yw, n = w_branch.shape
    nj = n // tn
    wb = jnp.transpose(w_branch.astype(BF16).reshape(depth, nb, yw, nj, tn), (0, 3, 1, 2, 4))
    wo = jnp.transpose(w_out.astype(BF16).reshape(depth, n, nj, tn), (0, 2, 1, 3))
    return wb, wo


def gated_merge_residual(ys, rest, logit_col, wb, wo, layer, x, gate, tm=512):
    m, yw = ys[0].shape
    _, nj, _, _, tn = wb.shape
    n = nj * tn
    tm = min(tm, m)
    first = lambda j: jnp.minimum(j, nj - 1)
    second = lambda j: jnp.maximum(j - nj, 0)
    yspec = pl.BlockSpec((tm, yw), lambda i, j: (i, 0))
    gspecs = [pl.BlockSpec((tm, tn), functools.partial(lambda i, j, k: (i, logit_col + k * nj + first(j)), k=k))
              for k in range(N_BRANCH)]
    return pl.pallas_call(
        _merge_kernel,
        grid=(m // tm, 2 * nj),
        in_specs=[yspec] * N_BRANCH + gspecs + [
            pl.BlockSpec((None, nj, N_BRANCH, yw, tn), lambda i, j: (layer, 0, 0, 0, 0)),
            pl.BlockSpec((None, nj, n, tn), lambda i, j: (layer, 0, 0, 0)),
            pl.BlockSpec((tm, tn), lambda i, j: (i, second(j))),
            pl.BlockSpec((1, tn), lambda i, j: (0, second(j)))],
        out_specs=pl.BlockSpec((tm, tn), lambda i, j: (i, second(j))),
        out_shape=jax.ShapeDtypeStruct((m, n), F32),
        scratch_shapes=[pltpu.VMEM((nj, tm, tn), BF16)],
        compiler_params=_cparams("arbitrary", "arbitrary"),
        name="gated_merge_residual",
    )(*ys, rest, rest, rest, rest, wb, wo, x, gate.reshape(1, n))


def _mlp_kernel(x_ref, g_ref, sc_ref, sh_ref, gate_ref, w1_ref, w2_ref, *rest, final_norm):
    if final_norm:
        gf_ref, o_ref, h_ref, acc_ref = rest
    else:
        o_ref, h_ref, acc_ref = rest
    f = pl.program_id(1)

    @pl.when(f == 0)
    def _():
        x = x_ref[...]
        y = x * lax.rsqrt(jnp.mean(x * x, axis=-1, keepdims=True) + EPS) * g_ref[...]
        h_ref[...] = (y * (1.0 + sc_ref[...]) + sh_ref[...]).astype(BF16)
        acc_ref[...] = jnp.zeros_like(acc_ref)

    t = jnp.dot(h_ref[...], w1_ref[...], preferred_element_type=F32)
    t = jnp.square(jnp.maximum(t, 0.0)).astype(BF16)
    acc_ref[...] += jnp.dot(t, w2_ref[...], preferred_element_type=F32)

    @pl.when(f == pl.num_programs(1) - 1)
    def _():
        y = x_ref[...] + gate_ref[...] * acc_ref[...]
        if final_norm:
            y = y * lax.rsqrt(jnp.mean(y * y, axis=-1, keepdims=True) + EPS) * gf_ref[...]
        o_ref[...] = y


def mlp_block(x, g, scale, shift, gate, w1, w2, layer, final_g=None, tm=512, tf=1024):
    m, d = x.shape
    ff = w1.shape[2]
    tm = min(tm, m)
    vec = pl.BlockSpec((1, d), lambda i, f: (0, 0))
    final_norm = final_g is not None
    args = [x, g.reshape(1, d), scale.reshape(1, d), shift.reshape(1, d), gate.reshape(1, d), w1, w2]
    in_specs = [pl.BlockSpec((tm, d), lambda i, f: (i, 0)), vec, vec, vec, vec,
                pl.BlockSpec((None, d, tf), lambda i, f: (layer, 0, f)),
                pl.BlockSpec((None, tf, d), lambda i, f: (layer, f, 0))]
    if final_norm:
        args.append(final_g.reshape(1, d))
        in_specs.append(vec)
    return pl.pallas_call(
        functools.partial(_mlp_kernel, final_norm=final_norm),
        grid=(m // tm, ff // tf),
        in_specs=in_specs,
        out_specs=pl.BlockSpec((tm, d), lambda i, f: (i, 0)),
        out_shape=jax.ShapeDtypeStruct((m, d), F32),
        scratch_shapes=[pltpu.VMEM((tm, d), BF16), pltpu.VMEM((tm, d), F32)],
        compiler_params=_cparams("arbitrary", "arbitrary"),
        name="mlp_block",
    )(*args)


_A_COL_K, _A_COL_V = 2, 3
_REST_COL0 = (A_HEADS + 2 * A_KV_HEADS) * HEAD_DIM
_COL_U, _COL_QC, _COL_KC, _COL_VC, _COL_VAL, _COL_GATE, _COL_MERGE = 0, 1, 2, 3, 4, 5, 6


def _rope_tables(seq):
    freqs = ROPE_BASE ** (-jnp.arange(ROPE_PAIRS, dtype=F32) / ROPE_PAIRS)
    rows = seq // GRID_W
    ang_r = jnp.arange(rows).astype(F32)[:, None] * freqs
    ang_c = jnp.arange(GRID_W).astype(F32)[:, None] * freqs

    def expand(tab_r, tab_c, first, second):
        r = jnp.repeat(tab_r, GRID_W, axis=0)
        cc = jnp.tile(tab_c, (rows, 1))
        head = jnp.concatenate([first * r, second * r, first * cc, second * cc], axis=1)
        return jnp.tile(head, (1, LANES // HEAD_DIM))

    cos = expand(jnp.cos(ang_r), jnp.cos(ang_c), 1.0, 1.0)
    sa = expand(jnp.sin(ang_r), jnp.sin(ang_c), 0.0, 1.0)
    sb = expand(jnp.sin(ang_r), jnp.sin(ang_c), -1.0, 0.0)
    return cos, sa, sb


def kernel(x, c, ctx, c_ctx, ada_w, ada_b, norm1_g, norm2_g, w_in, a_sink, s5_a_re, s5_a_im, s5_log_step, s5_b_re, s5_b_im, s5_c_re, s5_c_im, s5_d, s5_glu_w, s5_glu_b, c_rpb, d_conv_w, d_conv_b, d_ln_g, d_ln_b, w_branch, w_out, mlp_w1, mlp_w2, final_g):
    depth = ada_w.shape[0]
    _, seq, d = x.shape
    lc = ctx.shape[1]
    xs = x.reshape(seq, d)
    zs = ctx.reshape(lc, d)
    rope = _rope_tables(seq)

    cc = jnp.zeros((SUBLANES, d), F32).at[0].set(c[0]).at[1].set(c_ctx)
    mods = ada_modulation(cc, ada_w, ada_b)
    n_rest = w_in.shape[2] - _REST_COL0
    w1, w2 = mlp_w1.astype(BF16), mlp_w2.astype(BF16)
    wb, wo = merge_weights(w_branch, w_out)
    s5_params = jax.vmap(_s5_params)(*(a.reshape((2 * depth,) + a.shape[2:]) for a in
                                       (s5_a_re, s5_a_im, s5_log_step, s5_b_re, s5_b_im, s5_c_re, s5_c_im)),
                                     jnp.arange(2 * depth) % 2 == 1)

    for l in range(depth):
        need_ctx = l < depth - 1
        mx = [mods[l, 0, k * d:(k + 1) * d] for k in range(6)]
        mz = [mods[l, 1, k * d:(k + 1) * d] for k in range(6)]

        hx, qk_x = norm_head_projection(xs, norm1_g[l], mx[1], mx[0], w_in, l, rope=rope)
        hz, qk_z = norm_head_projection(zs, norm1_g[l], mz[1], mz[0], w_in, l)
        r_x = wide_projection(hx, w_in, l, _REST_COL0, n_rest)
        r_z = wide_projection(hz, w_in, l, _REST_COL0, n_rest)

        ya_x = window_attention(a_sink[l], qk_x, qk_x, qk_z, qk_z, _A_COL_V)
        yc_x = neighborhood_attention(r_x, r_z, c_rpb[l], _COL_QC, _COL_KC, _COL_VC)
        yd_x = conv_module(r_x, _COL_VAL, _COL_GATE, d_conv_w[l], d_conv_b[l], d_ln_g[l], d_ln_b[l])

        yf = s5_direction(r_x, r_z, _COL_U, s5_params, 2 * l, s5_d[l].astype(F32), reverse=False)
        yb_z, yb_x = s5_direction(r_x, r_z, _COL_U, s5_params, 2 * l + 1, s5_d[l].astype(F32), reverse=True, fwd=yf,
                                  glu=(s5_glu_w[l].astype(BF16), s5_glu_b[l].astype(F32)))

        xs = gated_merge_residual((ya_x, yb_x, yc_x, yd_x), r_x, _COL_MERGE, wb, wo, l, xs, mx[2])
        xs = mlp_block(xs, norm2_g[l], mx[4], mx[3], mx[5], w1, w2, l, final_g=None if need_ctx else final_g)

        if need_ctx:
            ya_z = context_attention(a_sink[l], qk_z, 0, qk_z, _A_COL_K, 2 * LANES, qk_z, _A_COL_V, True, True)
            yc_z = context_attention(a_sink[l], r_z, _COL_QC, r_z, _COL_KC, 4 * LANES, r_z, _COL_VC, False, False)
            yd_z = conv_module(r_z, _COL_VAL, _COL_GATE, d_conv_w[l], d_conv_b[l], d_ln_g[l], d_ln_b[l])
            zs = gated_merge_residual((ya_z, yb_z, yc_z, yd_z), r_z, _COL_MERGE, wb, wo, l, zs, mz[2])
            zs = mlp_block(zs, norm2_g[l], mz[4], mz[3], mz[5], w1, w2, l)

    return xs.reshape(x.shape)
```

```python
import functools
import math

import numpy as np
import jax
import jax.numpy as jnp
from jax import lax
from jax.experimental import pallas as pl
from jax.experimental.pallas import tpu as pltpu

F32 = jnp.float32
BF16 = jnp.bfloat16

LANES = 128
SUBLANES = 8
VMEM_LIMIT = 56 * 1024 * 1024

GRID_W = 64
HEAD_DIM = 64
N_BRANCH = 4
A_HEADS = 8
A_KV_HEADS = 2
A_WINDOW = 128
A_BLOCK = 128
ROPE_BASE = 10000.0
ROPE_PAIRS = HEAD_DIM // 4
S5_GROUP_CH = 16
S5_STATE = 64
C_HEADS = 8
C_WIN_R = 8
C_WIN_C = 16
C_QROWS = 2
C_KROWS = C_QROWS + C_WIN_R
CONV_K = 31
CONV_HALO = 16
EPS = 1e-6
NEG_INF = -1e30


def _cparams(*sem):
    return pltpu.CompilerParams(dimension_semantics=sem, vmem_limit_bytes=VMEM_LIMIT)


def _sigmoid(x):
    return 0.5 * jnp.tanh(0.5 * x) + 0.5


def _silu(x):
    return x * _sigmoid(x)


def _gelu_tanh(x):
    return 0.5 * x * (1.0 + jnp.tanh(math.sqrt(2.0 / math.pi) * (x + 0.044715 * (x * x * x))))


def _ada_kernel(c_ref, w_ref, b_ref, o_ref):
    a = _silu(c_ref[...]).astype(BF16)
    o_ref[...] = jnp.dot(a, w_ref[...].astype(BF16), preferred_element_type=F32) + b_ref[...]


def ada_modulation(cc, ada_w, ada_b, tn=1024):
    depth, d, n = ada_w.shape
    return pl.pallas_call(
        _ada_kernel,
        grid=(depth, n // tn),
        in_specs=[pl.BlockSpec((SUBLANES, d), lambda l, j: (0, 0)),
                  pl.BlockSpec((None, d, tn), lambda l, j: (l, 0, j)),
                  pl.BlockSpec((None, 1, tn), lambda l, j: (l, 0, j))],
        out_specs=pl.BlockSpec((None, SUBLANES, tn), lambda l, j: (l, 0, j)),
        out_shape=jax.ShapeDtypeStruct((depth, SUBLANES, n), F32),
        compiler_params=_cparams("arbitrary", "arbitrary"),
        name="ada_modulation",
    )(cc, ada_w, ada_b.reshape(depth, 1, n))


def _rope_block(y, cos, sa, sb):
    return y * cos + pltpu.roll(y, 16, 1) * sa + pltpu.roll(y, LANES - 16, 1) * sb


def _head_proj_kernel(x_ref, g_ref, sc_ref, sh_ref, w_ref, *rest, n_q, with_rope):
    if with_rope:
        cos_ref, sa_ref, sb_ref, h_ref, o_ref, wb_ref = rest
        cos, sa, sb = cos_ref[...], sa_ref[...], sb_ref[...]
    else:
        h_ref, o_ref, wb_ref = rest

    @pl.when(pl.program_id(0) == 0)
    def _():
        wb_ref[...] = w_ref[0].astype(BF16)

    x = x_ref[...]
    y = x * lax.rsqrt(jnp.mean(x * x, axis=-1, keepdims=True) + EPS) * g_ref[...]
    h = (y * (1.0 + sc_ref[...]) + sh_ref[...]).astype(BF16)
    h_ref[...] = h
    acc = jnp.dot(h, wb_ref[...], preferred_element_type=F32)
    lo = _lane_lo()
    out = 0
    for b in range(acc.shape[1] // LANES):
        y = acc[:, b * LANES:(b + 1) * LANES]
        if with_rope and b <= n_q:
            y = _rope_block(y, cos, sa, sb)
        if b < n_q:
            blocks = [y]
        else:
            swapped = pltpu.roll(y, HEAD_DIM, 1)
            blocks = [jnp.where(lo, y, swapped), jnp.where(lo, swapped, y)]
        for blk in blocks:
            o_ref[:, out * LANES:(out + 1) * LANES] = blk.astype(o_ref.dtype)
            out += 1


def norm_head_projection(x, g, scale, shift, w_in, layer, rope=None, tm=256):
    m, k = x.shape
    n_in = (A_HEADS + 2 * A_KV_HEADS) * HEAD_DIM
    n_q = A_HEADS * HEAD_DIM // LANES
    assert A_KV_HEADS * HEAD_DIM == LANES
    n_out = n_in + 2 * LANES
    tm = min(tm, m)
    vec = pl.BlockSpec((1, k), lambda i: (0, 0))
    in_specs = [pl.BlockSpec((tm, k), lambda i: (i, 0)), vec, vec, vec,
                pl.BlockSpec((pl.Element(1), pl.Element(k), pl.Element(n_in)), lambda i: (layer, 0, 0))]
    args = [x, g.reshape(1, k), scale.reshape(1, k), shift.reshape(1, k), w_in]
    if rope is not None:
        in_specs += [pl.BlockSpec((tm, LANES), lambda i: (i, 0))] * 3
        args += list(rope)
    return pl.pallas_call(
        functools.partial(_head_proj_kernel, n_q=n_q, with_rope=rope is not None),
        grid=(m // tm,),
        in_specs=in_specs,
        out_specs=[pl.BlockSpec((tm, k), lambda i: (i, 0)), pl.BlockSpec((tm, n_out), lambda i: (i, 0))],
        out_shape=[jax.ShapeDtypeStruct((m, k), BF16), jax.ShapeDtypeStruct((m, n_out), BF16)],
        scratch_shapes=[pltpu.VMEM((k, n_in), BF16)],
        compiler_params=_cparams("arbitrary"),
        name="norm_head_projection",
    )(*args)


def _wide_proj_kernel(a_ref, w_ref, o_ref, wb_ref):
    @pl.when(pl.program_id(1) == 0)
    def _():
        wb_ref[...] = w_ref[0].astype(BF16)

    o_ref[...] = jnp.dot(a_ref[...], wb_ref[...], preferred_element_type=F32).astype(o_ref.dtype)


def wide_projection(a, w, layer, col0, n, tm=1024, tn=1024):
    m, k = a.shape
    tm = min(tm, m)
    assert n % tn == 0 and col0 % LANES == 0 and tn % LANES == 0
    return pl.pallas_call(
        _wide_proj_kernel,
        grid=(n // tn, m // tm),
        in_specs=[pl.BlockSpec((tm, k), lambda j, i: (i, 0)),
                  pl.BlockSpec((pl.Element(1), pl.Element(k), pl.Element(tn)),
                               lambda j, i: (layer, 0, pl.multiple_of(col0 + j * tn, LANES)))],
        out_specs=pl.BlockSpec((tm, tn), lambda j, i: (i, j)),
        out_shape=jax.ShapeDtypeStruct((m, n), BF16),
        scratch_shapes=[pltpu.VMEM((k, tn), BF16)],
        compiler_params=_cparams("arbitrary", "arbitrary"),
        name="wide_projection",
    )(a, w)


def _lane_lo():
    return lax.broadcasted_iota(jnp.int32, (1, LANES), 1) < HEAD_DIM


def _half_select(x, lo, half):
    zero = jnp.zeros_like(x)
    return jnp.where(lo, x, zero) if half == 0 else jnp.where(lo, zero, x)


def _dot_nt(a, b):
    return lax.dot_general(a, b, (((1,), (1,)), ((), ())), preferred_element_type=F32)


def _softmax_pv(parts, values, sink=None):
    m = parts[0].max(axis=-1, keepdims=True)
    for s in parts[1:]:
        m = jnp.maximum(m, s.max(axis=-1, keepdims=True))
    if sink is not None:
        m = jnp.maximum(m, sink)
    den = jnp.exp(sink - m) if sink is not None else 0.0
    out = 0.0
    for s, v in zip(parts, values):
        p = jnp.exp(s - m)
        den = den + p.sum(axis=-1, keepdims=True)
        out = out + jnp.dot(p.astype(BF16), v, preferred_element_type=F32)
    return out / den


def _attn_a_kernel(sink_ref, q_ref, kp_ref, kc_ref, kn_ref, vp_ref, vc_ref, vn_ref, kz_ref, vz_ref, o_ref,
                   *, seq):
    i = pl.program_id(0)
    lo = _lane_lo()
    nq = 4 * A_BLOCK
    r = lax.broadcasted_iota(jnp.int32, (nq, 3 * A_BLOCK), 0) & (A_BLOCK - 1)
    c = lax.broadcasted_iota(jnp.int32, (nq, 3 * A_BLOCK), 1)
    kabs = c + (i - 1) * A_BLOCK
    valid = ((c - r) >= 0) & ((c - r) <= 2 * A_WINDOW) & (kabs >= 0) & (kabs < seq)
    rowblk = lax.broadcasted_iota(jnp.int32, (nq, 1), 0) // A_BLOCK
    for g in range(A_KV_HEADS):
        gs = slice(g * LANES, (g + 1) * LANES)
        kw = jnp.concatenate([kp_ref[:, gs], kc_ref[:, gs], kn_ref[:, gs]], axis=0)
        vw = jnp.concatenate([vp_ref[:, gs], vc_ref[:, gs], vn_ref[:, gs]], axis=0)
        kz, vz = kz_ref[:, gs], vz_ref[:, gs]
        blocks = [slice((2 * g + bb) * LANES, (2 * g + bb + 1) * LANES) for bb in range(2)]
        ql = jnp.concatenate([_half_select(q_ref[:, bs], lo, half) * (HEAD_DIM ** -0.5)
                              for half in range(2) for bs in blocks], axis=0)
        heads = [2 * (2 * g + bb) + half for half in range(2) for bb in range(2)]
        sink = jnp.full((nq, 1), sink_ref[heads[3]], F32)
        for k in range(3):
            sink = jnp.where(rowblk == k, sink_ref[heads[k]], sink)
        s = jnp.where(valid, _dot_nt(ql, kw), NEG_INF)
        sz = _dot_nt(ql, kz)
        m = jnp.maximum(jnp.maximum(s.max(axis=-1, keepdims=True), sz.max(axis=-1, keepdims=True)), sink)
        p, pz = jnp.exp(s - m), jnp.exp(sz - m)
        inv = 1.0 / (p.sum(axis=-1, keepdims=True) + pz.sum(axis=-1, keepdims=True) + jnp.exp(sink - m))
        p, pz = p.astype(BF16), pz.astype(BF16)
        out = None
        for half in range(2):
            hs = slice(half * 2 * A_BLOCK, (half + 1) * 2 * A_BLOCK)
            o = (jnp.dot(p[hs], _half_select(vw, lo, half), preferred_element_type=F32)
                 + jnp.dot(pz[hs], _half_select(vz, lo, half), preferred_element_type=F32)) * inv[hs]
            out = o if out is None else out + o
        for bb in range(2):
            o_ref[:, blocks[bb]] = out[bb * A_BLOCK:(bb + 1) * A_BLOCK].astype(o_ref.dtype)


def window_attention(sink, qk, rest, qk_ctx, rest_ctx, vv_col):
    seq = qk.shape[0]
    lc = qk_ctx.shape[0]
    nb = seq // A_BLOCK
    kw = 2 * LANES
    prev = lambda i: jnp.maximum(i - 1, 0)
    nxt = lambda i: jnp.minimum(i + 1, nb - 1)
    return pl.pallas_call(
        functools.partial(_attn_a_kernel, seq=seq),
        grid=(nb,),
        in_specs=[pl.BlockSpec(memory_space=pltpu.SMEM),
                  pl.BlockSpec((A_BLOCK, 4 * LANES), lambda i: (i, 0)),
                  pl.BlockSpec((A_BLOCK, kw), lambda i: (prev(i), 2)),
                  pl.BlockSpec((A_BLOCK, kw), lambda i: (i, 2)),
                  pl.BlockSpec((A_BLOCK, kw), lambda i: (nxt(i), 2)),
                  pl.BlockSpec((A_BLOCK, kw), lambda i: (prev(i), vv_col)),
                  pl.BlockSpec((A_BLOCK, kw), lambda i: (i, vv_col)),
                  pl.BlockSpec((A_BLOCK, kw), lambda i: (nxt(i), vv_col)),
                  pl.BlockSpec((lc, kw), lambda i: (0, 2)),
                  pl.BlockSpec((lc, kw), lambda i: (0, vv_col))],
        out_specs=pl.BlockSpec((A_BLOCK, 4 * LANES), lambda i: (i, 0)),
        out_shape=jax.ShapeDtypeStruct((seq, 4 * LANES), BF16),
        compiler_params=_cparams("arbitrary"),
        name="window_attention",
    )(sink, qk, qk, qk, qk, rest, rest, rest, qk_ctx, rest_ctx)


def _ctx_attn_kernel(sink_ref, q_ref, k_ref, v_ref, o_ref, *, kv_shared, has_sink):
    lo = _lane_lo()
    for b in range(q_ref.shape[1] // LANES):
        bs = slice(b * LANES, (b + 1) * LANES)
        kb = b // 2 if kv_shared else b
        ks = slice(kb * LANES, (kb + 1) * LANES)
        qb, k, v = q_ref[:, bs], k_ref[:, ks], v_ref[:, ks]
        acc = jnp.zeros((q_ref.shape[0], LANES), F32)
        for half in range(2):
            ql = _half_select(qb, lo, half) * (HEAD_DIM ** -0.5)
            sink = sink_ref[2 * b + half] if has_sink else None
            acc = acc + _softmax_pv([_dot_nt(ql, k)], [_half_select(v, lo, half)], sink=sink)
        o_ref[:, bs] = acc.astype(o_ref.dtype)


def context_attention(sink, q_arr, q_col, k_arr, k_col, k_w, v_arr, v_col, kv_shared, has_sink):
    lc = q_arr.shape[0]
    return pl.pallas_call(
        functools.partial(_ctx_attn_kernel, kv_shared=kv_shared, has_sink=has_sink),
        grid=(1,),
        in_specs=[pl.BlockSpec(memory_space=pltpu.SMEM),
                  pl.BlockSpec((lc, 4 * LANES), lambda i: (0, q_col)),
                  pl.BlockSpec((lc, k_w), lambda i: (0, k_col)),
                  pl.BlockSpec((lc, k_w), lambda i: (0, v_col))],
        out_specs=pl.BlockSpec((lc, 4 * LANES), lambda i: (0, 0)),
        out_shape=jax.ShapeDtypeStruct((lc, 4 * LANES), BF16),
        compiler_params=_cparams("arbitrary"),
        name="context_attention",
    )(sink, q_arr, k_arr, v_arr)


def _nbr_row_offsets(rows):
    nsteps = rows // C_QROWS
    nblk = C_KROWS * GRID_W // LANES
    wr = min(C_WIN_R, rows)
    reps = [0, 1, 2, nsteps - 2, nsteps - 1]
    dr = -np.ones((len(reps), C_QROWS, C_KROWS), np.int32)
    for v, i in enumerate(reps):
        w0 = C_QROWS * int(np.clip(i - 2, 0, nsteps - nblk))
        for a in range(C_QROWS):
            rq = C_QROWS * i + a
            k0 = int(np.clip(rq - C_WIN_R // 2, 0, rows - wr))
            for j in range(C_KROWS):
                if k0 <= w0 + j < k0 + wr:
                    dr[v, a, j] = w0 + j - rq + C_WIN_R - 1
    return dr


def _nbr_bias(rpb, rows):
    h = rpb.shape[0]
    qc = np.arange(GRID_W)[:, None]
    kc = np.arange(GRID_W)[None, :]
    cstart = np.clip(qc - C_WIN_C // 2, 0, GRID_W - C_WIN_C)
    col_ok = jnp.asarray((kc >= cstart) & (kc < cstart + C_WIN_C))
    pad = GRID_W - C_WIN_C
    ext = jnp.pad(rpb.astype(F32), ((0, 0), (0, 0), (pad, pad)), mode='edge')
    toep = jnp.stack([ext[:, :, GRID_W - 1 - q:2 * GRID_W - 1 - q] for q in range(GRID_W)], axis=2)
    toep = jnp.where(col_ok, toep, NEG_INF)
    masked = jnp.full((h, GRID_W, GRID_W), NEG_INF, F32)
    dr = _nbr_row_offsets(rows)
    variants = []
    for v in range(dr.shape[0]):
        qrows = [jnp.concatenate([toep[:, dr[v, a, j]] if dr[v, a, j] >= 0 else masked for j in range(C_KROWS)],
                                 axis=-1) for a in range(C_QROWS)]
        variants.append(jnp.concatenate(qrows, axis=-2).reshape(h * C_QROWS * GRID_W, C_KROWS * GRID_W))
    return jnp.stack(variants)


def _attn_c_kernel(q_ref, *refs):
    nblk = C_KROWS * GRID_W // LANES
    k_refs, v_refs = refs[:nblk], refs[nblk:2 * nblk]
    kz_ref, vz_ref, bias_ref, o_ref = refs[2 * nblk:]
    lo = _lane_lo()
    nq = q_ref.shape[0]
    nb = C_HEADS // 2
    blocks = [slice(b * LANES, (b + 1) * LANES) for b in range(nb)]
    s_parts, sz_parts = [], []
    for bs in blocks:
        ql = jnp.concatenate([_half_select(q_ref[:, bs], lo, half) * (HEAD_DIM ** -0.5) for half in range(2)], axis=0)
        s_parts.append(_dot_nt(ql, jnp.concatenate([kr[:, bs] for kr in k_refs], axis=0)))
        sz_parts.append(_dot_nt(ql, kz_ref[:, bs]))
    s = jnp.concatenate(s_parts, axis=0) + bias_ref[...]
    sz = jnp.concatenate(sz_parts, axis=0)
    m = jnp.maximum(s.max(axis=-1, keepdims=True), sz.max(axis=-1, keepdims=True))
    p, pz = jnp.exp(s - m), jnp.exp(sz - m)
    inv = 1.0 / (p.sum(axis=-1, keepdims=True) + pz.sum(axis=-1, keepdims=True))
    p, pz = p.astype(BF16), pz.astype(BF16)
    for b, bs in enumerate(blocks):
        vw = jnp.concatenate([vr[:, bs] for vr in v_refs], axis=0)
        vz = vz_ref[:, bs]
        out = None
        for half in range(2):
            hs = slice((2 * b + half) * nq, (2 * b + half + 1) * nq)
            o = (jnp.dot(p[hs], _half_select(vw, lo, half), preferred_element_type=F32)
                 + jnp.dot(pz[hs], _half_select(vz, lo, half), preferred_element_type=F32)) * inv[hs]
            out = o if out is None else out + o
        o_ref[:, bs] = out.astype(o_ref.dtype)


def neighborhood_attention(rest, rest_ctx, rpb, q_col, k_col, v_col):
    seq = rest.shape[0]
    lc = rest_ctx.shape[0]
    rows = seq // GRID_W
    nsteps = rows // C_QROWS
    nblk = C_KROWS * GRID_W // LANES
    assert nsteps > nblk and C_QROWS * GRID_W == LANES
    bias = _nbr_bias(rpb, rows)

    def first_blk(i):
        return jnp.clip(i - 2, 0, nsteps - nblk)

    def variant(i):
        return jnp.where(i < 2, i, jnp.where(i >= nsteps - 2, i - (nsteps - 2) + 3, 2))

    w = 4 * LANES
    kspecs = [pl.BlockSpec((LANES, w), functools.partial(lambda i, t: (first_blk(i) + t, k_col), t=t))
              for t in range(nblk)]
    vspecs = [pl.BlockSpec((LANES, w), functools.partial(lambda i, t: (first_blk(i) + t, v_col), t=t))
              for t in range(nblk)]
    return pl.pallas_call(
        _attn_c_kernel,
        grid=(nsteps,),
        in_specs=[pl.BlockSpec((LANES, w), lambda i: (i, q_col))] + kspecs + vspecs + [
            pl.BlockSpec((lc, w), lambda i: (0, k_col)),
            pl.BlockSpec((lc, w), lambda i: (0, v_col)),
            pl.BlockSpec((None, C_HEADS * LANES, nblk * LANES), lambda i: (variant(i), 0, 0))],
        out_specs=pl.BlockSpec((LANES, w), lambda i: (i, 0)),
        out_shape=jax.ShapeDtypeStruct((seq, w), BF16),
        compiler_params=_cparams("arbitrary"),
        name="neighborhood_attention",
    )(rest, *([rest] * (2 * nblk)), rest_ctx, rest_ctx, bias)


def _s5_window_scan(sr, si, tabs, cr, ci, reverse):
    n_tiles = sr.shape[0] // SUBLANES
    edge = 0 if reverse else SUBLANES - 1
    out_r, out_i = [None] * n_tiles, [None] * n_tiles
    for step in range(n_tiles):
        tile = (n_tiles - 1 - step) if reverse else step
        xr = sr[tile * SUBLANES:(tile + 1) * SUBLANES]
        xi = si[tile * SUBLANES:(tile + 1) * SUBLANES]
        for lvl, k in enumerate((1, 2, 4)):
            shift = (SUBLANES - k) if reverse else k
            rr, ri = pltpu.roll(xr, shift, 0), pltpu.roll(xi, shift, 0)
            mr, mi = tabs[2 * lvl], tabs[2 * lvl + 1]
            xr, xi = xr + mr * rr - mi * ri, xi + mr * ri + mi * rr
        pr, pi = tabs[6], tabs[7]
        xr, xi = xr + pr * cr - pi * ci, xi + pr * ci + pi * cr
        cr, ci = jnp.broadcast_to(xr[edge:edge + 1], xr.shape), jnp.broadcast_to(xi[edge:edge + 1], xi.shape)
        out_r[tile], out_i[tile] = xr, xi
    return jnp.concatenate(out_r, axis=0), jnp.concatenate(out_i, axis=0), cr, ci


def _s5_kernel(uz_ref, ul_ref, wc_ref, kc_ref, cc_ref, ew_ref, ek_ref, ec_ref, tab_ref, d_ref, *rest, reverse, final):
    if final:
        yfz_ref, yfl_ref, gw_ref, gb_ref, oz_ref, ol_ref, carry_ref, uf_ref, yc_ref, b_ref, k_ref, c_ref = rest
    else:
        oz_ref, ol_ref, carry_ref, uf_ref, yc_ref, b_ref, k_ref, c_ref = rest
    s = pl.program_id(0)
    nblk = b_ref.shape[0]
    ps = b_ref.shape[2] // 2
    gq = LANES // S5_GROUP_CH

    @pl.when(s == 0)
    def _():
        carry_ref[...] = jnp.zeros_like(carry_ref)

        def expand(compact, spread, row_group, per):
            full = jnp.dot(compact, spread, preferred_element_type=F32)
            col_group = (lax.broadcasted_iota(jnp.int32, (1, full.shape[1]), 1) // per) % gq
            return jnp.where(row_group == col_group, full, 0.0).astype(BF16)

        rows = lax.broadcasted_iota(jnp.int32, (wc_ref.shape[1], 1), 0)
        lag_rows_group = (rows // S5_GROUP_CH) % gq
        state_rows_group = (rows // S5_STATE) % gq
        for q in range(nblk):
            b_ref[q] = expand(wc_ref[q], ew_ref[...], lag_rows_group, S5_STATE)
            k_ref[q] = expand(kc_ref[q], ek_ref[...], lag_rows_group, S5_GROUP_CH)
            c_ref[q] = expand(cc_ref[q], ec_ref[...], state_rows_group, S5_GROUP_CH)

    def chunk(u_ref, yf_ref, o_ref):
        t = u_ref.shape[0]
        nw = t // SUBLANES
        row = lax.broadcasted_iota(jnp.int32, (t, 1), 0) & (SUBLANES - 1)
        wrow = lax.broadcasted_iota(jnp.int32, (nw, 1), 0)
        u = u_ref[...]
        uf = u.astype(F32)
        for q in range(nblk):
            uf_ref[q, 0:t] = uf[:, q * LANES:(q + 1) * LANES]
        lagged = [u]
        for k in range(1, SUBLANES):
            if reverse:
                lag = jnp.where(row < SUBLANES - k, pltpu.roll(uf, t - k, 0), 0.0)
            else:
                lag = jnp.where(row >= k, pltpu.roll(uf, k, 0), 0.0)
            lagged.append(lag.astype(BF16))
        outs = []
        for q in range(nblk):
            qs = slice(q * ps, (q + 1) * ps)
            lhs = jnp.concatenate([x[:, q * LANES:(q + 1) * LANES] for x in lagged], axis=1)
            y_loc = jnp.dot(lhs, k_ref[q], preferred_element_type=F32)
            ends = [uf_ref[q, pl.ds(k if reverse else SUBLANES - 1 - k, nw, stride=SUBLANES), :].astype(BF16)
                    for k in range(SUBLANES)]
            sw = jnp.dot(jnp.concatenate(ends, axis=1), b_ref[q], preferred_element_type=F32)
            tabs = [tab_ref[i, :, qs] for i in range(8)]
            cin_r, cin_i = carry_ref[0, :, qs], carry_ref[1, :, qs]
            c_r, c_i, cr, ci = _s5_window_scan(sw[:, :ps], sw[:, ps:], tabs, cin_r, cin_i, reverse)
            carry_ref[0, :, qs] = cr
            carry_ref[1, :, qs] = ci
            if reverse:
                e_r = jnp.where(wrow == nw - 1, cin_r[0:1], pltpu.roll(c_r, nw - 1, 0))
                e_i = jnp.where(wrow == nw - 1, cin_i[0:1], pltpu.roll(c_i, nw - 1, 0))
            else:
                e_r = jnp.where(wrow == 0, cin_r[0:1], pltpu.roll(c_r, 1, 0))
                e_i = jnp.where(wrow == 0, cin_i[0:1], pltpu.roll(c_i, 1, 0))
            y_in = jnp.dot(jnp.concatenate([e_r, e_i], axis=1).astype(BF16), c_ref[q],
                           preferred_element_type=F32)
            for j in range(SUBLANES):
                yc_ref[q, pl.ds(j, nw, stride=SUBLANES), :] = y_in[:, j * LANES:(j + 1) * LANES]
            outs.append(y_loc + yc_ref[q, 0:t])
        out = jnp.concatenate(outs, axis=1)
        if not final:
            o_ref[...] = u.astype(F32) * d_ref[...] + out
        else:
            g = _gelu_tanh(yf_ref[...] + out)
            z = jnp.dot(g.astype(BF16), gw_ref[...], preferred_element_type=F32) + gb_ref[...]
            o_ref[...] = (g * _sigmoid(z)).astype(o_ref.dtype)

    @pl.when(s == 0)
    def _():
        chunk(uz_ref, yfz_ref if final else None, oz_ref)

    @pl.when(s > 0)
    def _():
        chunk(ul_ref, yfl_ref if final else None, ol_ref)


def s5_direction(rest, rest_ctx, u_col, params, which, dskip, reverse, fwd=None, glu=None, t=1024):
    seq = rest.shape[0]
    tz = rest_ctx.shape[0]
    t = min(t, seq)
    nl = seq // t
    w = 4 * LANES
    wc, kc, cc, tabs = params
    _, nq, kq, _ = wc.shape
    gq = LANES // S5_GROUP_CH
    ps2 = 2 * gq * S5_STATE
    p2 = nq * ps2
    e_w = np.zeros((LANES, ps2), np.float32)
    for c in range(2):
        for gl in range(gq):
            for p in range(S5_STATE):
                e_w[c * S5_STATE + p, c * gq * S5_STATE + gl * S5_STATE + p] = 1.0
    e_k = np.zeros((LANES, LANES), np.float32)
    e_c = np.zeros((LANES, SUBLANES * LANES), np.float32)
    for gl in range(gq):
        for o in range(S5_GROUP_CH):
            e_k[o, gl * S5_GROUP_CH + o] = 1.0
            for j in range(SUBLANES):
                e_c[j * S5_GROUP_CH + o, j * LANES + gl * S5_GROUP_CH + o] = 1.0
    spreads = [jnp.asarray(e, BF16) for e in (e_w, e_k, e_c)]
    final = fwd is not None
    if reverse:
        lat = lambda s: (jnp.where(s == 0, nl - 1, nl - s), 0)
    else:
        lat = lambda s: (jnp.maximum(s - 1, 0), 0)
    lat_u = lambda s: (lat(s)[0], u_col)
    const = lambda s: (0, 0)
    table = lambda shape: pl.BlockSpec((None,) + shape, lambda s: (which,) + (0,) * len(shape),
                                       pipeline_mode=pl.Buffered(1))
    in_specs = [pl.BlockSpec((tz, w), lambda s: (0, u_col)),
                pl.BlockSpec((t, w), lat_u),
                table((nq, kq, LANES)),
                table((nq, kq, LANES)),
                table((nq, ps2, LANES)),
                pl.BlockSpec(e_w.shape, const),
                pl.BlockSpec(e_k.shape, const),
                pl.BlockSpec(e_c.shape, const),
                table((8, SUBLANES, p2 // 2)),
                pl.BlockSpec((1, w), const)]
    args = [rest_ctx, rest, wc, kc, cc, *spreads, tabs, dskip.reshape(1, w)]
    if final:
        in_specs += [pl.BlockSpec((tz, w), const), pl.BlockSpec((t, w), lat),
                     pl.BlockSpec((w, w), const), pl.BlockSpec((1, w), const)]
        args += [fwd[0], fwd[1], glu[0], glu[1].reshape(1, w)]
    odt = BF16 if final else F32
    return pl.pallas_call(
        functools.partial(_s5_kernel, reverse=reverse, final=final),
        grid=(nl + 1,),
        in_specs=in_specs,
        out_specs=[pl.BlockSpec((tz, w), const), pl.BlockSpec((t, w), lat)],
        out_shape=[jax.ShapeDtypeStruct((tz, w), odt), jax.ShapeDtypeStruct((seq, w), odt)],
        scratch_shapes=[pltpu.VMEM((2, SUBLANES, p2 // 2), F32),
                        pltpu.VMEM((nq, max(t, tz), LANES), F32),
                        pltpu.VMEM((nq, max(t, tz), LANES), F32),
                        pltpu.VMEM((nq, kq, ps2), BF16),
                        pltpu.VMEM((nq, kq, LANES), BF16),
                        pltpu.VMEM((nq, ps2, SUBLANES * LANES), BF16)],
        compiler_params=_cparams("arbitrary"),
        name="s5_direction",
    )(*args)


def _s5_params(a_re, a_im, log_step, b_re, b_im, c_re, c_im, rev):
    ar, ai = a_re.astype(F32), a_im.astype(F32)
    step = jnp.exp(log_step.astype(F32))[:, None]
    mag = jnp.exp(ar * step)
    ab_re, ab_im = mag * jnp.cos(ai * step), mag * jnp.sin(ai * step)
    den = ar * ar + ai * ai
    nr = ab_re - 1.0
    f_re = (nr * ar + ab_im * ai) / den
    f_im = (ab_im * ar - nr * ai) / den
    br, bi = b_re.astype(F32), b_im.astype(F32)
    bb_re = f_re[..., None] * br - f_im[..., None] * bi
    bb_im = f_re[..., None] * bi + f_im[..., None] * br
    cr, ci = c_re.astype(F32), c_im.astype(F32)
    g, p, h = bb_re.shape
    gq = LANES // h
    nq = g // gq
    nl = SUBLANES

    def cmul(xr, xi, yr, yi):
        return xr * yr - xi * yi, xr * yi + xi * yr

    def powers(xr, xi, n):
        out = [(jnp.ones_like(xr), jnp.zeros_like(xi)), (xr, xi)]
        for _ in range(n - 1):
            out.append(cmul(*out[-1], xr, xi))
        return jnp.stack([o[0] for o in out]), jnp.stack([o[1] for o in out])

    pr, pi = powers(ab_re, ab_im, nl)
    w_re = pr[:nl, ..., None] * bb_re - pi[:nl, ..., None] * bb_im
    w_im = pr[:nl, ..., None] * bb_im + pi[:nl, ..., None] * bb_re

    def rows_lag_group_channel(x):
        x = x.reshape(nl, nq, gq, h, x.shape[-1])
        return jnp.transpose(x, (1, 0, 2, 3, 4)).reshape(nq, nl * gq * h, x.shape[-1])

    wc = jnp.concatenate([rows_lag_group_channel(jnp.swapaxes(w, 2, 3)) for w in (w_re, w_im)], axis=2)
    hp = lax.Precision.HIGHEST
    kk = (jnp.einsum('kgph,gop->kgho', w_re, cr, precision=hp)
          - jnp.einsum('kgph,gop->kgho', w_im, ci, precision=hp))
    kc = jnp.pad(rows_lag_group_channel(kk), ((0, 0), (0, 0), (0, LANES - h)))
    dr = jnp.where(rev, pr[nl:0:-1], pr[1:nl + 1])
    di = jnp.where(rev, pi[nl:0:-1], pi[1:nl + 1])
    cd_re = cr[None] * dr[:, :, None, :] - ci[None] * di[:, :, None, :]
    cd_im = cr[None] * di[:, :, None, :] + ci[None] * dr[:, :, None, :]

    def rows_group_state(cd):
        return jnp.transpose(cd.reshape(nl, nq, gq, h, p), (1, 2, 4, 0, 3)).reshape(nq, gq * p, nl * h)

    cc = jnp.concatenate([rows_group_state(cd_re), -rows_group_state(cd_im)], axis=1)
    qr, qi = powers(pr[nl].reshape(-1), pi[nl].reshape(-1), SUBLANES)
    row = np.arange(SUBLANES)[:, None]
    tabs = []
    for k in (1, 2, 4):
        keep = jnp.where(rev, jnp.asarray(row < SUBLANES - k), jnp.asarray(row >= k))
        tabs += [jnp.where(keep, qr[k][None], 0.0), jnp.where(keep, qi[k][None], 0.0)]
    tabs += [jnp.where(rev, qr[SUBLANES:0:-1], qr[1:SUBLANES + 1]),
             jnp.where(rev, qi[SUBLANES:0:-1], qi[1:SUBLANES + 1])]
    return wc.astype(BF16), kc.astype(BF16), cc.astype(BF16), jnp.stack(tabs)


def _conv_kernel(val_ref, gate_ref, vp_ref, gp_ref, vn_ref, gn_ref, w_ref, b_ref, g_ref, beta_ref, o_ref, hs_ref,
                 *, sub):
    i = pl.program_id(0)
    n = pl.num_programs(0)
    t = val_ref.shape[0]

    def glu(v, g):
        return v[...].astype(F32) * _sigmoid(g[...].astype(F32))

    hs_ref[0:CONV_HALO] = jnp.where(i > 0, glu(vp_ref, gp_ref), 0.0)
    hs_ref[CONV_HALO:CONV_HALO + t] = glu(val_ref, gate_ref)
    hs_ref[CONV_HALO + t:2 * CONV_HALO + t] = jnp.where(i < n - 1, glu(vn_ref, gn_ref), 0.0)
    base = CONV_HALO - CONV_K // 2
    for r0 in range(t // sub):
        acc = jnp.zeros((sub, val_ref.shape[1]), F32) + b_ref[...]
        for r in range(SUBLANES):
            part = None
            for m in range((base + CONV_K - 1) // SUBLANES + 1):
                k = SUBLANES * m + r - base
                if 0 <= k < CONV_K:
                    lo_row = r0 * sub + SUBLANES * m
                    term = hs_ref[lo_row:lo_row + sub + SUBLANES] * w_ref[k:k + 1]
                    part = term if part is None else part + term
            acc = acc + part[r:r + sub]
        mu = jnp.mean(acc, axis=-1, keepdims=True)
        cen = acc - mu
        var = jnp.mean(cen * cen, axis=-1, keepdims=True)
        yn = cen * lax.rsqrt(var + EPS) * g_ref[...] + beta_ref[...]
        o_ref[r0 * sub:(r0 + 1) * sub] = _silu(yn).astype(o_ref.dtype)


def conv_module(rest, val_col, gate_col, w, b, g, beta, t=256, sub=64):
    seq = rest.shape[0]
    t = min(t, seq)
    cw = 4 * LANES
    hb = t // CONV_HALO
    nh = seq // CONV_HALO
    prev = lambda i: jnp.maximum(i * hb - 1, 0)
    nxt = lambda i: jnp.minimum((i + 1) * hb, nh - 1)
    vec = pl.BlockSpec((1, cw), lambda i: (0, 0))
    return pl.pallas_call(
        functools.partial(_conv_kernel, sub=sub),
        grid=(seq // t,),
        in_specs=[pl.BlockSpec((t, cw), lambda i: (i, val_col)),
                  pl.BlockSpec((t, cw), lambda i: (i, gate_col)),
                  pl.BlockSpec((CONV_HALO, cw), lambda i: (prev(i), val_col)),
                  pl.BlockSpec((CONV_HALO, cw), lambda i: (prev(i), gate_col)),
                  pl.BlockSpec((CONV_HALO, cw), lambda i: (nxt(i), val_col)),
                  pl.BlockSpec((CONV_HALO, cw), lambda i: (nxt(i), gate_col)),
                  pl.BlockSpec((CONV_K, cw), lambda i: (0, 0)), vec, vec, vec],
        out_specs=pl.BlockSpec((t, cw), lambda i: (i, 0)),
        out_shape=jax.ShapeDtypeStruct((seq, cw), BF16),
        scratch_shapes=[pltpu.VMEM((t + 2 * CONV_HALO, cw), F32)],
        compiler_params=_cparams("arbitrary"),
        name="conv_module",
    )(rest, rest, rest, rest, rest, rest, w.astype(F32), b.reshape(1, cw).astype(F32),
      g.reshape(1, cw).astype(F32), beta.reshape(1, cw).astype(F32))


def _merge_kernel(ya_ref, yb_ref, yc_ref, yd_ref, g0_ref, g1_ref, g2_ref, g3_ref, wb_ref, wo_ref, x_ref, gate_ref,
                  o_ref, m_ref):
    j = pl.program_id(1)
    nj = m_ref.shape[0]

    @pl.when(j < nj)
    def _():
        m = None
        for y_ref, gl_ref, k in zip((ya_ref, yb_ref, yc_ref, yd_ref), (g0_ref, g1_ref, g2_ref, g3_ref),
                                    range(N_BRANCH)):
            term = _sigmoid(gl_ref[...].astype(F32)) * jnp.dot(y_ref[...], wb_ref[j, k],
                                                                  preferred_element_type=F32)
            m = term if m is None else m + term
        m_ref[j] = m.astype(BF16)

    @pl.when(j >= nj)
    def _():
        merged = jnp.concatenate([m_ref[k] for k in range(nj)], axis=1)
        o_ref[...] = x_ref[...] + gate_ref[...] * jnp.dot(merged, wo_ref[j - nj], preferred_element_type=F32)


def merge_weights(w_branch, w_out, tn=512):
    depth, nb, yw, n = w_branch.shape
    nj = n // tn
    wb = jnp.transpose(w_branch.astype(BF16).reshape(depth, nb, yw, nj, tn), (0, 3, 1, 2, 4))
    wo = jnp.transpose(w_out.astype(BF16).reshape(depth, n, nj, tn), (0, 2, 1, 3))
    return wb, wo


def gated_merge_residual(ys, rest, logit_col, wb, wo, layer, x, gate, tm=512):
    m, yw = ys[0].shape
    _, nj, _, _, tn = wb.shape
    n = nj * tn
    tm = min(tm, m)
    first = lambda j: jnp.minimum(j, nj - 1)
    second = lambda j: jnp.maximum(j - nj, 0)
    yspec = pl.BlockSpec((tm, yw), lambda i, j: (i, 0))
    gspecs = [pl.BlockSpec((tm, tn), functools.partial(lambda i, j, k: (i, logit_col + k * nj + first(j)), k=k))
              for k in range(N_BRANCH)]
    return pl.pallas_call(
        _merge_kernel,
        grid=(m // tm, 2 * nj),
        in_specs=[yspec] * N_BRANCH + gspecs + [
            pl.BlockSpec((None, nj, N_BRANCH, yw, tn), lambda i, j: (layer, 0, 0, 0, 0)),
            pl.BlockSpec((None, nj, n, tn), lambda i, j: (layer, 0, 0, 0)),
            pl.BlockSpec((tm, tn), lambda i, j: (i, second(j))),
            pl.BlockSpec((1, tn), lambda i, j: (0, second(j)))],
        out_specs=pl.BlockSpec((tm, tn), lambda i, j: (i, second(j))),
        out_shape=jax.ShapeDtypeStruct((m, n), F32),
        scratch_shapes=[pltpu.VMEM((nj, tm, tn), BF16)],
        compiler_params=_cparams("arbitrary", "arbitrary"),
        name="gated_merge_residual",
    )(*ys, rest, rest, rest, rest, wb, wo, x, gate.reshape(1, n))


def _mlp_kernel(x_ref, g_ref, sc_ref, sh_ref, gate_ref, w1_ref, w2_ref, *rest, final_norm):
    if final_norm:
        gf_ref, o_ref, h_ref, acc_ref = rest
    else:
        o_ref, h_ref, acc_ref = rest
    f = pl.program_id(1)

    @pl.when(f == 0)
    def _():
        x = x_ref[...]
        y = x * lax.rsqrt(jnp.mean(x * x, axis=-1, keepdims=True) + EPS) * g_ref[...]
        h_ref[...] = (y * (1.0 + sc_ref[...]) + sh_ref[...]).astype(BF16)
        acc_ref[...] = jnp.zeros_like(acc_ref)

    t = jnp.dot(h_ref[...], w1_ref[...], preferred_element_type=F32)
    t = jnp.square(jnp.maximum(t, 0.0)).astype(BF16)
    acc_ref[...] += jnp.dot(t, w2_ref[...], preferred_element_type=F32)

    @pl.when(f == pl.num_programs(1) - 1)
    def _():
        y = x_ref[...] + gate_ref[...] * acc_ref[...]
        if final_norm:
            y = y * lax.rsqrt(jnp.mean(y * y, axis=-1, keepdims=True) + EPS) * gf_ref[...]
        o_ref[...] = y


def mlp_block(x, g, scale, shift, gate, w1, w2, layer, final_g=None, tm=512, tf=1024):
    m, d = x.shape
    ff = w1.shape[2]
    tm = min(tm, m)
    vec = pl.BlockSpec((1, d), lambda i, f: (0, 0))
    final_norm = final_g is not None
    args = [x, g.reshape(1, d), scale.reshape(1, d), shift.reshape(1, d), gate.reshape(1, d), w1, w2]
    in_specs = [pl.BlockSpec((tm, d), lambda i, f: (i, 0)), vec, vec, vec, vec,
                pl.BlockSpec((None, d, tf), lambda i, f: (layer, 0, f)),
                pl.BlockSpec((None, tf, d), lambda i, f: (layer, f, 0))]
    if final_norm:
        args.append(final_g.reshape(1, d))
        in_specs.append(vec)
    return pl.pallas_call(
        functools.partial(_mlp_kernel, final_norm=final_norm),
        grid=(m // tm, ff // tf),
        in_specs=in_specs,
        out_specs=pl.BlockSpec((tm, d), lambda i, f: (i, 0)),
        out_shape=jax.ShapeDtypeStruct((m, d), F32),
        scratch_shapes=[pltpu.VMEM((tm, d), BF16), pltpu.VMEM((tm, d), F32)],
        compiler_params=_cparams("arbitrary", "arbitrary"),
        name="mlp_block",
    )(*args)


_A_COL_K, _A_COL_V = 2, 3
_REST_COL0 = (A_HEADS + 2 * A_KV_HEADS) * HEAD_DIM
_COL_U, _COL_QC, _COL_KC, _COL_VC, _COL_VAL, _COL_GATE, _COL_MERGE = 0, 1, 2, 3, 4, 5, 6


def _rope_tables(seq):
    freqs = ROPE_BASE ** (-jnp.arange(ROPE_PAIRS, dtype=F32) / ROPE_PAIRS)
    rows = seq // GRID_W
    ang_r = jnp.arange(rows).astype(F32)[:, None] * freqs
    ang_c = jnp.arange(GRID_W).astype(F32)[:, None] * freqs

    def expand(tab_r, tab_c, first, second):
        r = jnp.repeat(tab_r, GRID_W, axis=0)
        cc = jnp.tile(tab_c, (rows, 1))
        head = jnp.concatenate([first * r, second * r, first * cc, second * cc], axis=1)
        return jnp.tile(head, (1, LANES // HEAD_DIM))

    cos = expand(jnp.cos(ang_r), jnp.cos(ang_c), 1.0, 1.0)
    sa = expand(jnp.sin(ang_r), jnp.sin(ang_c), 0.0, 1.0)
    sb = expand(jnp.sin(ang_r), jnp.sin(ang_c), -1.0, 0.0)
    return cos, sa, sb


def kernel(x, c, ctx, c_ctx, ada_w, ada_b, norm1_g, norm2_g, w_in, a_sink, s5_a_re, s5_a_im, s5_log_step, s5_b_re, s5_b_im, s5_c_re, s5_c_im, s5_d, s5_glu_w, s5_glu_b, c_rpb, d_conv_w, d_conv_b, d_ln_g, d_ln_b, w_branch, w_out, mlp_w1, mlp_w2, final_g):
    depth = ada_w.shape[0]
    _, seq, d = x.shape
    lc = ctx.shape[1]
    xs = x.reshape(seq, d)
    zs = ctx.reshape(lc, d)
    rope = _rope_tables(seq)

    cc = jnp.zeros((SUBLANES, d), F32).at[0].set(c[0]).at[1].set(c_ctx)
    mods = ada_modulation(cc, ada_w, ada_b)
    n_rest = w_in.shape[2] - _REST_COL0
    w1, w2 = mlp_w1.astype(BF16), mlp_w2.astype(BF16)
    wb, wo = merge_weights(w_branch, w_out)
    s5_params = jax.vmap(_s5_params)(*(a.reshape((2 * depth,) + a.shape[2:]) for a in
                                       (s5_a_re, s5_a_im, s5_log_step, s5_b_re, s5_b_im, s5_c_re, s5_c_im)),
                                     jnp.arange(2 * depth) % 2 == 1)

    for l in range(depth):
        need_ctx = l < depth - 1
        mx = [mods[l, 0, k * d:(k + 1) * d] for k in range(6)]
        mz = [mods[l, 1, k * d:(k + 1) * d] for k in range(6)]

        hx, qk_x = norm_head_projection(xs, norm1_g[l], mx[1], mx[0], w_in, l, rope=rope)
        hz, qk_z = norm_head_projection(zs, norm1_g[l], mz[1], mz[0], w_in, l)
        r_x = wide_projection(hx, w_in, l, _REST_COL0, n_rest)
        r_z = wide_projection(hz, w_in, l, _REST_COL0, n_rest)

        ya_x = window_attention(a_sink[l], qk_x, qk_x, qk_z, qk_z, _A_COL_V)
        yc_x = neighborhood_attention(r_x, r_z, c_rpb[l], _COL_QC, _COL_KC, _COL_VC)
        yd_x = conv_module(r_x, _COL_VAL, _COL_GATE, d_conv_w[l], d_conv_b[l], d_ln_g[l], d_ln_b[l])

        yf = s5_direction(r_x, r_z, _COL_U, s5_params, 2 * l, s5_d[l].astype(F32), reverse=False)
        yb_z, yb_x = s5_direction(r_x, r_z, _COL_U, s5_params, 2 * l + 1, s5_d[l].astype(F32), reverse=True, fwd=yf,
                                  glu=(s5_glu_w[l].astype(BF16), s5_glu_b[l].astype(F32)))

        xs = gated_merge_residual((ya_x, yb_x, yc_x, yd_x), r_x, _COL_MERGE, wb, wo, l, xs, mx[2])
        xs = mlp_block(xs, norm2_g[l], mx[4], mx[3], mx[5], w1, w2, l, final_g=None if need_ctx else final_g)

        if need_ctx:
            ya_z = context_attention(a_sink[l], qk_z, 0, qk_z, _A_COL_K, 2 * LANES, qk_z, _A_COL_V, True, True)
            yc_z = context_attention(a_sink[l], r_z, _COL_QC, r_z, _COL_KC, 4 * LANES, r_z, _COL_VC, False, False)
            yd_z = conv_module(r_z, _COL_VAL, _COL_GATE, d_conv_w[l], d_conv_b[l], d_ln_g[l], d_ln_b[l])
            zs = gated_merge_residual((ya_z, yb_z, yc_z, yd_z), r_z, _COL_MERGE, wb, wo, l, zs, mz[2])
            zs = mlp_block(zs, norm2_g[l], mz[4], mz[3], mz[5], w1, w2, l)

    return xs.reshape(x.shape)
```

```python
import functools
import math

import numpy as np
import jax
import jax.numpy as jnp
from jax import lax
from jax.experimental import pallas as pl
from jax.experimental.pallas import tpu as pltpu

F32 = jnp.float32
BF16 = jnp.bfloat16

LANES = 128
SUBLANES = 8
VMEM_LIMIT = 56 * 1024 * 1024

GRID_W = 64
HEAD_DIM = 64
N_BRANCH = 4
A_HEADS = 8
A_KV_HEADS = 2
A_WINDOW = 128
A_BLOCK = 128
ROPE_BASE = 10000.0
ROPE_PAIRS = HEAD_DIM // 4
S5_GROUP_CH = 16
S5_STATE = 64
C_HEADS = 8
C_WIN_R = 8
C_WIN_C = 16
C_QROWS = 2
C_KROWS = C_QROWS + C_WIN_R
CONV_K = 31
CONV_HALO = 16
EPS = 1e-6
NEG_INF = -1e30


def _cparams(*sem):
    return pltpu.CompilerParams(dimension_semantics=sem, vmem_limit_bytes=VMEM_LIMIT)


def _sigmoid(x):
    return 0.5 * jnp.tanh(0.5 * x) + 0.5


def _silu(x):
    return x * _sigmoid(x)


def _gelu_tanh(x):
    return 0.5 * x * (1.0 + jnp.tanh(math.sqrt(2.0 / math.pi) * (x + 0.044715 * (x * x * x))))


def _ada_kernel(c_ref, w_ref, b_ref, o_ref):
    a = _silu(c_ref[...]).astype(BF16)
    o_ref[...] = jnp.dot(a, w_ref[...].astype(BF16), preferred_element_type=F32) + b_ref[...]


def ada_modulation(cc, ada_w, ada_b, tn=1024):
    depth, d, n = ada_w.shape
    return pl.pallas_call(
        _ada_kernel,
        grid=(depth, n // tn),
        in_specs=[pl.BlockSpec((SUBLANES, d), lambda l, j: (0, 0)),
                  pl.BlockSpec((None, d, tn), lambda l, j: (l, 0, j)),
                  pl.BlockSpec((None, 1, tn), lambda l, j: (l, 0, j))],
        out_specs=pl.BlockSpec((None, SUBLANES, tn), lambda l, j: (l, 0, j)),
        out_shape=jax.ShapeDtypeStruct((depth, SUBLANES, n), F32),
        compiler_params=_cparams("arbitrary", "arbitrary"),
        name="ada_modulation",
    )(cc, ada_w, ada_b.reshape(depth, 1, n))


def _rope_block(y, cos, sa, sb):
    return y * cos + pltpu.roll(y, 16, 1) * sa + pltpu.roll(y, LANES - 16, 1) * sb


def _head_proj_kernel(x_ref, g_ref, sc_ref, sh_ref, w_ref, *rest, n_q, with_rope):
    if with_rope:
        cos_ref, sa_ref, sb_ref, h_ref, o_ref, wb_ref = rest
        cos, sa, sb = cos_ref[...], sa_ref[...], sb_ref[...]
    else:
        h_ref, o_ref, wb_ref = rest

    @pl.when(pl.program_id(0) == 0)
    def _():
        wb_ref[...] = w_ref[0].astype(BF16)

    x = x_ref[...]
    y = x * lax.rsqrt(jnp.mean(x * x, axis=-1, keepdims=True) + EPS) * g_ref[...]
    h = (y * (1.0 + sc_ref[...]) + sh_ref[...]).astype(BF16)
    h_ref[...] = h
    acc = jnp.dot(h, wb_ref[...], preferred_element_type=F32)
    lo = _lane_lo()
    out = 0
    for b in range(acc.shape[1] // LANES):
        y = acc[:, b * LANES:(b + 1) * LANES]
        if with_rope and b <= n_q:
            y = _rope_block(y, cos, sa, sb)
        if b < n_q:
            blocks = [y]
        else:
            swapped = pltpu.roll(y, HEAD_DIM, 1)
            blocks = [jnp.where(lo, y, swapped), jnp.where(lo, swapped, y)]
        for blk in blocks:
            o_ref[:, out * LANES:(out + 1) * LANES] = blk.astype(o_ref.dtype)
            out += 1


def norm_head_projection(x, g, scale, shift, w_in, layer, rope=None, tm=512):
    m, k = x.shape
    n_in = (A_HEADS + 2 * A_KV_HEADS) * HEAD_DIM
    n_q = A_HEADS * HEAD_DIM // LANES
    assert A_KV_HEADS * HEAD_DIM == LANES
    n_out = n_in + 2 * LANES
    tm = min(tm, m)
    vec = pl.BlockSpec((1, k), lambda i: (0, 0))
    in_specs = [pl.BlockSpec((tm, k), lambda i: (i, 0)), vec, vec, vec,
                pl.BlockSpec((pl.Element(1), pl.Element(k), pl.Element(n_in)), lambda i: (layer, 0, 0))]
    args = [x, g.reshape(1, k), scale.reshape(1, k), shift.reshape(1, k), w_in]
    if rope is not None:
        in_specs += [pl.BlockSpec((tm, LANES), lambda i: (i, 0))] * 3
        args += list(rope)
    return pl.pallas_call(
        functools.partial(_head_proj_kernel, n_q=n_q, with_rope=rope is not None),
        grid=(m // tm,),
        in_specs=in_specs,
        out_specs=[pl.BlockSpec((tm, k), lambda i: (i, 0)), pl.BlockSpec((tm, n_out), lambda i: (i, 0))],
        out_shape=[jax.ShapeDtypeStruct((m, k), BF16), jax.ShapeDtypeStruct((m, n_out), BF16)],
        scratch_shapes=[pltpu.VMEM((k, n_in), BF16)],
        compiler_params=_cparams("arbitrary"),
        name="norm_head_projection",
    )(*args)


def _wide_proj_kernel(a_ref, w_ref, o_ref, wb_ref):
    @pl.when(pl.program_id(1) == 0)
    def _():
        wb_ref[...] = w_ref[0].astype(BF16)

    o_ref[...] = jnp.dot(a_ref[...], wb_ref[...], preferred_element_type=F32).astype(o_ref.dtype)


def wide_projection(a, w, layer, col0, n, tm=1024, tn=1024):
    m, k = a.shape
    tm = min(tm, m)
    assert n % tn == 0 and col0 % LANES == 0 and tn % LANES == 0
    return pl.pallas_call(
        _wide_proj_kernel,
        grid=(n // tn, m // tm),
        in_specs=[pl.BlockSpec((tm, k), lambda j, i: (i, 0)),
                  pl.BlockSpec((pl.Element(1), pl.Element(k), pl.Element(tn)),
                               lambda j, i: (layer, 0, pl.multiple_of(col0 + j * tn, LANES)))],
        out_specs=pl.BlockSpec((tm, tn), lambda j, i: (i, j)),
        out_shape=jax.ShapeDtypeStruct((m, n), BF16),
        scratch_shapes=[pltpu.VMEM((k, tn), BF16)],
        compiler_params=_cparams("arbitrary", "arbitrary"),
        name="wide_projection",
    )(a, w)


def _lane_lo():
    return lax.broadcasted_iota(jnp.int32, (1, LANES), 1) < HEAD_DIM


def _half_select(x, lo, half):
    zero = jnp.zeros_like(x)
    return jnp.where(lo, x, zero) if half == 0 else jnp.where(lo, zero, x)


def _dot_nt(a, b):
    return lax.dot_general(a, b, (((1,), (1,)), ((), ())), preferred_element_type=F32)


def _softmax_pv(parts, values, sink=None):
    m = parts[0].max(axis=-1, keepdims=True)
    for s in parts[1:]:
        m = jnp.maximum(m, s.max(axis=-1, keepdims=True))
    if sink is not None:
        m = jnp.maximum(m, sink)
    den = jnp.exp(sink - m) if sink is not None else 0.0
    out = 0.0
    for s, v in zip(parts, values):
        p = jnp.exp(s - m)
        den = den + p.sum(axis=-1, keepdims=True)
        out = out + jnp.dot(p.astype(BF16), v, preferred_element_type=F32)
    return out / den


def _attn_a_kernel(sink_ref, q_ref, kp_ref, kc_ref, kn_ref, vp_ref, vc_ref, vn_ref, kz_ref, vz_ref, o_ref,
                   *, seq):
    i = pl.program_id(0)
    lo = _lane_lo()
    nq = 4 * A_BLOCK
    r = lax.broadcasted_iota(jnp.int32, (nq, 3 * A_BLOCK), 0) & (A_BLOCK - 1)
    c = lax.broadcasted_iota(jnp.int32, (nq, 3 * A_BLOCK), 1)
    kabs = c + (i - 1) * A_BLOCK
    valid = ((c - r) >= 0) & ((c - r) <= 2 * A_WINDOW) & (kabs >= 0) & (kabs < seq)
    rowblk = lax.broadcasted_iota(jnp.int32, (nq, 1), 0) // A_BLOCK
    for g in range(A_KV_HEADS):
        gs = slice(g * LANES, (g + 1) * LANES)
        kw = jnp.concatenate([kp_ref[:, gs], kc_ref[:, gs], kn_ref[:, gs]], axis=0)
        vw = jnp.concatenate([vp_ref[:, gs], vc_ref[:, gs], vn_ref[:, gs]], axis=0)
        kz, vz = kz_ref[:, gs], vz_ref[:, gs]
        blocks = [slice((2 * g + bb) * LANES, (2 * g + bb + 1) * LANES) for bb in range(2)]
        ql = jnp.concatenate([_half_select(q_ref[:, bs], lo, half) * (HEAD_DIM ** -0.5)
                              for half in range(2) for bs in blocks], axis=0)
        heads = [2 * (2 * g + bb) + half for half in range(2) for bb in range(2)]
        sink = jnp.full((nq, 1), sink_ref[heads[3]], F32)
        for k in range(3):
            sink = jnp.where(rowblk == k, sink_ref[heads[k]], sink)
        s = jnp.where(valid, _dot_nt(ql, kw), NEG_INF)
        sz = _dot_nt(ql, kz)
        m = jnp.maximum(jnp.maximum(s.max(axis=-1, keepdims=True), sz.max(axis=-1, keepdims=True)), sink)
        p, pz = jnp.exp(s - m), jnp.exp(sz - m)
        inv = 1.0 / (p.sum(axis=-1, keepdims=True) + pz.sum(axis=-1, keepdims=True) + jnp.exp(sink - m))
        p, pz = p.astype(BF16), pz.astype(BF16)
        out = None
        for half in range(2):
            hs = slice(half * 2 * A_BLOCK, (half + 1) * 2 * A_BLOCK)
            o = (jnp.dot(p[hs], _half_select(vw, lo, half), preferred_element_type=F32)
                 + jnp.dot(pz[hs], _half_select(vz, lo, half), preferred_element_type=F32)) * inv[hs]
            out = o if out is None else out + o
        for bb in range(2):
            o_ref[:, blocks[bb]] = out[bb * A_BLOCK:(bb + 1) * A_BLOCK].astype(o_ref.dtype)


def window_attention(sink, qk, rest, qk_ctx, rest_ctx, vv_col):
    seq = qk.shape[0]
    lc = qk_ctx.shape[0]
    nb = seq // A_BLOCK
    kw = 2 * LANES
    prev = lambda i: jnp.maximum(i - 1, 0)
    nxt = lambda i: jnp.minimum(i + 1, nb - 1)
    return pl.pallas_call(
        functools.partial(_attn_a_kernel, seq=seq),
        grid=(nb,),
        in_specs=[pl.BlockSpec(memory_space=pltpu.SMEM),
                  pl.BlockSpec((A_BLOCK, 4 * LANES), lambda i: (i, 0)),
                  pl.BlockSpec((A_BLOCK, kw), lambda i: (prev(i), 2)),
                  pl.BlockSpec((A_BLOCK, kw), lambda i: (i, 2)),
                  pl.BlockSpec((A_BLOCK, kw), lambda i: (nxt(i), 2)),
                  pl.BlockSpec((A_BLOCK, kw), lambda i: (prev(i), vv_col)),
                  pl.BlockSpec((A_BLOCK, kw), lambda i: (i, vv_col)),
                  pl.BlockSpec((A_BLOCK, kw), lambda i: (nxt(i), vv_col)),
                  pl.BlockSpec((lc, kw), lambda i: (0, 2)),
                  pl.BlockSpec((lc, kw), lambda i: (0, vv_col))],
        out_specs=pl.BlockSpec((A_BLOCK, 4 * LANES), lambda i: (i, 0)),
        out_shape=jax.ShapeDtypeStruct((seq, 4 * LANES), BF16),
        compiler_params=_cparams("arbitrary"),
        name="window_attention",
    )(sink, qk, qk, qk, qk, rest, rest, rest, qk_ctx, rest_ctx)


def _ctx_attn_kernel(sink_ref, q_ref, k_ref, v_ref, o_ref, *, kv_shared, has_sink):
    lo = _lane_lo()
    for b in range(q_ref.shape[1] // LANES):
        bs = slice(b * LANES, (b + 1) * LANES)
        kb = b // 2 if kv_shared else b
        ks = slice(kb * LANES, (kb + 1) * LANES)
        qb, k, v = q_ref[:, bs], k_ref[:, ks], v_ref[:, ks]
        acc = jnp.zeros((q_ref.shape[0], LANES), F32)
        for half in range(2):
            ql = _half_select(qb, lo, half) * (HEAD_DIM ** -0.5)
            sink = sink_ref[2 * b + half] if has_sink else None
            acc = acc + _softmax_pv([_dot_nt(ql, k)], [_half_select(v, lo, half)], sink=sink)
        o_ref[:, bs] = acc.astype(o_ref.dtype)


def context_attention(sink, q_arr, q_col, k_arr, k_col, k_w, v_arr, v_col, kv_shared, has_sink):
    lc = q_arr.shape[0]
    return pl.pallas_call(
        functools.partial(_ctx_attn_kernel, kv_shared=kv_shared, has_sink=has_sink),
        grid=(1,),
        in_specs=[pl.BlockSpec(memory_space=pltpu.SMEM),
                  pl.BlockSpec((lc, 4 * LANES), lambda i: (0, q_col)),
                  pl.BlockSpec((lc, k_w), lambda i: (0, k_col)),
                  pl.BlockSpec((lc, k_w), lambda i: (0, v_col))],
        out_specs=pl.BlockSpec((lc, 4 * LANES), lambda i: (0, 0)),
        out_shape=jax.ShapeDtypeStruct((lc, 4 * LANES), BF16),
        compiler_params=_cparams("arbitrary"),
        name="context_attention",
    )(sink, q_arr, k_arr, v_arr)


def _nbr_row_offsets(rows):
    nsteps = rows // C_QROWS
    nblk = C_KROWS * GRID_W // LANES
    wr = min(C_WIN_R, rows)
    reps = [0, 1, 2, nsteps - 2, nsteps - 1]
    dr = -np.ones((len(reps), C_QROWS, C_KROWS), np.int32)
    for v, i in enumerate(reps):
        w0 = C_QROWS * int(np.clip(i - 2, 0, nsteps - nblk))
        for a in range(C_QROWS):
            rq = C_QROWS * i + a
            k0 = int(np.clip(rq - C_WIN_R // 2, 0, rows - wr))
            for j in range(C_KROWS):
                if k0 <= w0 + j < k0 + wr:
                    dr[v, a, j] = w0 + j - rq + C_WIN_R - 1
    return dr


def _nbr_bias(rpb, rows):
    h = rpb.shape[0]
    qc = np.arange(GRID_W)[:, None]
    kc = np.arange(GRID_W)[None, :]
    cstart = np.clip(qc - C_WIN_C // 2, 0, GRID_W - C_WIN_C)
    col_ok = jnp.asarray((kc >= cstart) & (kc < cstart + C_WIN_C))
    pad = GRID_W - C_WIN_C
    ext = jnp.pad(rpb.astype(F32), ((0, 0), (0, 0), (pad, pad)), mode='edge')
    toep = jnp.stack([ext[:, :, GRID_W - 1 - q:2 * GRID_W - 1 - q] for q in range(GRID_W)], axis=2)
    toep = jnp.where(col_ok, toep, NEG_INF)
    masked = jnp.full((h, GRID_W, GRID_W), NEG_INF, F32)
    dr = _nbr_row_offsets(rows)
    variants = []
    for v in range(dr.shape[0]):
        qrows = [jnp.concatenate([toep[:, dr[v, a, j]] if dr[v, a, j] >= 0 else masked for j in range(C_KROWS)],
                                 axis=-1) for a in range(C_QROWS)]
        variants.append(jnp.concatenate(qrows, axis=-2).reshape(h * C_QROWS * GRID_W, C_KROWS * GRID_W))
    return jnp.stack(variants)


def _attn_c_kernel(q_ref, *refs):
    nblk = C_KROWS * GRID_W // LANES
    k_refs, v_refs = refs[:nblk], refs[nblk:2 * nblk]
    kz_ref, vz_ref, bias_ref, o_ref = refs[2 * nblk:]
    lo = _lane_lo()
    nq = q_ref.shape[0]
    nb = C_HEADS // 2
    per_chain = 2
    all_blocks = [slice(b * LANES, (b + 1) * LANES) for b in range(nb)]
    s_all, sz_all = [], []
    for bs in all_blocks:
        ql = jnp.concatenate([_half_select(q_ref[:, bs], lo, half) * (HEAD_DIM ** -0.5) for half in range(2)],
                             axis=0)
        s_all.append(_dot_nt(ql, jnp.concatenate([kr[:, bs] for kr in k_refs], axis=0)))
        sz_all.append(_dot_nt(ql, kz_ref[:, bs]))
    for first in range(0, nb, per_chain):
        blocks = all_blocks[first:first + per_chain]
        s = (jnp.concatenate(s_all[first:first + per_chain], axis=0)
             + bias_ref[2 * first * nq:2 * (first + per_chain) * nq])
        sz = jnp.concatenate(sz_all[first:first + per_chain], axis=0)
        m = jnp.maximum(s.max(axis=-1, keepdims=True), sz.max(axis=-1, keepdims=True))
        p, pz = jnp.exp(s - m), jnp.exp(sz - m)
        inv = 1.0 / (p.sum(axis=-1, keepdims=True) + pz.sum(axis=-1, keepdims=True))
        p, pz = p.astype(BF16), pz.astype(BF16)
        for b, bs in enumerate(blocks):
            vw = jnp.concatenate([vr[:, bs] for vr in v_refs], axis=0)
            vz = vz_ref[:, bs]
            out = None
            for half in range(2):
                hs = slice((2 * b + half) * nq, (2 * b + half + 1) * nq)
                o = (jnp.dot(p[hs], _half_select(vw, lo, half), preferred_element_type=F32)
                     + jnp.dot(pz[hs], _half_select(vz, lo, half), preferred_element_type=F32)) * inv[hs]
                out = o if out is None else out + o
            o_ref[:, bs] = out.astype(o_ref.dtype)


def neighborhood_attention(rest, rest_ctx, rpb, q_col, k_col, v_col):
    seq = rest.shape[0]
    lc = rest_ctx.shape[0]
    rows = seq // GRID_W
    nsteps = rows // C_QROWS
    nblk = C_KROWS * GRID_W // LANES
    assert nsteps > nblk and C_QROWS * GRID_W == LANES
    bias = _nbr_bias(rpb, rows)

    def first_blk(i):
        return jnp.clip(i - 2, 0, nsteps - nblk)

    def variant(i):
        return jnp.where(i < 2, i, jnp.where(i >= nsteps - 2, i - (nsteps - 2) + 3, 2))

    w = 4 * LANES
    kspecs = [pl.BlockSpec((LANES, w), functools.partial(lambda i, t: (first_blk(i) + t, k_col), t=t))
              for t in range(nblk)]
    vspecs = [pl.BlockSpec((LANES, w), functools.partial(lambda i, t: (first_blk(i) + t, v_col), t=t))
              for t in range(nblk)]
    return pl.pallas_call(
        _attn_c_kernel,
        grid=(nsteps,),
        in_specs=[pl.BlockSpec((LANES, w), lambda i: (i, q_col))] + kspecs + vspecs + [
            pl.BlockSpec((lc, w), lambda i: (0, k_col)),
            pl.BlockSpec((lc, w), lambda i: (0, v_col)),
            pl.BlockSpec((None, C_HEADS * LANES, nblk * LANES), lambda i: (variant(i), 0, 0))],
        out_specs=pl.BlockSpec((LANES, w), lambda i: (i, 0)),
        out_shape=jax.ShapeDtypeStruct((seq, w), BF16),
        compiler_params=_cparams("arbitrary"),
        name="neighborhood_attention",
    )(rest, *([rest] * (2 * nblk)), rest_ctx, rest_ctx, bias)


def _s5_window_scan(sr, si, tabs, cr, ci, reverse):
    n_tiles = sr.shape[0] // SUBLANES
    edge = 0 if reverse else SUBLANES - 1
    out_r, out_i = [None] * n_tiles, [None] * n_tiles
    for step in range(n_tiles):
        tile = (n_tiles - 1 - step) if reverse else step
        xr = sr[tile * SUBLANES:(tile + 1) * SUBLANES]
        xi = si[tile * SUBLANES:(tile + 1) * SUBLANES]
        for lvl, k in enumerate((1, 2, 4)):
            shift = (SUBLANES - k) if reverse else k
            rr, ri = pltpu.roll(xr, shift, 0), pltpu.roll(xi, shift, 0)
            mr, mi = tabs[2 * lvl], tabs[2 * lvl + 1]
            xr, xi = xr + mr * rr - mi * ri, xi + mr * ri + mi * rr
        pr, pi = tabs[6], tabs[7]
        xr, xi = xr + pr * cr - pi * ci, xi + pr * ci + pi * cr
        cr, ci = jnp.broadcast_to(xr[edge:edge + 1], xr.shape), jnp.broadcast_to(xi[edge:edge + 1], xi.shape)
        out_r[tile], out_i[tile] = xr, xi
    return jnp.concatenate(out_r, axis=0), jnp.concatenate(out_i, axis=0), cr, ci


def _s5_kernel(uz_ref, ul_ref, wc_ref, kc_ref, cc_ref, ew_ref, ek_ref, ec_ref, tab_ref, d_ref, *rest, reverse, final):
    if final:
        yfz_ref, yfl_ref, gw_ref, gb_ref, oz_ref, ol_ref, carry_ref, uf_ref, yc_ref, b_ref, k_ref, c_ref = rest
    else:
        oz_ref, ol_ref, carry_ref, uf_ref, yc_ref, b_ref, k_ref, c_ref = rest
    s = pl.program_id(0)
    nblk = b_ref.shape[0]
    ps = b_ref.shape[2] // 2
    gq = LANES // S5_GROUP_CH

    @pl.when(s == 0)
    def _():
        carry_ref[...] = jnp.zeros_like(carry_ref)

        def expand(compact, spread, row_group, per):
            full = jnp.dot(compact, spread, preferred_element_type=F32)
            col_group = (lax.broadcasted_iota(jnp.int32, (1, full.shape[1]), 1) // per) % gq
            return jnp.where(row_group == col_group, full, 0.0).astype(BF16)

        rows = lax.broadcasted_iota(jnp.int32, (wc_ref.shape[1], 1), 0)
        lag_rows_group = (rows // S5_GROUP_CH) % gq
        state_rows_group = (rows // S5_STATE) % gq
        for q in range(nblk):
            b_ref[q] = expand(wc_ref[q], ew_ref[...], lag_rows_group, S5_STATE)
            k_ref[q] = expand(kc_ref[q], ek_ref[...], lag_rows_group, S5_GROUP_CH)
            c_ref[q] = expand(cc_ref[q], ec_ref[...], state_rows_group, S5_GROUP_CH)

    def chunk(u_ref, yf_ref, o_ref):
        t = u_ref.shape[0]
        nw = t // SUBLANES
        row = lax.broadcasted_iota(jnp.int32, (t, 1), 0) & (SUBLANES - 1)
        wrow = lax.broadcasted_iota(jnp.int32, (nw, 1), 0)
        u = u_ref[...]
        uf = u.astype(F32)
        for q in range(nblk):
            uf_ref[q, 0:t] = uf[:, q * LANES:(q + 1) * LANES]
        lagged = [u]
        for k in range(1, SUBLANES):
            if reverse:
                lag = jnp.where(row < SUBLANES - k, pltpu.roll(uf, t - k, 0), 0.0)
            else:
                lag = jnp.where(row >= k, pltpu.roll(uf, k, 0), 0.0)
            lagged.append(lag.astype(BF16))
        outs = []
        for q in range(nblk):
            qs = slice(q * ps, (q + 1) * ps)
            lhs = jnp.concatenate([x[:, q * LANES:(q + 1) * LANES] for x in lagged], axis=1)
            y_loc = jnp.dot(lhs, k_ref[q], preferred_element_type=F32)
            ends = [uf_ref[q, pl.ds(k if reverse else SUBLANES - 1 - k, nw, stride=SUBLANES), :].astype(BF16)
                    for k in range(SUBLANES)]
            sw = jnp.dot(jnp.concatenate(ends, axis=1), b_ref[q], preferred_element_type=F32)
            tabs = [tab_ref[i, :, qs] for i in range(8)]
            cin_r, cin_i = carry_ref[0, :, qs], carry_ref[1, :, qs]
            c_r, c_i, cr, ci = _s5_window_scan(sw[:, :ps], sw[:, ps:], tabs, cin_r, cin_i, reverse)
            carry_ref[0, :, qs] = cr
            carry_ref[1, :, qs] = ci
            if reverse:
                e_r = jnp.where(wrow == nw - 1, cin_r[0:1], pltpu.roll(c_r, nw - 1, 0))
                e_i = jnp.where(wrow == nw - 1, cin_i[0:1], pltpu.roll(c_i, nw - 1, 0))
            else:
                e_r = jnp.where(wrow == 0, cin_r[0:1], pltpu.roll(c_r, 1, 0))
                e_i = jnp.where(wrow == 0, cin_i[0:1], pltpu.roll(c_i, 1, 0))
            y_in = jnp.dot(jnp.concatenate([e_r, e_i], axis=1).astype(BF16), c_ref[q],
                           preferred_element_type=F32)
            for j in range(SUBLANES):
                yc_ref[q, pl.ds(j, nw, stride=SUBLANES), :] = y_in[:, j * LANES:(j + 1) * LANES]
            outs.append(y_loc + yc_ref[q, 0:t])
        out = jnp.concatenate(outs, axis=1)
        if not final:
            o_ref[...] = u.astype(F32) * d_ref[...] + out
        else:
            g = _gelu_tanh(yf_ref[...] + out)
            z = jnp.dot(g.astype(BF16), gw_ref[...], preferred_element_type=F32) + gb_ref[...]
            o_ref[...] = (g * _sigmoid(z)).astype(o_ref.dtype)

    @pl.when(s == 0)
    def _():
        chunk(uz_ref, yfz_ref if final else None, oz_ref)

    @pl.when(s > 0)
    def _():
        chunk(ul_ref, yfl_ref if final else None, ol_ref)


def s5_direction(rest, rest_ctx, u_col, params, which, dskip, reverse, fwd=None, glu=None, t=1024):
    seq = rest.shape[0]
    tz = rest_ctx.shape[0]
    t = min(t, seq)
    nl = seq // t
    w = 4 * LANES
    wc, kc, cc, tabs = params
    _, nq, kq, _ = wc.shape
    gq = LANES // S5_GROUP_CH
    ps2 = 2 * gq * S5_STATE
    p2 = nq * ps2
    e_w = np.zeros((LANES, ps2), np.float32)
    for c in range(2):
        for gl in range(gq):
            for p in range(S5_STATE):
                e_w[c * S5_STATE + p, c * gq * S5_STATE + gl * S5_STATE + p] = 1.0
    e_k = np.zeros((LANES, LANES), np.float32)
    e_c = np.zeros((LANES, SUBLANES * LANES), np.float32)
    for gl in range(gq):
        for o in range(S5_GROUP_CH):
            e_k[o, gl * S5_GROUP_CH + o] = 1.0
            for j in range(SUBLANES):
                e_c[j * S5_GROUP_CH + o, j * LANES + gl * S5_GROUP_CH + o] = 1.0
    spreads = [jnp.asarray(e, BF16) for e in (e_w, e_k, e_c)]
    final = fwd is not None
    if reverse:
        lat = lambda s: (jnp.where(s == 0, nl - 1, nl - s), 0)
    else:
        lat = lambda s: (jnp.maximum(s - 1, 0), 0)
    lat_u = lambda s: (lat(s)[0], u_col)
    const = lambda s: (0, 0)
    table = lambda shape: pl.BlockSpec((None,) + shape, lambda s: (which,) + (0,) * len(shape),
                                       pipeline_mode=pl.Buffered(1))
    in_specs = [pl.BlockSpec((tz, w), lambda s: (0, u_col)),
                pl.BlockSpec((t, w), lat_u),
                table((nq, kq, LANES)),
                table((nq, kq, LANES)),
                table((nq, ps2, LANES)),
                pl.BlockSpec(e_w.shape, const),
                pl.BlockSpec(e_k.shape, const),
                pl.BlockSpec(e_c.shape, const),
                table((8, SUBLANES, p2 // 2)),
                pl.BlockSpec((1, w), const)]
    args = [rest_ctx, rest, wc, kc, cc, *spreads, tabs, dskip.reshape(1, w)]
    if final:
        in_specs += [pl.BlockSpec((tz, w), const), pl.BlockSpec((t, w), lat),
                     pl.BlockSpec((w, w), const), pl.BlockSpec((1, w), const)]
        args += [fwd[0], fwd[1], glu[0], glu[1].reshape(1, w)]
    odt = BF16 if final else F32
    return pl.pallas_call(
        functools.partial(_s5_kernel, reverse=reverse, final=final),
        grid=(nl + 1,),
        in_specs=in_specs,
        out_specs=[pl.BlockSpec((tz, w), const), pl.BlockSpec((t, w), lat)],
        out_shape=[jax.ShapeDtypeStruct((tz, w), odt), jax.ShapeDtypeStruct((seq, w), odt)],
        scratch_shapes=[pltpu.VMEM((2, SUBLANES, p2 // 2), F32),
                        pltpu.VMEM((nq, max(t, tz), LANES), F32),
                        pltpu.VMEM((nq, max(t, tz), LANES), F32),
                        pltpu.VMEM((nq, kq, ps2), BF16),
                        pltpu.VMEM((nq, kq, LANES), BF16),
                        pltpu.VMEM((nq, ps2, SUBLANES * LANES), BF16)],
        compiler_params=_cparams("arbitrary"),
        name="s5_direction",
    )(*args)


def _s5_params(a_re, a_im, log_step, b_re, b_im, c_re, c_im, rev):
    ar, ai = a_re.astype(F32), a_im.astype(F32)
    step = jnp.exp(log_step.astype(F32))[:, None]
    mag = jnp.exp(ar * step)
    ab_re, ab_im = mag * jnp.cos(ai * step), mag * jnp.sin(ai * step)
    den = ar * ar + ai * ai
    nr = ab_re - 1.0
    f_re = (nr * ar + ab_im * ai) / den
    f_im = (ab_im * ar - nr * ai) / den
    br, bi = b_re.astype(F32), b_im.astype(F32)
    bb_re = f_re[..., None] * br - f_im[..., None] * bi
    bb_im = f_re[..., None] * bi + f_im[..., None] * br
    cr, ci = c_re.astype(F32), c_im.astype(F32)
    g, p, h = bb_re.shape
    gq = LANES // h
    nq = g // gq
    nl = SUBLANES

    def cmul(xr, xi, yr, yi):
        return xr * yr - xi * yi, xr * yi + xi * yr

    def powers(xr, xi, n):
        out = [(jnp.ones_like(xr), jnp.zeros_like(xi)), (xr, xi)]
        for _ in range(n - 1):
            out.append(cmul(*out[-1], xr, xi))
        return jnp.stack([o[0] for o in out]), jnp.stack([o[1] for o in out])

    pr, pi = powers(ab_re, ab_im, nl)
    w_re = pr[:nl, ..., None] * bb_re - pi[:nl, ..., None] * bb_im
    w_im = pr[:nl, ..., None] * bb_im + pi[:nl, ..., None] * bb_re

    def rows_lag_group_channel(x):
        x = x.reshape(nl, nq, gq, h, x.shape[-1])
        return jnp.transpose(x, (1, 0, 2, 3, 4)).reshape(nq, nl * gq * h, x.shape[-1])

    wc = jnp.concatenate([rows_lag_group_channel(jnp.swapaxes(w, 2, 3)) for w in (w_re, w_im)], axis=2)
    hp = lax.Precision.HIGHEST
    kk = (jnp.einsum('kgph,gop->kgho', w_re, cr, precision=hp)
          - jnp.einsum('kgph,gop->kgho', w_im, ci, precision=hp))
    kc = jnp.pad(rows_lag_group_channel(kk), ((0, 0), (0, 0), (0, LANES - h)))
    dr = jnp.where(rev, pr[nl:0:-1], pr[1:nl + 1])
    di = jnp.where(rev, pi[nl:0:-1], pi[1:nl + 1])
    cd_re = cr[None] * dr[:, :, None, :] - ci[None] * di[:, :, None, :]
    cd_im = cr[None] * di[:, :, None, :] + ci[None] * dr[:, :, None, :]

    def rows_group_state(cd):
        return jnp.transpose(cd.reshape(nl, nq, gq, h, p), (1, 2, 4, 0, 3)).reshape(nq, gq * p, nl * h)

    cc = jnp.concatenate([rows_group_state(cd_re), -rows_group_state(cd_im)], axis=1)
    qr, qi = powers(pr[nl].reshape(-1), pi[nl].reshape(-1), SUBLANES)
    row = np.arange(SUBLANES)[:, None]
    tabs = []
    for k in (1, 2, 4):
        keep = jnp.where(rev, jnp.asarray(row < SUBLANES - k), jnp.asarray(row >= k))
        tabs += [jnp.where(keep, qr[k][None], 0.0), jnp.where(keep, qi[k][None], 0.0)]
    tabs += [jnp.where(rev, qr[SUBLANES:0:-1], qr[1:SUBLANES + 1]),
             jnp.where(rev, qi[SUBLANES:0:-1], qi[1:SUBLANES + 1])]
    return wc.astype(BF16), kc.astype(BF16), cc.astype(BF16), jnp.stack(tabs)


def _conv_kernel(val_ref, gate_ref, vp_ref, gp_ref, vn_ref, gn_ref, w_ref, b_ref, g_ref, beta_ref, o_ref, hs_ref,
                 *, sub):
    i = pl.program_id(0)
    n = pl.num_programs(0)
    t = val_ref.shape[0]

    def glu(v, g):
        return v[...].astype(F32) * _sigmoid(g[...].astype(F32))

    hs_ref[0:CONV_HALO] = jnp.where(i > 0, glu(vp_ref, gp_ref), 0.0)
    hs_ref[CONV_HALO:CONV_HALO + t] = glu(val_ref, gate_ref)
    hs_ref[CONV_HALO + t:2 * CONV_HALO + t] = jnp.where(i < n - 1, glu(vn_ref, gn_ref), 0.0)
    base = CONV_HALO - CONV_K // 2
    for r0 in range(t // sub):
        acc = jnp.zeros((sub, val_ref.shape[1]), F32) + b_ref[...]
        for r in range(SUBLANES):
            part = None
            for m in range((base + CONV_K - 1) // SUBLANES + 1):
                k = SUBLANES * m + r - base
                if 0 <= k < CONV_K:
                    lo_row = r0 * sub + SUBLANES * m
                    term = hs_ref[lo_row:lo_row + sub + SUBLANES] * w_ref[k:k + 1]
                    part = term if part is None else part + term
            acc = acc + part[r:r + sub]
        mu = jnp.mean(acc, axis=-1, keepdims=True)
        cen = acc - mu
        var = jnp.mean(cen * cen, axis=-1, keepdims=True)
        yn = cen * lax.rsqrt(var + EPS) * g_ref[...] + beta_ref[...]
        o_ref[r0 * sub:(r0 + 1) * sub] = _silu(yn).astype(o_ref.dtype)


def conv_module(rest, val_col, gate_col, w, b, g, beta, t=256, sub=64):
    seq = rest.shape[0]
    t = min(t, seq)
    cw = 4 * LANES
    hb = t // CONV_HALO
    nh = seq // CONV_HALO
    prev = lambda i: jnp.maximum(i * hb - 1, 0)
    nxt = lambda i: jnp.minimum((i + 1) * hb, nh - 1)
    vec = pl.BlockSpec((1, cw), lambda i: (0, 0))
    return pl.pallas_call(
        functools.partial(_conv_kernel, sub=sub),
        grid=(seq // t,),
        in_specs=[pl.BlockSpec((t, cw), lambda i: (i, val_col)),
                  pl.BlockSpec((t, cw), lambda i: (i, gate_col)),
                  pl.BlockSpec((CONV_HALO, cw), lambda i: (prev(i), val_col)),
                  pl.BlockSpec((CONV_HALO, cw), lambda i: (prev(i), gate_col)),
                  pl.BlockSpec((CONV_HALO, cw), lambda i: (nxt(i), val_col)),
                  pl.BlockSpec((CONV_HALO, cw), lambda i: (nxt(i), gate_col)),
                  pl.BlockSpec((CONV_K, cw), lambda i: (0, 0)), vec, vec, vec],
        out_specs=pl.BlockSpec((t, cw), lambda i: (i, 0)),
        out_shape=jax.ShapeDtypeStruct((seq, cw), BF16),
        scratch_shapes=[pltpu.VMEM((t + 2 * CONV_HALO, cw), F32)],
        compiler_params=_cparams("arbitrary"),
        name="conv_module",
    )(rest, rest, rest, rest, rest, rest, w.astype(F32), b.reshape(1, cw).astype(F32),
      g.reshape(1, cw).astype(F32), beta.reshape(1, cw).astype(F32))


def _merge_kernel(*refs, n_parts):
    y_refs = refs[:N_BRANCH]
    gl_refs = refs[N_BRANCH:N_BRANCH * (1 + n_parts)]
    wb_ref, wo_ref, x_ref, gate_ref, o_ref = refs[N_BRANCH * (1 + n_parts):]
    pw = gl_refs[0].shape[1]
    parts = []
    for part in range(n_parts):
        cols = slice(part * pw, (part + 1) * pw)
        m = None
        for k in range(N_BRANCH):
            proj = jnp.dot(y_refs[k][...], wb_ref[k, :, cols], preferred_element_type=F32)
            term = _sigmoid(gl_refs[k * n_parts + part][...].astype(F32)) * proj
            m = term if m is None else m + term
        parts.append(m.astype(BF16))
    merged = jnp.concatenate(parts, axis=1)
    o_ref[...] = x_ref[...] + gate_ref[...] * jnp.dot(merged, wo_ref[...], preferred_element_type=F32)


def gated_merge_residual(ys, rest, logit_col, part_w, wb, wo, layer, x, gate, tm=256):
    m, yw = ys[0].shape
    n = wb.shape[3]
    n_parts = n // part_w
    tm = min(tm, m)
    resident = lambda shape: pl.BlockSpec((None,) + shape, lambda i: (layer,) + (0,) * len(shape),
                                          pipeline_mode=pl.Buffered(1))
    yspec = pl.BlockSpec((tm, yw), lambda i: (i, 0))
    gspecs = [pl.BlockSpec((tm, part_w), functools.partial(lambda i, c: (i, c), c=logit_col + k * n_parts + part))
              for k in range(N_BRANCH) for part in range(n_parts)]
    return pl.pallas_call(
        functools.partial(_merge_kernel, n_parts=n_parts),
        grid=(m // tm,),
        in_specs=[yspec] * N_BRANCH + gspecs + [
            resident((N_BRANCH, yw, n)),
            resident((n, n)),
            pl.BlockSpec((tm, n), lambda i: (i, 0)),
            pl.BlockSpec((1, n), lambda i: (0, 0))],
        out_specs=pl.BlockSpec((tm, n), lambda i: (i, 0)),
        out_shape=jax.ShapeDtypeStruct((m, n), F32),
        compiler_params=_cparams("arbitrary"),
        name="gated_merge_residual",
    )(*ys, *([rest] * (N_BRANCH * n_parts)), wb, wo, x, gate.reshape(1, n))


def _mlp_kernel(x_ref, g_ref, sc_ref, sh_ref, gate_ref, w1_ref, w2_ref, *rest, final_norm):
    if final_norm:
        gf_ref, o_ref, h_ref, acc_ref = rest
    else:
        o_ref, h_ref, acc_ref = rest
    f = pl.program_id(1)

    @pl.when(f == 0)
    def _():
        x = x_ref[...]
        y = x * lax.rsqrt(jnp.mean(x * x, axis=-1, keepdims=True) + EPS) * g_ref[...]
        h_ref[...] = (y * (1.0 + sc_ref[...]) + sh_ref[...]).astype(BF16)
        acc_ref[...] = jnp.zeros_like(acc_ref)

    t = jnp.dot(h_ref[...], w1_ref[...], preferred_element_type=F32)
    t = jnp.square(jnp.maximum(t, 0.0)).astype(BF16)
    acc_ref[...] += jnp.dot(t, w2_ref[...], preferred_element_type=F32)

    @pl.when(f == pl.num_programs(1) - 1)
    def _():
        y = x_ref[...] + gate_ref[...] * acc_ref[...]
        if final_norm:
            y = y * lax.rsqrt(jnp.mean(y * y, axis=-1, keepdims=True) + EPS) * gf_ref[...]
        o_ref[...] = y


def mlp_block(x, g, scale, shift, gate, w1, w2, layer, final_g=None, tm=512, tf=1024):
    m, d = x.shape
    ff = w1.shape[2]
    tm = min(tm, m)
    vec = pl.BlockSpec((1, d), lambda i, f: (0, 0))
    final_norm = final_g is not None
    args = [x, g.reshape(1, d), scale.reshape(1, d), shift.reshape(1, d), gate.reshape(1, d), w1, w2]
    in_specs = [pl.BlockSpec((tm, d), lambda i, f: (i, 0)), vec, vec, vec, vec,
                pl.BlockSpec((None, d, tf), lambda i, f: (layer, 0, f)),
                pl.BlockSpec((None, tf, d), lambda i, f: (layer, f, 0))]
    if final_norm:
        args.append(final_g.reshape(1, d))
        in_specs.append(vec)
    return pl.pallas_call(
        functools.partial(_mlp_kernel, final_norm=final_norm),
        grid=(m // tm, ff // tf),
        in_specs=in_specs,
        out_specs=pl.BlockSpec((tm, d), lambda i, f: (i, 0)),
        out_shape=jax.ShapeDtypeStruct((m, d), F32),
        scratch_shapes=[pltpu.VMEM((tm, d), BF16), pltpu.VMEM((tm, d), F32)],
        compiler_params=_cparams("arbitrary", "arbitrary"),
        name="mlp_block",
    )(*args)


_A_COL_K, _A_COL_V = 2, 3
_REST_COL0 = (A_HEADS + 2 * A_KV_HEADS) * HEAD_DIM
_COL_U, _COL_QC, _COL_KC, _COL_VC, _COL_VAL, _COL_GATE = 0, 1, 2, 3, 4, 5
_MERGE_PART = 1024
_COL_MERGE = 3


def _rope_tables(seq):
    freqs = ROPE_BASE ** (-jnp.arange(ROPE_PAIRS, dtype=F32) / ROPE_PAIRS)
    rows = seq // GRID_W
    ang_r = jnp.arange(rows).astype(F32)[:, None] * freqs
    ang_c = jnp.arange(GRID_W).astype(F32)[:, None] * freqs

    def expand(tab_r, tab_c, first, second):
        r = jnp.repeat(tab_r, GRID_W, axis=0)
        cc = jnp.tile(tab_c, (rows, 1))
        head = jnp.concatenate([first * r, second * r, first * cc, second * cc], axis=1)
        return jnp.tile(head, (1, LANES // HEAD_DIM))

    cos = expand(jnp.cos(ang_r), jnp.cos(ang_c), 1.0, 1.0)
    sa = expand(jnp.sin(ang_r), jnp.sin(ang_c), 0.0, 1.0)
    sb = expand(jnp.sin(ang_r), jnp.sin(ang_c), -1.0, 0.0)
    return cos, sa, sb


def kernel(x, c, ctx, c_ctx, ada_w, ada_b, norm1_g, norm2_g, w_in, a_sink, s5_a_re, s5_a_im, s5_log_step, s5_b_re, s5_b_im, s5_c_re, s5_c_im, s5_d, s5_glu_w, s5_glu_b, c_rpb, d_conv_w, d_conv_b, d_ln_g, d_ln_b, w_branch, w_out, mlp_w1, mlp_w2, final_g):
    depth = ada_w.shape[0]
    _, seq, d = x.shape
    lc = ctx.shape[1]
    xs = x.reshape(seq, d)
    zs = ctx.reshape(lc, d)
    rope = _rope_tables(seq)

    cc = jnp.zeros((SUBLANES, d), F32).at[0].set(c[0]).at[1].set(c_ctx)
    mods = ada_modulation(cc, ada_w, ada_b)
    n_rest = w_in.shape[2] - _REST_COL0
    w1, w2 = mlp_w1.astype(BF16), mlp_w2.astype(BF16)
    wb, wo = w_branch.astype(BF16), w_out.astype(BF16)
    s5_params = jax.vmap(_s5_params)(*(a.reshape((2 * depth,) + a.shape[2:]) for a in
                                       (s5_a_re, s5_a_im, s5_log_step, s5_b_re, s5_b_im, s5_c_re, s5_c_im)),
                                     jnp.arange(2 * depth) % 2 == 1)

    for l in range(depth):
        need_ctx = l < depth - 1
        mx = [mods[l, 0, k * d:(k + 1) * d] for k in range(6)]
        mz = [mods[l, 1, k * d:(k + 1) * d] for k in range(6)]

        hx, qk_x = norm_head_projection(xs, norm1_g[l], mx[1], mx[0], w_in, l, rope=rope)
        hz, qk_z = norm_head_projection(zs, norm1_g[l], mz[1], mz[0], w_in, l)
        r_x = wide_projection(hx, w_in, l, _REST_COL0, n_rest)
        r_z = wide_projection(hz, w_in, l, _REST_COL0, n_rest)

        ya_x = window_attention(a_sink[l], qk_x, qk_x, qk_z, qk_z, _A_COL_V)
        yc_x = neighborhood_attention(r_x, r_z, c_rpb[l], _COL_QC, _COL_KC, _COL_VC)
        yd_x = conv_module(r_x, _COL_VAL, _COL_GATE, d_conv_w[l], d_conv_b[l], d_ln_g[l], d_ln_b[l])

        yf = s5_direction(r_x, r_z, _COL_U, s5_params, 2 * l, s5_d[l].astype(F32), reverse=False)
        yb_z, yb_x = s5_direction(r_x, r_z, _COL_U, s5_params, 2 * l + 1, s5_d[l].astype(F32), reverse=True, fwd=yf,
                                  glu=(s5_glu_w[l].astype(BF16), s5_glu_b[l].astype(F32)))

        xs = gated_merge_residual((ya_x, yb_x, yc_x, yd_x), r_x, _COL_MERGE, _MERGE_PART, wb, wo, l, xs, mx[2])
        xs = mlp_block(xs, norm2_g[l], mx[4], mx[3], mx[5], w1, w2, l, final_g=None if need_ctx else final_g)

        if need_ctx:
            ya_z = context_attention(a_sink[l], qk_z, 0, qk_z, _A_COL_K, 2 * LANES, qk_z, _A_COL_V, True, True)
            yc_z = context_attention(a_sink[l], r_z, _COL_QC, r_z, _COL_KC, 4 * LANES, r_z, _COL_VC, False, False)
            yd_z = conv_module(r_z, _COL_VAL, _COL_GATE, d_conv_w[l], d_conv_b[l], d_ln_g[l], d_ln_b[l])
            zs = gated_merge_residual((ya_z, yb_z, yc_z, yd_z), r_z, _COL_MERGE, _MERGE_PART, wb, wo, l, zs, mz[2])
            zs = mlp_block(zs, norm2_g[l], mz[4], mz[3], mz[5], w1, w2, l)

    return xs.reshape(x.shape)
```

```python
import functools
import math

import numpy as np
import jax
import jax.numpy as jnp
from jax import lax
from jax.experimental import pallas as pl
from jax.experimental.pallas import tpu as pltpu

F32 = jnp.float32
BF16 = jnp.bfloat16

LANES = 128
SUBLANES = 8
VMEM_LIMIT = 56 * 1024 * 1024

GRID_W = 64
HEAD_DIM = 64
N_BRANCH = 4
A_HEADS = 8
A_KV_HEADS = 2
A_WINDOW = 128
A_BLOCK = 128
ROPE_BASE = 10000.0
ROPE_PAIRS = HEAD_DIM // 4
S5_GROUP_CH = 16
S5_STATE = 64
C_HEADS = 8
C_WIN_R = 8
C_WIN_C = 16
C_QROWS = 2
C_KROWS = C_QROWS + C_WIN_R
CONV_K = 31
CONV_HALO = 16
EPS = 1e-6
NEG_INF = -1e30


def _cparams(*sem):
    return pltpu.CompilerParams(dimension_semantics=sem, vmem_limit_bytes=VMEM_LIMIT)


def _sigmoid(x):
    return 0.5 * jnp.tanh(0.5 * x) + 0.5


def _silu(x):
    return x * _sigmoid(x)


def _gelu_tanh(x):
    return 0.5 * x * (1.0 + jnp.tanh(math.sqrt(2.0 / math.pi) * (x + 0.044715 * (x * x * x))))


def _ada_kernel(c_ref, w_ref, b_ref, o_ref):
    a = _silu(c_ref[...]).astype(BF16)
    o_ref[...] = jnp.dot(a, w_ref[...].astype(BF16), preferred_element_type=F32) + b_ref[...]


def ada_modulation(cc, ada_w, ada_b, tn=1024):
    depth, d, n = ada_w.shape
    return pl.pallas_call(
        _ada_kernel,
        grid=(depth, n // tn),
        in_specs=[pl.BlockSpec((SUBLANES, d), lambda l, j: (0, 0)),
                  pl.BlockSpec((None, d, tn), lambda l, j: (l, 0, j)),
                  pl.BlockSpec((None, 1, tn), lambda l, j: (l, 0, j))],
        out_specs=pl.BlockSpec((None, SUBLANES, tn), lambda l, j: (l, 0, j)),
        out_shape=jax.ShapeDtypeStruct((depth, SUBLANES, n), F32),
        compiler_params=_cparams("arbitrary", "arbitrary"),
        name="ada_modulation",
    )(cc, ada_w, ada_b.reshape(depth, 1, n))


def _rope_block(y, cos, sa, sb):
    return y * cos + pltpu.roll(y, 16, 1) * sa + pltpu.roll(y, LANES - 16, 1) * sb


def _head_proj_kernel(x_ref, g_ref, sc_ref, sh_ref, w_ref, *rest, n_q, with_rope):
    if with_rope:
        cos_ref, sa_ref, sb_ref, h_ref, o_ref, wb_ref = rest
        cos, sa, sb = cos_ref[...], sa_ref[...], sb_ref[...]
    else:
        h_ref, o_ref, wb_ref = rest

    @pl.when(pl.program_id(0) == 0)
    def _():
        wb_ref[...] = w_ref[0].astype(BF16)

    x = x_ref[...]
    y = x * lax.rsqrt(jnp.mean(x * x, axis=-1, keepdims=True) + EPS) * g_ref[...]
    h = (y * (1.0 + sc_ref[...]) + sh_ref[...]).astype(BF16)
    h_ref[...] = h
    acc = jnp.dot(h, wb_ref[...], preferred_element_type=F32)
    lo = _lane_lo()
    out = 0
    for b in range(acc.shape[1] // LANES):
        y = acc[:, b * LANES:(b + 1) * LANES]
        if with_rope and b <= n_q:
            y = _rope_block(y, cos, sa, sb)
        if b < n_q:
            blocks = [y]
        else:
            swapped = pltpu.roll(y, HEAD_DIM, 1)
            blocks = [jnp.where(lo, y, swapped), jnp.where(lo, swapped, y)]
        for blk in blocks:
            o_ref[:, out * LANES:(out + 1) * LANES] = blk.astype(o_ref.dtype)
            out += 1


def norm_head_projection(x, g, scale, shift, w_in, layer, rope=None, tm=512):
    m, k = x.shape
    n_in = (A_HEADS + 2 * A_KV_HEADS) * HEAD_DIM
    n_q = A_HEADS * HEAD_DIM // LANES
    assert A_KV_HEADS * HEAD_DIM == LANES
    n_out = n_in + 2 * LANES
    tm = min(tm, m)
    vec = pl.BlockSpec((1, k), lambda i: (0, 0))
    in_specs = [pl.BlockSpec((tm, k), lambda i: (i, 0)), vec, vec, vec,
                pl.BlockSpec((pl.Element(1), pl.Element(k), pl.Element(n_in)), lambda i: (layer, 0, 0))]
    args = [x, g.reshape(1, k), scale.reshape(1, k), shift.reshape(1, k), w_in]
    if rope is not None:
        in_specs += [pl.BlockSpec((tm, LANES), lambda i: (i, 0))] * 3
        args += list(rope)
    return pl.pallas_call(
        functools.partial(_head_proj_kernel, n_q=n_q, with_rope=rope is not None),
        grid=(m // tm,),
        in_specs=in_specs,
        out_specs=[pl.BlockSpec((tm, k), lambda i: (i, 0)), pl.BlockSpec((tm, n_out), lambda i: (i, 0))],
        out_shape=[jax.ShapeDtypeStruct((m, k), BF16), jax.ShapeDtypeStruct((m, n_out), BF16)],
        scratch_shapes=[pltpu.VMEM((k, n_in), BF16)],
        compiler_params=_cparams("arbitrary"),
        name="norm_head_projection",
    )(*args)


def _wide_proj_kernel(a_ref, w_ref, o_ref, wb_ref):
    @pl.when(pl.program_id(1) == 0)
    def _():
        wb_ref[...] = w_ref[0].astype(BF16)

    o_ref[...] = jnp.dot(a_ref[...], wb_ref[...], preferred_element_type=F32).astype(o_ref.dtype)


def wide_projection(a, w, layer, col0, n, tm=1024, tn=1024):
    m, k = a.shape
    tm = min(tm, m)
    assert n % tn == 0 and col0 % LANES == 0 and tn % LANES == 0
    return pl.pallas_call(
        _wide_proj_kernel,
        grid=(n // tn, m // tm),
        in_specs=[pl.BlockSpec((tm, k), lambda j, i: (i, 0)),
                  pl.BlockSpec((pl.Element(1), pl.Element(k), pl.Element(tn)),
                               lambda j, i: (layer, 0, pl.multiple_of(col0 + j * tn, LANES)))],
        out_specs=pl.BlockSpec((tm, tn), lambda j, i: (i, j)),
        out_shape=jax.ShapeDtypeStruct((m, n), BF16),
        scratch_shapes=[pltpu.VMEM((k, tn), BF16)],
        compiler_params=_cparams("arbitrary", "arbitrary"),
        name="wide_projection",
    )(a, w)


def _lane_lo():
    return lax.broadcasted_iota(jnp.int32, (1, LANES), 1) < HEAD_DIM


def _half_select(x, lo, half):
    zero = jnp.zeros_like(x)
    return jnp.where(lo, x, zero) if half == 0 else jnp.where(lo, zero, x)


def _dot_nt(a, b):
    return lax.dot_general(a, b, (((1,), (1,)), ((), ())), preferred_element_type=F32)


def _softmax_pv(parts, values, sink=None):
    m = parts[0].max(axis=-1, keepdims=True)
    for s in parts[1:]:
        m = jnp.maximum(m, s.max(axis=-1, keepdims=True))
    if sink is not None:
        m = jnp.maximum(m, sink)
    den = jnp.exp(sink - m) if sink is not None else 0.0
    out = 0.0
    for s, v in zip(parts, values):
        p = jnp.exp(s - m)
        den = den + p.sum(axis=-1, keepdims=True)
        out = out + jnp.dot(p.astype(BF16), v, preferred_element_type=F32)
    return out / den


def _attn_a_window_start(i, seq):
    return jnp.clip((i - 1) * A_BLOCK, 0, seq - 3 * A_BLOCK)


def _attn_a_kernel(sink_ref, q_ref, kv_ref, kvz_ref, o_ref, *, seq):
    i = pl.program_id(0)
    lo = _lane_lo()
    nq = 4 * A_BLOCK
    r = lax.broadcasted_iota(jnp.int32, (nq, 3 * A_BLOCK), 0) & (A_BLOCK - 1)
    c = lax.broadcasted_iota(jnp.int32, (nq, 3 * A_BLOCK), 1)
    dist = c - r + (_attn_a_window_start(i, seq) - i * A_BLOCK)
    valid = (dist >= -A_WINDOW) & (dist <= A_WINDOW)
    rowblk = lax.broadcasted_iota(jnp.int32, (nq, 1), 0) // A_BLOCK
    vcol = A_KV_HEADS * LANES
    for g in range(A_KV_HEADS):
        gs = slice(g * LANES, (g + 1) * LANES)
        vs = slice(vcol + g * LANES, vcol + (g + 1) * LANES)
        kw, vw = kv_ref[:, gs], kv_ref[:, vs]
        kz, vz = kvz_ref[:, gs], kvz_ref[:, vs]
        blocks = [slice((2 * g + bb) * LANES, (2 * g + bb + 1) * LANES) for bb in range(2)]
        ql = jnp.concatenate([_half_select(q_ref[:, bs], lo, half) * (HEAD_DIM ** -0.5)
                              for half in range(2) for bs in blocks], axis=0)
        heads = [2 * (2 * g + bb) + half for half in range(2) for bb in range(2)]
        sink = jnp.full((nq, 1), sink_ref[heads[3]], F32)
        for k in range(3):
            sink = jnp.where(rowblk == k, sink_ref[heads[k]], sink)
        s = jnp.where(valid, _dot_nt(ql, kw), NEG_INF)
        sz = _dot_nt(ql, kz)
        m = jnp.maximum(jnp.maximum(s.max(axis=-1, keepdims=True), sz.max(axis=-1, keepdims=True)), sink)
        p, pz = jnp.exp(s - m), jnp.exp(sz - m)
        inv = 1.0 / (p.sum(axis=-1, keepdims=True) + pz.sum(axis=-1, keepdims=True) + jnp.exp(sink - m))
        p, pz = p.astype(BF16), pz.astype(BF16)
        out = None
        for half in range(2):
            hs = slice(half * 2 * A_BLOCK, (half + 1) * 2 * A_BLOCK)
            o = (jnp.dot(p[hs], _half_select(vw, lo, half), preferred_element_type=F32)
                 + jnp.dot(pz[hs], _half_select(vz, lo, half), preferred_element_type=F32)) * inv[hs]
            out = o if out is None else out + o
        for bb in range(2):
            o_ref[:, blocks[bb]] = out[bb * A_BLOCK:(bb + 1) * A_BLOCK].astype(o_ref.dtype)


def window_attention(sink, qk, qk_ctx):
    seq = qk.shape[0]
    lc = qk_ctx.shape[0]
    nb = seq // A_BLOCK
    qw = A_HEADS * HEAD_DIM
    kvw = 4 * A_KV_HEADS * HEAD_DIM
    assert nb >= 3 and qw == kvw
    return pl.pallas_call(
        functools.partial(_attn_a_kernel, seq=seq),
        grid=(nb,),
        in_specs=[pl.BlockSpec(memory_space=pltpu.SMEM),
                  pl.BlockSpec((A_BLOCK, qw), lambda i: (i, 0)),
                  pl.BlockSpec((pl.Element(3 * A_BLOCK), pl.Element(kvw)),
                               lambda i: (pl.multiple_of(_attn_a_window_start(i, seq), A_BLOCK), qw)),
                  pl.BlockSpec((lc, kvw), lambda i: (0, 1))],
        out_specs=pl.BlockSpec((A_BLOCK, qw), lambda i: (i, 0)),
        out_shape=jax.ShapeDtypeStruct((seq, qw), BF16),
        compiler_params=_cparams("arbitrary"),
        name="window_attention",
    )(sink, qk, qk, qk_ctx)


def _ctx_attn_kernel(sink_ref, q_ref, k_ref, v_ref, o_ref, *, kv_shared, has_sink):
    lo = _lane_lo()
    for b in range(q_ref.shape[1] // LANES):
        bs = slice(b * LANES, (b + 1) * LANES)
        kb = b // 2 if kv_shared else b
        ks = slice(kb * LANES, (kb + 1) * LANES)
        qb, k, v = q_ref[:, bs], k_ref[:, ks], v_ref[:, ks]
        acc = jnp.zeros((q_ref.shape[0], LANES), F32)
        for half in range(2):
            ql = _half_select(qb, lo, half) * (HEAD_DIM ** -0.5)
            sink = sink_ref[2 * b + half] if has_sink else None
            acc = acc + _softmax_pv([_dot_nt(ql, k)], [_half_select(v, lo, half)], sink=sink)
        o_ref[:, bs] = acc.astype(o_ref.dtype)


def context_attention(sink, q_arr, q_col, k_arr, k_col, k_w, v_arr, v_col, kv_shared, has_sink):
    lc = q_arr.shape[0]
    return pl.pallas_call(
        functools.partial(_ctx_attn_kernel, kv_shared=kv_shared, has_sink=has_sink),
        grid=(1,),
        in_specs=[pl.BlockSpec(memory_space=pltpu.SMEM),
                  pl.BlockSpec((lc, 4 * LANES), lambda i: (0, q_col)),
                  pl.BlockSpec((lc, k_w), lambda i: (0, k_col)),
                  pl.BlockSpec((lc, k_w), lambda i: (0, v_col))],
        out_specs=pl.BlockSpec((lc, 4 * LANES), lambda i: (0, 0)),
        out_shape=jax.ShapeDtypeStruct((lc, 4 * LANES), BF16),
        compiler_params=_cparams("arbitrary"),
        name="context_attention",
    )(sink, q_arr, k_arr, v_arr)


def _nbr_row_offsets(rows):
    nsteps = rows // C_QROWS
    nblk = C_KROWS * GRID_W // LANES
    wr = min(C_WIN_R, rows)
    reps = [0, 1, 2, nsteps - 2, nsteps - 1]
    dr = -np.ones((len(reps), C_QROWS, C_KROWS), np.int32)
    for v, i in enumerate(reps):
        w0 = C_QROWS * int(np.clip(i - 2, 0, nsteps - nblk))
        for a in range(C_QROWS):
            rq = C_QROWS * i + a
            k0 = int(np.clip(rq - C_WIN_R // 2, 0, rows - wr))
            for j in range(C_KROWS):
                if k0 <= w0 + j < k0 + wr:
                    dr[v, a, j] = w0 + j - rq + C_WIN_R - 1
    return dr


def _nbr_bias(rpb, rows):
    h = rpb.shape[0]
    qc = np.arange(GRID_W)[:, None]
    kc = np.arange(GRID_W)[None, :]
    cstart = np.clip(qc - C_WIN_C // 2, 0, GRID_W - C_WIN_C)
    col_ok = jnp.asarray((kc >= cstart) & (kc < cstart + C_WIN_C))
    pad = GRID_W - C_WIN_C
    ext = jnp.pad(rpb.astype(F32), ((0, 0), (0, 0), (pad, pad)), mode='edge')
    toep = jnp.stack([ext[:, :, GRID_W - 1 - q:2 * GRID_W - 1 - q] for q in range(GRID_W)], axis=2)
    toep = jnp.where(col_ok, toep, NEG_INF)
    masked = jnp.full((h, GRID_W, GRID_W), NEG_INF, F32)
    dr = _nbr_row_offsets(rows)
    variants = []
    for v in range(dr.shape[0]):
        qrows = [jnp.concatenate([toep[:, dr[v, a, j]] if dr[v, a, j] >= 0 else masked for j in range(C_KROWS)],
                                 axis=-1) for a in range(C_QROWS)]
        variants.append(jnp.concatenate(qrows, axis=-2).reshape(h * C_QROWS * GRID_W, C_KROWS * GRID_W))
    return jnp.stack(variants)


def _attn_c_kernel(q_ref, kv_ref, kvz_ref, bias_ref, o_ref):
    lo = _lane_lo()
    nq = q_ref.shape[0]
    nb = C_HEADS // 2
    vcol = C_HEADS * HEAD_DIM
    blocks = [slice(b * LANES, (b + 1) * LANES) for b in range(nb)]
    s_parts, sz_parts = [], []
    for bs in blocks:
        ql = jnp.concatenate([_half_select(q_ref[:, bs], lo, half) * (HEAD_DIM ** -0.5) for half in range(2)],
                             axis=0)
        s_parts.append(_dot_nt(ql, kv_ref[:, bs]))
        sz_parts.append(_dot_nt(ql, kvz_ref[:, bs]))
    s = jnp.concatenate(s_parts, axis=0) + bias_ref[...]
    sz = jnp.concatenate(sz_parts, axis=0)
    m = jnp.maximum(s.max(axis=-1, keepdims=True), sz.max(axis=-1, keepdims=True))
    p, pz = jnp.exp(s - m), jnp.exp(sz - m)
    inv = 1.0 / (p.sum(axis=-1, keepdims=True) + pz.sum(axis=-1, keepdims=True))
    p, pz = p.astype(BF16), pz.astype(BF16)
    for b, bs in enumerate(blocks):
        vs = slice(vcol + b * LANES, vcol + (b + 1) * LANES)
        vw, vz = kv_ref[:, vs], kvz_ref[:, vs]
        out = None
        for half in range(2):
            hs = slice((2 * b + half) * nq, (2 * b + half + 1) * nq)
            o = (jnp.dot(p[hs], _half_select(vw, lo, half), preferred_element_type=F32)
                 + jnp.dot(pz[hs], _half_select(vz, lo, half), preferred_element_type=F32)) * inv[hs]
            out = o if out is None else out + o
        o_ref[:, bs] = out.astype(o_ref.dtype)


def neighborhood_attention(rest, rest_ctx, rpb, q_col, k_col):
    seq = rest.shape[0]
    lc = rest_ctx.shape[0]
    rows = seq // GRID_W
    nsteps = rows // C_QROWS
    nblk = C_KROWS * GRID_W // LANES
    assert nsteps > nblk and C_QROWS * GRID_W == LANES and k_col % 2 == 0
    bias = _nbr_bias(rpb, rows)

    def first_blk(i):
        return jnp.clip(i - 2, 0, nsteps - nblk)

    def variant(i):
        return jnp.where(i < 2, i, jnp.where(i >= nsteps - 2, i - (nsteps - 2) + 3, 2))

    w = 4 * LANES
    return pl.pallas_call(
        _attn_c_kernel,
        grid=(nsteps,),
        in_specs=[pl.BlockSpec((LANES, w), lambda i: (i, q_col)),
                  pl.BlockSpec((pl.Element(nblk * LANES), pl.Element(2 * w)),
                               lambda i: (pl.multiple_of(first_blk(i) * LANES, LANES), k_col * w)),
                  pl.BlockSpec((lc, 2 * w), lambda i: (0, k_col // 2)),
                  pl.BlockSpec((None, C_HEADS * LANES, nblk * LANES), lambda i: (variant(i), 0, 0))],
        out_specs=pl.BlockSpec((LANES, w), lambda i: (i, 0)),
        out_shape=jax.ShapeDtypeStruct((seq, w), BF16),
        compiler_params=_cparams("arbitrary"),
        name="neighborhood_attention",
    )(rest, rest, rest_ctx, bias)


def _s5_window_scan(sr, si, tabs, cr, ci, reverse):
    n_tiles = sr.shape[0] // SUBLANES
    edge = 0 if reverse else SUBLANES - 1
    out_r, out_i = [None] * n_tiles, [None] * n_tiles
    for step in range(n_tiles):
        tile = (n_tiles - 1 - step) if reverse else step
        xr = sr[tile * SUBLANES:(tile + 1) * SUBLANES]
        xi = si[tile * SUBLANES:(tile + 1) * SUBLANES]
        for lvl, k in enumerate((1, 2, 4)):
            shift = (SUBLANES - k) if reverse else k
            rr, ri = pltpu.roll(xr, shift, 0), pltpu.roll(xi, shift, 0)
            mr, mi = tabs[2 * lvl], tabs[2 * lvl + 1]
            xr, xi = xr + mr * rr - mi * ri, xi + mr * ri + mi * rr
        pr, pi = tabs[6], tabs[7]
        xr, xi = xr + pr * cr - pi * ci, xi + pr * ci + pi * cr
        cr, ci = jnp.broadcast_to(xr[edge:edge + 1], xr.shape), jnp.broadcast_to(xi[edge:edge + 1], xi.shape)
        out_r[tile], out_i[tile] = xr, xi
    return jnp.concatenate(out_r, axis=0), jnp.concatenate(out_i, axis=0), cr, ci


def _s5_kernel(uz_ref, ul_ref, wc_ref, kc_ref, cc_ref, ew_ref, ek_ref, ec_ref, tab_ref, d_ref, *rest, reverse, final):
    if final:
        yfz_ref, yfl_ref, gw_ref, gb_ref, oz_ref, ol_ref, carry_ref, uf_ref, yc_ref, b_ref, k_ref, c_ref = rest
    else:
        oz_ref, ol_ref, carry_ref, uf_ref, yc_ref, b_ref, k_ref, c_ref = rest
    s = pl.program_id(0)
    nblk = b_ref.shape[0]
    ps = b_ref.shape[2] // 2
    gq = LANES // S5_GROUP_CH

    @pl.when(s == 0)
    def _():
        carry_ref[...] = jnp.zeros_like(carry_ref)

        def expand(compact, spread, row_group, per):
            full = jnp.dot(compact, spread, preferred_element_type=F32)
            col_group = (lax.broadcasted_iota(jnp.int32, (1, full.shape[1]), 1) // per) % gq
            return jnp.where(row_group == col_group, full, 0.0).astype(BF16)

        rows = lax.broadcasted_iota(jnp.int32, (wc_ref.shape[1], 1), 0)
        lag_rows_group = (rows // S5_GROUP_CH) % gq
        state_rows_group = (rows // S5_STATE) % gq
        for q in range(nblk):
            b_ref[q] = expand(wc_ref[q], ew_ref[...], lag_rows_group, S5_STATE)
            k_ref[q] = expand(kc_ref[q], ek_ref[...], lag_rows_group, S5_GROUP_CH)
            c_ref[q] = expand(cc_ref[q], ec_ref[...], state_rows_group, S5_GROUP_CH)

    def chunk(u_ref, yf_ref, o_ref):
        t = u_ref.shape[0]
        nw = t // SUBLANES
        row = lax.broadcasted_iota(jnp.int32, (t, 1), 0) & (SUBLANES - 1)
        wrow = lax.broadcasted_iota(jnp.int32, (nw, 1), 0)
        u = u_ref[...]
        uf = u.astype(F32)
        for q in range(nblk):
            uf_ref[q, 0:t] = uf[:, q * LANES:(q + 1) * LANES]
        lagged = [u]
        for k in range(1, SUBLANES):
            if reverse:
                lag = jnp.where(row < SUBLANES - k, pltpu.roll(uf, t - k, 0), 0.0)
            else:
                lag = jnp.where(row >= k, pltpu.roll(uf, k, 0), 0.0)
            lagged.append(lag.astype(BF16))
        outs = []
        for q in range(nblk):
            qs = slice(q * ps, (q + 1) * ps)
            lhs = jnp.concatenate([x[:, q * LANES:(q + 1) * LANES] for x in lagged], axis=1)
            y_loc = jnp.dot(lhs, k_ref[q], preferred_element_type=F32)
            ends = [uf_ref[q, pl.ds(k if reverse else SUBLANES - 1 - k, nw, stride=SUBLANES), :].astype(BF16)
                    for k in range(SUBLANES)]
            sw = jnp.dot(jnp.concatenate(ends, axis=1), b_ref[q], preferred_element_type=F32)
            tabs = [tab_ref[i, :, qs] for i in range(8)]
            cin_r, cin_i = carry_ref[0, :, qs], carry_ref[1, :, qs]
            c_r, c_i, cr, ci = _s5_window_scan(sw[:, :ps], sw[:, ps:], tabs, cin_r, cin_i, reverse)
            carry_ref[0, :, qs] = cr
            carry_ref[1, :, qs] = ci
            if reverse:
                e_r = jnp.where(wrow == nw - 1, cin_r[0:1], pltpu.roll(c_r, nw - 1, 0))
                e_i = jnp.where(wrow == nw - 1, cin_i[0:1], pltpu.roll(c_i, nw - 1, 0))
            else:
                e_r = jnp.where(wrow == 0, cin_r[0:1], pltpu.roll(c_r, 1, 0))
                e_i = jnp.where(wrow == 0, cin_i[0:1], pltpu.roll(c_i, 1, 0))
            y_in = jnp.dot(jnp.concatenate([e_r, e_i], axis=1).astype(BF16), c_ref[q],
                           preferred_element_type=F32)
            for j in range(SUBLANES):
                yc_ref[q, pl.ds(j, nw, stride=SUBLANES), :] = y_in[:, j * LANES:(j + 1) * LANES]
            outs.append(y_loc + yc_ref[q, 0:t])
        out = jnp.concatenate(outs, axis=1)
        if not final:
            o_ref[...] = u.astype(F32) * d_ref[...] + out
        else:
            g = _gelu_tanh(yf_ref[...] + out)
            z = jnp.dot(g.astype(BF16), gw_ref[...], preferred_element_type=F32) + gb_ref[...]
            o_ref[...] = (g * _sigmoid(z)).astype(o_ref.dtype)

    @pl.when(s == 0)
    def _():
        chunk(uz_ref, yfz_ref if final else None, oz_ref)

    @pl.when(s > 0)
    def _():
        chunk(ul_ref, yfl_ref if final else None, ol_ref)


def s5_direction(rest, rest_ctx, u_col, params, which, dskip, reverse, fwd=None, glu=None, t=1024):
    seq = rest.shape[0]
    tz = rest_ctx.shape[0]
    t = min(t, seq)
    nl = seq // t
    w = 4 * LANES
    wc, kc, cc, tabs = params
    _, nq, kq, _ = wc.shape
    gq = LANES // S5_GROUP_CH
    ps2 = 2 * gq * S5_STATE
    p2 = nq * ps2
    e_w = np.zeros((LANES, ps2), np.float32)
    for c in range(2):
        for gl in range(gq):
            for p in range(S5_STATE):
                e_w[c * S5_STATE + p, c * gq * S5_STATE + gl * S5_STATE + p] = 1.0
    e_k = np.zeros((LANES, LANES), np.float32)
    e_c = np.zeros((LANES, SUBLANES * LANES), np.float32)
    for gl in range(gq):
        for o in range(S5_GROUP_CH):
            e_k[o, gl * S5_GROUP_CH + o] = 1.0
            for j in range(SUBLANES):
                e_c[j * S5_GROUP_CH + o, j * LANES + gl * S5_GROUP_CH + o] = 1.0
    spreads = [jnp.asarray(e, BF16) for e in (e_w, e_k, e_c)]
    final = fwd is not None
    if reverse:
        lat = lambda s: (jnp.where(s == 0, nl - 1, nl - s), 0)
    else:
        lat = lambda s: (jnp.maximum(s - 1, 0), 0)
    lat_u = lambda s: (lat(s)[0], u_col)
    const = lambda s: (0, 0)
    table = lambda shape: pl.BlockSpec((None,) + shape, lambda s: (which,) + (0,) * len(shape),
                                       pipeline_mode=pl.Buffered(1))
    in_specs = [pl.BlockSpec((tz, w), lambda s: (0, u_col)),
                pl.BlockSpec((t, w), lat_u),
                table((nq, kq, LANES)),
                table((nq, kq, LANES)),
                table((nq, ps2, LANES)),
                pl.BlockSpec(e_w.shape, const),
                pl.BlockSpec(e_k.shape, const),
                pl.BlockSpec(e_c.shape, const),
                table((8, SUBLANES, p2 // 2)),
                pl.BlockSpec((1, w), const)]
    args = [rest_ctx, rest, wc, kc, cc, *spreads, tabs, dskip.reshape(1, w)]
    if final:
        in_specs += [pl.BlockSpec((tz, w), const), pl.BlockSpec((t, w), lat),
                     pl.BlockSpec((w, w), const), pl.BlockSpec((1, w), const)]
        args += [fwd[0], fwd[1], glu[0], glu[1].reshape(1, w)]
    odt = BF16 if final else F32
    return pl.pallas_call(
        functools.partial(_s5_kernel, reverse=reverse, final=final),
        grid=(nl + 1,),
        in_specs=in_specs,
        out_specs=[pl.BlockSpec((tz, w), const), pl.BlockSpec((t, w), lat)],
        out_shape=[jax.ShapeDtypeStruct((tz, w), odt), jax.ShapeDtypeStruct((seq, w), odt)],
        scratch_shapes=[pltpu.VMEM((2, SUBLANES, p2 // 2), F32),
                        pltpu.VMEM((nq, max(t, tz), LANES), F32),
                        pltpu.VMEM((nq, max(t, tz), LANES), F32),
                        pltpu.VMEM((nq, kq, ps2), BF16),
                        pltpu.VMEM((nq, kq, LANES), BF16),
                        pltpu.VMEM((nq, ps2, SUBLANES * LANES), BF16)],
        compiler_params=_cparams("arbitrary"),
        name="s5_direction",
    )(*args)


def _s5_params(a_re, a_im, log_step, b_re, b_im, c_re, c_im, rev):
    ar, ai = a_re.astype(F32), a_im.astype(F32)
    step = jnp.exp(log_step.astype(F32))[:, None]
    mag = jnp.exp(ar * step)
    ab_re, ab_im = mag * jnp.cos(ai * step), mag * jnp.sin(ai * step)
    den = ar * ar + ai * ai
    nr = ab_re - 1.0
    f_re = (nr * ar + ab_im * ai) / den
    f_im = (ab_im * ar - nr * ai) / den
    br, bi = b_re.astype(F32), b_im.astype(F32)
    bb_re = f_re[..., None] * br - f_im[..., None] * bi
    bb_im = f_re[..., None] * bi + f_im[..., None] * br
    cr, ci = c_re.astype(F32), c_im.astype(F32)
    g, p, h = bb_re.shape
    gq = LANES // h
    nq = g // gq
    nl = SUBLANES

    def cmul(xr, xi, yr, yi):
        return xr * yr - xi * yi, xr * yi + xi * yr

    def powers(xr, xi, n):
        out = [(jnp.ones_like(xr), jnp.zeros_like(xi)), (xr, xi)]
        for _ in range(n - 1):
            out.append(cmul(*out[-1], xr, xi))
        return jnp.stack([o[0] for o in out]), jnp.stack([o[1] for o in out])

    pr, pi = powers(ab_re, ab_im, nl)
    w_re = pr[:nl, ..., None] * bb_re - pi[:nl, ..., None] * bb_im
    w_im = pr[:nl, ..., None] * bb_im + pi[:nl, ..., None] * bb_re

    def rows_lag_group_channel(x):
        x = x.reshape(nl, nq, gq, h, x.shape[-1])
        return jnp.transpose(x, (1, 0, 2, 3, 4)).reshape(nq, nl * gq * h, x.shape[-1])

    wc = jnp.concatenate([rows_lag_group_channel(jnp.swapaxes(w, 2, 3)) for w in (w_re, w_im)], axis=2)
    hp = lax.Precision.HIGHEST
    kk = (jnp.einsum('kgph,gop->kgho', w_re, cr, precision=hp)
          - jnp.einsum('kgph,gop->kgho', w_im, ci, precision=hp))
    kc = jnp.pad(rows_lag_group_channel(kk), ((0, 0), (0, 0), (0, LANES - h)))
    dr = jnp.where(rev, pr[nl:0:-1], pr[1:nl + 1])
    di = jnp.where(rev, pi[nl:0:-1], pi[1:nl + 1])
    cd_re = cr[None] * dr[:, :, None, :] - ci[None] * di[:, :, None, :]
    cd_im = cr[None] * di[:, :, None, :] + ci[None] * dr[:, :, None, :]

    def rows_group_state(cd):
        return jnp.transpose(cd.reshape(nl, nq, gq, h, p), (1, 2, 4, 0, 3)).reshape(nq, gq * p, nl * h)

    cc = jnp.concatenate([rows_group_state(cd_re), -rows_group_state(cd_im)], axis=1)
    qr, qi = powers(pr[nl].reshape(-1), pi[nl].reshape(-1), SUBLANES)
    row = np.arange(SUBLANES)[:, None]
    tabs = []
    for k in (1, 2, 4):
        keep = jnp.where(rev, jnp.asarray(row < SUBLANES - k), jnp.asarray(row >= k))
        tabs += [jnp.where(keep, qr[k][None], 0.0), jnp.where(keep, qi[k][None], 0.0)]
    tabs += [jnp.where(rev, qr[SUBLANES:0:-1], qr[1:SUBLANES + 1]),
             jnp.where(rev, qi[SUBLANES:0:-1], qi[1:SUBLANES + 1])]
    return wc.astype(BF16), kc.astype(BF16), cc.astype(BF16), jnp.stack(tabs)


def _conv_kernel(val_ref, gate_ref, vp_ref, gp_ref, vn_ref, gn_ref, w_ref, b_ref, g_ref, beta_ref, o_ref, hs_ref,
                 *, sub):
    i = pl.program_id(0)
    n = pl.num_programs(0)
    t = val_ref.shape[0]

    def glu(v, g):
        return v[...].astype(F32) * _sigmoid(g[...].astype(F32))

    hs_ref[0:CONV_HALO] = jnp.where(i > 0, glu(vp_ref, gp_ref), 0.0)
    hs_ref[CONV_HALO:CONV_HALO + t] = glu(val_ref, gate_ref)
    hs_ref[CONV_HALO + t:2 * CONV_HALO + t] = jnp.where(i < n - 1, glu(vn_ref, gn_ref), 0.0)
    base = CONV_HALO - CONV_K // 2
    for r0 in range(t // sub):
        acc = jnp.zeros((sub, val_ref.shape[1]), F32) + b_ref[...]
        for r in range(SUBLANES):
            part = None
            for m in range((base + CONV_K - 1) // SUBLANES + 1):
                k = SUBLANES * m + r - base
                if 0 <= k < CONV_K:
                    lo_row = r0 * sub + SUBLANES * m
                    term = hs_ref[lo_row:lo_row + sub + SUBLANES] * w_ref[k:k + 1]
                    part = term if part is None else part + term
            acc = acc + part[r:r + sub]
        mu = jnp.mean(acc, axis=-1, keepdims=True)
        cen = acc - mu
        var = jnp.mean(cen * cen, axis=-1, keepdims=True)
        yn = cen * lax.rsqrt(var + EPS) * g_ref[...] + beta_ref[...]
        o_ref[r0 * sub:(r0 + 1) * sub] = _silu(yn).astype(o_ref.dtype)


def conv_module(rest, val_col, gate_col, w, b, g, beta, t=256, sub=64):
    seq = rest.shape[0]
    t = min(t, seq)
    cw = 4 * LANES
    hb = t // CONV_HALO
    nh = seq // CONV_HALO
    prev = lambda i: jnp.maximum(i * hb - 1, 0)
    nxt = lambda i: jnp.minimum((i + 1) * hb, nh - 1)
    vec = pl.BlockSpec((1, cw), lambda i: (0, 0))
    return pl.pallas_call(
        functools.partial(_conv_kernel, sub=sub),
        grid=(seq // t,),
        in_specs=[pl.BlockSpec((t, cw), lambda i: (i, val_col)),
                  pl.BlockSpec((t, cw), lambda i: (i, gate_col)),
                  pl.BlockSpec((CONV_HALO, cw), lambda i: (prev(i), val_col)),
                  pl.BlockSpec((CONV_HALO, cw), lambda i: (prev(i), gate_col)),
                  pl.BlockSpec((CONV_HALO, cw), lambda i: (nxt(i), val_col)),
                  pl.BlockSpec((CONV_HALO, cw), lambda i: (nxt(i), gate_col)),
                  pl.BlockSpec((CONV_K, cw), lambda i: (0, 0)), vec, vec, vec],
        out_specs=pl.BlockSpec((t, cw), lambda i: (i, 0)),
        out_shape=jax.ShapeDtypeStruct((seq, cw), BF16),
        scratch_shapes=[pltpu.VMEM((t + 2 * CONV_HALO, cw), F32)],
        compiler_params=_cparams("arbitrary"),
        name="conv_module",
    )(rest, rest, rest, rest, rest, rest, w.astype(F32), b.reshape(1, cw).astype(F32),
      g.reshape(1, cw).astype(F32), beta.reshape(1, cw).astype(F32))


def _merge_kernel(*refs, n_parts):
    y_refs = refs[:N_BRANCH]
    gl_refs = refs[N_BRANCH:N_BRANCH * (1 + n_parts)]
    wb_ref, wo_ref, x_ref, gate_ref, g_ref, sc_ref, sh_ref, o_ref, h_ref = refs[N_BRANCH * (1 + n_parts):]
    pw = gl_refs[0].shape[1]
    parts = []
    for part in range(n_parts):
        cols = slice(part * pw, (part + 1) * pw)
        m = None
        for k in range(N_BRANCH):
            proj = jnp.dot(y_refs[k][...], wb_ref[k, :, cols], preferred_element_type=F32)
            term = _sigmoid(gl_refs[k * n_parts + part][...].astype(F32)) * proj
            m = term if m is None else m + term
        parts.append(m.astype(BF16))
    merged = jnp.concatenate(parts, axis=1)
    x = x_ref[...] + gate_ref[...] * jnp.dot(merged, wo_ref[...], preferred_element_type=F32)
    o_ref[...] = x
    y = x * lax.rsqrt(jnp.mean(x * x, axis=-1, keepdims=True) + EPS) * g_ref[...]
    h_ref[...] = (y * (1.0 + sc_ref[...]) + sh_ref[...]).astype(h_ref.dtype)


def gated_merge_residual(ys, rest, logit_col, part_w, wb, wo, layer, x, gate, g2, scale2, shift2, tm=256):
    m, yw = ys[0].shape
    n = wb.shape[3]
    n_parts = n // part_w
    tm = min(tm, m)
    resident = lambda shape: pl.BlockSpec((None,) + shape, lambda i: (layer,) + (0,) * len(shape),
                                          pipeline_mode=pl.Buffered(1))
    yspec = pl.BlockSpec((tm, yw), lambda i: (i, 0))
    gspecs = [pl.BlockSpec((tm, part_w), functools.partial(lambda i, c: (i, c), c=logit_col + k * n_parts + part))
              for k in range(N_BRANCH) for part in range(n_parts)]
    vec = pl.BlockSpec((1, n), lambda i: (0, 0))
    rows = pl.BlockSpec((tm, n), lambda i: (i, 0))
    return pl.pallas_call(
        functools.partial(_merge_kernel, n_parts=n_parts),
        grid=(m // tm,),
        in_specs=[yspec] * N_BRANCH + gspecs + [resident((N_BRANCH, yw, n)), resident((n, n)), rows,
                                                vec, vec, vec, vec],
        out_specs=[rows, rows],
        out_shape=[jax.ShapeDtypeStruct((m, n), F32), jax.ShapeDtypeStruct((m, n), BF16)],
        compiler_params=_cparams("arbitrary"),
        name="gated_merge_residual",
    )(*ys, *([rest] * (N_BRANCH * n_parts)), wb, wo, x, gate.reshape(1, n), g2.reshape(1, n),
      scale2.reshape(1, n), shift2.reshape(1, n))


def _mlp_kernel(x_ref, h_ref, gate_ref, w1_ref, w2_ref, *rest, final_norm):
    if final_norm:
        gf_ref, o_ref, acc_ref = rest
    else:
        o_ref, acc_ref = rest
    f = pl.program_id(1)

    @pl.when(f == 0)
    def _():
        acc_ref[...] = jnp.zeros_like(acc_ref)

    t = jnp.dot(h_ref[...], w1_ref[...], preferred_element_type=F32)
    t = jnp.square(jnp.maximum(t, 0.0)).astype(BF16)
    acc_ref[...] += jnp.dot(t, w2_ref[...], preferred_element_type=F32)

    @pl.when(f == pl.num_programs(1) - 1)
    def _():
        y = x_ref[...] + gate_ref[...] * acc_ref[...]
        if final_norm:
            y = y * lax.rsqrt(jnp.mean(y * y, axis=-1, keepdims=True) + EPS) * gf_ref[...]
        o_ref[...] = y


def mlp_block(x, h, gate, w1, w2, layer, final_g=None, tm=512, tf=1024):
    m, d = x.shape
    ff = w1.shape[2]
    tm = min(tm, m)
    vec = pl.BlockSpec((1, d), lambda i, f: (0, 0))
    rows = pl.BlockSpec((tm, d), lambda i, f: (i, 0))
    final_norm = final_g is not None
    args = [x, h, gate.reshape(1, d), w1, w2]
    in_specs = [rows, rows, vec,
                pl.BlockSpec((None, d, tf), lambda i, f: (layer, 0, f)),
                pl.BlockSpec((None, tf, d), lambda i, f: (layer, f, 0))]
    if final_norm:
        args.append(final_g.reshape(1, d))
        in_specs.append(vec)
    return pl.pallas_call(
        functools.partial(_mlp_kernel, final_norm=final_norm),
        grid=(m // tm, ff // tf),
        in_specs=in_specs,
        out_specs=pl.BlockSpec((tm, d), lambda i, f: (i, 0)),
        out_shape=jax.ShapeDtypeStruct((m, d), F32),
        scratch_shapes=[pltpu.VMEM((tm, d), F32)],
        compiler_params=_cparams("arbitrary", "arbitrary"),
        name="mlp_block",
    )(*args)


_A_COL_K, _A_COL_V = 2, 3
_REST_COL0 = (A_HEADS + 2 * A_KV_HEADS) * HEAD_DIM
_COL_U, _COL_QC, _COL_KC, _COL_VC, _COL_VAL, _COL_GATE = 0, 1, 2, 3, 4, 5
_MERGE_PART = 1024
_COL_MERGE = 3


def _rope_tables(seq):
    freqs = ROPE_BASE ** (-jnp.arange(ROPE_PAIRS, dtype=F32) / ROPE_PAIRS)
    rows = seq // GRID_W
    ang_r = jnp.arange(rows).astype(F32)[:, None] * freqs
    ang_c = jnp.arange(GRID_W).astype(F32)[:, None] * freqs

    def expand(tab_r, tab_c, first, second):
        r = jnp.repeat(tab_r, GRID_W, axis=0)
        cc = jnp.tile(tab_c, (rows, 1))
        head = jnp.concatenate([first * r, second * r, first * cc, second * cc], axis=1)
        return jnp.tile(head, (1, LANES // HEAD_DIM))

    cos = expand(jnp.cos(ang_r), jnp.cos(ang_c), 1.0, 1.0)
    sa = expand(jnp.sin(ang_r), jnp.sin(ang_c), 0.0, 1.0)
    sb = expand(jnp.sin(ang_r), jnp.sin(ang_c), -1.0, 0.0)
    return cos, sa, sb


def kernel(x, c, ctx, c_ctx, ada_w, ada_b, norm1_g, norm2_g, w_in, a_sink, s5_a_re, s5_a_im, s5_log_step, s5_b_re, s5_b_im, s5_c_re, s5_c_im, s5_d, s5_glu_w, s5_glu_b, c_rpb, d_conv_w, d_conv_b, d_ln_g, d_ln_b, w_branch, w_out, mlp_w1, mlp_w2, final_g):
    depth = ada_w.shape[0]
    _, seq, d = x.shape
    lc = ctx.shape[1]
    xs = x.reshape(seq, d)
    zs = ctx.reshape(lc, d)
    rope = _rope_tables(seq)

    cc = jnp.zeros((SUBLANES, d), F32).at[0].set(c[0]).at[1].set(c_ctx)
    mods = ada_modulation(cc, ada_w, ada_b)
    n_rest = w_in.shape[2] - _REST_COL0
    w1, w2 = mlp_w1.astype(BF16), mlp_w2.astype(BF16)
    wb, wo = w_branch.astype(BF16), w_out.astype(BF16)
    s5_params = jax.vmap(_s5_params)(*(a.reshape((2 * depth,) + a.shape[2:]) for a in
                                       (s5_a_re, s5_a_im, s5_log_step, s5_b_re, s5_b_im, s5_c_re, s5_c_im)),
                                     jnp.arange(2 * depth) % 2 == 1)

    for l in range(depth):
        need_ctx = l < depth - 1
        mx = [mods[l, 0, k * d:(k + 1) * d] for k in range(6)]
        mz = [mods[l, 1, k * d:(k + 1) * d] for k in range(6)]

        hx, qk_x = norm_head_projection(xs, norm1_g[l], mx[1], mx[0], w_in, l, rope=rope)
        hz, qk_z = norm_head_projection(zs, norm1_g[l], mz[1], mz[0], w_in, l)
        r_x = wide_projection(hx, w_in, l, _REST_COL0, n_rest)
        r_z = wide_projection(hz, w_in, l, _REST_COL0, n_rest if need_ctx else (_COL_VC + 1) * 4 * LANES)

        ya_x = window_attention(a_sink[l], qk_x, qk_z)
        yc_x = neighborhood_attention(r_x, r_z, c_rpb[l], _COL_QC, _COL_KC)
        yd_x = conv_module(r_x, _COL_VAL, _COL_GATE, d_conv_w[l], d_conv_b[l], d_ln_g[l], d_ln_b[l])

        yf = s5_direction(r_x, r_z, _COL_U, s5_params, 2 * l, s5_d[l].astype(F32), reverse=False)
        yb_z, yb_x = s5_direction(r_x, r_z, _COL_U, s5_params, 2 * l + 1, s5_d[l].astype(F32), reverse=True, fwd=yf,
                                  glu=(s5_glu_w[l].astype(BF16), s5_glu_b[l].astype(F32)))

        xs, h2 = gated_merge_residual((ya_x, yb_x, yc_x, yd_x), r_x, _COL_MERGE, _MERGE_PART, wb, wo, l, xs, mx[2],
                                      norm2_g[l], mx[4], mx[3])
        xs = mlp_block(xs, h2, mx[5], w1, w2, l, final_g=None if need_ctx else final_g)

        if need_ctx:
            ya_z = context_attention(a_sink[l], qk_z, 0, qk_z, _A_COL_K, 2 * LANES, qk_z, _A_COL_V, True, True)
            yc_z = context_attention(a_sink[l], r_z, _COL_QC, r_z, _COL_KC, 4 * LANES, r_z, _COL_VC, False, False)
            yd_z = conv_module(r_z, _COL_VAL, _COL_GATE, d_conv_w[l], d_conv_b[l], d_ln_g[l], d_ln_b[l])
            zs, h2 = gated_merge_residual((ya_z, yb_z, yc_z, yd_z), r_z, _COL_MERGE, _MERGE_PART, wb, wo, l, zs,
                                          mz[2], norm2_g[l], mz[4], mz[3])
            zs = mlp_block(zs, h2, mz[5], w1, w2, l)

    return xs.reshape(x.shape)
```

```python
import functools
import math

import numpy as np
import jax
import jax.numpy as jnp
from jax import lax
from jax.experimental import pallas as pl
from jax.experimental.pallas import tpu as pltpu

F32 = jnp.float32
BF16 = jnp.bfloat16

LANES = 128
SUBLANES = 8
VMEM_LIMIT = 56 * 1024 * 1024

GRID_W = 64
HEAD_DIM = 64
N_BRANCH = 4
A_HEADS = 8
A_KV_HEADS = 2
A_WINDOW = 128
A_BLOCK = 128
ROPE_BASE = 10000.0
ROPE_PAIRS = HEAD_DIM // 4
S5_GROUP_CH = 16
S5_STATE = 64
C_HEADS = 8
C_WIN_R = 8
C_WIN_C = 16
C_QROWS = 2
C_KROWS = C_QROWS + C_WIN_R
CONV_K = 31
CONV_HALO = 16
EPS = 1e-6
NEG_INF = -1e30


def _cparams(*sem):
    return pltpu.CompilerParams(dimension_semantics=sem, vmem_limit_bytes=VMEM_LIMIT)


def _sigmoid(x):
    return 0.5 * jnp.tanh(0.5 * x) + 0.5


def _silu(x):
    return x * _sigmoid(x)


def _gelu_tanh(x):
    return 0.5 * x * (1.0 + jnp.tanh(math.sqrt(2.0 / math.pi) * (x + 0.044715 * (x * x * x))))


def _ada_kernel(c_ref, w_ref, b_ref, o_ref):
    a = _silu(c_ref[...]).astype(BF16)
    o_ref[...] = jnp.dot(a, w_ref[...].astype(BF16), preferred_element_type=F32) + b_ref[...]


def ada_modulation(cc, ada_w, ada_b, tn=1024):
    depth, d, n = ada_w.shape
    return pl.pallas_call(
        _ada_kernel,
        grid=(depth, n // tn),
        in_specs=[pl.BlockSpec((SUBLANES, d), lambda l, j: (0, 0)),
                  pl.BlockSpec((None, d, tn), lambda l, j: (l, 0, j)),
                  pl.BlockSpec((None, 1, tn), lambda l, j: (l, 0, j))],
        out_specs=pl.BlockSpec((None, SUBLANES, tn), lambda l, j: (l, 0, j)),
        out_shape=jax.ShapeDtypeStruct((depth, SUBLANES, n), F32),
        compiler_params=_cparams("arbitrary", "arbitrary"),
        name="ada_modulation",
    )(cc, ada_w, ada_b.reshape(depth, 1, n))


def _rope_block(y, cos, sa, sb):
    return y * cos + pltpu.roll(y, 16, 1) * sa + pltpu.roll(y, LANES - 16, 1) * sb


def _head_proj_kernel(x_ref, g_ref, sc_ref, sh_ref, w_ref, *rest, n_q, with_rope):
    if with_rope:
        cos_ref, sa_ref, sb_ref, h_ref, o_ref, wb_ref = rest
        cos, sa, sb = cos_ref[...], sa_ref[...], sb_ref[...]
    else:
        h_ref, o_ref, wb_ref = rest

    @pl.when(pl.program_id(0) == 0)
    def _():
        wb_ref[...] = w_ref[0].astype(BF16)

    x = x_ref[...]
    y = x * lax.rsqrt(jnp.mean(x * x, axis=-1, keepdims=True) + EPS) * g_ref[...]
    h = (y * (1.0 + sc_ref[...]) + sh_ref[...]).astype(BF16)
    h_ref[...] = h
    acc = jnp.dot(h, wb_ref[...], preferred_element_type=F32)
    lo = _lane_lo()
    out = 0
    for b in range(acc.shape[1] // LANES):
        y = acc[:, b * LANES:(b + 1) * LANES]
        if with_rope and b <= n_q:
            y = _rope_block(y, cos, sa, sb)
        if b < n_q:
            blocks = [y]
        else:
            swapped = pltpu.roll(y, HEAD_DIM, 1)
            blocks = [jnp.where(lo, y, swapped), jnp.where(lo, swapped, y)]
        for blk in blocks:
            o_ref[:, out * LANES:(out + 1) * LANES] = blk.astype(o_ref.dtype)
            out += 1


def norm_head_projection(x, g, scale, shift, w_in, layer, rope=None, tm=512):
    m, k = x.shape
    n_in = (A_HEADS + 2 * A_KV_HEADS) * HEAD_DIM
    n_q = A_HEADS * HEAD_DIM // LANES
    assert A_KV_HEADS * HEAD_DIM == LANES
    n_out = n_in + 2 * LANES
    tm = min(tm, m)
    vec = pl.BlockSpec((1, k), lambda i: (0, 0))
    in_specs = [pl.BlockSpec((tm, k), lambda i: (i, 0)), vec, vec, vec,
                pl.BlockSpec((pl.Element(1), pl.Element(k), pl.Element(n_in)), lambda i: (layer, 0, 0))]
    args = [x, g.reshape(1, k), scale.reshape(1, k), shift.reshape(1, k), w_in]
    if rope is not None:
        in_specs += [pl.BlockSpec((tm, LANES), lambda i: (i, 0))] * 3
        args += list(rope)
    return pl.pallas_call(
        functools.partial(_head_proj_kernel, n_q=n_q, with_rope=rope is not None),
        grid=(m // tm,),
        in_specs=in_specs,
        out_specs=[pl.BlockSpec((tm, k), lambda i: (i, 0)), pl.BlockSpec((tm, n_out), lambda i: (i, 0))],
        out_shape=[jax.ShapeDtypeStruct((m, k), BF16), jax.ShapeDtypeStruct((m, n_out), BF16)],
        scratch_shapes=[pltpu.VMEM((k, n_in), BF16)],
        compiler_params=_cparams("arbitrary"),
        name="norm_head_projection",
    )(*args)


def _wide_proj_kernel(a_ref, w_ref, o_ref, wb_ref):
    @pl.when(pl.program_id(1) == 0)
    def _():
        wb_ref[...] = w_ref[0].astype(BF16)

    o_ref[...] = jnp.dot(a_ref[...], wb_ref[...], preferred_element_type=F32).astype(o_ref.dtype)


def wide_projection(a, w, layer, col0, n, tm=2048, tn=1024):
    m, k = a.shape
    tm = min(tm, m)
    assert n % tn == 0 and col0 % LANES == 0 and tn % LANES == 0
    return pl.pallas_call(
        _wide_proj_kernel,
        grid=(n // tn, m // tm),
        in_specs=[pl.BlockSpec((tm, k), lambda j, i: (i, 0)),
                  pl.BlockSpec((pl.Element(1), pl.Element(k), pl.Element(tn)),
                               lambda j, i: (layer, 0, pl.multiple_of(col0 + j * tn, LANES)))],
        out_specs=pl.BlockSpec((tm, tn), lambda j, i: (i, j)),
        out_shape=jax.ShapeDtypeStruct((m, n), BF16),
        scratch_shapes=[pltpu.VMEM((k, tn), BF16)],
        compiler_params=_cparams("arbitrary", "arbitrary"),
        name="wide_projection",
    )(a, w)


def _lane_lo():
    return lax.broadcasted_iota(jnp.int32, (1, LANES), 1) < HEAD_DIM


def _half_select(x, lo, half):
    zero = jnp.zeros_like(x)
    return jnp.where(lo, x, zero) if half == 0 else jnp.where(lo, zero, x)


def _dot_nt(a, b):
    return lax.dot_general(a, b, (((1,), (1,)), ((), ())), preferred_element_type=F32)


def _softmax_pv(parts, values, sink=None):
    m = parts[0].max(axis=-1, keepdims=True)
    for s in parts[1:]:
        m = jnp.maximum(m, s.max(axis=-1, keepdims=True))
    if sink is not None:
        m = jnp.maximum(m, sink)
    den = jnp.exp(sink - m) if sink is not None else 0.0
    out = 0.0
    for s, v in zip(parts, values):
        p = jnp.exp(s - m)
        den = den + p.sum(axis=-1, keepdims=True)
        out = out + jnp.dot(p.astype(BF16), v, preferred_element_type=F32)
    return out / den


def _attn_a_window_start(i, seq):
    return jnp.clip((i - 1) * A_BLOCK, 0, seq - 3 * A_BLOCK)


def _attn_a_kernel(sink_ref, q_ref, kv0_ref, kv1_ref, kv2_ref, kvz_ref, o_ref, *, seq):
    i = pl.program_id(0)
    lo = _lane_lo()
    nq = 4 * A_BLOCK
    r = lax.broadcasted_iota(jnp.int32, (nq, 3 * A_BLOCK), 0) & (A_BLOCK - 1)
    c = lax.broadcasted_iota(jnp.int32, (nq, 3 * A_BLOCK), 1)
    dist = c - r + (_attn_a_window_start(i, seq) - i * A_BLOCK)
    valid = (dist >= -A_WINDOW) & (dist <= A_WINDOW)
    rowblk = lax.broadcasted_iota(jnp.int32, (nq, 1), 0) // A_BLOCK
    vcol = A_KV_HEADS * LANES
    for g in range(A_KV_HEADS):
        gs = slice(g * LANES, (g + 1) * LANES)
        vs = slice(vcol + g * LANES, vcol + (g + 1) * LANES)
        kw = jnp.concatenate([ref[:, gs] for ref in (kv0_ref, kv1_ref, kv2_ref)], axis=0)
        vw = jnp.concatenate([ref[:, vs] for ref in (kv0_ref, kv1_ref, kv2_ref)], axis=0)
        kz, vz = kvz_ref[:, gs], kvz_ref[:, vs]
        blocks = [slice((2 * g + bb) * LANES, (2 * g + bb + 1) * LANES) for bb in range(2)]
        ql = jnp.concatenate([_half_select(q_ref[:, bs], lo, half) * (HEAD_DIM ** -0.5)
                              for half in range(2) for bs in blocks], axis=0)
        heads = [2 * (2 * g + bb) + half for half in range(2) for bb in range(2)]
        sink = jnp.full((nq, 1), sink_ref[heads[3]], F32)
        for k in range(3):
            sink = jnp.where(rowblk == k, sink_ref[heads[k]], sink)
        s = jnp.where(valid, _dot_nt(ql, kw), NEG_INF)
        sz = _dot_nt(ql, kz)
        m = jnp.maximum(jnp.maximum(s.max(axis=-1, keepdims=True), sz.max(axis=-1, keepdims=True)), sink)
        p, pz = jnp.exp(s - m), jnp.exp(sz - m)
        inv = 1.0 / (p.sum(axis=-1, keepdims=True) + pz.sum(axis=-1, keepdims=True) + jnp.exp(sink - m))
        p, pz = p.astype(BF16), pz.astype(BF16)
        out = None
        for half in range(2):
            hs = slice(half * 2 * A_BLOCK, (half + 1) * 2 * A_BLOCK)
            o = (jnp.dot(p[hs], _half_select(vw, lo, half), preferred_element_type=F32)
                 + jnp.dot(pz[hs], _half_select(vz, lo, half), preferred_element_type=F32)) * inv[hs]
            out = o if out is None else out + o
        for bb in range(2):
            o_ref[:, blocks[bb]] = out[bb * A_BLOCK:(bb + 1) * A_BLOCK].astype(o_ref.dtype)


def window_attention(sink, qk, qk_ctx):
    seq = qk.shape[0]
    lc = qk_ctx.shape[0]
    nb = seq // A_BLOCK
    qw = A_HEADS * HEAD_DIM
    kvw = 4 * A_KV_HEADS * HEAD_DIM
    assert nb >= 3 and qw == kvw
    kv_specs = [pl.BlockSpec((A_BLOCK, kvw), functools.partial(
        lambda i, t: (_attn_a_window_start(i, seq) // A_BLOCK + t, 1), t=t)) for t in range(3)]
    return pl.pallas_call(
        functools.partial(_attn_a_kernel, seq=seq),
        grid=(nb,),
        in_specs=[pl.BlockSpec(memory_space=pltpu.SMEM),
                  pl.BlockSpec((A_BLOCK, qw), lambda i: (i, 0))] + kv_specs + [
                  pl.BlockSpec((lc, kvw), lambda i: (0, 1))],
        out_specs=pl.BlockSpec((A_BLOCK, qw), lambda i: (i, 0)),
        out_shape=jax.ShapeDtypeStruct((seq, qw), BF16),
        compiler_params=_cparams("arbitrary"),
        name="window_attention",
    )(sink, qk, qk, qk, qk, qk_ctx)


def _ctx_attn_kernel(sink_ref, q_ref, k_ref, v_ref, o_ref, *, kv_shared, has_sink):
    lo = _lane_lo()
    for b in range(q_ref.shape[1] // LANES):
        bs = slice(b * LANES, (b + 1) * LANES)
        kb = b // 2 if kv_shared else b
        ks = slice(kb * LANES, (kb + 1) * LANES)
        qb, k, v = q_ref[:, bs], k_ref[:, ks], v_ref[:, ks]
        acc = jnp.zeros((q_ref.shape[0], LANES), F32)
        for half in range(2):
            ql = _half_select(qb, lo, half) * (HEAD_DIM ** -0.5)
            sink = sink_ref[2 * b + half] if has_sink else None
            acc = acc + _softmax_pv([_dot_nt(ql, k)], [_half_select(v, lo, half)], sink=sink)
        o_ref[:, bs] = acc.astype(o_ref.dtype)


def context_attention(sink, q_arr, q_col, k_arr, k_col, k_w, v_arr, v_col, kv_shared, has_sink):
    lc = q_arr.shape[0]
    return pl.pallas_call(
        functools.partial(_ctx_attn_kernel, kv_shared=kv_shared, has_sink=has_sink),
        grid=(1,),
        in_specs=[pl.BlockSpec(memory_space=pltpu.SMEM),
                  pl.BlockSpec((lc, 4 * LANES), lambda i: (0, q_col)),
                  pl.BlockSpec((lc, k_w), lambda i: (0, k_col)),
                  pl.BlockSpec((lc, k_w), lambda i: (0, v_col))],
        out_specs=pl.BlockSpec((lc, 4 * LANES), lambda i: (0, 0)),
        out_shape=jax.ShapeDtypeStruct((lc, 4 * LANES), BF16),
        compiler_params=_cparams("arbitrary"),
        name="context_attention",
    )(sink, q_arr, k_arr, v_arr)


def _nbr_row_offsets(rows):
    nsteps = rows // C_QROWS
    nblk = C_KROWS * GRID_W // LANES
    wr = min(C_WIN_R, rows)
    reps = [0, 1, 2, nsteps - 2, nsteps - 1]
    dr = -np.ones((len(reps), C_QROWS, C_KROWS), np.int32)
    for v, i in enumerate(reps):
        w0 = C_QROWS * int(np.clip(i - 2, 0, nsteps - nblk))
        for a in range(C_QROWS):
            rq = C_QROWS * i + a
            k0 = int(np.clip(rq - C_WIN_R // 2, 0, rows - wr))
            for j in range(C_KROWS):
                if k0 <= w0 + j < k0 + wr:
                    dr[v, a, j] = w0 + j - rq + C_WIN_R - 1
    return dr


def _nbr_bias(rpb, rows):
    h = rpb.shape[0]
    qc = np.arange(GRID_W)[:, None]
    kc = np.arange(GRID_W)[None, :]
    cstart = np.clip(qc - C_WIN_C // 2, 0, GRID_W - C_WIN_C)
    col_ok = jnp.asarray((kc >= cstart) & (kc < cstart + C_WIN_C))
    pad = GRID_W - C_WIN_C
    ext = jnp.pad(rpb.astype(F32), ((0, 0), (0, 0), (pad, pad)), mode='edge')
    toep = jnp.stack([ext[:, :, GRID_W - 1 - q:2 * GRID_W - 1 - q] for q in range(GRID_W)], axis=2)
    toep = jnp.where(col_ok, toep, NEG_INF)
    masked = jnp.full((h, GRID_W, GRID_W), NEG_INF, F32)
    dr = _nbr_row_offsets(rows)
    variants = []
    for v in range(dr.shape[0]):
        qrows = [jnp.concatenate([toep[:, dr[v, a, j]] if dr[v, a, j] >= 0 else masked for j in range(C_KROWS)],
                                 axis=-1) for a in range(C_QROWS)]
        variants.append(jnp.concatenate(qrows, axis=-2).reshape(h * C_QROWS * GRID_W, C_KROWS * GRID_W))
    return jnp.stack(variants)


def _attn_c_kernel(q_ref, *refs):
    kv_refs = refs[:-3]
    kvz_ref, bias_ref, o_ref = refs[-3:]
    lo = _lane_lo()
    nq = q_ref.shape[0]
    nb = C_HEADS // 2
    vcol = C_HEADS * HEAD_DIM
    blocks = [slice(b * LANES, (b + 1) * LANES) for b in range(nb)]
    s_parts, sz_parts = [], []
    for bs in blocks:
        ql = jnp.concatenate([_half_select(q_ref[:, bs], lo, half) * (HEAD_DIM ** -0.5) for half in range(2)],
                             axis=0)
        s_parts.append(_dot_nt(ql, jnp.concatenate([ref[:, bs] for ref in kv_refs], axis=0)))
        sz_parts.append(_dot_nt(ql, kvz_ref[:, bs]))
    s = jnp.concatenate(s_parts, axis=0) + bias_ref[...]
    sz = jnp.concatenate(sz_parts, axis=0)
    m = jnp.maximum(s.max(axis=-1, keepdims=True), sz.max(axis=-1, keepdims=True))
    p, pz = jnp.exp(s - m), jnp.exp(sz - m)
    inv = 1.0 / (p.sum(axis=-1, keepdims=True) + pz.sum(axis=-1, keepdims=True))
    p, pz = p.astype(BF16), pz.astype(BF16)
    for b, bs in enumerate(blocks):
        vs = slice(vcol + b * LANES, vcol + (b + 1) * LANES)
        vw, vz = jnp.concatenate([ref[:, vs] for ref in kv_refs], axis=0), kvz_ref[:, vs]
        out = None
        for half in range(2):
            hs = slice((2 * b + half) * nq, (2 * b + half + 1) * nq)
            o = (jnp.dot(p[hs], _half_select(vw, lo, half), preferred_element_type=F32)
                 + jnp.dot(pz[hs], _half_select(vz, lo, half), preferred_element_type=F32)) * inv[hs]
            out = o if out is None else out + o
        o_ref[:, bs] = out.astype(o_ref.dtype)


def neighborhood_attention(rest, rest_ctx, bias, layer, q_col, k_col):
    seq = rest.shape[0]
    lc = rest_ctx.shape[0]
    rows = seq // GRID_W
    nsteps = rows // C_QROWS
    nblk = C_KROWS * GRID_W // LANES
    assert nsteps > nblk and C_QROWS * GRID_W == LANES and k_col % 2 == 0

    def first_blk(i):
        return jnp.clip(i - 2, 0, nsteps - nblk)

    def variant(i):
        return jnp.where(i < 2, i, jnp.where(i >= nsteps - 2, i - (nsteps - 2) + 3, 2))

    w = 4 * LANES
    kv_specs = [pl.BlockSpec((LANES, 2 * w), functools.partial(lambda i, t: (first_blk(i) + t, k_col // 2), t=t))
                for t in range(nblk)]
    return pl.pallas_call(
        _attn_c_kernel,
        grid=(nsteps,),
        in_specs=[pl.BlockSpec((LANES, w), lambda i: (i, q_col))] + kv_specs + [
                  pl.BlockSpec((lc, 2 * w), lambda i: (0, k_col // 2)),
                  pl.BlockSpec((None, None, C_HEADS * LANES, nblk * LANES),
                               lambda i: (layer, variant(i), 0, 0))],
        out_specs=pl.BlockSpec((LANES, w), lambda i: (i, 0)),
        out_shape=jax.ShapeDtypeStruct((seq, w), BF16),
        compiler_params=_cparams("arbitrary"),
        name="neighborhood_attention",
    )(rest, *([rest] * nblk), rest_ctx, bias)


def _s5_window_scan(sr, si, tabs, cr, ci, reverse):
    n_tiles = sr.shape[0] // SUBLANES
    edge = 0 if reverse else SUBLANES - 1
    out_r, out_i = [None] * n_tiles, [None] * n_tiles
    for step in range(n_tiles):
        tile = (n_tiles - 1 - step) if reverse else step
        xr = sr[tile * SUBLANES:(tile + 1) * SUBLANES]
        xi = si[tile * SUBLANES:(tile + 1) * SUBLANES]
        for lvl, k in enumerate((1, 2, 4)):
            shift = (SUBLANES - k) if reverse else k
            rr, ri = pltpu.roll(xr, shift, 0), pltpu.roll(xi, shift, 0)
            mr, mi = tabs[2 * lvl], tabs[2 * lvl + 1]
            xr, xi = xr + mr * rr - mi * ri, xi + mr * ri + mi * rr
        pr, pi = tabs[6], tabs[7]
        xr, xi = xr + pr * cr - pi * ci, xi + pr * ci + pi * cr
        cr, ci = jnp.broadcast_to(xr[edge:edge + 1], xr.shape), jnp.broadcast_to(xi[edge:edge + 1], xi.shape)
        out_r[tile], out_i[tile] = xr, xi
    return jnp.concatenate(out_r, axis=0), jnp.concatenate(out_i, axis=0), cr, ci


def _s5_kernel(uz_ref, ul_ref, wc_ref, kc_ref, cc_ref, ew_ref, ek_ref, ec_ref, tab_ref, d_ref, *rest, reverse, final):
    if final:
        yfz_ref, yfl_ref, gw_ref, gb_ref, oz_ref, ol_ref, carry_ref, uf_ref, yc_ref, b_ref, k_ref, c_ref = rest
    else:
        oz_ref, ol_ref, carry_ref, uf_ref, yc_ref, b_ref, k_ref, c_ref = rest
    s = pl.program_id(0)
    nblk = b_ref.shape[0]
    ps = b_ref.shape[2] // 2
    gq = LANES // S5_GROUP_CH

    @pl.when(s == 0)
    def _():
        carry_ref[...] = jnp.zeros_like(carry_ref)

        def expand(compact, spread, row_group, per):
            full = jnp.dot(compact, spread, preferred_element_type=F32)
            col_group = (lax.broadcasted_iota(jnp.int32, (1, full.shape[1]), 1) // per) % gq
            return jnp.where(row_group == col_group, full, 0.0).astype(BF16)

        rows = lax.broadcasted_iota(jnp.int32, (wc_ref.shape[1], 1), 0)
        lag_rows_group = (rows // S5_GROUP_CH) % gq
        state_rows_group = (rows // S5_STATE) % gq
        for q in range(nblk):
            b_ref[q] = expand(wc_ref[q], ew_ref[...], lag_rows_group, S5_STATE)
            k_ref[q] = expand(kc_ref[q], ek_ref[...], lag_rows_group, S5_GROUP_CH)
            c_ref[q] = expand(cc_ref[q], ec_ref[...], state_rows_group, S5_GROUP_CH)

    def chunk(u_ref, yf_ref, o_ref):
        t = u_ref.shape[0]
        nw = t // SUBLANES
        row = lax.broadcasted_iota(jnp.int32, (t, 1), 0) & (SUBLANES - 1)
        wrow = lax.broadcasted_iota(jnp.int32, (nw, 1), 0)
        u = u_ref[...]
        uf = u.astype(F32)
        for q in range(nblk):
            uf_ref[q, 0:t] = uf[:, q * LANES:(q + 1) * LANES]
        lagged = [u]
        for k in range(1, SUBLANES):
            if reverse:
                lag = jnp.where(row < SUBLANES - k, pltpu.roll(uf, t - k, 0), 0.0)
            else:
                lag = jnp.where(row >= k, pltpu.roll(uf, k, 0), 0.0)
            lagged.append(lag.astype(BF16))
        outs = []
        for q in range(nblk):
            qs = slice(q * ps, (q + 1) * ps)
            lhs = jnp.concatenate([x[:, q * LANES:(q + 1) * LANES] for x in lagged], axis=1)
            y_loc = jnp.dot(lhs, k_ref[q], preferred_element_type=F32)
            ends = [uf_ref[q, pl.ds(k if reverse else SUBLANES - 1 - k, nw, stride=SUBLANES), :].astype(BF16)
                    for k in range(SUBLANES)]
            sw = jnp.dot(jnp.concatenate(ends, axis=1), b_ref[q], preferred_element_type=F32)
            tabs = [tab_ref[i, :, qs] for i in range(8)]
            cin_r, cin_i = carry_ref[0, :, qs], carry_ref[1, :, qs]
            c_r, c_i, cr, ci = _s5_window_scan(sw[:, :ps], sw[:, ps:], tabs, cin_r, cin_i, reverse)
            carry_ref[0, :, qs] = cr
            carry_ref[1, :, qs] = ci
            if reverse:
                e_r = jnp.where(wrow == nw - 1, cin_r[0:1], pltpu.roll(c_r, nw - 1, 0))
                e_i = jnp.where(wrow == nw - 1, cin_i[0:1], pltpu.roll(c_i, nw - 1, 0))
            else:
                e_r = jnp.where(wrow == 0, cin_r[0:1], pltpu.roll(c_r, 1, 0))
                e_i = jnp.where(wrow == 0, cin_i[0:1], pltpu.roll(c_i, 1, 0))
            y_in = jnp.dot(jnp.concatenate([e_r, e_i], axis=1).astype(BF16), c_ref[q],
                           preferred_element_type=F32)
            for j in range(SUBLANES):
                yc_ref[q, pl.ds(j, nw, stride=SUBLANES), :] = y_in[:, j * LANES:(j + 1) * LANES]
            outs.append(y_loc + yc_ref[q, 0:t])
        out = jnp.concatenate(outs, axis=1)
        if not final:
            o_ref[...] = u.astype(F32) * d_ref[...] + out
        else:
            g = _gelu_tanh(yf_ref[...] + out)
            z = jnp.dot(g.astype(BF16), gw_ref[...], preferred_element_type=F32) + gb_ref[...]
            o_ref[...] = (g * _sigmoid(z)).astype(o_ref.dtype)

    @pl.when(s == 0)
    def _():
        chunk(uz_ref, yfz_ref if final else None, oz_ref)

    @pl.when(s > 0)
    def _():
        chunk(ul_ref, yfl_ref if final else None, ol_ref)


def s5_direction(rest, rest_ctx, u_col, params, which, dskip, reverse, fwd=None, glu=None, t=1024):
    seq = rest.shape[0]
    tz = rest_ctx.shape[0]
    t = min(t, seq)
    nl = seq // t
    w = 4 * LANES
    wc, kc, cc, tabs = params
    _, nq, kq, _ = wc.shape
    gq = LANES // S5_GROUP_CH
    ps2 = 2 * gq * S5_STATE
    p2 = nq * ps2
    e_w = np.zeros((LANES, ps2), np.float32)
    for c in range(2):
        for gl in range(gq):
            for p in range(S5_STATE):
                e_w[c * S5_STATE + p, c * gq * S5_STATE + gl * S5_STATE + p] = 1.0
    e_k = np.zeros((LANES, LANES), np.float32)
    e_c = np.zeros((LANES, SUBLANES * LANES), np.float32)
    for gl in range(gq):
        for o in range(S5_GROUP_CH):
            e_k[o, gl * S5_GROUP_CH + o] = 1.0
            for j in range(SUBLANES):
                e_c[j * S5_GROUP_CH + o, j * LANES + gl * S5_GROUP_CH + o] = 1.0
    spreads = [jnp.asarray(e, BF16) for e in (e_w, e_k, e_c)]
    final = fwd is not None
    if reverse:
        lat = lambda s: (jnp.where(s == 0, nl - 1, nl - s), 0)
    else:
        lat = lambda s: (jnp.maximum(s - 1, 0), 0)
    lat_u = lambda s: (lat(s)[0], u_col)
    const = lambda s: (0, 0)
    table = lambda shape: pl.BlockSpec((None,) + shape, lambda s: (which,) + (0,) * len(shape),
                                       pipeline_mode=pl.Buffered(1))
    in_specs = [pl.BlockSpec((tz, w), lambda s: (0, u_col)),
                pl.BlockSpec((t, w), lat_u),
                table((nq, kq, LANES)),
                table((nq, kq, LANES)),
                table((nq, ps2, LANES)),
                pl.BlockSpec(e_w.shape, const),
                pl.BlockSpec(e_k.shape, const),
                pl.BlockSpec(e_c.shape, const),
                table((8, SUBLANES, p2 // 2)),
                pl.BlockSpec((1, w), const)]
    args = [rest_ctx, rest, wc, kc, cc, *spreads, tabs, dskip.reshape(1, w)]
    if final:
        in_specs += [pl.BlockSpec((tz, w), const), pl.BlockSpec((t, w), lat),
                     pl.BlockSpec((w, w), const), pl.BlockSpec((1, w), const)]
        args += [fwd[0], fwd[1], glu[0], glu[1].reshape(1, w)]
    odt = BF16 if final else F32
    return pl.pallas_call(
        functools.partial(_s5_kernel, reverse=reverse, final=final),
        grid=(nl + 1,),
        in_specs=in_specs,
        out_specs=[pl.BlockSpec((tz, w), const), pl.BlockSpec((t, w), lat)],
        out_shape=[jax.ShapeDtypeStruct((tz, w), odt), jax.ShapeDtypeStruct((seq, w), odt)],
        scratch_shapes=[pltpu.VMEM((2, SUBLANES, p2 // 2), F32),
                        pltpu.VMEM((nq, max(t, tz), LANES), F32),
                        pltpu.VMEM((nq, max(t, tz), LANES), F32),
                        pltpu.VMEM((nq, kq, ps2), BF16),
                        pltpu.VMEM((nq, kq, LANES), BF16),
                        pltpu.VMEM((nq, ps2, SUBLANES * LANES), BF16)],
        compiler_params=_cparams("arbitrary"),
        name="s5_direction",
    )(*args)


def _s5_params(a_re, a_im, log_step, b_re, b_im, c_re, c_im, rev):
    ar, ai = a_re.astype(F32), a_im.astype(F32)
    step = jnp.exp(log_step.astype(F32))[:, None]
    mag = jnp.exp(ar * step)
    ab_re, ab_im = mag * jnp.cos(ai * step), mag * jnp.sin(ai * step)
    den = ar * ar + ai * ai
    nr = ab_re - 1.0
    f_re = (nr * ar + ab_im * ai) / den
    f_im = (ab_im * ar - nr * ai) / den
    br, bi = b_re.astype(F32), b_im.astype(F32)
    bb_re = f_re[..., None] * br - f_im[..., None] * bi
    bb_im = f_re[..., None] * bi + f_im[..., None] * br
    cr, ci = c_re.astype(F32), c_im.astype(F32)
    g, p, h = bb_re.shape
    gq = LANES // h
    nq = g // gq
    nl = SUBLANES

    def cmul(xr, xi, yr, yi):
        return xr * yr - xi * yi, xr * yi + xi * yr

    def powers(xr, xi, n):
        out = [(jnp.ones_like(xr), jnp.zeros_like(xi)), (xr, xi)]
        for _ in range(n - 1):
            out.append(cmul(*out[-1], xr, xi))
        return jnp.stack([o[0] for o in out]), jnp.stack([o[1] for o in out])

    pr, pi = powers(ab_re, ab_im, nl)
    w_re = pr[:nl, ..., None] * bb_re - pi[:nl, ..., None] * bb_im
    w_im = pr[:nl, ..., None] * bb_im + pi[:nl, ..., None] * bb_re

    def rows_lag_group_channel(x):
        x = x.reshape(nl, nq, gq, h, x.shape[-1])
        return jnp.transpose(x, (1, 0, 2, 3, 4)).reshape(nq, nl * gq * h, x.shape[-1])

    wc = jnp.concatenate([rows_lag_group_channel(jnp.swapaxes(w, 2, 3)) for w in (w_re, w_im)], axis=2)
    hp = lax.Precision.HIGHEST
    kk = (jnp.einsum('kgph,gop->kgho', w_re, cr, precision=hp)
          - jnp.einsum('kgph,gop->kgho', w_im, ci, precision=hp))
    kc = jnp.pad(rows_lag_group_channel(kk), ((0, 0), (0, 0), (0, LANES - h)))
    dr = jnp.where(rev, pr[nl:0:-1], pr[1:nl + 1])
    di = jnp.where(rev, pi[nl:0:-1], pi[1:nl + 1])
    cd_re = cr[None] * dr[:, :, None, :] - ci[None] * di[:, :, None, :]
    cd_im = cr[None] * di[:, :, None, :] + ci[None] * dr[:, :, None, :]

    def rows_group_state(cd):
        return jnp.transpose(cd.reshape(nl, nq, gq, h, p), (1, 2, 4, 0, 3)).reshape(nq, gq * p, nl * h)

    cc = jnp.concatenate([rows_group_state(cd_re), -rows_group_state(cd_im)], axis=1)
    qr, qi = powers(pr[nl].reshape(-1), pi[nl].reshape(-1), SUBLANES)
    row = np.arange(SUBLANES)[:, None]
    tabs = []
    for k in (1, 2, 4):
        keep = jnp.where(rev, jnp.asarray(row < SUBLANES - k), jnp.asarray(row >= k))
        tabs += [jnp.where(keep, qr[k][None], 0.0), jnp.where(keep, qi[k][None], 0.0)]
    tabs += [jnp.where(rev, qr[SUBLANES:0:-1], qr[1:SUBLANES + 1]),
             jnp.where(rev, qi[SUBLANES:0:-1], qi[1:SUBLANES + 1])]
    return wc.astype(BF16), kc.astype(BF16), cc.astype(BF16), jnp.stack(tabs)


def _conv_kernel(val_ref, gate_ref, vp_ref, gp_ref, vn_ref, gn_ref, w_ref, b_ref, g_ref, beta_ref, o_ref, hs_ref,
                 *, sub):
    i = pl.program_id(0)
    n = pl.num_programs(0)
    t = val_ref.shape[0]

    def glu(v, g):
        return v[...].astype(F32) * _sigmoid(g[...].astype(F32))

    hs_ref[0:CONV_HALO] = jnp.where(i > 0, glu(vp_ref, gp_ref), 0.0)
    hs_ref[CONV_HALO:CONV_HALO + t] = glu(val_ref, gate_ref)
    hs_ref[CONV_HALO + t:2 * CONV_HALO + t] = jnp.where(i < n - 1, glu(vn_ref, gn_ref), 0.0)
    base = CONV_HALO - CONV_K // 2
    for r0 in range(t // sub):
        acc = jnp.zeros((sub, val_ref.shape[1]), F32) + b_ref[...]
        for r in range(SUBLANES):
            part = None
            for m in range((base + CONV_K - 1) // SUBLANES + 1):
                k = SUBLANES * m + r - base
                if 0 <= k < CONV_K:
                    lo_row = r0 * sub + SUBLANES * m
                    term = hs_ref[lo_row:lo_row + sub + SUBLANES] * w_ref[k:k + 1]
                    part = term if part is None else part + term
            acc = acc + part[r:r + sub]
        mu = jnp.mean(acc, axis=-1, keepdims=True)
        cen = acc - mu
        var = jnp.mean(cen * cen, axis=-1, keepdims=True)
        yn = cen * lax.rsqrt(var + EPS) * g_ref[...] + beta_ref[...]
        o_ref[r0 * sub:(r0 + 1) * sub] = _silu(yn).astype(o_ref.dtype)


def conv_module(rest, val_col, gate_col, w, b, g, beta, t=256, sub=64):
    seq = rest.shape[0]
    t = min(t, seq)
    cw = 4 * LANES
    hb = t // CONV_HALO
    nh = seq // CONV_HALO
    prev = lambda i: jnp.maximum(i * hb - 1, 0)
    nxt = lambda i: jnp.minimum((i + 1) * hb, nh - 1)
    vec = pl.BlockSpec((1, cw), lambda i: (0, 0))
    return pl.pallas_call(
        functools.partial(_conv_kernel, sub=sub),
        grid=(seq // t,),
        in_specs=[pl.BlockSpec((t, cw), lambda i: (i, val_col)),
                  pl.BlockSpec((t, cw), lambda i: (i, gate_col)),
                  pl.BlockSpec((CONV_HALO, cw), lambda i: (prev(i), val_col)),
                  pl.BlockSpec((CONV_HALO, cw), lambda i: (prev(i), gate_col)),
                  pl.BlockSpec((CONV_HALO, cw), lambda i: (nxt(i), val_col)),
                  pl.BlockSpec((CONV_HALO, cw), lambda i: (nxt(i), gate_col)),
                  pl.BlockSpec((CONV_K, cw), lambda i: (0, 0)), vec, vec, vec],
        out_specs=pl.BlockSpec((t, cw), lambda i: (i, 0)),
        out_shape=jax.ShapeDtypeStruct((seq, cw), BF16),
        scratch_shapes=[pltpu.VMEM((t + 2 * CONV_HALO, cw), F32)],
        compiler_params=_cparams("arbitrary"),
        name="conv_module",
    )(rest, rest, rest, rest, rest, rest, w.astype(F32), b.reshape(1, cw).astype(F32),
      g.reshape(1, cw).astype(F32), beta.reshape(1, cw).astype(F32))


def _merge_kernel(*refs, n_parts):
    y_refs = refs[:N_BRANCH]
    gl_refs = refs[N_BRANCH:N_BRANCH * (1 + n_parts)]
    wb_ref, wo_ref, x_ref, gate_ref, g_ref, sc_ref, sh_ref, o_ref, h_ref = refs[N_BRANCH * (1 + n_parts):]
    pw = gl_refs[0].shape[1]
    parts = []
    for part in range(n_parts):
        cols = slice(part * pw, (part + 1) * pw)
        m = None
        for k in range(N_BRANCH):
            proj = jnp.dot(y_refs[k][...], wb_ref[k, :, cols], preferred_element_type=F32)
            term = _sigmoid(gl_refs[k * n_parts + part][...].astype(F32)) * proj
            m = term if m is None else m + term
        parts.append(m.astype(BF16))
    merged = jnp.concatenate(parts, axis=1)
    x = x_ref[...] + gate_ref[...] * jnp.dot(merged, wo_ref[...], preferred_element_type=F32)
    o_ref[...] = x
    y = x * lax.rsqrt(jnp.mean(x * x, axis=-1, keepdims=True) + EPS) * g_ref[...]
    h_ref[...] = (y * (1.0 + sc_ref[...]) + sh_ref[...]).astype(h_ref.dtype)


def gated_merge_residual(ys, rest, logit_col, part_w, wb, wo, layer, x, gate, g2, scale2, shift2, tm=256):
    m, yw = ys[0].shape
    n = wb.shape[3]
    n_parts = n // part_w
    tm = min(tm, m)
    resident = lambda shape: pl.BlockSpec((None,) + shape, lambda i: (layer,) + (0,) * len(shape),
                                          pipeline_mode=pl.Buffered(1))
    yspec = pl.BlockSpec((tm, yw), lambda i: (i, 0))
    gspecs = [pl.BlockSpec((tm, part_w), functools.partial(lambda i, c: (i, c), c=logit_col + k * n_parts + part))
              for k in range(N_BRANCH) for part in range(n_parts)]
    vec = pl.BlockSpec((1, n), lambda i: (0, 0))
    rows = pl.BlockSpec((tm, n), lambda i: (i, 0))
    return pl.pallas_call(
        functools.partial(_merge_kernel, n_parts=n_parts),
        grid=(m // tm,),
        in_specs=[yspec] * N_BRANCH + gspecs + [resident((N_BRANCH, yw, n)), resident((n, n)), rows,
                                                vec, vec, vec, vec],
        out_specs=[rows, rows],
        out_shape=[jax.ShapeDtypeStruct((m, n), F32), jax.ShapeDtypeStruct((m, n), BF16)],
        compiler_params=_cparams("arbitrary"),
        name="gated_merge_residual",
    )(*ys, *([rest] * (N_BRANCH * n_parts)), wb, wo, x, gate.reshape(1, n), g2.reshape(1, n),
      scale2.reshape(1, n), shift2.reshape(1, n))


def _mlp_kernel(x_ref, h_ref, gate_ref, w1_ref, w2_ref, *rest, final_norm):
    if final_norm:
        gf_ref, o_ref, acc_ref = rest
    else:
        o_ref, acc_ref = rest
    f = pl.program_id(1)

    @pl.when(f == 0)
    def _():
        acc_ref[...] = jnp.zeros_like(acc_ref)

    t = jnp.dot(h_ref[...], w1_ref[...], preferred_element_type=F32)
    t = jnp.square(jnp.maximum(t, 0.0)).astype(BF16)
    acc_ref[...] += jnp.dot(t, w2_ref[...], preferred_element_type=F32)

    @pl.when(f == pl.num_programs(1) - 1)
    def _():
        y = x_ref[...] + gate_ref[...] * acc_ref[...]
        if final_norm:
            y = y * lax.rsqrt(jnp.mean(y * y, axis=-1, keepdims=True) + EPS) * gf_ref[...]
        o_ref[...] = y


def mlp_block(x, h, gate, w1, w2, layer, final_g=None, tm=512, tf=1024):
    m, d = x.shape
    ff = w1.shape[2]
    tm = min(tm, m)
    vec = pl.BlockSpec((1, d), lambda i, f: (0, 0))
    rows = pl.BlockSpec((tm, d), lambda i, f: (i, 0))
    final_norm = final_g is not None
    args = [x, h, gate.reshape(1, d), w1, w2]
    in_specs = [rows, rows, vec,
                pl.BlockSpec((None, d, tf), lambda i, f: (layer, 0, f)),
                pl.BlockSpec((None, tf, d), lambda i, f: (layer, f, 0))]
    if final_norm:
        args.append(final_g.reshape(1, d))
        in_specs.append(vec)
    return pl.pallas_call(
        functools.partial(_mlp_kernel, final_norm=final_norm),
        grid=(m // tm, ff // tf),
        in_specs=in_specs,
        out_specs=pl.BlockSpec((tm, d), lambda i, f: (i, 0)),
        out_shape=jax.ShapeDtypeStruct((m, d), F32),
        scratch_shapes=[pltpu.VMEM((tm, d), F32)],
        compiler_params=_cparams("arbitrary", "arbitrary"),
        name="mlp_block",
    )(*args)


_A_COL_K, _A_COL_V = 2, 3
_REST_COL0 = (A_HEADS + 2 * A_KV_HEADS) * HEAD_DIM
_COL_U, _COL_QC, _COL_KC, _COL_VC, _COL_VAL, _COL_GATE = 0, 1, 2, 3, 4, 5
_MERGE_PART = 1024
_COL_MERGE = 3


def _rope_tables(seq):
    freqs = ROPE_BASE ** (-jnp.arange(ROPE_PAIRS, dtype=F32) / ROPE_PAIRS)
    rows = seq // GRID_W
    ang_r = jnp.arange(rows).astype(F32)[:, None] * freqs
    ang_c = jnp.arange(GRID_W).astype(F32)[:, None] * freqs

    def expand(tab_r, tab_c, first, second):
        r = jnp.repeat(tab_r, GRID_W, axis=0)
        cc = jnp.tile(tab_c, (rows, 1))
        head = jnp.concatenate([first * r, second * r, first * cc, second * cc], axis=1)
        return jnp.tile(head, (1, LANES // HEAD_DIM))

    cos = expand(jnp.cos(ang_r), jnp.cos(ang_c), 1.0, 1.0)
    sa = expand(jnp.sin(ang_r), jnp.sin(ang_c), 0.0, 1.0)
    sb = expand(jnp.sin(ang_r), jnp.sin(ang_c), -1.0, 0.0)
    return cos, sa, sb


def kernel(x, c, ctx, c_ctx, ada_w, ada_b, norm1_g, norm2_g, w_in, a_sink, s5_a_re, s5_a_im, s5_log_step, s5_b_re, s5_b_im, s5_c_re, s5_c_im, s5_d, s5_glu_w, s5_glu_b, c_rpb, d_conv_w, d_conv_b, d_ln_g, d_ln_b, w_branch, w_out, mlp_w1, mlp_w2, final_g):
    depth = ada_w.shape[0]
    _, seq, d = x.shape
    lc = ctx.shape[1]
    xs = x.reshape(seq, d)
    zs = ctx.reshape(lc, d)
    rope = _rope_tables(seq)

    cc = jnp.zeros((SUBLANES, d), F32).at[0].set(c[0]).at[1].set(c_ctx)
    mods = ada_modulation(cc, ada_w, ada_b)
    n_rest = w_in.shape[2] - _REST_COL0
    w1, w2 = mlp_w1.astype(BF16), mlp_w2.astype(BF16)
    wb, wo = w_branch.astype(BF16), w_out.astype(BF16)
    nbr_bias = jax.vmap(lambda rpb: _nbr_bias(rpb, seq // GRID_W))(c_rpb)
    s5_params = jax.vmap(_s5_params)(*(a.reshape((2 * depth,) + a.shape[2:]) for a in
                                       (s5_a_re, s5_a_im, s5_log_step, s5_b_re, s5_b_im, s5_c_re, s5_c_im)),
                                     jnp.arange(2 * depth) % 2 == 1)

    for l in range(depth):
        need_ctx = l < depth - 1
        mx = [mods[l, 0, k * d:(k + 1) * d] for k in range(6)]
        mz = [mods[l, 1, k * d:(k + 1) * d] for k in range(6)]

        hx, qk_x = norm_head_projection(xs, norm1_g[l], mx[1], mx[0], w_in, l, rope=rope)
        hz, qk_z = norm_head_projection(zs, norm1_g[l], mz[1], mz[0], w_in, l)
        r_x = wide_projection(hx, w_in, l, _REST_COL0, n_rest)
        r_z = wide_projection(hz, w_in, l, _REST_COL0, n_rest if need_ctx else (_COL_VC + 1) * 4 * LANES)

        ya_x = window_attention(a_sink[l], qk_x, qk_z)
        yc_x = neighborhood_attention(r_x, r_z, nbr_bias, l, _COL_QC, _COL_KC)
        yd_x = conv_module(r_x, _COL_VAL, _COL_GATE, d_conv_w[l], d_conv_b[l], d_ln_g[l], d_ln_b[l])

        yf = s5_direction(r_x, r_z, _COL_U, s5_params, 2 * l, s5_d[l].astype(F32), reverse=False)
        yb_z, yb_x = s5_direction(r_x, r_z, _COL_U, s5_params, 2 * l + 1, s5_d[l].astype(F32), reverse=True, fwd=yf,
                                  glu=(s5_glu_w[l].astype(BF16), s5_glu_b[l].astype(F32)))

        xs, h2 = gated_merge_residual((ya_x, yb_x, yc_x, yd_x), r_x, _COL_MERGE, _MERGE_PART, wb, wo, l, xs, mx[2],
                                      norm2_g[l], mx[4], mx[3])
        xs = mlp_block(xs, h2, mx[5], w1, w2, l, final_g=None if need_ctx else final_g)

        if need_ctx:
            ya_z = context_attention(a_sink[l], qk_z, 0, qk_z, _A_COL_K, 2 * LANES, qk_z, _A_COL_V, True, True)
            yc_z = context_attention(a_sink[l], r_z, _COL_QC, r_z, _COL_KC, 4 * LANES, r_z, _COL_VC, False, False)
            yd_z = conv_module(r_z, _COL_VAL, _COL_GATE, d_conv_w[l], d_conv_b[l], d_ln_g[l], d_ln_b[l])
            zs, h2 = gated_merge_residual((ya_z, yb_z, yc_z, yd_z), r_z, _COL_MERGE, _MERGE_PART, wb, wo, l, zs,
                                          mz[2], norm2_g[l], mz[4], mz[3])
            zs = mlp_block(zs, h2, mz[5], w1, w2, l)

    return xs.reshape(x.shape)
```

```python
import functools
import math

import numpy as np
import jax
import jax.numpy as jnp
from jax import lax
from jax.experimental import pallas as pl
from jax.experimental.pallas import tpu as pltpu

F32 = jnp.float32
BF16 = jnp.bfloat16

LANES = 128
SUBLANES = 8
VMEM_LIMIT = 56 * 1024 * 1024

GRID_W = 64
HEAD_DIM = 64
N_BRANCH = 4
A_HEADS = 8
A_KV_HEADS = 2
A_WINDOW = 128
A_BLOCK = 128
ROPE_BASE = 10000.0
ROPE_PAIRS = HEAD_DIM // 4
S5_GROUP_CH = 16
S5_STATE = 64
C_HEADS = 8
C_WIN_R = 8
C_WIN_C = 16
C_QROWS = 2
C_KROWS = C_QROWS + C_WIN_R
CONV_K = 31
CONV_HALO = 16
EPS = 1e-6
NEG_INF = -1e30


def _cparams(*sem):
    return pltpu.CompilerParams(dimension_semantics=sem, vmem_limit_bytes=VMEM_LIMIT)


def _sigmoid(x):
    return 0.5 * jnp.tanh(0.5 * x) + 0.5


def _silu(x):
    return x * _sigmoid(x)


def _gelu_tanh(x):
    return 0.5 * x * (1.0 + jnp.tanh(math.sqrt(2.0 / math.pi) * (x + 0.044715 * (x * x * x))))


def _ada_kernel(c_ref, w_ref, b_ref, o_ref):
    a = _silu(c_ref[...]).astype(BF16)
    o_ref[...] = jnp.dot(a, w_ref[...].astype(BF16), preferred_element_type=F32) + b_ref[...]


def ada_modulation(cc, ada_w, ada_b, tn=1024):
    depth, d, n = ada_w.shape
    return pl.pallas_call(
        _ada_kernel,
        grid=(depth, n // tn),
        in_specs=[pl.BlockSpec((SUBLANES, d), lambda l, j: (0, 0)),
                  pl.BlockSpec((None, d, tn), lambda l, j: (l, 0, j)),
                  pl.BlockSpec((None, 1, tn), lambda l, j: (l, 0, j))],
        out_specs=pl.BlockSpec((None, SUBLANES, tn), lambda l, j: (l, 0, j)),
        out_shape=jax.ShapeDtypeStruct((depth, SUBLANES, n), F32),
        compiler_params=_cparams("arbitrary", "arbitrary"),
        name="ada_modulation",
    )(cc, ada_w, ada_b.reshape(depth, 1, n))


def _rope_block(y, cos, sa, sb):
    return y * cos + pltpu.roll(y, 16, 1) * sa + pltpu.roll(y, LANES - 16, 1) * sb


def _head_proj_kernel(x_ref, g_ref, sc_ref, sh_ref, w_ref, *rest, n_q, with_rope):
    if with_rope:
        cos_ref, sa_ref, sb_ref, h_ref, o_ref, wb_ref = rest
        cos, sa, sb = cos_ref[...], sa_ref[...], sb_ref[...]
    else:
        h_ref, o_ref, wb_ref = rest

    @pl.when(pl.program_id(0) == 0)
    def _():
        wb_ref[...] = w_ref[0].astype(BF16)

    x = x_ref[...]
    y = x * lax.rsqrt(jnp.mean(x * x, axis=-1, keepdims=True) + EPS) * g_ref[...]
    h = (y * (1.0 + sc_ref[...]) + sh_ref[...]).astype(BF16)
    h_ref[...] = h
    acc = jnp.dot(h, wb_ref[...], preferred_element_type=F32)
    lo = _lane_lo()
    out = 0
    for b in range(acc.shape[1] // LANES):
        y = acc[:, b * LANES:(b + 1) * LANES]
        if with_rope and b <= n_q:
            y = _rope_block(y, cos, sa, sb)
        if b < n_q:
            blocks = [y]
        else:
            swapped = pltpu.roll(y, HEAD_DIM, 1)
            blocks = [jnp.where(lo, y, swapped), jnp.where(lo, swapped, y)]
        for blk in blocks:
            o_ref[:, out * LANES:(out + 1) * LANES] = blk.astype(o_ref.dtype)
            out += 1


def norm_head_projection(x, g, scale, shift, w_in, layer, rope=None, tm=512):
    m, k = x.shape
    n_in = (A_HEADS + 2 * A_KV_HEADS) * HEAD_DIM
    n_q = A_HEADS * HEAD_DIM // LANES
    assert A_KV_HEADS * HEAD_DIM == LANES
    n_out = n_in + 2 * LANES
    tm = min(tm, m)
    vec = pl.BlockSpec((1, k), lambda i: (0, 0))
    in_specs = [pl.BlockSpec((tm, k), lambda i: (i, 0)), vec, vec, vec,
                pl.BlockSpec((pl.Element(1), pl.Element(k), pl.Element(n_in)), lambda i: (layer, 0, 0))]
    args = [x, g.reshape(1, k), scale.reshape(1, k), shift.reshape(1, k), w_in]
    if rope is not None:
        in_specs += [pl.BlockSpec((tm, LANES), lambda i: (i, 0))] * 3
        args += list(rope)
    return pl.pallas_call(
        functools.partial(_head_proj_kernel, n_q=n_q, with_rope=rope is not None),
        grid=(m // tm,),
        in_specs=in_specs,
        out_specs=[pl.BlockSpec((tm, k), lambda i: (i, 0)), pl.BlockSpec((tm, n_out), lambda i: (i, 0))],
        out_shape=[jax.ShapeDtypeStruct((m, k), BF16), jax.ShapeDtypeStruct((m, n_out), BF16)],
        scratch_shapes=[pltpu.VMEM((k, n_in), BF16)],
        compiler_params=_cparams("arbitrary"),
        name="norm_head_projection",
    )(*args)


def _wide_proj_kernel(a_ref, w_ref, o_ref, wb_ref):
    @pl.when(pl.program_id(1) == 0)
    def _():
        wb_ref[...] = w_ref[0].astype(BF16)

    o_ref[...] = jnp.dot(a_ref[...], wb_ref[...], preferred_element_type=F32).astype(o_ref.dtype)


def wide_projection(a, w, layer, col0, n, tm=2048, tn=1024):
    m, k = a.shape
    tm = min(tm, m)
    assert n % tn == 0 and col0 % LANES == 0 and tn % LANES == 0
    return pl.pallas_call(
        _wide_proj_kernel,
        grid=(n // tn, m // tm),
        in_specs=[pl.BlockSpec((tm, k), lambda j, i: (i, 0)),
                  pl.BlockSpec((pl.Element(1), pl.Element(k), pl.Element(tn)),
                               lambda j, i: (layer, 0, pl.multiple_of(col0 + j * tn, LANES)))],
        out_specs=pl.BlockSpec((tm, tn), lambda j, i: (i, j)),
        out_shape=jax.ShapeDtypeStruct((m, n), BF16),
        scratch_shapes=[pltpu.VMEM((k, tn), BF16)],
        compiler_params=_cparams("arbitrary", "arbitrary"),
        name="wide_projection",
    )(a, w)


def _lane_lo():
    return lax.broadcasted_iota(jnp.int32, (1, LANES), 1) < HEAD_DIM


def _half_select(x, lo, half):
    zero = jnp.zeros_like(x)
    return jnp.where(lo, x, zero) if half == 0 else jnp.where(lo, zero, x)


def _dot_nt(a, b):
    return lax.dot_general(a, b, (((1,), (1,)), ((), ())), preferred_element_type=F32)


def _softmax_pv(parts, values, sink=None):
    m = parts[0].max(axis=-1, keepdims=True)
    for s in parts[1:]:
        m = jnp.maximum(m, s.max(axis=-1, keepdims=True))
    if sink is not None:
        m = jnp.maximum(m, sink)
    den = jnp.exp(sink - m) if sink is not None else 0.0
    out = 0.0
    for s, v in zip(parts, values):
        p = jnp.exp(s - m)
        den = den + p.sum(axis=-1, keepdims=True)
        out = out + jnp.dot(p.astype(BF16), v, preferred_element_type=F32)
    return out / den


def _attn_a_window_start(i, seq):
    return jnp.clip((i - 1) * A_BLOCK, 0, seq - 3 * A_BLOCK)


def _attn_a_kernel(sink_ref, q_ref, kv0_ref, kv1_ref, kv2_ref, kvz_ref, o_ref, *, seq):
    i = pl.program_id(0)
    lo = _lane_lo()
    nq = 4 * A_BLOCK
    r = lax.broadcasted_iota(jnp.int32, (nq, 3 * A_BLOCK), 0) & (A_BLOCK - 1)
    c = lax.broadcasted_iota(jnp.int32, (nq, 3 * A_BLOCK), 1)
    dist = c - r + (_attn_a_window_start(i, seq) - i * A_BLOCK)
    valid = (dist >= -A_WINDOW) & (dist <= A_WINDOW)
    rowblk = lax.broadcasted_iota(jnp.int32, (nq, 1), 0) // A_BLOCK
    vcol = A_KV_HEADS * LANES
    for g in range(A_KV_HEADS):
        gs = slice(g * LANES, (g + 1) * LANES)
        vs = slice(vcol + g * LANES, vcol + (g + 1) * LANES)
        kw = jnp.concatenate([ref[:, gs] for ref in (kv0_ref, kv1_ref, kv2_ref)], axis=0)
        vw = jnp.concatenate([ref[:, vs] for ref in (kv0_ref, kv1_ref, kv2_ref)], axis=0)
        kz, vz = kvz_ref[:, gs], kvz_ref[:, vs]
        blocks = [slice((2 * g + bb) * LANES, (2 * g + bb + 1) * LANES) for bb in range(2)]
        ql = jnp.concatenate([_half_select(q_ref[:, bs], lo, half) * (HEAD_DIM ** -0.5)
                              for half in range(2) for bs in blocks], axis=0)
        heads = [2 * (2 * g + bb) + half for half in range(2) for bb in range(2)]
        sink = jnp.full((nq, 1), sink_ref[heads[3]], F32)
        for k in range(3):
            sink = jnp.where(rowblk == k, sink_ref[heads[k]], sink)
        s = jnp.where(valid, _dot_nt(ql, kw), NEG_INF)
        sz = _dot_nt(ql, kz)
        m = jnp.maximum(jnp.maximum(s.max(axis=-1, keepdims=True), sz.max(axis=-1, keepdims=True)), sink)
        p, pz = jnp.exp(s - m), jnp.exp(sz - m)
        inv = 1.0 / (p.sum(axis=-1, keepdims=True) + pz.sum(axis=-1, keepdims=True) + jnp.exp(sink - m))
        p, pz = p.astype(BF16), pz.astype(BF16)
        out = None
        for half in range(2):
            hs = slice(half * 2 * A_BLOCK, (half + 1) * 2 * A_BLOCK)
            o = (jnp.dot(p[hs], _half_select(vw, lo, half), preferred_element_type=F32)
                 + jnp.dot(pz[hs], _half_select(vz, lo, half), preferred_element_type=F32)) * inv[hs]
            out = o if out is None else out + o
        for bb in range(2):
            o_ref[:, blocks[bb]] = out[bb * A_BLOCK:(bb + 1) * A_BLOCK].astype(o_ref.dtype)


def window_attention(sink, qk, qk_ctx):
    seq = qk.shape[0]
    lc = qk_ctx.shape[0]
    nb = seq // A_BLOCK
    qw = A_HEADS * HEAD_DIM
    kvw = 4 * A_KV_HEADS * HEAD_DIM
    assert nb >= 3 and qw == kvw
    kv_specs = [pl.BlockSpec((A_BLOCK, kvw), functools.partial(
        lambda i, t: (_attn_a_window_start(i, seq) // A_BLOCK + t, 1), t=t)) for t in range(3)]
    return pl.pallas_call(
        functools.partial(_attn_a_kernel, seq=seq),
        grid=(nb,),
        in_specs=[pl.BlockSpec(memory_space=pltpu.SMEM),
                  pl.BlockSpec((A_BLOCK, qw), lambda i: (i, 0))] + kv_specs + [
                  pl.BlockSpec((lc, kvw), lambda i: (0, 1))],
        out_specs=pl.BlockSpec((A_BLOCK, qw), lambda i: (i, 0)),
        out_shape=jax.ShapeDtypeStruct((seq, qw), BF16),
        compiler_params=_cparams("arbitrary"),
        name="window_attention",
    )(sink, qk, qk, qk, qk, qk_ctx)


def _ctx_attn_kernel(sink_ref, q_ref, k_ref, v_ref, o_ref, *, kv_shared, has_sink):
    lo = _lane_lo()
    for b in range(q_ref.shape[1] // LANES):
        bs = slice(b * LANES, (b + 1) * LANES)
        kb = b // 2 if kv_shared else b
        ks = slice(kb * LANES, (kb + 1) * LANES)
        qb, k, v = q_ref[:, bs], k_ref[:, ks], v_ref[:, ks]
        acc = jnp.zeros((q_ref.shape[0], LANES), F32)
        for half in range(2):
            ql = _half_select(qb, lo, half) * (HEAD_DIM ** -0.5)
            sink = sink_ref[2 * b + half] if has_sink else None
            acc = acc + _softmax_pv([_dot_nt(ql, k)], [_half_select(v, lo, half)], sink=sink)
        o_ref[:, bs] = acc.astype(o_ref.dtype)


def context_attention(sink, q_arr, q_col, k_arr, k_col, k_w, v_arr, v_col, kv_shared, has_sink):
    lc = q_arr.shape[0]
    return pl.pallas_call(
        functools.partial(_ctx_attn_kernel, kv_shared=kv_shared, has_sink=has_sink),
        grid=(1,),
        in_specs=[pl.BlockSpec(memory_space=pltpu.SMEM),
                  pl.BlockSpec((lc, 4 * LANES), lambda i: (0, q_col)),
                  pl.BlockSpec((lc, k_w), lambda i: (0, k_col)),
                  pl.BlockSpec((lc, k_w), lambda i: (0, v_col))],
        out_specs=pl.BlockSpec((lc, 4 * LANES), lambda i: (0, 0)),
        out_shape=jax.ShapeDtypeStruct((lc, 4 * LANES), BF16),
        compiler_params=_cparams("arbitrary"),
        name="context_attention",
    )(sink, q_arr, k_arr, v_arr)


def _nbr_row_offsets(rows):
    nsteps = rows // C_QROWS
    nblk = C_KROWS * GRID_W // LANES
    wr = min(C_WIN_R, rows)
    reps = [0, 1, 2, nsteps - 2, nsteps - 1]
    dr = -np.ones((len(reps), C_QROWS, C_KROWS), np.int32)
    for v, i in enumerate(reps):
        w0 = C_QROWS * int(np.clip(i - 2, 0, nsteps - nblk))
        for a in range(C_QROWS):
            rq = C_QROWS * i + a
            k0 = int(np.clip(rq - C_WIN_R // 2, 0, rows - wr))
            for j in range(C_KROWS):
                if k0 <= w0 + j < k0 + wr:
                    dr[v, a, j] = w0 + j - rq + C_WIN_R - 1
    return dr


def _nbr_bias(rpb, rows):
    h = rpb.shape[0]
    qc = np.arange(GRID_W)[:, None]
    kc = np.arange(GRID_W)[None, :]
    cstart = np.clip(qc - C_WIN_C // 2, 0, GRID_W - C_WIN_C)
    col_ok = jnp.asarray((kc >= cstart) & (kc < cstart + C_WIN_C))
    pad = GRID_W - C_WIN_C
    ext = jnp.pad(rpb.astype(F32), ((0, 0), (0, 0), (pad, pad)), mode='edge')
    toep = jnp.stack([ext[:, :, GRID_W - 1 - q:2 * GRID_W - 1 - q] for q in range(GRID_W)], axis=2)
    toep = jnp.where(col_ok, toep, NEG_INF)
    masked = jnp.full((h, GRID_W, GRID_W), NEG_INF, F32)
    dr = _nbr_row_offsets(rows)
    variants = []
    for v in range(dr.shape[0]):
        qrows = [jnp.concatenate([toep[:, dr[v, a, j]] if dr[v, a, j] >= 0 else masked for j in range(C_KROWS)],
                                 axis=-1) for a in range(C_QROWS)]
        variants.append(jnp.concatenate(qrows, axis=-2).reshape(h * C_QROWS * GRID_W, C_KROWS * GRID_W))
    return jnp.stack(variants)


def _attn_c_kernel(q_ref, *refs):
    nblk = (len(refs) - 3) // 2
    k_refs, v_refs = refs[:nblk], refs[nblk:2 * nblk]
    kvz_ref, bias_ref, o_ref = refs[-3:]
    lo = _lane_lo()
    nq = q_ref.shape[0]
    nb = C_HEADS // 2
    vcol = C_HEADS * HEAD_DIM
    blocks = [slice(b * LANES, (b + 1) * LANES) for b in range(nb)]
    s_parts, sz_parts = [], []
    for bs in blocks:
        ql = jnp.concatenate([_half_select(q_ref[:, bs], lo, half) * (HEAD_DIM ** -0.5) for half in range(2)],
                             axis=0)
        s_parts.append(_dot_nt(ql, jnp.concatenate([ref[:, bs] for ref in k_refs], axis=0)))
        sz_parts.append(_dot_nt(ql, kvz_ref[:, bs]))
    s = jnp.concatenate(s_parts, axis=0) + bias_ref[...]
    sz = jnp.concatenate(sz_parts, axis=0)
    m = jnp.maximum(s.max(axis=-1, keepdims=True), sz.max(axis=-1, keepdims=True))
    p, pz = jnp.exp(s - m), jnp.exp(sz - m)
    inv = 1.0 / (p.sum(axis=-1, keepdims=True) + pz.sum(axis=-1, keepdims=True))
    p, pz = p.astype(BF16), pz.astype(BF16)
    for b, bs in enumerate(blocks):
        vs = slice(vcol + b * LANES, vcol + (b + 1) * LANES)
        vw, vz = jnp.concatenate([ref[:, bs] for ref in v_refs], axis=0), kvz_ref[:, vs]
        out = None
        for half in range(2):
            hs = slice((2 * b + half) * nq, (2 * b + half + 1) * nq)
            o = (jnp.dot(p[hs], _half_select(vw, lo, half), preferred_element_type=F32)
                 + jnp.dot(pz[hs], _half_select(vz, lo, half), preferred_element_type=F32)) * inv[hs]
            out = o if out is None else out + o
        o_ref[:, bs] = out.astype(o_ref.dtype)


def neighborhood_attention(rest, rest_ctx, bias, layer, q_col, k_col):
    seq = rest.shape[0]
    lc = rest_ctx.shape[0]
    rows = seq // GRID_W
    nsteps = rows // C_QROWS
    nblk = C_KROWS * GRID_W // LANES
    assert nsteps > nblk and C_QROWS * GRID_W == LANES and k_col % 2 == 0

    def first_blk(i):
        return jnp.clip(i - 2, 0, nsteps - nblk)

    def variant(i):
        return jnp.where(i < 2, i, jnp.where(i >= nsteps - 2, i - (nsteps - 2) + 3, 2))

    w = 4 * LANES
    kv_specs = [pl.BlockSpec((LANES, w), functools.partial(lambda i, t, c: (first_blk(i) + t, c), t=t, c=c))
                for c in (k_col, k_col + 1) for t in range(nblk)]
    return pl.pallas_call(
        _attn_c_kernel,
        grid=(nsteps,),
        in_specs=[pl.BlockSpec((LANES, w), lambda i: (i, q_col))] + kv_specs + [
                  pl.BlockSpec((lc, 2 * w), lambda i: (0, k_col // 2)),
                  pl.BlockSpec((None, None, C_HEADS * LANES, nblk * LANES),
                               lambda i: (layer, variant(i), 0, 0))],
        out_specs=pl.BlockSpec((LANES, w), lambda i: (i, 0)),
        out_shape=jax.ShapeDtypeStruct((seq, w), BF16),
        compiler_params=_cparams("arbitrary"),
        name="neighborhood_attention",
    )(rest, *([rest] * (2 * nblk)), rest_ctx, bias)


def _s5_window_scan(sr, si, tabs, cr, ci, reverse):
    n_tiles = sr.shape[0] // SUBLANES
    edge = 0 if reverse else SUBLANES - 1
    out_r, out_i = [None] * n_tiles, [None] * n_tiles
    for step in range(n_tiles):
        tile = (n_tiles - 1 - step) if reverse else step
        xr = sr[tile * SUBLANES:(tile + 1) * SUBLANES]
        xi = si[tile * SUBLANES:(tile + 1) * SUBLANES]
        for lvl, k in enumerate((1, 2, 4)):
            shift = (SUBLANES - k) if reverse else k
            rr, ri = pltpu.roll(xr, shift, 0), pltpu.roll(xi, shift, 0)
            mr, mi = tabs[2 * lvl], tabs[2 * lvl + 1]
            xr, xi = xr + mr * rr - mi * ri, xi + mr * ri + mi * rr
        pr, pi = tabs[6], tabs[7]
        xr, xi = xr + pr * cr - pi * ci, xi + pr * ci + pi * cr
        cr, ci = jnp.broadcast_to(xr[edge:edge + 1], xr.shape), jnp.broadcast_to(xi[edge:edge + 1], xi.shape)
        out_r[tile], out_i[tile] = xr, xi
    return jnp.concatenate(out_r, axis=0), jnp.concatenate(out_i, axis=0), cr, ci


def _s5_kernel(uz_ref, ul_ref, wc_ref, kc_ref, cc_ref, ew_ref, ek_ref, ec_ref, tab_ref, d_ref, *rest, reverse, final):
    if final:
        yfz_ref, yfl_ref, gw_ref, gb_ref, oz_ref, ol_ref, carry_ref, uf_ref, yc_ref, b_ref, k_ref, c_ref = rest
    else:
        oz_ref, ol_ref, carry_ref, uf_ref, yc_ref, b_ref, k_ref, c_ref = rest
    s = pl.program_id(0)
    nblk = b_ref.shape[0]
    ps = b_ref.shape[2] // 2
    gq = LANES // S5_GROUP_CH

    @pl.when(s == 0)
    def _():
        carry_ref[...] = jnp.zeros_like(carry_ref)

        def expand(compact, spread, row_group, per):
            full = jnp.dot(compact, spread, preferred_element_type=F32)
            col_group = (lax.broadcasted_iota(jnp.int32, (1, full.shape[1]), 1) // per) % gq
            return jnp.where(row_group == col_group, full, 0.0).astype(BF16)

        rows = lax.broadcasted_iota(jnp.int32, (wc_ref.shape[1], 1), 0)
        lag_rows_group = (rows // S5_GROUP_CH) % gq
        state_rows_group = (rows // S5_STATE) % gq
        for q in range(nblk):
            b_ref[q] = expand(wc_ref[q], ew_ref[...], lag_rows_group, S5_STATE)
            k_ref[q] = expand(kc_ref[q], ek_ref[...], lag_rows_group, S5_GROUP_CH)
            c_ref[q] = expand(cc_ref[q], ec_ref[...], state_rows_group, S5_GROUP_CH)

    def chunk(u_ref, yf_ref, o_ref):
        t = u_ref.shape[0]
        nw = t // SUBLANES
        row = lax.broadcasted_iota(jnp.int32, (t, 1), 0) & (SUBLANES - 1)
        wrow = lax.broadcasted_iota(jnp.int32, (nw, 1), 0)
        u = u_ref[...]
        uf = u.astype(F32)
        for q in range(nblk):
            uf_ref[q, 0:t] = uf[:, q * LANES:(q + 1) * LANES]
        lagged = [u]
        for k in range(1, SUBLANES):
            if reverse:
                lag = jnp.where(row < SUBLANES - k, pltpu.roll(uf, t - k, 0), 0.0)
            else:
                lag = jnp.where(row >= k, pltpu.roll(uf, k, 0), 0.0)
            lagged.append(lag.astype(BF16))
        outs = []
        for q in range(nblk):
            qs = slice(q * ps, (q + 1) * ps)
            lhs = jnp.concatenate([x[:, q * LANES:(q + 1) * LANES] for x in lagged], axis=1)
            y_loc = jnp.dot(lhs, k_ref[q], preferred_element_type=F32)
            ends = [uf_ref[q, pl.ds(k if reverse else SUBLANES - 1 - k, nw, stride=SUBLANES), :].astype(BF16)
                    for k in range(SUBLANES)]
            sw = jnp.dot(jnp.concatenate(ends, axis=1), b_ref[q], preferred_element_type=F32)
            tabs = [tab_ref[i, :, qs] for i in range(8)]
            cin_r, cin_i = carry_ref[0, :, qs], carry_ref[1, :, qs]
            c_r, c_i, cr, ci = _s5_window_scan(sw[:, :ps], sw[:, ps:], tabs, cin_r, cin_i, reverse)
            carry_ref[0, :, qs] = cr
            carry_ref[1, :, qs] = ci
            if reverse:
                e_r = jnp.where(wrow == nw - 1, cin_r[0:1], pltpu.roll(c_r, nw - 1, 0))
                e_i = jnp.where(wrow == nw - 1, cin_i[0:1], pltpu.roll(c_i, nw - 1, 0))
            else:
                e_r = jnp.where(wrow == 0, cin_r[0:1], pltpu.roll(c_r, 1, 0))
                e_i = jnp.where(wrow == 0, cin_i[0:1], pltpu.roll(c_i, 1, 0))
            y_in = jnp.dot(jnp.concatenate([e_r, e_i], axis=1).astype(BF16), c_ref[q],
                           preferred_element_type=F32)
            for j in range(SUBLANES):
                yc_ref[q, pl.ds(j, nw, stride=SUBLANES), :] = y_in[:, j * LANES:(j + 1) * LANES]
            outs.append(y_loc + yc_ref[q, 0:t])
        out = jnp.concatenate(outs, axis=1)
        if not final:
            o_ref[...] = u.astype(F32) * d_ref[...] + out
        else:
            g = _gelu_tanh(yf_ref[...] + out)
            z = jnp.dot(g.astype(BF16), gw_ref[...], preferred_element_type=F32) + gb_ref[...]
            o_ref[...] = (g * _sigmoid(z)).astype(o_ref.dtype)

    @pl.when(s == 0)
    def _():
        chunk(uz_ref, yfz_ref if final else None, oz_ref)

    @pl.when(s > 0)
    def _():
        chunk(ul_ref, yfl_ref if final else None, ol_ref)


def s5_direction(rest, rest_ctx, u_col, params, which, dskip, reverse, fwd=None, glu=None, t=1024):
    seq = rest.shape[0]
    tz = rest_ctx.shape[0]
    t = min(t, seq)
    nl = seq // t
    w = 4 * LANES
    wc, kc, cc, tabs = params
    _, nq, kq, _ = wc.shape
    gq = LANES // S5_GROUP_CH
    ps2 = 2 * gq * S5_STATE
    p2 = nq * ps2
    e_w = np.zeros((LANES, ps2), np.float32)
    for c in range(2):
        for gl in range(gq):
            for p in range(S5_STATE):
                e_w[c * S5_STATE + p, c * gq * S5_STATE + gl * S5_STATE + p] = 1.0
    e_k = np.zeros((LANES, LANES), np.float32)
    e_c = np.zeros((LANES, SUBLANES * LANES), np.float32)
    for gl in range(gq):
        for o in range(S5_GROUP_CH):
            e_k[o, gl * S5_GROUP_CH + o] = 1.0
            for j in range(SUBLANES):
                e_c[j * S5_GROUP_CH + o, j * LANES + gl * S5_GROUP_CH + o] = 1.0
    spreads = [jnp.asarray(e, BF16) for e in (e_w, e_k, e_c)]
    final = fwd is not None
    if reverse:
        lat = lambda s: (jnp.where(s == 0, nl - 1, nl - s), 0)
    else:
        lat = lambda s: (jnp.maximum(s - 1, 0), 0)
    lat_u = lambda s: (lat(s)[0], u_col)
    const = lambda s: (0, 0)
    table = lambda shape: pl.BlockSpec((None,) + shape, lambda s: (which,) + (0,) * len(shape),
                                       pipeline_mode=pl.Buffered(1))
    in_specs = [pl.BlockSpec((tz, w), lambda s: (0, u_col)),
                pl.BlockSpec((t, w), lat_u),
                table((nq, kq, LANES)),
                table((nq, kq, LANES)),
                table((nq, ps2, LANES)),
                pl.BlockSpec(e_w.shape, const),
                pl.BlockSpec(e_k.shape, const),
                pl.BlockSpec(e_c.shape, const),
                table((8, SUBLANES, p2 // 2)),
                pl.BlockSpec((1, w), const)]
    args = [rest_ctx, rest, wc, kc, cc, *spreads, tabs, dskip.reshape(1, w)]
    if final:
        in_specs += [pl.BlockSpec((tz, w), const), pl.BlockSpec((t, w), lat),
                     pl.BlockSpec((w, w), const), pl.BlockSpec((1, w), const)]
        args += [fwd[0], fwd[1], glu[0], glu[1].reshape(1, w)]
    odt = BF16 if final else F32
    return pl.pallas_call(
        functools.partial(_s5_kernel, reverse=reverse, final=final),
        grid=(nl + 1,),
        in_specs=in_specs,
        out_specs=[pl.BlockSpec((tz, w), const), pl.BlockSpec((t, w), lat)],
        out_shape=[jax.ShapeDtypeStruct((tz, w), odt), jax.ShapeDtypeStruct((seq, w), odt)],
        scratch_shapes=[pltpu.VMEM((2, SUBLANES, p2 // 2), F32),
                        pltpu.VMEM((nq, max(t, tz), LANES), F32),
                        pltpu.VMEM((nq, max(t, tz), LANES), F32),
                        pltpu.VMEM((nq, kq, ps2), BF16),
                        pltpu.VMEM((nq, kq, LANES), BF16),
                        pltpu.VMEM((nq, ps2, SUBLANES * LANES), BF16)],
        compiler_params=_cparams("arbitrary"),
        name="s5_direction",
    )(*args)


def _s5_params(a_re, a_im, log_step, b_re, b_im, c_re, c_im, rev):
    ar, ai = a_re.astype(F32), a_im.astype(F32)
    step = jnp.exp(log_step.astype(F32))[:, None]
    mag = jnp.exp(ar * step)
    ab_re, ab_im = mag * jnp.cos(ai * step), mag * jnp.sin(ai * step)
    den = ar * ar + ai * ai
    nr = ab_re - 1.0
    f_re = (nr * ar + ab_im * ai) / den
    f_im = (ab_im * ar - nr * ai) / den
    br, bi = b_re.astype(F32), b_im.astype(F32)
    bb_re = f_re[..., None] * br - f_im[..., None] * bi
    bb_im = f_re[..., None] * bi + f_im[..., None] * br
    cr, ci = c_re.astype(F32), c_im.astype(F32)
    g, p, h = bb_re.shape
    gq = LANES // h
    nq = g // gq
    nl = SUBLANES

    def cmul(xr, xi, yr, yi):
        return xr * yr - xi * yi, xr * yi + xi * yr

    def powers(xr, xi, n):
        out = [(jnp.ones_like(xr), jnp.zeros_like(xi)), (xr, xi)]
        for _ in range(n - 1):
            out.append(cmul(*out[-1], xr, xi))
        return jnp.stack([o[0] for o in out]), jnp.stack([o[1] for o in out])

    pr, pi = powers(ab_re, ab_im, nl)
    w_re = pr[:nl, ..., None] * bb_re - pi[:nl, ..., None] * bb_im
    w_im = pr[:nl, ..., None] * bb_im + pi[:nl, ..., None] * bb_re

    def rows_lag_group_channel(x):
        x = x.reshape(nl, nq, gq, h, x.shape[-1])
        return jnp.transpose(x, (1, 0, 2, 3, 4)).reshape(nq, nl * gq * h, x.shape[-1])

    wc = jnp.concatenate([rows_lag_group_channel(jnp.swapaxes(w, 2, 3)) for w in (w_re, w_im)], axis=2)
    hp = lax.Precision.HIGHEST
    kk = (jnp.einsum('kgph,gop->kgho', w_re, cr, precision=hp)
          - jnp.einsum('kgph,gop->kgho', w_im, ci, precision=hp))
    kc = jnp.pad(rows_lag_group_channel(kk), ((0, 0), (0, 0), (0, LANES - h)))
    dr = jnp.where(rev, pr[nl:0:-1], pr[1:nl + 1])
    di = jnp.where(rev, pi[nl:0:-1], pi[1:nl + 1])
    cd_re = cr[None] * dr[:, :, None, :] - ci[None] * di[:, :, None, :]
    cd_im = cr[None] * di[:, :, None, :] + ci[None] * dr[:, :, None, :]

    def rows_group_state(cd):
        return jnp.transpose(cd.reshape(nl, nq, gq, h, p), (1, 2, 4, 0, 3)).reshape(nq, gq * p, nl * h)

    cc = jnp.concatenate([rows_group_state(cd_re), -rows_group_state(cd_im)], axis=1)
    qr, qi = powers(pr[nl].reshape(-1), pi[nl].reshape(-1), SUBLANES)
    row = np.arange(SUBLANES)[:, None]
    tabs = []
    for k in (1, 2, 4):
        keep = jnp.where(rev, jnp.asarray(row < SUBLANES - k), jnp.asarray(row >= k))
        tabs += [jnp.where(keep, qr[k][None], 0.0), jnp.where(keep, qi[k][None], 0.0)]
    tabs += [jnp.where(rev, qr[SUBLANES:0:-1], qr[1:SUBLANES + 1]),
             jnp.where(rev, qi[SUBLANES:0:-1], qi[1:SUBLANES + 1])]
    return wc.astype(BF16), kc.astype(BF16), cc.astype(BF16), jnp.stack(tabs)


def _conv_kernel(val_ref, gate_ref, vp_ref, gp_ref, vn_ref, gn_ref, w_ref, b_ref, g_ref, beta_ref, o_ref, hs_ref,
                 *, sub):
    i = pl.program_id(0)
    n = pl.num_programs(0)
    t = val_ref.shape[0]

    def glu(v, g):
        return v[...].astype(F32) * _sigmoid(g[...].astype(F32))

    hs_ref[0:CONV_HALO] = jnp.where(i > 0, glu(vp_ref, gp_ref), 0.0)
    hs_ref[CONV_HALO:CONV_HALO + t] = glu(val_ref, gate_ref)
    hs_ref[CONV_HALO + t:2 * CONV_HALO + t] = jnp.where(i < n - 1, glu(vn_ref, gn_ref), 0.0)
    base = CONV_HALO - CONV_K // 2
    for r0 in range(t // sub):
        acc = jnp.zeros((sub, val_ref.shape[1]), F32) + b_ref[...]
        for r in range(SUBLANES):
            part = None
            for m in range((base + CONV_K - 1) // SUBLANES + 1):
                k = SUBLANES * m + r - base
                if 0 <= k < CONV_K:
                    lo_row = r0 * sub + SUBLANES * m
                    term = hs_ref[lo_row:lo_row + sub + SUBLANES] * w_ref[k:k + 1]
                    part = term if part is None else part + term
            acc = acc + part[r:r + sub]
        mu = jnp.mean(acc, axis=-1, keepdims=True)
        cen = acc - mu
        var = jnp.mean(cen * cen, axis=-1, keepdims=True)
        yn = cen * lax.rsqrt(var + EPS) * g_ref[...] + beta_ref[...]
        o_ref[r0 * sub:(r0 + 1) * sub] = _silu(yn).astype(o_ref.dtype)


def conv_module(rest, val_col, gate_col, w, b, g, beta, t=256, sub=64):
    seq = rest.shape[0]
    t = min(t, seq)
    cw = 4 * LANES
    hb = t // CONV_HALO
    nh = seq // CONV_HALO
    prev = lambda i: jnp.maximum(i * hb - 1, 0)
    nxt = lambda i: jnp.minimum((i + 1) * hb, nh - 1)
    vec = pl.BlockSpec((1, cw), lambda i: (0, 0))
    return pl.pallas_call(
        functools.partial(_conv_kernel, sub=sub),
        grid=(seq // t,),
        in_specs=[pl.BlockSpec((t, cw), lambda i: (i, val_col)),
                  pl.BlockSpec((t, cw), lambda i: (i, gate_col)),
                  pl.BlockSpec((CONV_HALO, cw), lambda i: (prev(i), val_col)),
                  pl.BlockSpec((CONV_HALO, cw), lambda i: (prev(i), gate_col)),
                  pl.BlockSpec((CONV_HALO, cw), lambda i: (nxt(i), val_col)),
                  pl.BlockSpec((CONV_HALO, cw), lambda i: (nxt(i), gate_col)),
                  pl.BlockSpec((CONV_K, cw), lambda i: (0, 0)), vec, vec, vec],
        out_specs=pl.BlockSpec((t, cw), lambda i: (i, 0)),
        out_shape=jax.ShapeDtypeStruct((seq, cw), BF16),
        scratch_shapes=[pltpu.VMEM((t + 2 * CONV_HALO, cw), F32)],
        compiler_params=_cparams("arbitrary"),
        name="conv_module",
    )(rest, rest, rest, rest, rest, rest, w.astype(F32), b.reshape(1, cw).astype(F32),
      g.reshape(1, cw).astype(F32), beta.reshape(1, cw).astype(F32))


def _merge_kernel(*refs, n_parts):
    y_refs = refs[:N_BRANCH]
    gl_refs = refs[N_BRANCH:N_BRANCH * (1 + n_parts)]
    wb_ref, wo_ref, x_ref, gate_ref, g_ref, sc_ref, sh_ref, o_ref, h_ref = refs[N_BRANCH * (1 + n_parts):]
    pw = gl_refs[0].shape[1]
    parts = []
    for part in range(n_parts):
        cols = slice(part * pw, (part + 1) * pw)
        m = None
        for k in range(N_BRANCH):
            proj = jnp.dot(y_refs[k][...], wb_ref[k, :, cols], preferred_element_type=F32)
            term = _sigmoid(gl_refs[k * n_parts + part][...].astype(F32)) * proj
            m = term if m is None else m + term
        parts.append(m.astype(BF16))
    merged = jnp.concatenate(parts, axis=1)
    x = x_ref[...] + gate_ref[...] * jnp.dot(merged, wo_ref[...], preferred_element_type=F32)
    o_ref[...] = x
    y = x * lax.rsqrt(jnp.mean(x * x, axis=-1, keepdims=True) + EPS) * g_ref[...]
    h_ref[...] = (y * (1.0 + sc_ref[...]) + sh_ref[...]).astype(h_ref.dtype)


def gated_merge_residual(ys, rest, logit_col, part_w, wb, wo, layer, x, gate, g2, scale2, shift2, tm=256):
    m, yw = ys[0].shape
    n = wb.shape[3]
    n_parts = n // part_w
    tm = min(tm, m)
    resident = lambda shape: pl.BlockSpec((None,) + shape, lambda i: (layer,) + (0,) * len(shape),
                                          pipeline_mode=pl.Buffered(1))
    yspec = pl.BlockSpec((tm, yw), lambda i: (i, 0))
    gspecs = [pl.BlockSpec((tm, part_w), functools.partial(lambda i, c: (i, c), c=logit_col + k * n_parts + part))
              for k in range(N_BRANCH) for part in range(n_parts)]
    vec = pl.BlockSpec((1, n), lambda i: (0, 0))
    rows = pl.BlockSpec((tm, n), lambda i: (i, 0))
    return pl.pallas_call(
        functools.partial(_merge_kernel, n_parts=n_parts),
        grid=(m // tm,),
        in_specs=[yspec] * N_BRANCH + gspecs + [resident((N_BRANCH, yw, n)), resident((n, n)), rows,
                                                vec, vec, vec, vec],
        out_specs=[rows, rows],
        out_shape=[jax.ShapeDtypeStruct((m, n), F32), jax.ShapeDtypeStruct((m, n), BF16)],
        compiler_params=_cparams("arbitrary"),
        name="gated_merge_residual",
    )(*ys, *([rest] * (N_BRANCH * n_parts)), wb, wo, x, gate.reshape(1, n), g2.reshape(1, n),
      scale2.reshape(1, n), shift2.reshape(1, n))


def _mlp_kernel(x_ref, h_ref, gate_ref, w1_ref, w2_ref, *rest, final_norm):
    if final_norm:
        gf_ref, o_ref, acc_ref = rest
    else:
        o_ref, acc_ref = rest
    f = pl.program_id(1)

    @pl.when(f == 0)
    def _():
        acc_ref[...] = jnp.zeros_like(acc_ref)

    t = jnp.dot(h_ref[...], w1_ref[...], preferred_element_type=F32)
    t = jnp.square(jnp.maximum(t, 0.0)).astype(BF16)
    acc_ref[...] += jnp.dot(t, w2_ref[...], preferred_element_type=F32)

    @pl.when(f == pl.num_programs(1) - 1)
    def _():
        y = x_ref[...] + gate_ref[...] * acc_ref[...]
        if final_norm:
            y = y * lax.rsqrt(jnp.mean(y * y, axis=-1, keepdims=True) + EPS) * gf_ref[...]
        o_ref[...] = y


def mlp_block(x, h, gate, w1, w2, layer, final_g=None, tm=512, tf=1024):
    m, d = x.shape
    ff = w1.shape[2]
    tm = min(tm, m)
    vec = pl.BlockSpec((1, d), lambda i, f: (0, 0))
    rows = pl.BlockSpec((tm, d), lambda i, f: (i, 0))
    final_norm = final_g is not None
    args = [x, h, gate.reshape(1, d), w1, w2]
    in_specs = [rows, rows, vec,
                pl.BlockSpec((None, d, tf), lambda i, f: (layer, 0, f)),
                pl.BlockSpec((None, tf, d), lambda i, f: (layer, f, 0))]
    if final_norm:
        args.append(final_g.reshape(1, d))
        in_specs.append(vec)
    return pl.pallas_call(
        functools.partial(_mlp_kernel, final_norm=final_norm),
        grid=(m // tm, ff // tf),
        in_specs=in_specs,
        out_specs=pl.BlockSpec((tm, d), lambda i, f: (i, 0)),
        out_shape=jax.ShapeDtypeStruct((m, d), F32),
        scratch_shapes=[pltpu.VMEM((tm, d), F32)],
        compiler_params=_cparams("arbitrary", "arbitrary"),
        name="mlp_block",
    )(*args)


_A_COL_K, _A_COL_V = 2, 3
_REST_COL0 = (A_HEADS + 2 * A_KV_HEADS) * HEAD_DIM
_COL_U, _COL_QC, _COL_KC, _COL_VC, _COL_VAL, _COL_GATE = 0, 1, 2, 3, 4, 5
_MERGE_PART = 1024
_COL_MERGE = 3


def _rope_tables(seq):
    freqs = ROPE_BASE ** (-jnp.arange(ROPE_PAIRS, dtype=F32) / ROPE_PAIRS)
    rows = seq // GRID_W
    ang_r = jnp.arange(rows).astype(F32)[:, None] * freqs
    ang_c = jnp.arange(GRID_W).astype(F32)[:, None] * freqs

    def expand(tab_r, tab_c, first, second):
        r = jnp.repeat(tab_r, GRID_W, axis=0)
        cc = jnp.tile(tab_c, (rows, 1))
        head = jnp.concatenate([first * r, second * r, first * cc, second * cc], axis=1)
        return jnp.tile(head, (1, LANES // HEAD_DIM))

    cos = expand(jnp.cos(ang_r), jnp.cos(ang_c), 1.0, 1.0)
    sa = expand(jnp.sin(ang_r), jnp.sin(ang_c), 0.0, 1.0)
    sb = expand(jnp.sin(ang_r), jnp.sin(ang_c), -1.0, 0.0)
    return cos, sa, sb


def kernel(x, c, ctx, c_ctx, ada_w, ada_b, norm1_g, norm2_g, w_in, a_sink, s5_a_re, s5_a_im, s5_log_step, s5_b_re, s5_b_im, s5_c_re, s5_c_im, s5_d, s5_glu_w, s5_glu_b, c_rpb, d_conv_w, d_conv_b, d_ln_g, d_ln_b, w_branch, w_out, mlp_w1, mlp_w2, final_g):
    depth = ada_w.shape[0]
    _, seq, d = x.shape
    lc = ctx.shape[1]
    xs = x.reshape(seq, d)
    zs = ctx.reshape(lc, d)
    rope = _rope_tables(seq)

    cc = jnp.zeros((SUBLANES, d), F32).at[0].set(c[0]).at[1].set(c_ctx)
    mods = ada_modulation(cc, ada_w, ada_b)
    n_rest = w_in.shape[2] - _REST_COL0
    w1, w2 = mlp_w1.astype(BF16), mlp_w2.astype(BF16)
    wb, wo = w_branch.astype(BF16), w_out.astype(BF16)
    nbr_bias = jax.vmap(lambda rpb: _nbr_bias(rpb, seq // GRID_W))(c_rpb)
    s5_params = jax.vmap(_s5_params)(*(a.reshape((2 * depth,) + a.shape[2:]) for a in
                                       (s5_a_re, s5_a_im, s5_log_step, s5_b_re, s5_b_im, s5_c_re, s5_c_im)),
                                     jnp.arange(2 * depth) % 2 == 1)

    for l in range(depth):
        need_ctx = l < depth - 1
        mx = [mods[l, 0, k * d:(k + 1) * d] for k in range(6)]
        mz = [mods[l, 1, k * d:(k + 1) * d] for k in range(6)]

        hx, qk_x = norm_head_projection(xs, norm1_g[l], mx[1], mx[0], w_in, l, rope=rope)
        hz, qk_z = norm_head_projection(zs, norm1_g[l], mz[1], mz[0], w_in, l)
        r_x = wide_projection(hx, w_in, l, _REST_COL0, n_rest)
        r_z = wide_projection(hz, w_in, l, _REST_COL0, n_rest if need_ctx else (_COL_VC + 1) * 4 * LANES)

        ya_x = window_attention(a_sink[l], qk_x, qk_z)
        yc_x = neighborhood_attention(r_x, r_z, nbr_bias, l, _COL_QC, _COL_KC)
        yd_x = conv_module(r_x, _COL_VAL, _COL_GATE, d_conv_w[l], d_conv_b[l], d_ln_g[l], d_ln_b[l])

        yf = s5_direction(r_x, r_z, _COL_U, s5_params, 2 * l, s5_d[l].astype(F32), reverse=False)
        yb_z, yb_x = s5_direction(r_x, r_z, _COL_U, s5_params, 2 * l + 1, s5_d[l].astype(F32), reverse=True, fwd=yf,
                                  glu=(s5_glu_w[l].astype(BF16), s5_glu_b[l].astype(F32)))

        xs, h2 = gated_merge_residual((ya_x, yb_x, yc_x, yd_x), r_x, _COL_MERGE, _MERGE_PART, wb, wo, l, xs, mx[2],
                                      norm2_g[l], mx[4], mx[3])
        xs = mlp_block(xs, h2, mx[5], w1, w2, l, final_g=None if need_ctx else final_g)

        if need_ctx:
            ya_z = context_attention(a_sink[l], qk_z, 0, qk_z, _A_COL_K, 2 * LANES, qk_z, _A_COL_V, True, True)
            yc_z = context_attention(a_sink[l], r_z, _COL_QC, r_z, _COL_KC, 4 * LANES, r_z, _COL_VC, False, False)
            yd_z = conv_module(r_z, _COL_VAL, _COL_GATE, d_conv_w[l], d_conv_b[l], d_ln_g[l], d_ln_b[l])
            zs, h2 = gated_merge_residual((ya_z, yb_z, yc_z, yd_z), r_z, _COL_MERGE, _MERGE_PART, wb, wo, l, zs,
                                          mz[2], norm2_g[l], mz[4], mz[3])
            zs = mlp_block(zs, h2, mz[5], w1, w2, l)

    return xs.reshape(x.shape)
```

```python
import functools
import math

import numpy as np
import jax
import jax.numpy as jnp
from jax import lax
from jax.experimental import pallas as pl
from jax.experimental.pallas import tpu as pltpu

F32 = jnp.float32
BF16 = jnp.bfloat16

LANES = 128
SUBLANES = 8
VMEM_LIMIT = 56 * 1024 * 1024

GRID_W = 64
HEAD_DIM = 64
N_BRANCH = 4
A_HEADS = 8
A_KV_HEADS = 2
A_WINDOW = 128
A_BLOCK = 128
ROPE_BASE = 10000.0
ROPE_PAIRS = HEAD_DIM // 4
S5_GROUP_CH = 16
S5_STATE = 64
C_HEADS = 8
C_WIN_R = 8
C_WIN_C = 16
C_QROWS = 2
C_KROWS = C_QROWS + C_WIN_R
CONV_K = 31
CONV_HALO = 16
EPS = 1e-6
NEG_INF = -1e30


def _cparams(*sem):
    return pltpu.CompilerParams(dimension_semantics=sem, vmem_limit_bytes=VMEM_LIMIT)


def _sigmoid(x):
    return 0.5 * jnp.tanh(0.5 * x) + 0.5


def _silu(x):
    return x * _sigmoid(x)


def _gelu_tanh(x):
    return 0.5 * x * (1.0 + jnp.tanh(math.sqrt(2.0 / math.pi) * (x + 0.044715 * (x * x * x))))


def _ada_kernel(c_ref, w_ref, b_ref, o_ref):
    a = _silu(c_ref[...]).astype(BF16)
    o_ref[...] = jnp.dot(a, w_ref[...].astype(BF16), preferred_element_type=F32) + b_ref[...]


def ada_modulation(cc, ada_w, ada_b, tn=1024):
    depth, d, n = ada_w.shape
    return pl.pallas_call(
        _ada_kernel,
        grid=(depth, n // tn),
        in_specs=[pl.BlockSpec((SUBLANES, d), lambda l, j: (0, 0)),
                  pl.BlockSpec((None, d, tn), lambda l, j: (l, 0, j)),
                  pl.BlockSpec((None, 1, tn), lambda l, j: (l, 0, j))],
        out_specs=pl.BlockSpec((None, SUBLANES, tn), lambda l, j: (l, 0, j)),
        out_shape=jax.ShapeDtypeStruct((depth, SUBLANES, n), F32),
        compiler_params=_cparams("arbitrary", "arbitrary"),
        name="ada_modulation",
    )(cc, ada_w, ada_b.reshape(depth, 1, n))


def _rope_block(y, cos, sa, sb):
    return y * cos + pltpu.roll(y, 16, 1) * sa + pltpu.roll(y, LANES - 16, 1) * sb


def _head_proj_kernel(x_ref, g_ref, sc_ref, sh_ref, w_ref, *rest, n_q, with_rope):
    if with_rope:
        cos_ref, sa_ref, sb_ref, h_ref, o_ref, wb_ref = rest
        cos, sa, sb = cos_ref[...], sa_ref[...], sb_ref[...]
    else:
        h_ref, o_ref, wb_ref = rest

    @pl.when(pl.program_id(0) == 0)
    def _():
        wb_ref[...] = w_ref[0].astype(BF16)

    x = x_ref[...]
    y = x * lax.rsqrt(jnp.mean(x * x, axis=-1, keepdims=True) + EPS) * g_ref[...]
    h = (y * (1.0 + sc_ref[...]) + sh_ref[...]).astype(BF16)
    h_ref[...] = h
    acc = jnp.dot(h, wb_ref[...], preferred_element_type=F32)
    lo = _lane_lo()
    out = 0
    for b in range(acc.shape[1] // LANES):
        y = acc[:, b * LANES:(b + 1) * LANES]
        if with_rope and b <= n_q:
            y = _rope_block(y, cos, sa, sb)
        if b < n_q:
            blocks = [y]
        else:
            swapped = pltpu.roll(y, HEAD_DIM, 1)
            blocks = [jnp.where(lo, y, swapped), jnp.where(lo, swapped, y)]
        for blk in blocks:
            o_ref[:, out * LANES:(out + 1) * LANES] = blk.astype(o_ref.dtype)
            out += 1


def norm_head_projection(x, g, scale, shift, w_in, layer, rope=None, tm=512):
    m, k = x.shape
    n_in = (A_HEADS + 2 * A_KV_HEADS) * HEAD_DIM
    n_q = A_HEADS * HEAD_DIM // LANES
    assert A_KV_HEADS * HEAD_DIM == LANES
    n_out = n_in + 2 * LANES
    tm = min(tm, m)
    vec = pl.BlockSpec((1, k), lambda i: (0, 0))
    in_specs = [pl.BlockSpec((tm, k), lambda i: (i, 0)), vec, vec, vec,
                pl.BlockSpec((pl.Element(1), pl.Element(k), pl.Element(n_in)), lambda i: (layer, 0, 0))]
    args = [x, g.reshape(1, k), scale.reshape(1, k), shift.reshape(1, k), w_in]
    if rope is not None:
        in_specs += [pl.BlockSpec((tm, LANES), lambda i: (i, 0))] * 3
        args += list(rope)
    return pl.pallas_call(
        functools.partial(_head_proj_kernel, n_q=n_q, with_rope=rope is not None),
        grid=(m // tm,),
        in_specs=in_specs,
        out_specs=[pl.BlockSpec((tm, k), lambda i: (i, 0)), pl.BlockSpec((tm, n_out), lambda i: (i, 0))],
        out_shape=[jax.ShapeDtypeStruct((m, k), BF16), jax.ShapeDtypeStruct((m, n_out), BF16)],
        scratch_shapes=[pltpu.VMEM((k, n_in), BF16)],
        compiler_params=_cparams("arbitrary"),
        name="norm_head_projection",
    )(*args)


def _wide_proj_kernel(a_ref, w_ref, o_ref, wb_ref):
    @pl.when(pl.program_id(1) == 0)
    def _():
        wb_ref[...] = w_ref[0].astype(BF16)

    o_ref[...] = jnp.dot(a_ref[...], wb_ref[...], preferred_element_type=F32).astype(o_ref.dtype)


def wide_projection(a, w, layer, col0, n, tm=2048, tn=1024):
    m, k = a.shape
    tm = min(tm, m)
    assert n % tn == 0 and col0 % LANES == 0 and tn % LANES == 0
    return pl.pallas_call(
        _wide_proj_kernel,
        grid=(n // tn, m // tm),
        in_specs=[pl.BlockSpec((tm, k), lambda j, i: (i, 0)),
                  pl.BlockSpec((pl.Element(1), pl.Element(k), pl.Element(tn)),
                               lambda j, i: (layer, 0, pl.multiple_of(col0 + j * tn, LANES)))],
        out_specs=pl.BlockSpec((tm, tn), lambda j, i: (i, j)),
        out_shape=jax.ShapeDtypeStruct((m, n), BF16),
        scratch_shapes=[pltpu.VMEM((k, tn), BF16)],
        compiler_params=_cparams("arbitrary", "arbitrary"),
        name="wide_projection",
    )(a, w)


def _lane_lo():
    return lax.broadcasted_iota(jnp.int32, (1, LANES), 1) < HEAD_DIM


def _half_select(x, lo, half):
    zero = jnp.zeros_like(x)
    return jnp.where(lo, x, zero) if half == 0 else jnp.where(lo, zero, x)


def _dot_nt(a, b):
    return lax.dot_general(a, b, (((1,), (1,)), ((), ())), preferred_element_type=F32)


def _softmax_pv(parts, values, sink=None):
    m = parts[0].max(axis=-1, keepdims=True)
    for s in parts[1:]:
        m = jnp.maximum(m, s.max(axis=-1, keepdims=True))
    if sink is not None:
        m = jnp.maximum(m, sink)
    den = jnp.exp(sink - m) if sink is not None else 0.0
    out = 0.0
    for s, v in zip(parts, values):
        p = jnp.exp(s - m)
        den = den + p.sum(axis=-1, keepdims=True)
        out = out + jnp.dot(p.astype(BF16), v, preferred_element_type=F32)
    return out / den


def _attn_a_window_start(i, seq):
    return jnp.clip((i - 1) * A_BLOCK, 0, seq - 3 * A_BLOCK)


def _attn_a_kernel(sink_ref, q_ref, kv0_ref, kv1_ref, kv2_ref, kvz_ref, o_ref, *, seq):
    i = pl.program_id(0)
    lo = _lane_lo()
    nq = 4 * A_BLOCK
    r = lax.broadcasted_iota(jnp.int32, (nq, 3 * A_BLOCK), 0) & (A_BLOCK - 1)
    c = lax.broadcasted_iota(jnp.int32, (nq, 3 * A_BLOCK), 1)
    dist = c - r + (_attn_a_window_start(i, seq) - i * A_BLOCK)
    valid = (dist >= -A_WINDOW) & (dist <= A_WINDOW)
    rowblk = lax.broadcasted_iota(jnp.int32, (nq, 1), 0) // A_BLOCK
    vcol = A_KV_HEADS * LANES
    for g in range(A_KV_HEADS):
        gs = slice(g * LANES, (g + 1) * LANES)
        vs = slice(vcol + g * LANES, vcol + (g + 1) * LANES)
        kw = jnp.concatenate([ref[:, gs] for ref in (kv0_ref, kv1_ref, kv2_ref)], axis=0)
        vw = jnp.concatenate([ref[:, vs] for ref in (kv0_ref, kv1_ref, kv2_ref)], axis=0)
        kz, vz = kvz_ref[:, gs], kvz_ref[:, vs]
        blocks = [slice((2 * g + bb) * LANES, (2 * g + bb + 1) * LANES) for bb in range(2)]
        ql = jnp.concatenate([_half_select(q_ref[:, bs], lo, half) * (HEAD_DIM ** -0.5)
                              for half in range(2) for bs in blocks], axis=0)
        heads = [2 * (2 * g + bb) + half for half in range(2) for bb in range(2)]
        sink = jnp.full((nq, 1), sink_ref[heads[3]], F32)
        for k in range(3):
            sink = jnp.where(rowblk == k, sink_ref[heads[k]], sink)
        s = jnp.where(valid, _dot_nt(ql, kw), NEG_INF)
        sz = _dot_nt(ql, kz)
        m = jnp.maximum(jnp.maximum(s.max(axis=-1, keepdims=True), sz.max(axis=-1, keepdims=True)), sink)
        p, pz = jnp.exp(s - m), jnp.exp(sz - m)
        inv = 1.0 / (p.sum(axis=-1, keepdims=True) + pz.sum(axis=-1, keepdims=True) + jnp.exp(sink - m))
        p, pz = p.astype(BF16), pz.astype(BF16)
        out = None
        for half in range(2):
            hs = slice(half * 2 * A_BLOCK, (half + 1) * 2 * A_BLOCK)
            o = (jnp.dot(p[hs], _half_select(vw, lo, half), preferred_element_type=F32)
                 + jnp.dot(pz[hs], _half_select(vz, lo, half), preferred_element_type=F32)) * inv[hs]
            out = o if out is None else out + o
        for bb in range(2):
            o_ref[:, blocks[bb]] = out[bb * A_BLOCK:(bb + 1) * A_BLOCK].astype(o_ref.dtype)


def window_attention(sink, qk, qk_ctx):
    seq = qk.shape[0]
    lc = qk_ctx.shape[0]
    nb = seq // A_BLOCK
    qw = A_HEADS * HEAD_DIM
    kvw = 4 * A_KV_HEADS * HEAD_DIM
    assert nb >= 3 and qw == kvw
    kv_specs = [pl.BlockSpec((A_BLOCK, kvw), functools.partial(
        lambda i, t: (_attn_a_window_start(i, seq) // A_BLOCK + t, 1), t=t)) for t in range(3)]
    return pl.pallas_call(
        functools.partial(_attn_a_kernel, seq=seq),
        grid=(nb,),
        in_specs=[pl.BlockSpec(memory_space=pltpu.SMEM),
                  pl.BlockSpec((A_BLOCK, qw), lambda i: (i, 0))] + kv_specs + [
                  pl.BlockSpec((lc, kvw), lambda i: (0, 1))],
        out_specs=pl.BlockSpec((A_BLOCK, qw), lambda i: (i, 0)),
        out_shape=jax.ShapeDtypeStruct((seq, qw), BF16),
        compiler_params=_cparams("arbitrary"),
        name="window_attention",
    )(sink, qk, qk, qk, qk, qk_ctx)


def _ctx_attn_kernel(sink_ref, q_ref, k_ref, v_ref, o_ref, *, kv_shared, has_sink):
    lo = _lane_lo()
    for b in range(q_ref.shape[1] // LANES):
        bs = slice(b * LANES, (b + 1) * LANES)
        kb = b // 2 if kv_shared else b
        ks = slice(kb * LANES, (kb + 1) * LANES)
        qb, k, v = q_ref[:, bs], k_ref[:, ks], v_ref[:, ks]
        acc = jnp.zeros((q_ref.shape[0], LANES), F32)
        for half in range(2):
            ql = _half_select(qb, lo, half) * (HEAD_DIM ** -0.5)
            sink = sink_ref[2 * b + half] if has_sink else None
            acc = acc + _softmax_pv([_dot_nt(ql, k)], [_half_select(v, lo, half)], sink=sink)
        o_ref[:, bs] = acc.astype(o_ref.dtype)


def context_attention(sink, q_arr, q_col, k_arr, k_col, k_w, v_arr, v_col, kv_shared, has_sink):
    lc = q_arr.shape[0]
    return pl.pallas_call(
        functools.partial(_ctx_attn_kernel, kv_shared=kv_shared, has_sink=has_sink),
        grid=(1,),
        in_specs=[pl.BlockSpec(memory_space=pltpu.SMEM),
                  pl.BlockSpec((lc, 4 * LANES), lambda i: (0, q_col)),
                  pl.BlockSpec((lc, k_w), lambda i: (0, k_col)),
                  pl.BlockSpec((lc, k_w), lambda i: (0, v_col))],
        out_specs=pl.BlockSpec((lc, 4 * LANES), lambda i: (0, 0)),
        out_shape=jax.ShapeDtypeStruct((lc, 4 * LANES), BF16),
        compiler_params=_cparams("arbitrary"),
        name="context_attention",
    )(sink, q_arr, k_arr, v_arr)


def _nbr_row_offsets(rows):
    nsteps = rows // C_QROWS
    nblk = C_KROWS * GRID_W // LANES
    wr = min(C_WIN_R, rows)
    reps = [0, 1, 2, nsteps - 2, nsteps - 1]
    dr = -np.ones((len(reps), C_QROWS, C_KROWS), np.int32)
    for v, i in enumerate(reps):
        w0 = C_QROWS * int(np.clip(i - 2, 0, nsteps - nblk))
        for a in range(C_QROWS):
            rq = C_QROWS * i + a
            k0 = int(np.clip(rq - C_WIN_R // 2, 0, rows - wr))
            for j in range(C_KROWS):
                if k0 <= w0 + j < k0 + wr:
                    dr[v, a, j] = w0 + j - rq + C_WIN_R - 1
    return dr


def _nbr_bias(rpb, rows):
    h = rpb.shape[0]
    qc = np.arange(GRID_W)[:, None]
    kc = np.arange(GRID_W)[None, :]
    cstart = np.clip(qc - C_WIN_C // 2, 0, GRID_W - C_WIN_C)
    col_ok = jnp.asarray((kc >= cstart) & (kc < cstart + C_WIN_C))
    pad = GRID_W - C_WIN_C
    ext = jnp.pad(rpb.astype(F32), ((0, 0), (0, 0), (pad, pad)), mode='edge')
    toep = jnp.stack([ext[:, :, GRID_W - 1 - q:2 * GRID_W - 1 - q] for q in range(GRID_W)], axis=2)
    toep = jnp.where(col_ok, toep, NEG_INF)
    masked = jnp.full((h, GRID_W, GRID_W), NEG_INF, F32)
    dr = _nbr_row_offsets(rows)
    variants = []
    for v in range(dr.shape[0]):
        qrows = [jnp.concatenate([toep[:, dr[v, a, j]] if dr[v, a, j] >= 0 else masked for j in range(C_KROWS)],
                                 axis=-1) for a in range(C_QROWS)]
        variants.append(jnp.concatenate(qrows, axis=-2).reshape(h * C_QROWS * GRID_W, C_KROWS * GRID_W))
    return jnp.stack(variants)


def _attn_c_kernel(q_ref, *refs):
    nblk = (len(refs) - 3) // 2
    k_refs, v_refs = refs[:nblk], refs[nblk:2 * nblk]
    kvz_ref, bias_ref, o_ref = refs[-3:]
    lo = _lane_lo()
    nq = q_ref.shape[0]
    nb = C_HEADS // 2
    vcol = C_HEADS * HEAD_DIM
    blocks = [slice(b * LANES, (b + 1) * LANES) for b in range(nb)]
    s_parts, sz_parts = [], []
    for bs in blocks:
        ql = jnp.concatenate([_half_select(q_ref[:, bs], lo, half) * (HEAD_DIM ** -0.5) for half in range(2)],
                             axis=0)
        s_parts.append(_dot_nt(ql, jnp.concatenate([ref[:, bs] for ref in k_refs], axis=0)))
        sz_parts.append(_dot_nt(ql, kvz_ref[:, bs]))
    s = jnp.concatenate(s_parts, axis=0) + bias_ref[...]
    sz = jnp.concatenate(sz_parts, axis=0)
    m = jnp.maximum(s.max(axis=-1, keepdims=True), sz.max(axis=-1, keepdims=True))
    p, pz = jnp.exp(s - m), jnp.exp(sz - m)
    inv = 1.0 / (p.sum(axis=-1, keepdims=True) + pz.sum(axis=-1, keepdims=True))
    p, pz = p.astype(BF16), pz.astype(BF16)
    for b, bs in enumerate(blocks):
        vs = slice(vcol + b * LANES, vcol + (b + 1) * LANES)
        vw, vz = jnp.concatenate([ref[:, bs] for ref in v_refs], axis=0), kvz_ref[:, vs]
        out = None
        for half in range(2):
            hs = slice((2 * b + half) * nq, (2 * b + half + 1) * nq)
            o = (jnp.dot(p[hs], _half_select(vw, lo, half), preferred_element_type=F32)
                 + jnp.dot(pz[hs], _half_select(vz, lo, half), preferred_element_type=F32)) * inv[hs]
            out = o if out is None else out + o
        o_ref[:, bs] = out.astype(o_ref.dtype)


def neighborhood_attention(rest, rest_ctx, bias, layer, q_col, k_col):
    seq = rest.shape[0]
    lc = rest_ctx.shape[0]
    rows = seq // GRID_W
    nsteps = rows // C_QROWS
    nblk = C_KROWS * GRID_W // LANES
    assert nsteps > nblk and C_QROWS * GRID_W == LANES and k_col % 2 == 0

    def first_blk(i):
        return jnp.clip(i - 2, 0, nsteps - nblk)

    def variant(i):
        return jnp.where(i < 2, i, jnp.where(i >= nsteps - 2, i - (nsteps - 2) + 3, 2))

    w = 4 * LANES
    kv_specs = [pl.BlockSpec((LANES, w), functools.partial(lambda i, t, c: (first_blk(i) + t, c), t=t, c=c))
                for c in (k_col, k_col + 1) for t in range(nblk)]
    return pl.pallas_call(
        _attn_c_kernel,
        grid=(nsteps,),
        in_specs=[pl.BlockSpec((LANES, w), lambda i: (i, q_col))] + kv_specs + [
                  pl.BlockSpec((lc, 2 * w), lambda i: (0, k_col // 2)),
                  pl.BlockSpec((None, None, C_HEADS * LANES, nblk * LANES),
                               lambda i: (layer, variant(i), 0, 0))],
        out_specs=pl.BlockSpec((LANES, w), lambda i: (i, 0)),
        out_shape=jax.ShapeDtypeStruct((seq, w), BF16),
        compiler_params=_cparams("arbitrary"),
        name="neighborhood_attention",
    )(rest, *([rest] * (2 * nblk)), rest_ctx, bias)


def _s5_window_scan(sr, si, tabs, cr, ci, reverse):
    n_tiles = sr.shape[0] // SUBLANES
    edge = 0 if reverse else SUBLANES - 1
    out_r, out_i = [None] * n_tiles, [None] * n_tiles
    for step in range(n_tiles):
        tile = (n_tiles - 1 - step) if reverse else step
        xr = sr[tile * SUBLANES:(tile + 1) * SUBLANES]
        xi = si[tile * SUBLANES:(tile + 1) * SUBLANES]
        for lvl, k in enumerate((1, 2, 4)):
            shift = (SUBLANES - k) if reverse else k
            rr, ri = pltpu.roll(xr, shift, 0), pltpu.roll(xi, shift, 0)
            mr, mi = tabs[2 * lvl], tabs[2 * lvl + 1]
            xr, xi = xr + mr * rr - mi * ri, xi + mr * ri + mi * rr
        pr, pi = tabs[6], tabs[7]
        xr, xi = xr + pr * cr - pi * ci, xi + pr * ci + pi * cr
        cr, ci = jnp.broadcast_to(xr[edge:edge + 1], xr.shape), jnp.broadcast_to(xi[edge:edge + 1], xi.shape)
        out_r[tile], out_i[tile] = xr, xi
    return jnp.concatenate(out_r, axis=0), jnp.concatenate(out_i, axis=0), cr, ci


def _s5_kernel(uz_ref, ul_ref, wc_ref, kc_ref, cc_ref, ew_ref, ek_ref, ec_ref, tab_ref, d_ref, *rest, reverse, final):
    if final:
        yfz_ref, yfl_ref, gw_ref, gb_ref, oz_ref, ol_ref, carry_ref, uf_ref, yc_ref, b_ref, k_ref, c_ref = rest
    else:
        oz_ref, ol_ref, carry_ref, uf_ref, yc_ref, b_ref, k_ref, c_ref = rest
    s = pl.program_id(0)
    nblk = b_ref.shape[0]
    ps = b_ref.shape[2] // 2
    gq = LANES // S5_GROUP_CH

    @pl.when(s == 0)
    def _():
        carry_ref[...] = jnp.zeros_like(carry_ref)

        def expand(compact, spread, row_group, per):
            full = jnp.dot(compact, spread, preferred_element_type=F32)
            col_group = (lax.broadcasted_iota(jnp.int32, (1, full.shape[1]), 1) // per) % gq
            return jnp.where(row_group == col_group, full, 0.0).astype(BF16)

        rows = lax.broadcasted_iota(jnp.int32, (wc_ref.shape[1], 1), 0)
        lag_rows_group = (rows // S5_GROUP_CH) % gq
        state_rows_group = (rows // S5_STATE) % gq
        for q in range(nblk):
            b_ref[q] = expand(wc_ref[q], ew_ref[...], lag_rows_group, S5_STATE)
            k_ref[q] = expand(kc_ref[q], ek_ref[...], lag_rows_group, S5_GROUP_CH)
            c_ref[q] = expand(cc_ref[q], ec_ref[...], state_rows_group, S5_GROUP_CH)

    def chunk(u_ref, yf_ref, o_ref):
        t = u_ref.shape[0]
        nw = t // SUBLANES
        row = lax.broadcasted_iota(jnp.int32, (t, 1), 0) & (SUBLANES - 1)
        wrow = lax.broadcasted_iota(jnp.int32, (nw, 1), 0)
        u = u_ref[...]
        uf = u.astype(F32)
        for q in range(nblk):
            uf_ref[q, 0:t] = uf[:, q * LANES:(q + 1) * LANES]
        lagged = [u]
        for k in range(1, SUBLANES):
            if reverse:
                lag = jnp.where(row < SUBLANES - k, pltpu.roll(uf, t - k, 0), 0.0)
            else:
                lag = jnp.where(row >= k, pltpu.roll(uf, k, 0), 0.0)
            lagged.append(lag.astype(BF16))
        outs = []
        for q in range(nblk):
            qs = slice(q * ps, (q + 1) * ps)
            lhs = jnp.concatenate([x[:, q * LANES:(q + 1) * LANES] for x in lagged], axis=1)
            y_loc = jnp.dot(lhs, k_ref[q], preferred_element_type=F32)
            ends = [uf_ref[q, pl.ds(k if reverse else SUBLANES - 1 - k, nw, stride=SUBLANES), :].astype(BF16)
                    for k in range(SUBLANES)]
            sw = jnp.dot(jnp.concatenate(ends, axis=1), b_ref[q], preferred_element_type=F32)
            tabs = [tab_ref[i, :, qs] for i in range(8)]
            cin_r, cin_i = carry_ref[0, :, qs], carry_ref[1, :, qs]
            c_r, c_i, cr, ci = _s5_window_scan(sw[:, :ps], sw[:, ps:], tabs, cin_r, cin_i, reverse)
            carry_ref[0, :, qs] = cr
            carry_ref[1, :, qs] = ci
            if reverse:
                e_r = jnp.where(wrow == nw - 1, cin_r[0:1], pltpu.roll(c_r, nw - 1, 0))
                e_i = jnp.where(wrow == nw - 1, cin_i[0:1], pltpu.roll(c_i, nw - 1, 0))
            else:
                e_r = jnp.where(wrow == 0, cin_r[0:1], pltpu.roll(c_r, 1, 0))
                e_i = jnp.where(wrow == 0, cin_i[0:1], pltpu.roll(c_i, 1, 0))
            y_in = jnp.dot(jnp.concatenate([e_r, e_i], axis=1).astype(BF16), c_ref[q],
                           preferred_element_type=F32)
            for j in range(SUBLANES):
                yc_ref[q, pl.ds(j, nw, stride=SUBLANES), :] = y_in[:, j * LANES:(j + 1) * LANES]
            outs.append(y_loc + yc_ref[q, 0:t])
        out = jnp.concatenate(outs, axis=1)
        if not final:
            o_ref[...] = u.astype(F32) * d_ref[...] + out
        else:
            g = _gelu_tanh(yf_ref[...] + out)
            z = jnp.dot(g.astype(BF16), gw_ref[...], preferred_element_type=F32) + gb_ref[...]
            o_ref[...] = (g * _sigmoid(z)).astype(o_ref.dtype)

    @pl.when(s == 0)
    def _():
        chunk(uz_ref, yfz_ref if final else None, oz_ref)

    @pl.when(s > 0)
    def _():
        chunk(ul_ref, yfl_ref if final else None, ol_ref)


def s5_direction(rest, rest_ctx, u_col, params, which, dskip, reverse, fwd=None, glu=None, t=1024):
    seq = rest.shape[0]
    tz = rest_ctx.shape[0]
    t = min(t, seq)
    nl = seq // t
    w = 4 * LANES
    wc, kc, cc, tabs = params
    _, nq, kq, _ = wc.shape
    gq = LANES // S5_GROUP_CH
    ps2 = 2 * gq * S5_STATE
    p2 = nq * ps2
    e_w = np.zeros((LANES, ps2), np.float32)
    for c in range(2):
        for gl in range(gq):
            for p in range(S5_STATE):
                e_w[c * S5_STATE + p, c * gq * S5_STATE + gl * S5_STATE + p] = 1.0
    e_k = np.zeros((LANES, LANES), np.float32)
    e_c = np.zeros((LANES, SUBLANES * LANES), np.float32)
    for gl in range(gq):
        for o in range(S5_GROUP_CH):
            e_k[o, gl * S5_GROUP_CH + o] = 1.0
            for j in range(SUBLANES):
                e_c[j * S5_GROUP_CH + o, j * LANES + gl * S5_GROUP_CH + o] = 1.0
    spreads = [jnp.asarray(e, BF16) for e in (e_w, e_k, e_c)]
    final = fwd is not None
    if reverse:
        lat = lambda s: (jnp.where(s == 0, nl - 1, nl - s), 0)
    else:
        lat = lambda s: (jnp.maximum(s - 1, 0), 0)
    lat_u = lambda s: (lat(s)[0], u_col)
    const = lambda s: (0, 0)
    table = lambda shape: pl.BlockSpec((None,) + shape, lambda s: (which,) + (0,) * len(shape),
                                       pipeline_mode=pl.Buffered(1))
    in_specs = [pl.BlockSpec((tz, w), lambda s: (0, u_col)),
                pl.BlockSpec((t, w), lat_u),
                table((nq, kq, LANES)),
                table((nq, kq, LANES)),
                table((nq, ps2, LANES)),
                pl.BlockSpec(e_w.shape, const),
                pl.BlockSpec(e_k.shape, const),
                pl.BlockSpec(e_c.shape, const),
                table((8, SUBLANES, p2 // 2)),
                pl.BlockSpec((1, w), const)]
    args = [rest_ctx, rest, wc, kc, cc, *spreads, tabs, dskip.reshape(1, w)]
    if final:
        in_specs += [pl.BlockSpec((tz, w), const), pl.BlockSpec((t, w), lat),
                     pl.BlockSpec((w, w), const), pl.BlockSpec((1, w), const)]
        args += [fwd[0], fwd[1], glu[0], glu[1].reshape(1, w)]
    odt = BF16 if final else F32
    return pl.pallas_call(
        functools.partial(_s5_kernel, reverse=reverse, final=final),
        grid=(nl + 1,),
        in_specs=in_specs,
        out_specs=[pl.BlockSpec((tz, w), const), pl.BlockSpec((t, w), lat)],
        out_shape=[jax.ShapeDtypeStruct((tz, w), odt), jax.ShapeDtypeStruct((seq, w), odt)],
        scratch_shapes=[pltpu.VMEM((2, SUBLANES, p2 // 2), F32),
                        pltpu.VMEM((nq, max(t, tz), LANES), F32),
                        pltpu.VMEM((nq, max(t, tz), LANES), F32),
                        pltpu.VMEM((nq, kq, ps2), BF16),
                        pltpu.VMEM((nq, kq, LANES), BF16),
                        pltpu.VMEM((nq, ps2, SUBLANES * LANES), BF16)],
        compiler_params=_cparams("arbitrary"),
        name="s5_direction",
    )(*args)


def _s5_params(a_re, a_im, log_step, b_re, b_im, c_re, c_im, rev):
    ar, ai = a_re.astype(F32), a_im.astype(F32)
    step = jnp.exp(log_step.astype(F32))[:, None]
    mag = jnp.exp(ar * step)
    ab_re, ab_im = mag * jnp.cos(ai * step), mag * jnp.sin(ai * step)
    den = ar * ar + ai * ai
    nr = ab_re - 1.0
    f_re = (nr * ar + ab_im * ai) / den
    f_im = (ab_im * ar - nr * ai) / den
    br, bi = b_re.astype(F32), b_im.astype(F32)
    bb_re = f_re[..., None] * br - f_im[..., None] * bi
    bb_im = f_re[..., None] * bi + f_im[..., None] * br
    cr, ci = c_re.astype(F32), c_im.astype(F32)
    g, p, h = bb_re.shape
    gq = LANES // h
    nq = g // gq
    nl = SUBLANES

    def cmul(xr, xi, yr, yi):
        return xr * yr - xi * yi, xr * yi + xi * yr

    def powers(xr, xi, n):
        out = [(jnp.ones_like(xr), jnp.zeros_like(xi)), (xr, xi)]
        for _ in range(n - 1):
            out.append(cmul(*out[-1], xr, xi))
        return jnp.stack([o[0] for o in out]), jnp.stack([o[1] for o in out])

    pr, pi = powers(ab_re, ab_im, nl)
    w_re = pr[:nl, ..., None] * bb_re - pi[:nl, ..., None] * bb_im
    w_im = pr[:nl, ..., None] * bb_im + pi[:nl, ..., None] * bb_re

    def rows_lag_group_channel(x):
        x = x.reshape(nl, nq, gq, h, x.shape[-1])
        return jnp.transpose(x, (1, 0, 2, 3, 4)).reshape(nq, nl * gq * h, x.shape[-1])

    wc = jnp.concatenate([rows_lag_group_channel(jnp.swapaxes(w, 2, 3)) for w in (w_re, w_im)], axis=2)
    hp = lax.Precision.HIGHEST
    kk = (jnp.einsum('kgph,gop->kgho', w_re, cr, precision=hp)
          - jnp.einsum('kgph,gop->kgho', w_im, ci, precision=hp))
    kc = jnp.pad(rows_lag_group_channel(kk), ((0, 0), (0, 0), (0, LANES - h)))
    dr = jnp.where(rev, pr[nl:0:-1], pr[1:nl + 1])
    di = jnp.where(rev, pi[nl:0:-1], pi[1:nl + 1])
    cd_re = cr[None] * dr[:, :, None, :] - ci[None] * di[:, :, None, :]
    cd_im = cr[None] * di[:, :, None, :] + ci[None] * dr[:, :, None, :]

    def rows_group_state(cd):
        return jnp.transpose(cd.reshape(nl, nq, gq, h, p), (1, 2, 4, 0, 3)).reshape(nq, gq * p, nl * h)

    cc = jnp.concatenate([rows_group_state(cd_re), -rows_group_state(cd_im)], axis=1)
    qr, qi = powers(pr[nl].reshape(-1), pi[nl].reshape(-1), SUBLANES)
    row = np.arange(SUBLANES)[:, None]
    tabs = []
    for k in (1, 2, 4):
        keep = jnp.where(rev, jnp.asarray(row < SUBLANES - k), jnp.asarray(row >= k))
        tabs += [jnp.where(keep, qr[k][None], 0.0), jnp.where(keep, qi[k][None], 0.0)]
    tabs += [jnp.where(rev, qr[SUBLANES:0:-1], qr[1:SUBLANES + 1]),
             jnp.where(rev, qi[SUBLANES:0:-1], qi[1:SUBLANES + 1])]
    return wc.astype(BF16), kc.astype(BF16), cc.astype(BF16), jnp.stack(tabs)


def _conv_kernel(val_ref, gate_ref, vp_ref, gp_ref, vn_ref, gn_ref, w_ref, b_ref, g_ref, beta_ref, o_ref, hs_ref,
                 *, sub):
    i = pl.program_id(0)
    n = pl.num_programs(0)
    t = val_ref.shape[0]

    def glu(v, g):
        return v[...].astype(F32) * _sigmoid(g[...].astype(F32))

    hs_ref[0:CONV_HALO] = jnp.where(i > 0, glu(vp_ref, gp_ref), 0.0)
    hs_ref[CONV_HALO:CONV_HALO + t] = glu(val_ref, gate_ref)
    hs_ref[CONV_HALO + t:2 * CONV_HALO + t] = jnp.where(i < n - 1, glu(vn_ref, gn_ref), 0.0)
    base = CONV_HALO - CONV_K // 2
    for r0 in range(t // sub):
        acc = jnp.zeros((sub, val_ref.shape[1]), F32) + b_ref[...]
        for r in range(SUBLANES):
            part = None
            for m in range((base + CONV_K - 1) // SUBLANES + 1):
                k = SUBLANES * m + r - base
                if 0 <= k < CONV_K:
                    lo_row = r0 * sub + SUBLANES * m
                    term = hs_ref[lo_row:lo_row + sub + SUBLANES] * w_ref[k:k + 1]
                    part = term if part is None else part + term
            acc = acc + part[r:r + sub]
        mu = jnp.mean(acc, axis=-1, keepdims=True)
        cen = acc - mu
        var = jnp.mean(cen * cen, axis=-1, keepdims=True)
        yn = cen * lax.rsqrt(var + EPS) * g_ref[...] + beta_ref[...]
        o_ref[r0 * sub:(r0 + 1) * sub] = _silu(yn).astype(o_ref.dtype)


def conv_module(rest, val_col, gate_col, w, b, g, beta, t=256, sub=64):
    seq = rest.shape[0]
    t = min(t, seq)
    cw = 4 * LANES
    hb = t // CONV_HALO
    nh = seq // CONV_HALO
    prev = lambda i: jnp.maximum(i * hb - 1, 0)
    nxt = lambda i: jnp.minimum((i + 1) * hb, nh - 1)
    vec = pl.BlockSpec((1, cw), lambda i: (0, 0))
    return pl.pallas_call(
        functools.partial(_conv_kernel, sub=sub),
        grid=(seq // t,),
        in_specs=[pl.BlockSpec((t, cw), lambda i: (i, val_col)),
                  pl.BlockSpec((t, cw), lambda i: (i, gate_col)),
                  pl.BlockSpec((CONV_HALO, cw), lambda i: (prev(i), val_col)),
                  pl.BlockSpec((CONV_HALO, cw), lambda i: (prev(i), gate_col)),
                  pl.BlockSpec((CONV_HALO, cw), lambda i: (nxt(i), val_col)),
                  pl.BlockSpec((CONV_HALO, cw), lambda i: (nxt(i), gate_col)),
                  pl.BlockSpec((CONV_K, cw), lambda i: (0, 0)), vec, vec, vec],
        out_specs=pl.BlockSpec((t, cw), lambda i: (i, 0)),
        out_shape=jax.ShapeDtypeStruct((seq, cw), BF16),
        scratch_shapes=[pltpu.VMEM((t + 2 * CONV_HALO, cw), F32)],
        compiler_params=_cparams("arbitrary"),
        name="conv_module",
    )(rest, rest, rest, rest, rest, rest, w.astype(F32), b.reshape(1, cw).astype(F32),
      g.reshape(1, cw).astype(F32), beta.reshape(1, cw).astype(F32))


def _merge_kernel(*refs, n_parts, cast_mlp):
    y_refs = refs[:N_BRANCH]
    gl_refs = refs[N_BRANCH:N_BRANCH * (1 + n_parts)]
    rest = refs[N_BRANCH * (1 + n_parts):]
    if cast_mlp:
        wb_ref, wo_ref, x_ref, gate_ref, g_ref, sc_ref, sh_ref, w1_ref, w2_ref, o_ref, h_ref, w1b_ref, w2b_ref = rest
        w1b_ref[...] = w1_ref[...].astype(BF16)
        w2b_ref[...] = w2_ref[...].astype(BF16)
    else:
        wb_ref, wo_ref, x_ref, gate_ref, g_ref, sc_ref, sh_ref, o_ref, h_ref = rest
    pw = gl_refs[0].shape[1]
    parts = []
    for part in range(n_parts):
        cols = slice(part * pw, (part + 1) * pw)
        m = None
        for k in range(N_BRANCH):
            proj = jnp.dot(y_refs[k][...], wb_ref[k, :, cols], preferred_element_type=F32)
            term = _sigmoid(gl_refs[k * n_parts + part][...].astype(F32)) * proj
            m = term if m is None else m + term
        parts.append(m.astype(BF16))
    merged = jnp.concatenate(parts, axis=1)
    x = x_ref[...] + gate_ref[...] * jnp.dot(merged, wo_ref[...], preferred_element_type=F32)
    o_ref[...] = x
    y = x * lax.rsqrt(jnp.mean(x * x, axis=-1, keepdims=True) + EPS) * g_ref[...]
    h_ref[...] = (y * (1.0 + sc_ref[...]) + sh_ref[...]).astype(h_ref.dtype)


def gated_merge_residual(ys, rest, logit_col, part_w, wb, wo, layer, x, gate, g2, scale2, shift2, mlp_w=None, tm=256):
    m, yw = ys[0].shape
    n = wb.shape[3]
    n_parts = n // part_w
    tm = min(tm, m)
    steps = m // tm
    resident = lambda shape: pl.BlockSpec((None,) + shape, lambda i: (layer,) + (0,) * len(shape),
                                          pipeline_mode=pl.Buffered(1))
    yspec = pl.BlockSpec((tm, yw), lambda i: (i, 0))
    gspecs = [pl.BlockSpec((tm, part_w), functools.partial(lambda i, c: (i, c), c=logit_col + k * n_parts + part))
              for k in range(N_BRANCH) for part in range(n_parts)]
    vec = pl.BlockSpec((1, n), lambda i: (0, 0))
    rows = pl.BlockSpec((tm, n), lambda i: (i, 0))
    in_specs = [yspec] * N_BRANCH + gspecs + [resident((N_BRANCH, yw, n)), resident((n, n)), rows,
                                              vec, vec, vec, vec]
    args = [*ys, *([rest] * (N_BRANCH * n_parts)), wb, wo, x, gate.reshape(1, n), g2.reshape(1, n),
            scale2.reshape(1, n), shift2.reshape(1, n)]
    out_specs = [rows, rows]
    out_shape = [jax.ShapeDtypeStruct((m, n), F32), jax.ShapeDtypeStruct((m, n), BF16)]
    if mlp_w is not None:
        w1, w2 = mlp_w
        ff = w1.shape[2]
        slab = ff // steps
        assert ff % steps == 0 and slab % LANES == 0
        in_specs += [pl.BlockSpec((None, n, slab), lambda i: (layer, 0, i)),
                     pl.BlockSpec((None, slab, n), lambda i: (layer, i, 0))]
        args += [w1, w2]
        out_specs += [pl.BlockSpec((n, slab), lambda i: (0, i)), pl.BlockSpec((slab, n), lambda i: (i, 0))]
        out_shape += [jax.ShapeDtypeStruct((n, ff), BF16), jax.ShapeDtypeStruct((ff, n), BF16)]
    return pl.pallas_call(
        functools.partial(_merge_kernel, n_parts=n_parts, cast_mlp=mlp_w is not None),
        grid=(steps,),
        in_specs=in_specs,
        out_specs=out_specs,
        out_shape=out_shape,
        compiler_params=_cparams("arbitrary"),
        name="gated_merge_residual",
    )(*args)


def _mlp_kernel(x_ref, h_ref, gate_ref, w1_ref, w2_ref, *rest, final_norm):
    if final_norm:
        gf_ref, o_ref, acc_ref = rest
    else:
        o_ref, acc_ref = rest
    f = pl.program_id(1)

    @pl.when(f == 0)
    def _():
        acc_ref[...] = jnp.zeros_like(acc_ref)

    t = jnp.dot(h_ref[...], w1_ref[...], preferred_element_type=F32)
    t = jnp.square(jnp.maximum(t, 0.0)).astype(BF16)
    acc_ref[...] += jnp.dot(t, w2_ref[...], preferred_element_type=F32)

    @pl.when(f == pl.num_programs(1) - 1)
    def _():
        y = x_ref[...] + gate_ref[...] * acc_ref[...]
        if final_norm:
            y = y * lax.rsqrt(jnp.mean(y * y, axis=-1, keepdims=True) + EPS) * gf_ref[...]
        o_ref[...] = y


def mlp_block(x, h, gate, w1, w2, final_g=None, tm=512, tf=1024):
    m, d = x.shape
    ff = w1.shape[1]
    tm = min(tm, m)
    vec = pl.BlockSpec((1, d), lambda i, f: (0, 0))
    rows = pl.BlockSpec((tm, d), lambda i, f: (i, 0))
    final_norm = final_g is not None
    args = [x, h, gate.reshape(1, d), w1, w2]
    in_specs = [rows, rows, vec,
                pl.BlockSpec((d, tf), lambda i, f: (0, f)),
                pl.BlockSpec((tf, d), lambda i, f: (f, 0))]
    if final_norm:
        args.append(final_g.reshape(1, d))
        in_specs.append(vec)
    return pl.pallas_call(
        functools.partial(_mlp_kernel, final_norm=final_norm),
        grid=(m // tm, ff // tf),
        in_specs=in_specs,
        out_specs=pl.BlockSpec((tm, d), lambda i, f: (i, 0)),
        out_shape=jax.ShapeDtypeStruct((m, d), F32),
        scratch_shapes=[pltpu.VMEM((tm, d), F32)],
        compiler_params=_cparams("arbitrary", "arbitrary"),
        name="mlp_block",
    )(*args)


_A_COL_K, _A_COL_V = 2, 3
_REST_COL0 = (A_HEADS + 2 * A_KV_HEADS) * HEAD_DIM
_COL_U, _COL_QC, _COL_KC, _COL_VC, _COL_VAL, _COL_GATE = 0, 1, 2, 3, 4, 5
_MERGE_PART = 1024
_COL_MERGE = 3


def _rope_tables(seq):
    freqs = ROPE_BASE ** (-jnp.arange(ROPE_PAIRS, dtype=F32) / ROPE_PAIRS)
    rows = seq // GRID_W
    ang_r = jnp.arange(rows).astype(F32)[:, None] * freqs
    ang_c = jnp.arange(GRID_W).astype(F32)[:, None] * freqs

    def expand(tab_r, tab_c, first, second):
        r = jnp.repeat(tab_r, GRID_W, axis=0)
        cc = jnp.tile(tab_c, (rows, 1))
        head = jnp.concatenate([first * r, second * r, first * cc, second * cc], axis=1)
        return jnp.tile(head, (1, LANES // HEAD_DIM))

    cos = expand(jnp.cos(ang_r), jnp.cos(ang_c), 1.0, 1.0)
    sa = expand(jnp.sin(ang_r), jnp.sin(ang_c), 0.0, 1.0)
    sb = expand(jnp.sin(ang_r), jnp.sin(ang_c), -1.0, 0.0)
    return cos, sa, sb


def kernel(x, c, ctx, c_ctx, ada_w, ada_b, norm1_g, norm2_g, w_in, a_sink, s5_a_re, s5_a_im, s5_log_step, s5_b_re, s5_b_im, s5_c_re, s5_c_im, s5_d, s5_glu_w, s5_glu_b, c_rpb, d_conv_w, d_conv_b, d_ln_g, d_ln_b, w_branch, w_out, mlp_w1, mlp_w2, final_g):
    depth = ada_w.shape[0]
    _, seq, d = x.shape
    lc = ctx.shape[1]
    xs = x.reshape(seq, d)
    zs = ctx.reshape(lc, d)
    rope = _rope_tables(seq)

    cc = jnp.zeros((SUBLANES, d), F32).at[0].set(c[0]).at[1].set(c_ctx)
    mods = ada_modulation(cc, ada_w, ada_b)
    n_rest = w_in.shape[2] - _REST_COL0
    wb, wo = w_branch.astype(BF16), w_out.astype(BF16)
    nbr_bias = jax.vmap(lambda rpb: _nbr_bias(rpb, seq // GRID_W))(c_rpb)
    s5_params = jax.vmap(_s5_params)(*(a.reshape((2 * depth,) + a.shape[2:]) for a in
                                       (s5_a_re, s5_a_im, s5_log_step, s5_b_re, s5_b_im, s5_c_re, s5_c_im)),
                                     jnp.arange(2 * depth) % 2 == 1)

    for l in range(depth):
        need_ctx = l < depth - 1
        mx = [mods[l, 0, k * d:(k + 1) * d] for k in range(6)]
        mz = [mods[l, 1, k * d:(k + 1) * d] for k in range(6)]

        hx, qk_x = norm_head_projection(xs, norm1_g[l], mx[1], mx[0], w_in, l, rope=rope)
        hz, qk_z = norm_head_projection(zs, norm1_g[l], mz[1], mz[0], w_in, l)
        r_x = wide_projection(hx, w_in, l, _REST_COL0, n_rest)
        r_z = wide_projection(hz, w_in, l, _REST_COL0, n_rest if need_ctx else (_COL_VC + 1) * 4 * LANES)

        ya_x = window_attention(a_sink[l], qk_x, qk_z)
        yc_x = neighborhood_attention(r_x, r_z, nbr_bias, l, _COL_QC, _COL_KC)
        yd_x = conv_module(r_x, _COL_VAL, _COL_GATE, d_conv_w[l], d_conv_b[l], d_ln_g[l], d_ln_b[l])

        yf = s5_direction(r_x, r_z, _COL_U, s5_params, 2 * l, s5_d[l].astype(F32), reverse=False)
        yb_z, yb_x = s5_direction(r_x, r_z, _COL_U, s5_params, 2 * l + 1, s5_d[l].astype(F32), reverse=True, fwd=yf,
                                  glu=(s5_glu_w[l].astype(BF16), s5_glu_b[l].astype(F32)))

        xs, h2, w1, w2 = gated_merge_residual((ya_x, yb_x, yc_x, yd_x), r_x, _COL_MERGE, _MERGE_PART, wb, wo, l, xs,
                                              mx[2], norm2_g[l], mx[4], mx[3], mlp_w=(mlp_w1, mlp_w2))
        xs = mlp_block(xs, h2, mx[5], w1, w2, final_g=None if need_ctx else final_g)

        if need_ctx:
            ya_z = context_attention(a_sink[l], qk_z, 0, qk_z, _A_COL_K, 2 * LANES, qk_z, _A_COL_V, True, True)
            yc_z = context_attention(a_sink[l], r_z, _COL_QC, r_z, _COL_KC, 4 * LANES, r_z, _COL_VC, False, False)
            yd_z = conv_module(r_z, _COL_VAL, _COL_GATE, d_conv_w[l], d_conv_b[l], d_ln_g[l], d_ln_b[l])
            zs, h2 = gated_merge_residual((ya_z, yb_z, yc_z, yd_z), r_z, _COL_MERGE, _MERGE_PART, wb, wo, l, zs,
                                          mz[2], norm2_g[l], mz[4], mz[3])
            zs = mlp_block(zs, h2, mz[5], w1, w2)

    return xs.reshape(x.shape)
```

```python
import functools
import math

import numpy as np
import jax
import jax.numpy as jnp
from jax import lax
from jax.experimental import pallas as pl
from jax.experimental.pallas import tpu as pltpu

F32 = jnp.float32
BF16 = jnp.bfloat16

LANES = 128
SUBLANES = 8
VMEM_LIMIT = 56 * 1024 * 1024

GRID_W = 64
HEAD_DIM = 64
N_BRANCH = 4
A_HEADS = 8
A_KV_HEADS = 2
A_WINDOW = 128
A_BLOCK = 128
ROPE_BASE = 10000.0
ROPE_PAIRS = HEAD_DIM // 4
S5_GROUP_CH = 16
S5_STATE = 64
C_HEADS = 8
C_WIN_R = 8
C_WIN_C = 16
C_QROWS = 2
C_KROWS = C_QROWS + C_WIN_R
CONV_K = 31
CONV_HALO = 16
EPS = 1e-6
NEG_INF = -1e30


def _cparams(*sem):
    return pltpu.CompilerParams(dimension_semantics=sem, vmem_limit_bytes=VMEM_LIMIT)


def _sigmoid(x):
    return 0.5 * jnp.tanh(0.5 * x) + 0.5


def _silu(x):
    return x * _sigmoid(x)


def _gelu_tanh(x):
    return 0.5 * x * (1.0 + jnp.tanh(math.sqrt(2.0 / math.pi) * (x + 0.044715 * (x * x * x))))


def _ada_kernel(c_ref, w_ref, b_ref, o_ref):
    a = _silu(c_ref[...]).astype(BF16)
    o_ref[...] = jnp.dot(a, w_ref[...].astype(BF16), preferred_element_type=F32) + b_ref[...]


def ada_modulation(cc, ada_w, ada_b, tn=2048):
    depth, d, n = ada_w.shape
    return pl.pallas_call(
        _ada_kernel,
        grid=(depth, n // tn),
        in_specs=[pl.BlockSpec((SUBLANES, d), lambda l, j: (0, 0)),
                  pl.BlockSpec((None, d, tn), lambda l, j: (l, 0, j)),
                  pl.BlockSpec((None, 1, tn), lambda l, j: (l, 0, j))],
        out_specs=pl.BlockSpec((None, SUBLANES, tn), lambda l, j: (l, 0, j)),
        out_shape=jax.ShapeDtypeStruct((depth, SUBLANES, n), F32),
        compiler_params=_cparams("arbitrary", "arbitrary"),
        name="ada_modulation",
    )(cc, ada_w, ada_b.reshape(depth, 1, n))


def _rope_block(y, cos, sa, sb):
    return y * cos + pltpu.roll(y, 16, 1) * sa + pltpu.roll(y, LANES - 16, 1) * sb


def _head_proj_kernel(x_ref, g_ref, sc_ref, sh_ref, w_ref, *rest, n_q, with_rope):
    if with_rope:
        cos_ref, sa_ref, sb_ref, h_ref, o_ref, wb_ref = rest
        cos, sa, sb = cos_ref[...], sa_ref[...], sb_ref[...]
    else:
        h_ref, o_ref, wb_ref = rest

    @pl.when(pl.program_id(0) == 0)
    def _():
        wb_ref[...] = w_ref[0].astype(BF16)

    x = x_ref[...]
    y = x * lax.rsqrt(jnp.mean(x * x, axis=-1, keepdims=True) + EPS) * g_ref[...]
    h = (y * (1.0 + sc_ref[...]) + sh_ref[...]).astype(BF16)
    h_ref[...] = h
    acc = jnp.dot(h, wb_ref[...], preferred_element_type=F32)
    lo = _lane_lo()
    out = 0
    for b in range(acc.shape[1] // LANES):
        y = acc[:, b * LANES:(b + 1) * LANES]
        if with_rope and b <= n_q:
            y = _rope_block(y, cos, sa, sb)
        if b < n_q:
            blocks = [y]
        else:
            swapped = pltpu.roll(y, HEAD_DIM, 1)
            blocks = [jnp.where(lo, y, swapped), jnp.where(lo, swapped, y)]
        for blk in blocks:
            o_ref[:, out * LANES:(out + 1) * LANES] = blk.astype(o_ref.dtype)
            out += 1


def norm_head_projection(x, g, scale, shift, w_in, layer, rope=None, tm=1024):
    m, k = x.shape
    n_in = (A_HEADS + 2 * A_KV_HEADS) * HEAD_DIM
    n_q = A_HEADS * HEAD_DIM // LANES
    assert A_KV_HEADS * HEAD_DIM == LANES
    n_out = n_in + 2 * LANES
    tm = min(tm, m)
    vec = pl.BlockSpec((1, k), lambda i: (0, 0))
    in_specs = [pl.BlockSpec((tm, k), lambda i: (i, 0)), vec, vec, vec,
                pl.BlockSpec((pl.Element(1), pl.Element(k), pl.Element(n_in)), lambda i: (layer, 0, 0))]
    args = [x, g.reshape(1, k), scale.reshape(1, k), shift.reshape(1, k), w_in]
    if rope is not None:
        in_specs += [pl.BlockSpec((tm, LANES), lambda i: (i, 0))] * 3
        args += list(rope)
    return pl.pallas_call(
        functools.partial(_head_proj_kernel, n_q=n_q, with_rope=rope is not None),
        grid=(m // tm,),
        in_specs=in_specs,
        out_specs=[pl.BlockSpec((tm, k), lambda i: (i, 0)), pl.BlockSpec((tm, n_out), lambda i: (i, 0))],
        out_shape=[jax.ShapeDtypeStruct((m, k), BF16), jax.ShapeDtypeStruct((m, n_out), BF16)],
        scratch_shapes=[pltpu.VMEM((k, n_in), BF16)],
        compiler_params=_cparams("arbitrary"),
        name="norm_head_projection",
    )(*args)


def _wide_proj_kernel(a_ref, w_ref, o_ref, wb_ref):
    @pl.when(pl.program_id(1) == 0)
    def _():
        wb_ref[...] = w_ref[0].astype(BF16)

    o_ref[...] = jnp.dot(a_ref[...], wb_ref[...], preferred_element_type=F32).astype(o_ref.dtype)


def wide_projection(a, w, layer, col0, n, tm=2048, tn=1024):
    m, k = a.shape
    tm = min(tm, m)
    assert n % tn == 0 and col0 % LANES == 0 and tn % LANES == 0
    return pl.pallas_call(
        _wide_proj_kernel,
        grid=(n // tn, m // tm),
        in_specs=[pl.BlockSpec((tm, k), lambda j, i: (i, 0)),
                  pl.BlockSpec((pl.Element(1), pl.Element(k), pl.Element(tn)),
                               lambda j, i: (layer, 0, pl.multiple_of(col0 + j * tn, LANES)))],
        out_specs=pl.BlockSpec((tm, tn), lambda j, i: (i, j)),
        out_shape=jax.ShapeDtypeStruct((m, n), BF16),
        scratch_shapes=[pltpu.VMEM((k, tn), BF16)],
        compiler_params=_cparams("arbitrary", "arbitrary"),
        name="wide_projection",
    )(a, w)


def _lane_lo():
    return lax.broadcasted_iota(jnp.int32, (1, LANES), 1) < HEAD_DIM


def _half_select(x, lo, half):
    zero = jnp.zeros_like(x)
    return jnp.where(lo, x, zero) if half == 0 else jnp.where(lo, zero, x)


def _dot_nt(a, b):
    return lax.dot_general(a, b, (((1,), (1,)), ((), ())), preferred_element_type=F32)


def _softmax_pv(parts, values, sink=None):
    m = parts[0].max(axis=-1, keepdims=True)
    for s in parts[1:]:
        m = jnp.maximum(m, s.max(axis=-1, keepdims=True))
    if sink is not None:
        m = jnp.maximum(m, sink)
    den = jnp.exp(sink - m) if sink is not None else 0.0
    out = 0.0
    for s, v in zip(parts, values):
        p = jnp.exp(s - m)
        den = den + p.sum(axis=-1, keepdims=True)
        out = out + jnp.dot(p.astype(BF16), v, preferred_element_type=F32)
    return out / den


def _attn_a_window_start(i, seq):
    return jnp.clip((i - 1) * A_BLOCK, 0, seq - 3 * A_BLOCK)


def _attn_a_kernel(sink_ref, q_ref, kv0_ref, kv1_ref, kv2_ref, kvz_ref, o_ref, *, seq):
    i = pl.program_id(0)
    lo = _lane_lo()
    nq = 4 * A_BLOCK
    r = lax.broadcasted_iota(jnp.int32, (nq, 3 * A_BLOCK), 0) & (A_BLOCK - 1)
    c = lax.broadcasted_iota(jnp.int32, (nq, 3 * A_BLOCK), 1)
    dist = c - r + (_attn_a_window_start(i, seq) - i * A_BLOCK)
    valid = (dist >= -A_WINDOW) & (dist <= A_WINDOW)
    rowblk = lax.broadcasted_iota(jnp.int32, (nq, 1), 0) // A_BLOCK
    vcol = A_KV_HEADS * LANES
    for g in range(A_KV_HEADS):
        gs = slice(g * LANES, (g + 1) * LANES)
        vs = slice(vcol + g * LANES, vcol + (g + 1) * LANES)
        kw = jnp.concatenate([ref[:, gs] for ref in (kv0_ref, kv1_ref, kv2_ref)], axis=0)
        vw = jnp.concatenate([ref[:, vs] for ref in (kv0_ref, kv1_ref, kv2_ref)], axis=0)
        kz, vz = kvz_ref[:, gs], kvz_ref[:, vs]
        blocks = [slice((2 * g + bb) * LANES, (2 * g + bb + 1) * LANES) for bb in range(2)]
        ql = jnp.concatenate([_half_select(q_ref[:, bs], lo, half) * (HEAD_DIM ** -0.5)
                              for half in range(2) for bs in blocks], axis=0)
        heads = [2 * (2 * g + bb) + half for half in range(2) for bb in range(2)]
        sink = jnp.full((nq, 1), sink_ref[heads[3]], F32)
        for k in range(3):
            sink = jnp.where(rowblk == k, sink_ref[heads[k]], sink)
        s = jnp.where(valid, _dot_nt(ql, kw), NEG_INF)
        sz = _dot_nt(ql, kz)
        m = jnp.maximum(jnp.maximum(s.max(axis=-1, keepdims=True), sz.max(axis=-1, keepdims=True)), sink)
        p, pz = jnp.exp(s - m), jnp.exp(sz - m)
        inv = 1.0 / (p.sum(axis=-1, keepdims=True) + pz.sum(axis=-1, keepdims=True) + jnp.exp(sink - m))
        p, pz = p.astype(BF16), pz.astype(BF16)
        out = None
        for half in range(2):
            hs = slice(half * 2 * A_BLOCK, (half + 1) * 2 * A_BLOCK)
            o = (jnp.dot(p[hs], _half_select(vw, lo, half), preferred_element_type=F32)
                 + jnp.dot(pz[hs], _half_select(vz, lo, half), preferred_element_type=F32)) * inv[hs]
            out = o if out is None else out + o
        for bb in range(2):
            o_ref[:, blocks[bb]] = out[bb * A_BLOCK:(bb + 1) * A_BLOCK].astype(o_ref.dtype)


def window_attention(sink, qk, qk_ctx):
    seq = qk.shape[0]
    lc = qk_ctx.shape[0]
    nb = seq // A_BLOCK
    qw = A_HEADS * HEAD_DIM
    kvw = 4 * A_KV_HEADS * HEAD_DIM
    assert nb >= 3 and qw == kvw
    kv_specs = [pl.BlockSpec((A_BLOCK, kvw), functools.partial(
        lambda i, t: (_attn_a_window_start(i, seq) // A_BLOCK + t, 1), t=t)) for t in range(3)]
    return pl.pallas_call(
        functools.partial(_attn_a_kernel, seq=seq),
        grid=(nb,),
        in_specs=[pl.BlockSpec(memory_space=pltpu.SMEM),
                  pl.BlockSpec((A_BLOCK, qw), lambda i: (i, 0))] + kv_specs + [
                  pl.BlockSpec((lc, kvw), lambda i: (0, 1))],
        out_specs=pl.BlockSpec((A_BLOCK, qw), lambda i: (i, 0)),
        out_shape=jax.ShapeDtypeStruct((seq, qw), BF16),
        compiler_params=_cparams("arbitrary"),
        name="window_attention",
    )(sink, qk, qk, qk, qk, qk_ctx)


def _ctx_attn_kernel(sink_ref, q_ref, k_ref, v_ref, o_ref, *, kv_shared, has_sink):
    lo = _lane_lo()
    for b in range(q_ref.shape[1] // LANES):
        bs = slice(b * LANES, (b + 1) * LANES)
        kb = b // 2 if kv_shared else b
        ks = slice(kb * LANES, (kb + 1) * LANES)
        qb, k, v = q_ref[:, bs], k_ref[:, ks], v_ref[:, ks]
        acc = jnp.zeros((q_ref.shape[0], LANES), F32)
        for half in range(2):
            ql = _half_select(qb, lo, half) * (HEAD_DIM ** -0.5)
            sink = sink_ref[2 * b + half] if has_sink else None
            acc = acc + _softmax_pv([_dot_nt(ql, k)], [_half_select(v, lo, half)], sink=sink)
        o_ref[:, bs] = acc.astype(o_ref.dtype)


def context_attention(sink, q_arr, q_col, k_arr, k_col, k_w, v_arr, v_col, kv_shared, has_sink):
    lc = q_arr.shape[0]
    return pl.pallas_call(
        functools.partial(_ctx_attn_kernel, kv_shared=kv_shared, has_sink=has_sink),
        grid=(1,),
        in_specs=[pl.BlockSpec(memory_space=pltpu.SMEM),
                  pl.BlockSpec((lc, 4 * LANES), lambda i: (0, q_col)),
                  pl.BlockSpec((lc, k_w), lambda i: (0, k_col)),
                  pl.BlockSpec((lc, k_w), lambda i: (0, v_col))],
        out_specs=pl.BlockSpec((lc, 4 * LANES), lambda i: (0, 0)),
        out_shape=jax.ShapeDtypeStruct((lc, 4 * LANES), BF16),
        compiler_params=_cparams("arbitrary"),
        name="context_attention",
    )(sink, q_arr, k_arr, v_arr)


def _nbr_row_offsets(rows):
    nsteps = rows // C_QROWS
    nblk = C_KROWS * GRID_W // LANES
    wr = min(C_WIN_R, rows)
    reps = [0, 1, 2, nsteps - 2, nsteps - 1]
    dr = -np.ones((len(reps), C_QROWS, C_KROWS), np.int32)
    for v, i in enumerate(reps):
        w0 = C_QROWS * int(np.clip(i - 2, 0, nsteps - nblk))
        for a in range(C_QROWS):
            rq = C_QROWS * i + a
            k0 = int(np.clip(rq - C_WIN_R // 2, 0, rows - wr))
            for j in range(C_KROWS):
                if k0 <= w0 + j < k0 + wr:
                    dr[v, a, j] = w0 + j - rq + C_WIN_R - 1
    return dr


def _nbr_bias(rpb, rows):
    h = rpb.shape[0]
    qc = np.arange(GRID_W)[:, None]
    kc = np.arange(GRID_W)[None, :]
    cstart = np.clip(qc - C_WIN_C // 2, 0, GRID_W - C_WIN_C)
    col_ok = jnp.asarray((kc >= cstart) & (kc < cstart + C_WIN_C))
    pad = GRID_W - C_WIN_C
    ext = jnp.pad(rpb.astype(F32), ((0, 0), (0, 0), (pad, pad)), mode='edge')
    toep = jnp.stack([ext[:, :, GRID_W - 1 - q:2 * GRID_W - 1 - q] for q in range(GRID_W)], axis=2)
    toep = jnp.where(col_ok, toep, NEG_INF)
    masked = jnp.full((h, GRID_W, GRID_W), NEG_INF, F32)
    dr = _nbr_row_offsets(rows)
    variants = []
    for v in range(dr.shape[0]):
        qrows = [jnp.concatenate([toep[:, dr[v, a, j]] if dr[v, a, j] >= 0 else masked for j in range(C_KROWS)],
                                 axis=-1) for a in range(C_QROWS)]
        variants.append(jnp.concatenate(qrows, axis=-2).reshape(h * C_QROWS * GRID_W, C_KROWS * GRID_W))
    return jnp.stack(variants)


def _attn_c_kernel(q_ref, *refs):
    nblk = (len(refs) - 3) // 2
    k_refs, v_refs = refs[:nblk], refs[nblk:2 * nblk]
    kvz_ref, bias_ref, o_ref = refs[-3:]
    lo = _lane_lo()
    nq = q_ref.shape[0]
    nb = C_HEADS // 2
    vcol = C_HEADS * HEAD_DIM
    blocks = [slice(b * LANES, (b + 1) * LANES) for b in range(nb)]
    s_parts, sz_parts = [], []
    for bs in blocks:
        ql = jnp.concatenate([_half_select(q_ref[:, bs], lo, half) * (HEAD_DIM ** -0.5) for half in range(2)],
                             axis=0)
        s_parts.append(_dot_nt(ql, jnp.concatenate([ref[:, bs] for ref in k_refs], axis=0)))
        sz_parts.append(_dot_nt(ql, kvz_ref[:, bs]))
    s = jnp.concatenate(s_parts, axis=0) + bias_ref[...]
    sz = jnp.concatenate(sz_parts, axis=0)
    m = jnp.maximum(s.max(axis=-1, keepdims=True), sz.max(axis=-1, keepdims=True))
    p, pz = jnp.exp(s - m), jnp.exp(sz - m)
    inv = 1.0 / (p.sum(axis=-1, keepdims=True) + pz.sum(axis=-1, keepdims=True))
    p, pz = p.astype(BF16), pz.astype(BF16)
    for b, bs in enumerate(blocks):
        vs = slice(vcol + b * LANES, vcol + (b + 1) * LANES)
        vw, vz = jnp.concatenate([ref[:, bs] for ref in v_refs], axis=0), kvz_ref[:, vs]
        out = None
        for half in range(2):
            hs = slice((2 * b + half) * nq, (2 * b + half + 1) * nq)
            o = (jnp.dot(p[hs], _half_select(vw, lo, half), preferred_element_type=F32)
                 + jnp.dot(pz[hs], _half_select(vz, lo, half), preferred_element_type=F32)) * inv[hs]
            out = o if out is None else out + o
        o_ref[:, bs] = out.astype(o_ref.dtype)


def neighborhood_attention(rest, rest_ctx, bias, layer, q_col, k_col):
    seq = rest.shape[0]
    lc = rest_ctx.shape[0]
    rows = seq // GRID_W
    nsteps = rows // C_QROWS
    nblk = C_KROWS * GRID_W // LANES
    assert nsteps > nblk and C_QROWS * GRID_W == LANES and k_col % 2 == 0

    def first_blk(i):
        return jnp.clip(i - 2, 0, nsteps - nblk)

    def variant(i):
        return jnp.where(i < 2, i, jnp.where(i >= nsteps - 2, i - (nsteps - 2) + 3, 2))

    w = 4 * LANES
    kv_specs = [pl.BlockSpec((LANES, w), functools.partial(lambda i, t, c: (first_blk(i) + t, c), t=t, c=c))
                for c in (k_col, k_col + 1) for t in range(nblk)]
    return pl.pallas_call(
        _attn_c_kernel,
        grid=(nsteps,),
        in_specs=[pl.BlockSpec((LANES, w), lambda i: (i, q_col))] + kv_specs + [
                  pl.BlockSpec((lc, 2 * w), lambda i: (0, k_col // 2)),
                  pl.BlockSpec((None, None, C_HEADS * LANES, nblk * LANES),
                               lambda i: (layer, variant(i), 0, 0))],
        out_specs=pl.BlockSpec((LANES, w), lambda i: (i, 0)),
        out_shape=jax.ShapeDtypeStruct((seq, w), BF16),
        compiler_params=_cparams("arbitrary"),
        name="neighborhood_attention",
    )(rest, *([rest] * (2 * nblk)), rest_ctx, bias)


def _s5_window_scan(sr, si, tabs, cr, ci, reverse):
    n_tiles = sr.shape[0] // SUBLANES
    edge = 0 if reverse else SUBLANES - 1
    out_r, out_i = [None] * n_tiles, [None] * n_tiles
    for step in range(n_tiles):
        tile = (n_tiles - 1 - step) if reverse else step
        xr = sr[tile * SUBLANES:(tile + 1) * SUBLANES]
        xi = si[tile * SUBLANES:(tile + 1) * SUBLANES]
        for lvl, k in enumerate((1, 2, 4)):
            shift = (SUBLANES - k) if reverse else k
            rr, ri = pltpu.roll(xr, shift, 0), pltpu.roll(xi, shift, 0)
            mr, mi = tabs[2 * lvl], tabs[2 * lvl + 1]
            xr, xi = xr + mr * rr - mi * ri, xi + mr * ri + mi * rr
        pr, pi = tabs[6], tabs[7]
        xr, xi = xr + pr * cr - pi * ci, xi + pr * ci + pi * cr
        cr, ci = jnp.broadcast_to(xr[edge:edge + 1], xr.shape), jnp.broadcast_to(xi[edge:edge + 1], xi.shape)
        out_r[tile], out_i[tile] = xr, xi
    return jnp.concatenate(out_r, axis=0), jnp.concatenate(out_i, axis=0), cr, ci


def _s5_kernel(uz_ref, ul_ref, wc_ref, kc_ref, cc_ref, ew_ref, ek_ref, ec_ref, tab_ref, d_ref, *rest, reverse, final):
    if final:
        yfz_ref, yfl_ref, gw_ref, gb_ref, oz_ref, ol_ref, carry_ref, uf_ref, yc_ref, b_ref, k_ref, c_ref = rest
    else:
        oz_ref, ol_ref, carry_ref, uf_ref, yc_ref, b_ref, k_ref, c_ref = rest
    s = pl.program_id(0)
    nblk = b_ref.shape[0]
    ps = b_ref.shape[2] // 2
    gq = LANES // S5_GROUP_CH

    @pl.when(s == 0)
    def _():
        carry_ref[...] = jnp.zeros_like(carry_ref)

        def expand(compact, spread, row_group, per):
            full = jnp.dot(compact, spread, preferred_element_type=F32)
            col_group = (lax.broadcasted_iota(jnp.int32, (1, full.shape[1]), 1) // per) % gq
            return jnp.where(row_group == col_group, full, 0.0).astype(BF16)

        rows = lax.broadcasted_iota(jnp.int32, (wc_ref.shape[1], 1), 0)
        lag_rows_group = (rows // S5_GROUP_CH) % gq
        state_rows_group = (rows // S5_STATE) % gq
        for q in range(nblk):
            b_ref[q] = expand(wc_ref[q], ew_ref[...], lag_rows_group, S5_STATE)
            k_ref[q] = expand(kc_ref[q], ek_ref[...], lag_rows_group, S5_GROUP_CH)
            c_ref[q] = expand(cc_ref[q], ec_ref[...], state_rows_group, S5_GROUP_CH)

    def chunk(u_ref, yf_ref, o_ref):
        t = u_ref.shape[0]
        nw = t // SUBLANES
        row = lax.broadcasted_iota(jnp.int32, (t, 1), 0) & (SUBLANES - 1)
        wrow = lax.broadcasted_iota(jnp.int32, (nw, 1), 0)
        u = u_ref[...]
        uf = u.astype(F32)
        for q in range(nblk):
            uf_ref[q, 0:t] = uf[:, q * LANES:(q + 1) * LANES]
        lagged = [u]
        for k in range(1, SUBLANES):
            if reverse:
                lag = jnp.where(row < SUBLANES - k, pltpu.roll(uf, t - k, 0), 0.0)
            else:
                lag = jnp.where(row >= k, pltpu.roll(uf, k, 0), 0.0)
            lagged.append(lag.astype(BF16))
        outs = []
        for q in range(nblk):
            qs = slice(q * ps, (q + 1) * ps)
            lhs = jnp.concatenate([x[:, q * LANES:(q + 1) * LANES] for x in lagged], axis=1)
            y_loc = jnp.dot(lhs, k_ref[q], preferred_element_type=F32)
            ends = [uf_ref[q, pl.ds(k if reverse else SUBLANES - 1 - k, nw, stride=SUBLANES), :].astype(BF16)
                    for k in range(SUBLANES)]
            sw = jnp.dot(jnp.concatenate(ends, axis=1), b_ref[q], preferred_element_type=F32)
            tabs = [tab_ref[i, :, qs] for i in range(8)]
            cin_r, cin_i = carry_ref[0, :, qs], carry_ref[1, :, qs]
            c_r, c_i, cr, ci = _s5_window_scan(sw[:, :ps], sw[:, ps:], tabs, cin_r, cin_i, reverse)
            carry_ref[0, :, qs] = cr
            carry_ref[1, :, qs] = ci
            if reverse:
                e_r = jnp.where(wrow == nw - 1, cin_r[0:1], pltpu.roll(c_r, nw - 1, 0))
                e_i = jnp.where(wrow == nw - 1, cin_i[0:1], pltpu.roll(c_i, nw - 1, 0))
            else:
                e_r = jnp.where(wrow == 0, cin_r[0:1], pltpu.roll(c_r, 1, 0))
                e_i = jnp.where(wrow == 0, cin_i[0:1], pltpu.roll(c_i, 1, 0))
            y_in = jnp.dot(jnp.concatenate([e_r, e_i], axis=1).astype(BF16), c_ref[q],
                           preferred_element_type=F32)
            for j in range(SUBLANES):
                yc_ref[q, pl.ds(j, nw, stride=SUBLANES), :] = y_in[:, j * LANES:(j + 1) * LANES]
            outs.append(y_loc + yc_ref[q, 0:t])
        out = jnp.concatenate(outs, axis=1)
        if not final:
            o_ref[...] = u.astype(F32) * d_ref[...] + out
        else:
            g = _gelu_tanh(yf_ref[...] + out)
            z = jnp.dot(g.astype(BF16), gw_ref[...], preferred_element_type=F32) + gb_ref[...]
            o_ref[...] = (g * _sigmoid(z)).astype(o_ref.dtype)

    @pl.when(s == 0)
    def _():
        chunk(uz_ref, yfz_ref if final else None, oz_ref)

    @pl.when(s > 0)
    def _():
        chunk(ul_ref, yfl_ref if final else None, ol_ref)


def s5_direction(rest, rest_ctx, u_col, params, which, dskip, reverse, fwd=None, glu=None, t=1024):
    seq = rest.shape[0]
    tz = rest_ctx.shape[0]
    t = min(t, seq)
    nl = seq // t
    w = 4 * LANES
    wc, kc, cc, tabs = params
    _, nq, kq, _ = wc.shape
    gq = LANES // S5_GROUP_CH
    ps2 = 2 * gq * S5_STATE
    p2 = nq * ps2
    e_w = np.zeros((LANES, ps2), np.float32)
    for c in range(2):
        for gl in range(gq):
            for p in range(S5_STATE):
                e_w[c * S5_STATE + p, c * gq * S5_STATE + gl * S5_STATE + p] = 1.0
    e_k = np.zeros((LANES, LANES), np.float32)
    e_c = np.zeros((LANES, SUBLANES * LANES), np.float32)
    for gl in range(gq):
        for o in range(S5_GROUP_CH):
            e_k[o, gl * S5_GROUP_CH + o] = 1.0
            for j in range(SUBLANES):
                e_c[j * S5_GROUP_CH + o, j * LANES + gl * S5_GROUP_CH + o] = 1.0
    spreads = [jnp.asarray(e, BF16) for e in (e_w, e_k, e_c)]
    final = fwd is not None
    if reverse:
        lat = lambda s: (jnp.where(s == 0, nl - 1, nl - s), 0)
    else:
        lat = lambda s: (jnp.maximum(s - 1, 0), 0)
    lat_u = lambda s: (lat(s)[0], u_col)
    const = lambda s: (0, 0)
    table = lambda shape: pl.BlockSpec((None,) + shape, lambda s: (which,) + (0,) * len(shape),
                                       pipeline_mode=pl.Buffered(1))
    in_specs = [pl.BlockSpec((tz, w), lambda s: (0, u_col)),
                pl.BlockSpec((t, w), lat_u),
                table((nq, kq, LANES)),
                table((nq, kq, LANES)),
                table((nq, ps2, LANES)),
                pl.BlockSpec(e_w.shape, const),
                pl.BlockSpec(e_k.shape, const),
                pl.BlockSpec(e_c.shape, const),
                table((8, SUBLANES, p2 // 2)),
                pl.BlockSpec((1, w), const)]
    args = [rest_ctx, rest, wc, kc, cc, *spreads, tabs, dskip.reshape(1, w)]
    if final:
        in_specs += [pl.BlockSpec((tz, w), const), pl.BlockSpec((t, w), lat),
                     pl.BlockSpec((w, w), const), pl.BlockSpec((1, w), const)]
        args += [fwd[0], fwd[1], glu[0], glu[1].reshape(1, w)]
    odt = BF16 if final else F32
    return pl.pallas_call(
        functools.partial(_s5_kernel, reverse=reverse, final=final),
        grid=(nl + 1,),
        in_specs=in_specs,
        out_specs=[pl.BlockSpec((tz, w), const), pl.BlockSpec((t, w), lat)],
        out_shape=[jax.ShapeDtypeStruct((tz, w), odt), jax.ShapeDtypeStruct((seq, w), odt)],
        scratch_shapes=[pltpu.VMEM((2, SUBLANES, p2 // 2), F32),
                        pltpu.VMEM((nq, max(t, tz), LANES), F32),
                        pltpu.VMEM((nq, max(t, tz), LANES), F32),
                        pltpu.VMEM((nq, kq, ps2), BF16),
                        pltpu.VMEM((nq, kq, LANES), BF16),
                        pltpu.VMEM((nq, ps2, SUBLANES * LANES), BF16)],
        compiler_params=_cparams("arbitrary"),
        name="s5_direction",
    )(*args)


def _s5_params(a_re, a_im, log_step, b_re, b_im, c_re, c_im, rev):
    ar, ai = a_re.astype(F32), a_im.astype(F32)
    step = jnp.exp(log_step.astype(F32))[:, None]
    mag = jnp.exp(ar * step)
    ab_re, ab_im = mag * jnp.cos(ai * step), mag * jnp.sin(ai * step)
    den = ar * ar + ai * ai
    nr = ab_re - 1.0
    f_re = (nr * ar + ab_im * ai) / den
    f_im = (ab_im * ar - nr * ai) / den
    br, bi = b_re.astype(F32), b_im.astype(F32)
    bb_re = f_re[..., None] * br - f_im[..., None] * bi
    bb_im = f_re[..., None] * bi + f_im[..., None] * br
    cr, ci = c_re.astype(F32), c_im.astype(F32)
    g, p, h = bb_re.shape
    gq = LANES // h
    nq = g // gq
    nl = SUBLANES

    def cmul(xr, xi, yr, yi):
        return xr * yr - xi * yi, xr * yi + xi * yr

    def powers(xr, xi, n):
        out = [(jnp.ones_like(xr), jnp.zeros_like(xi)), (xr, xi)]
        for _ in range(n - 1):
            out.append(cmul(*out[-1], xr, xi))
        return jnp.stack([o[0] for o in out]), jnp.stack([o[1] for o in out])

    pr, pi = powers(ab_re, ab_im, nl)
    w_re = pr[:nl, ..., None] * bb_re - pi[:nl, ..., None] * bb_im
    w_im = pr[:nl, ..., None] * bb_im + pi[:nl, ..., None] * bb_re

    def rows_lag_group_channel(x):
        x = x.reshape(nl, nq, gq, h, x.shape[-1])
        return jnp.transpose(x, (1, 0, 2, 3, 4)).reshape(nq, nl * gq * h, x.shape[-1])

    wc = jnp.concatenate([rows_lag_group_channel(jnp.swapaxes(w, 2, 3)) for w in (w_re, w_im)], axis=2)
    hp = lax.Precision.HIGHEST
    kk = (jnp.einsum('kgph,gop->kgho', w_re, cr, precision=hp)
          - jnp.einsum('kgph,gop->kgho', w_im, ci, precision=hp))
    kc = jnp.pad(rows_lag_group_channel(kk), ((0, 0), (0, 0), (0, LANES - h)))
    dr = jnp.where(rev, pr[nl:0:-1], pr[1:nl + 1])
    di = jnp.where(rev, pi[nl:0:-1], pi[1:nl + 1])
    cd_re = cr[None] * dr[:, :, None, :] - ci[None] * di[:, :, None, :]
    cd_im = cr[None] * di[:, :, None, :] + ci[None] * dr[:, :, None, :]

    def rows_group_state(cd):
        return jnp.transpose(cd.reshape(nl, nq, gq, h, p), (1, 2, 4, 0, 3)).reshape(nq, gq * p, nl * h)

    cc = jnp.concatenate([rows_group_state(cd_re), -rows_group_state(cd_im)], axis=1)
    qr, qi = powers(pr[nl].reshape(-1), pi[nl].reshape(-1), SUBLANES)
    row = np.arange(SUBLANES)[:, None]
    tabs = []
    for k in (1, 2, 4):
        keep = jnp.where(rev, jnp.asarray(row < SUBLANES - k), jnp.asarray(row >= k))
        tabs += [jnp.where(keep, qr[k][None], 0.0), jnp.where(keep, qi[k][None], 0.0)]
    tabs += [jnp.where(rev, qr[SUBLANES:0:-1], qr[1:SUBLANES + 1]),
             jnp.where(rev, qi[SUBLANES:0:-1], qi[1:SUBLANES + 1])]
    return wc.astype(BF16), kc.astype(BF16), cc.astype(BF16), jnp.stack(tabs)


def _conv_kernel(val_ref, gate_ref, vp_ref, gp_ref, vn_ref, gn_ref, w_ref, b_ref, g_ref, beta_ref, o_ref, hs_ref,
                 *, sub):
    i = pl.program_id(0)
    n = pl.num_programs(0)
    t = val_ref.shape[0]

    def glu(v, g):
        return v[...].astype(F32) * _sigmoid(g[...].astype(F32))

    hs_ref[0:CONV_HALO] = jnp.where(i > 0, glu(vp_ref, gp_ref), 0.0)
    hs_ref[CONV_HALO:CONV_HALO + t] = glu(val_ref, gate_ref)
    hs_ref[CONV_HALO + t:2 * CONV_HALO + t] = jnp.where(i < n - 1, glu(vn_ref, gn_ref), 0.0)
    base = CONV_HALO - CONV_K // 2
    for r0 in range(t // sub):
        acc = jnp.zeros((sub, val_ref.shape[1]), F32) + b_ref[...]
        for r in range(SUBLANES):
            part = None
            for m in range((base + CONV_K - 1) // SUBLANES + 1):
                k = SUBLANES * m + r - base
                if 0 <= k < CONV_K:
                    lo_row = r0 * sub + SUBLANES * m
                    term = hs_ref[lo_row:lo_row + sub + SUBLANES] * w_ref[k:k + 1]
                    part = term if part is None else part + term
            acc = acc + part[r:r + sub]
        mu = jnp.mean(acc, axis=-1, keepdims=True)
        cen = acc - mu
        var = jnp.mean(cen * cen, axis=-1, keepdims=True)
        yn = cen * lax.rsqrt(var + EPS) * g_ref[...] + beta_ref[...]
        o_ref[r0 * sub:(r0 + 1) * sub] = _silu(yn).astype(o_ref.dtype)


def conv_module(rest, val_col, gate_col, w, b, g, beta, t=512, sub=64):
    seq = rest.shape[0]
    t = min(t, seq)
    cw = 4 * LANES
    hb = t // CONV_HALO
    nh = seq // CONV_HALO
    prev = lambda i: jnp.maximum(i * hb - 1, 0)
    nxt = lambda i: jnp.minimum((i + 1) * hb, nh - 1)
    vec = pl.BlockSpec((1, cw), lambda i: (0, 0))
    return pl.pallas_call(
        functools.partial(_conv_kernel, sub=sub),
        grid=(seq // t,),
        in_specs=[pl.BlockSpec((t, cw), lambda i: (i, val_col)),
                  pl.BlockSpec((t, cw), lambda i: (i, gate_col)),
                  pl.BlockSpec((CONV_HALO, cw), lambda i: (prev(i), val_col)),
                  pl.BlockSpec((CONV_HALO, cw), lambda i: (prev(i), gate_col)),
                  pl.BlockSpec((CONV_HALO, cw), lambda i: (nxt(i), val_col)),
                  pl.BlockSpec((CONV_HALO, cw), lambda i: (nxt(i), gate_col)),
                  pl.BlockSpec((CONV_K, cw), lambda i: (0, 0)), vec, vec, vec],
        out_specs=pl.BlockSpec((t, cw), lambda i: (i, 0)),
        out_shape=jax.ShapeDtypeStruct((seq, cw), BF16),
        scratch_shapes=[pltpu.VMEM((t + 2 * CONV_HALO, cw), F32)],
        compiler_params=_cparams("arbitrary"),
        name="conv_module",
    )(rest, rest, rest, rest, rest, rest, w.astype(F32), b.reshape(1, cw).astype(F32),
      g.reshape(1, cw).astype(F32), beta.reshape(1, cw).astype(F32))


def _merge_kernel(*refs, n_parts, cast_mlp):
    y_refs = refs[:N_BRANCH]
    gl_refs = refs[N_BRANCH:N_BRANCH * (1 + n_parts)]
    rest = refs[N_BRANCH * (1 + n_parts):]
    if cast_mlp:
        wb_ref, wo_ref, x_ref, gate_ref, g_ref, sc_ref, sh_ref, w1_ref, w2_ref, o_ref, h_ref, w1b_ref, w2b_ref = rest
        w1b_ref[...] = w1_ref[...].astype(BF16)
        w2b_ref[...] = w2_ref[...].astype(BF16)
    else:
        wb_ref, wo_ref, x_ref, gate_ref, g_ref, sc_ref, sh_ref, o_ref, h_ref = rest
    pw = gl_refs[0].shape[1]
    parts = []
    for part in range(n_parts):
        cols = slice(part * pw, (part + 1) * pw)
        m = None
        for k in range(N_BRANCH):
            proj = jnp.dot(y_refs[k][...], wb_ref[k, :, cols], preferred_element_type=F32)
            term = _sigmoid(gl_refs[k * n_parts + part][...].astype(F32)) * proj
            m = term if m is None else m + term
        parts.append(m.astype(BF16))
    merged = jnp.concatenate(parts, axis=1)
    x = x_ref[...] + gate_ref[...] * jnp.dot(merged, wo_ref[...], preferred_element_type=F32)
    o_ref[...] = x
    y = x * lax.rsqrt(jnp.mean(x * x, axis=-1, keepdims=True) + EPS) * g_ref[...]
    h_ref[...] = (y * (1.0 + sc_ref[...]) + sh_ref[...]).astype(h_ref.dtype)


def gated_merge_residual(ys, rest, logit_col, part_w, wb, wo, layer, x, gate, g2, scale2, shift2, mlp_w=None, tm=256):
    m, yw = ys[0].shape
    n = wb.shape[3]
    n_parts = n // part_w
    tm = min(tm, m)
    steps = m // tm
    resident = lambda shape: pl.BlockSpec((None,) + shape, lambda i: (layer,) + (0,) * len(shape),
                                          pipeline_mode=pl.Buffered(1))
    yspec = pl.BlockSpec((tm, yw), lambda i: (i, 0))
    gspecs = [pl.BlockSpec((tm, part_w), functools.partial(lambda i, c: (i, c), c=logit_col + k * n_parts + part))
              for k in range(N_BRANCH) for part in range(n_parts)]
    vec = pl.BlockSpec((1, n), lambda i: (0, 0))
    rows = pl.BlockSpec((tm, n), lambda i: (i, 0))
    in_specs = [yspec] * N_BRANCH + gspecs + [resident((N_BRANCH, yw, n)), resident((n, n)), rows,
                                              vec, vec, vec, vec]
    args = [*ys, *([rest] * (N_BRANCH * n_parts)), wb, wo, x, gate.reshape(1, n), g2.reshape(1, n),
            scale2.reshape(1, n), shift2.reshape(1, n)]
    out_specs = [rows, rows]
    out_shape = [jax.ShapeDtypeStruct((m, n), F32), jax.ShapeDtypeStruct((m, n), BF16)]
    if mlp_w is not None:
        w1, w2 = mlp_w
        ff = w1.shape[2]
        slab = ff // steps
        assert ff % steps == 0 and slab % LANES == 0
        in_specs += [pl.BlockSpec((None, n, slab), lambda i: (layer, 0, i)),
                     pl.BlockSpec((None, slab, n), lambda i: (layer, i, 0))]
        args += [w1, w2]
        out_specs += [pl.BlockSpec((n, slab), lambda i: (0, i)), pl.BlockSpec((slab, n), lambda i: (i, 0))]
        out_shape += [jax.ShapeDtypeStruct((n, ff), BF16), jax.ShapeDtypeStruct((ff, n), BF16)]
    return pl.pallas_call(
        functools.partial(_merge_kernel, n_parts=n_parts, cast_mlp=mlp_w is not None),
        grid=(steps,),
        in_specs=in_specs,
        out_specs=out_specs,
        out_shape=out_shape,
        compiler_params=_cparams("arbitrary"),
        name="gated_merge_residual",
    )(*args)


def _mlp_kernel(x_ref, h_ref, gate_ref, w1_ref, w2_ref, *rest, final_norm):
    if final_norm:
        gf_ref, o_ref, acc_ref = rest
    else:
        o_ref, acc_ref = rest
    f = pl.program_id(1)

    @pl.when(f == 0)
    def _():
        acc_ref[...] = jnp.zeros_like(acc_ref)

    t = jnp.dot(h_ref[...], w1_ref[...], preferred_element_type=F32)
    t = jnp.square(jnp.maximum(t, 0.0)).astype(BF16)
    acc_ref[...] += jnp.dot(t, w2_ref[...], preferred_element_type=F32)

    @pl.when(f == pl.num_programs(1) - 1)
    def _():
        y = x_ref[...] + gate_ref[...] * acc_ref[...]
        if final_norm:
            y = y * lax.rsqrt(jnp.mean(y * y, axis=-1, keepdims=True) + EPS) * gf_ref[...]
        o_ref[...] = y


def mlp_block(x, h, gate, w1, w2, final_g=None, tm=512, tf=1024):
    m, d = x.shape
    ff = w1.shape[1]
    tm = min(tm, m)
    vec = pl.BlockSpec((1, d), lambda i, f: (0, 0))
    rows = pl.BlockSpec((tm, d), lambda i, f: (i, 0))
    final_norm = final_g is not None
    args = [x, h, gate.reshape(1, d), w1, w2]
    in_specs = [rows, rows, vec,
                pl.BlockSpec((d, tf), lambda i, f: (0, f)),
                pl.BlockSpec((tf, d), lambda i, f: (f, 0))]
    if final_norm:
        args.append(final_g.reshape(1, d))
        in_specs.append(vec)
    return pl.pallas_call(
        functools.partial(_mlp_kernel, final_norm=final_norm),
        grid=(m // tm, ff // tf),
        in_specs=in_specs,
        out_specs=pl.BlockSpec((tm, d), lambda i, f: (i, 0)),
        out_shape=jax.ShapeDtypeStruct((m, d), F32),
        scratch_shapes=[pltpu.VMEM((tm, d), F32)],
        compiler_params=_cparams("arbitrary", "arbitrary"),
        name="mlp_block",
    )(*args)


_A_COL_K, _A_COL_V = 2, 3
_REST_COL0 = (A_HEADS + 2 * A_KV_HEADS) * HEAD_DIM
_COL_U, _COL_QC, _COL_KC, _COL_VC, _COL_VAL, _COL_GATE = 0, 1, 2, 3, 4, 5
_MERGE_PART = 1024
_COL_MERGE = 3


def _rope_tables(seq):
    freqs = ROPE_BASE ** (-jnp.arange(ROPE_PAIRS, dtype=F32) / ROPE_PAIRS)
    rows = seq // GRID_W
    ang_r = jnp.arange(rows).astype(F32)[:, None] * freqs
    ang_c = jnp.arange(GRID_W).astype(F32)[:, None] * freqs

    def expand(tab_r, tab_c, first, second):
        r = jnp.repeat(tab_r, GRID_W, axis=0)
        cc = jnp.tile(tab_c, (rows, 1))
        head = jnp.concatenate([first * r, second * r, first * cc, second * cc], axis=1)
        return jnp.tile(head, (1, LANES // HEAD_DIM))

    cos = expand(jnp.cos(ang_r), jnp.cos(ang_c), 1.0, 1.0)
    sa = expand(jnp.sin(ang_r), jnp.sin(ang_c), 0.0, 1.0)
    sb = expand(jnp.sin(ang_r), jnp.sin(ang_c), -1.0, 0.0)
    return cos, sa, sb


def kernel(x, c, ctx, c_ctx, ada_w, ada_b, norm1_g, norm2_g, w_in, a_sink, s5_a_re, s5_a_im, s5_log_step, s5_b_re, s5_b_im, s5_c_re, s5_c_im, s5_d, s5_glu_w, s5_glu_b, c_rpb, d_conv_w, d_conv_b, d_ln_g, d_ln_b, w_branch, w_out, mlp_w1, mlp_w2, final_g):
    depth = ada_w.shape[0]
    _, seq, d = x.shape
    lc = ctx.shape[1]
    xs = x.reshape(seq, d)
    zs = ctx.reshape(lc, d)
    rope = _rope_tables(seq)

    cc = jnp.zeros((SUBLANES, d), F32).at[0].set(c[0]).at[1].set(c_ctx)
    mods = ada_modulation(cc, ada_w, ada_b)
    n_rest = w_in.shape[2] - _REST_COL0
    wb, wo = w_branch.astype(BF16), w_out.astype(BF16)
    nbr_bias = jax.vmap(lambda rpb: _nbr_bias(rpb, seq // GRID_W))(c_rpb)
    s5_params = jax.vmap(_s5_params)(*(a.reshape((2 * depth,) + a.shape[2:]) for a in
                                       (s5_a_re, s5_a_im, s5_log_step, s5_b_re, s5_b_im, s5_c_re, s5_c_im)),
                                     jnp.arange(2 * depth) % 2 == 1)

    for l in range(depth):
        need_ctx = l < depth - 1
        mx = [mods[l, 0, k * d:(k + 1) * d] for k in range(6)]
        mz = [mods[l, 1, k * d:(k + 1) * d] for k in range(6)]

        hx, qk_x = norm_head_projection(xs, norm1_g[l], mx[1], mx[0], w_in, l, rope=rope)
        hz, qk_z = norm_head_projection(zs, norm1_g[l], mz[1], mz[0], w_in, l)
        r_x = wide_projection(hx, w_in, l, _REST_COL0, n_rest)
        r_z = wide_projection(hz, w_in, l, _REST_COL0, n_rest if need_ctx else (_COL_VC + 1) * 4 * LANES)

        ya_x = window_attention(a_sink[l], qk_x, qk_z)
        yc_x = neighborhood_attention(r_x, r_z, nbr_bias, l, _COL_QC, _COL_KC)
        yd_x = conv_module(r_x, _COL_VAL, _COL_GATE, d_conv_w[l], d_conv_b[l], d_ln_g[l], d_ln_b[l])

        yf = s5_direction(r_x, r_z, _COL_U, s5_params, 2 * l, s5_d[l].astype(F32), reverse=False)
        yb_z, yb_x = s5_direction(r_x, r_z, _COL_U, s5_params, 2 * l + 1, s5_d[l].astype(F32), reverse=True, fwd=yf,
                                  glu=(s5_glu_w[l].astype(BF16), s5_glu_b[l].astype(F32)))

        xs, h2, w1, w2 = gated_merge_residual((ya_x, yb_x, yc_x, yd_x), r_x, _COL_MERGE, _MERGE_PART, wb, wo, l, xs,
                                              mx[2], norm2_g[l], mx[4], mx[3], mlp_w=(mlp_w1, mlp_w2))
        xs = mlp_block(xs, h2, mx[5], w1, w2, final_g=None if need_ctx else final_g)

        if need_ctx:
            ya_z = context_attention(a_sink[l], qk_z, 0, qk_z, _A_COL_K, 2 * LANES, qk_z, _A_COL_V, True, True)
            yc_z = context_attention(a_sink[l], r_z, _COL_QC, r_z, _COL_KC, 4 * LANES, r_z, _COL_VC, False, False)
            yd_z = conv_module(r_z, _COL_VAL, _COL_GATE, d_conv_w[l], d_conv_b[l], d_ln_g[l], d_ln_b[l])
            zs, h2 = gated_merge_residual((ya_z, yb_z, yc_z, yd_z), r_z, _COL_MERGE, _MERGE_PART, wb, wo, l, zs,
                                          mz[2], norm2_g[l], mz[4], mz[3])
            zs = mlp_block(zs, h2, mz[5], w1, w2)

    return xs.reshape(x.shape)
```

```python
import functools
import math

import numpy as np
import jax
import jax.numpy as jnp
from jax import lax
from jax.experimental import pallas as pl
from jax.experimental.pallas import tpu as pltpu

F32 = jnp.float32
BF16 = jnp.bfloat16

LANES = 128
SUBLANES = 8
VMEM_LIMIT = 56 * 1024 * 1024

GRID_W = 64
HEAD_DIM = 64
N_BRANCH = 4
A_HEADS = 8
A_KV_HEADS = 2
A_WINDOW = 128
A_BLOCK = 128
ROPE_BASE = 10000.0
ROPE_PAIRS = HEAD_DIM // 4
S5_GROUP_CH = 16
S5_STATE = 64
C_HEADS = 8
C_WIN_R = 8
C_WIN_C = 16
C_QROWS = 2
C_KROWS = C_QROWS + C_WIN_R
CONV_K = 31
CONV_HALO = 16
EPS = 1e-6
NEG_INF = -1e30


def _cparams(*sem):
    return pltpu.CompilerParams(dimension_semantics=sem, vmem_limit_bytes=VMEM_LIMIT)


def _sigmoid(x):
    return 0.5 * jnp.tanh(0.5 * x) + 0.5


def _silu(x):
    return x * _sigmoid(x)


def _gelu_tanh(x):
    return 0.5 * x * (1.0 + jnp.tanh(math.sqrt(2.0 / math.pi) * (x + 0.044715 * (x * x * x))))


def _ada_kernel(c_ref, w_ref, b_ref, o_ref):
    a = _silu(c_ref[...]).astype(BF16)
    o_ref[...] = jnp.dot(a, w_ref[...].astype(BF16), preferred_element_type=F32) + b_ref[...]


def ada_modulation(cc, ada_w, ada_b, tn=2048):
    depth, d, n = ada_w.shape
    return pl.pallas_call(
        _ada_kernel,
        grid=(depth, n // tn),
        in_specs=[pl.BlockSpec((SUBLANES, d), lambda l, j: (0, 0)),
                  pl.BlockSpec((None, d, tn), lambda l, j: (l, 0, j)),
                  pl.BlockSpec((None, 1, tn), lambda l, j: (l, 0, j))],
        out_specs=pl.BlockSpec((None, SUBLANES, tn), lambda l, j: (l, 0, j)),
        out_shape=jax.ShapeDtypeStruct((depth, SUBLANES, n), F32),
        compiler_params=_cparams("arbitrary", "arbitrary"),
        name="ada_modulation",
    )(cc, ada_w, ada_b.reshape(depth, 1, n))


def _rope_block(y, cos, sa, sb):
    return y * cos + pltpu.roll(y, 16, 1) * sa + pltpu.roll(y, LANES - 16, 1) * sb


def _head_proj_kernel(x_ref, g_ref, sc_ref, sh_ref, w_ref, *rest, n_q, with_rope):
    if with_rope:
        cos_ref, sa_ref, sb_ref, h_ref, o_ref, wb_ref = rest
        cos, sa, sb = cos_ref[...], sa_ref[...], sb_ref[...]
    else:
        h_ref, o_ref, wb_ref = rest

    @pl.when(pl.program_id(0) == 0)
    def _():
        wb_ref[...] = w_ref[0].astype(BF16)

    x = x_ref[...]
    y = x * lax.rsqrt(jnp.mean(x * x, axis=-1, keepdims=True) + EPS) * g_ref[...]
    h = (y * (1.0 + sc_ref[...]) + sh_ref[...]).astype(BF16)
    h_ref[...] = h
    acc = jnp.dot(h, wb_ref[...], preferred_element_type=F32)
    lo = _lane_lo()
    out = 0
    for b in range(acc.shape[1] // LANES):
        y = acc[:, b * LANES:(b + 1) * LANES]
        if with_rope and b <= n_q:
            y = _rope_block(y, cos, sa, sb)
        if b < n_q:
            blocks = [y]
        else:
            swapped = pltpu.roll(y, HEAD_DIM, 1)
            blocks = [jnp.where(lo, y, swapped), jnp.where(lo, swapped, y)]
        for blk in blocks:
            o_ref[:, out * LANES:(out + 1) * LANES] = blk.astype(o_ref.dtype)
            out += 1


def norm_head_projection(x, g, scale, shift, w_in, layer, rope=None, tm=1024):
    m, k = x.shape
    n_in = (A_HEADS + 2 * A_KV_HEADS) * HEAD_DIM
    n_q = A_HEADS * HEAD_DIM // LANES
    assert A_KV_HEADS * HEAD_DIM == LANES
    n_out = n_in + 2 * LANES
    tm = min(tm, m)
    vec = pl.BlockSpec((1, k), lambda i: (0, 0))
    in_specs = [pl.BlockSpec((tm, k), lambda i: (i, 0)), vec, vec, vec,
                pl.BlockSpec((pl.Element(1), pl.Element(k), pl.Element(n_in)), lambda i: (layer, 0, 0))]
    args = [x, g.reshape(1, k), scale.reshape(1, k), shift.reshape(1, k), w_in]
    if rope is not None:
        in_specs += [pl.BlockSpec((tm, LANES), lambda i: (i, 0))] * 3
        args += list(rope)
    return pl.pallas_call(
        functools.partial(_head_proj_kernel, n_q=n_q, with_rope=rope is not None),
        grid=(m // tm,),
        in_specs=in_specs,
        out_specs=[pl.BlockSpec((tm, k), lambda i: (i, 0)), pl.BlockSpec((tm, n_out), lambda i: (i, 0))],
        out_shape=[jax.ShapeDtypeStruct((m, k), BF16), jax.ShapeDtypeStruct((m, n_out), BF16)],
        scratch_shapes=[pltpu.VMEM((k, n_in), BF16)],
        compiler_params=_cparams("arbitrary"),
        name="norm_head_projection",
    )(*args)


def _wide_proj_kernel(a_ref, w_ref, o_ref, wb_ref):
    @pl.when(pl.program_id(1) == 0)
    def _():
        wb_ref[...] = w_ref[0].astype(BF16)

    o_ref[...] = jnp.dot(a_ref[...], wb_ref[...], preferred_element_type=F32).astype(o_ref.dtype)


def wide_projection(a, w, layer, col0, n, tm=2048, tn=1024):
    m, k = a.shape
    tm = min(tm, m)
    assert n % tn == 0 and col0 % LANES == 0 and tn % LANES == 0
    return pl.pallas_call(
        _wide_proj_kernel,
        grid=(n // tn, m // tm),
        in_specs=[pl.BlockSpec((tm, k), lambda j, i: (i, 0)),
                  pl.BlockSpec((pl.Element(1), pl.Element(k), pl.Element(tn)),
                               lambda j, i: (layer, 0, pl.multiple_of(col0 + j * tn, LANES)))],
        out_specs=pl.BlockSpec((tm, tn), lambda j, i: (i, j)),
        out_shape=jax.ShapeDtypeStruct((m, n), BF16),
        scratch_shapes=[pltpu.VMEM((k, tn), BF16)],
        compiler_params=_cparams("arbitrary", "arbitrary"),
        name="wide_projection",
    )(a, w)


def _lane_lo():
    return lax.broadcasted_iota(jnp.int32, (1, LANES), 1) < HEAD_DIM


def _half_select(x, lo, half):
    zero = jnp.zeros_like(x)
    return jnp.where(lo, x, zero) if half == 0 else jnp.where(lo, zero, x)


def _dot_nt(a, b):
    return lax.dot_general(a, b, (((1,), (1,)), ((), ())), preferred_element_type=F32)


def _softmax_pv(parts, values, sink=None):
    m = parts[0].max(axis=-1, keepdims=True)
    for s in parts[1:]:
        m = jnp.maximum(m, s.max(axis=-1, keepdims=True))
    if sink is not None:
        m = jnp.maximum(m, sink)
    den = jnp.exp(sink - m) if sink is not None else 0.0
    out = 0.0
    for s, v in zip(parts, values):
        p = jnp.exp(s - m)
        den = den + p.sum(axis=-1, keepdims=True)
        out = out + jnp.dot(p.astype(BF16), v, preferred_element_type=F32)
    return out / den


def _attn_a_window_start(i, seq):
    return jnp.clip((i - 1) * A_BLOCK, 0, seq - 3 * A_BLOCK)


def _attn_a_kernel(sink_ref, q_ref, kv0_ref, kv1_ref, kv2_ref, kvz_ref, o_ref, *, seq):
    i = pl.program_id(0)
    lo = _lane_lo()
    nq = 4 * A_BLOCK
    r = lax.broadcasted_iota(jnp.int32, (nq, 3 * A_BLOCK), 0) & (A_BLOCK - 1)
    c = lax.broadcasted_iota(jnp.int32, (nq, 3 * A_BLOCK), 1)
    dist = c - r + (_attn_a_window_start(i, seq) - i * A_BLOCK)
    valid = (dist >= -A_WINDOW) & (dist <= A_WINDOW)
    rowblk = lax.broadcasted_iota(jnp.int32, (nq, 1), 0) // A_BLOCK
    vcol = A_KV_HEADS * LANES
    for g in range(A_KV_HEADS):
        gs = slice(g * LANES, (g + 1) * LANES)
        vs = slice(vcol + g * LANES, vcol + (g + 1) * LANES)
        kw = jnp.concatenate([ref[:, gs] for ref in (kv0_ref, kv1_ref, kv2_ref)], axis=0)
        vw = jnp.concatenate([ref[:, vs] for ref in (kv0_ref, kv1_ref, kv2_ref)], axis=0)
        kz, vz = kvz_ref[:, gs], kvz_ref[:, vs]
        blocks = [slice((2 * g + bb) * LANES, (2 * g + bb + 1) * LANES) for bb in range(2)]
        ql = jnp.concatenate([_half_select(q_ref[:, bs], lo, half) * (HEAD_DIM ** -0.5)
                              for half in range(2) for bs in blocks], axis=0)
        heads = [2 * (2 * g + bb) + half for half in range(2) for bb in range(2)]
        sink = jnp.full((nq, 1), sink_ref[heads[3]], F32)
        for k in range(3):
            sink = jnp.where(rowblk == k, sink_ref[heads[k]], sink)
        s = jnp.where(valid, _dot_nt(ql, kw), NEG_INF)
        sz = _dot_nt(ql, kz)
        m = jnp.maximum(jnp.maximum(s.max(axis=-1, keepdims=True), sz.max(axis=-1, keepdims=True)), sink)
        p, pz = jnp.exp(s - m), jnp.exp(sz - m)
        inv = 1.0 / (p.sum(axis=-1, keepdims=True) + pz.sum(axis=-1, keepdims=True) + jnp.exp(sink - m))
        p, pz = p.astype(BF16), pz.astype(BF16)
        halves = []
        for half in range(2):
            hs = slice(half * 2 * A_BLOCK, (half + 1) * 2 * A_BLOCK)
            halves.append((jnp.dot(p[hs], vw, preferred_element_type=F32)
                           + jnp.dot(pz[hs], vz, preferred_element_type=F32)) * inv[hs])
        out = jnp.where(lo, halves[0], halves[1])
        for bb in range(2):
            o_ref[:, blocks[bb]] = out[bb * A_BLOCK:(bb + 1) * A_BLOCK].astype(o_ref.dtype)


def window_attention(sink, qk, qk_ctx):
    seq = qk.shape[0]
    lc = qk_ctx.shape[0]
    nb = seq // A_BLOCK
    qw = A_HEADS * HEAD_DIM
    kvw = 4 * A_KV_HEADS * HEAD_DIM
    assert nb >= 3 and qw == kvw
    kv_specs = [pl.BlockSpec((A_BLOCK, kvw), functools.partial(
        lambda i, t: (_attn_a_window_start(i, seq) // A_BLOCK + t, 1), t=t)) for t in range(3)]
    return pl.pallas_call(
        functools.partial(_attn_a_kernel, seq=seq),
        grid=(nb,),
        in_specs=[pl.BlockSpec(memory_space=pltpu.SMEM),
                  pl.BlockSpec((A_BLOCK, qw), lambda i: (i, 0))] + kv_specs + [
                  pl.BlockSpec((lc, kvw), lambda i: (0, 1))],
        out_specs=pl.BlockSpec((A_BLOCK, qw), lambda i: (i, 0)),
        out_shape=jax.ShapeDtypeStruct((seq, qw), BF16),
        compiler_params=_cparams("arbitrary"),
        name="window_attention",
    )(sink, qk, qk, qk, qk, qk_ctx)


def _ctx_attn_kernel(sink_ref, q_ref, k_ref, v_ref, o_ref, *, kv_shared, has_sink):
    lo = _lane_lo()
    for b in range(q_ref.shape[1] // LANES):
        bs = slice(b * LANES, (b + 1) * LANES)
        kb = b // 2 if kv_shared else b
        ks = slice(kb * LANES, (kb + 1) * LANES)
        qb, k, v = q_ref[:, bs], k_ref[:, ks], v_ref[:, ks]
        acc = jnp.zeros((q_ref.shape[0], LANES), F32)
        for half in range(2):
            ql = _half_select(qb, lo, half) * (HEAD_DIM ** -0.5)
            sink = sink_ref[2 * b + half] if has_sink else None
            acc = acc + _softmax_pv([_dot_nt(ql, k)], [_half_select(v, lo, half)], sink=sink)
        o_ref[:, bs] = acc.astype(o_ref.dtype)


def context_attention(sink, q_arr, q_col, k_arr, k_col, k_w, v_arr, v_col, kv_shared, has_sink):
    lc = q_arr.shape[0]
    return pl.pallas_call(
        functools.partial(_ctx_attn_kernel, kv_shared=kv_shared, has_sink=has_sink),
        grid=(1,),
        in_specs=[pl.BlockSpec(memory_space=pltpu.SMEM),
                  pl.BlockSpec((lc, 4 * LANES), lambda i: (0, q_col)),
                  pl.BlockSpec((lc, k_w), lambda i: (0, k_col)),
                  pl.BlockSpec((lc, k_w), lambda i: (0, v_col))],
        out_specs=pl.BlockSpec((lc, 4 * LANES), lambda i: (0, 0)),
        out_shape=jax.ShapeDtypeStruct((lc, 4 * LANES), BF16),
        compiler_params=_cparams("arbitrary"),
        name="context_attention",
    )(sink, q_arr, k_arr, v_arr)


def _nbr_row_offsets(rows):
    nsteps = rows // C_QROWS
    nblk = C_KROWS * GRID_W // LANES
    wr = min(C_WIN_R, rows)
    reps = [0, 1, 2, nsteps - 2, nsteps - 1]
    dr = -np.ones((len(reps), C_QROWS, C_KROWS), np.int32)
    for v, i in enumerate(reps):
        w0 = C_QROWS * int(np.clip(i - 2, 0, nsteps - nblk))
        for a in range(C_QROWS):
            rq = C_QROWS * i + a
            k0 = int(np.clip(rq - C_WIN_R // 2, 0, rows - wr))
            for j in range(C_KROWS):
                if k0 <= w0 + j < k0 + wr:
                    dr[v, a, j] = w0 + j - rq + C_WIN_R - 1
    return dr


def _nbr_bias(rpb, rows):
    h = rpb.shape[0]
    qc = np.arange(GRID_W)[:, None]
    kc = np.arange(GRID_W)[None, :]
    cstart = np.clip(qc - C_WIN_C // 2, 0, GRID_W - C_WIN_C)
    col_ok = jnp.asarray((kc >= cstart) & (kc < cstart + C_WIN_C))
    pad = GRID_W - C_WIN_C
    ext = jnp.pad(rpb.astype(F32), ((0, 0), (0, 0), (pad, pad)), mode='edge')
    toep = jnp.stack([ext[:, :, GRID_W - 1 - q:2 * GRID_W - 1 - q] for q in range(GRID_W)], axis=2)
    toep = jnp.where(col_ok, toep, NEG_INF)
    masked = jnp.full((h, GRID_W, GRID_W), NEG_INF, F32)
    dr = _nbr_row_offsets(rows)
    variants = []
    for v in range(dr.shape[0]):
        qrows = [jnp.concatenate([toep[:, dr[v, a, j]] if dr[v, a, j] >= 0 else masked for j in range(C_KROWS)],
                                 axis=-1) for a in range(C_QROWS)]
        variants.append(jnp.concatenate(qrows, axis=-2).reshape(h * C_QROWS * GRID_W, C_KROWS * GRID_W))
    return jnp.stack(variants)


def _attn_c_kernel(q_ref, *refs):
    nblk = (len(refs) - 3) // 2
    k_refs, v_refs = refs[:nblk], refs[nblk:2 * nblk]
    kvz_ref, bias_ref, o_ref = refs[-3:]
    lo = _lane_lo()
    nq = q_ref.shape[0]
    nb = C_HEADS // 2
    vcol = C_HEADS * HEAD_DIM
    blocks = [slice(b * LANES, (b + 1) * LANES) for b in range(nb)]
    s_parts, sz_parts = [], []
    for bs in blocks:
        ql = jnp.concatenate([_half_select(q_ref[:, bs], lo, half) * (HEAD_DIM ** -0.5) for half in range(2)],
                             axis=0)
        s_parts.append(_dot_nt(ql, jnp.concatenate([ref[:, bs] for ref in k_refs], axis=0)))
        sz_parts.append(_dot_nt(ql, kvz_ref[:, bs]))
    s = jnp.concatenate(s_parts, axis=0) + bias_ref[...]
    sz = jnp.concatenate(sz_parts, axis=0)
    m = jnp.maximum(s.max(axis=-1, keepdims=True), sz.max(axis=-1, keepdims=True))
    p, pz = jnp.exp(s - m), jnp.exp(sz - m)
    inv = 1.0 / (p.sum(axis=-1, keepdims=True) + pz.sum(axis=-1, keepdims=True))
    p, pz = p.astype(BF16), pz.astype(BF16)
    for b, bs in enumerate(blocks):
        vs = slice(vcol + b * LANES, vcol + (b + 1) * LANES)
        vw, vz = jnp.concatenate([ref[:, bs] for ref in v_refs], axis=0), kvz_ref[:, vs]
        halves = []
        for half in range(2):
            hs = slice((2 * b + half) * nq, (2 * b + half + 1) * nq)
            halves.append((jnp.dot(p[hs], vw, preferred_element_type=F32)
                           + jnp.dot(pz[hs], vz, preferred_element_type=F32)) * inv[hs])
        o_ref[:, bs] = jnp.where(lo, halves[0], halves[1]).astype(o_ref.dtype)


def neighborhood_attention(rest, rest_ctx, bias, layer, q_col, k_col):
    seq = rest.shape[0]
    lc = rest_ctx.shape[0]
    rows = seq // GRID_W
    nsteps = rows // C_QROWS
    nblk = C_KROWS * GRID_W // LANES
    assert nsteps > nblk and C_QROWS * GRID_W == LANES and k_col % 2 == 0

    def first_blk(i):
        return jnp.clip(i - 2, 0, nsteps - nblk)

    def variant(i):
        return jnp.where(i < 2, i, jnp.where(i >= nsteps - 2, i - (nsteps - 2) + 3, 2))

    w = 4 * LANES
    kv_specs = [pl.BlockSpec((LANES, w), functools.partial(lambda i, t, c: (first_blk(i) + t, c), t=t, c=c))
                for c in (k_col, k_col + 1) for t in range(nblk)]
    return pl.pallas_call(
        _attn_c_kernel,
        grid=(nsteps,),
        in_specs=[pl.BlockSpec((LANES, w), lambda i: (i, q_col))] + kv_specs + [
                  pl.BlockSpec((lc, 2 * w), lambda i: (0, k_col // 2)),
                  pl.BlockSpec((None, None, C_HEADS * LANES, nblk * LANES),
                               lambda i: (layer, variant(i), 0, 0))],
        out_specs=pl.BlockSpec((LANES, w), lambda i: (i, 0)),
        out_shape=jax.ShapeDtypeStruct((seq, w), BF16),
        compiler_params=_cparams("arbitrary"),
        name="neighborhood_attention",
    )(rest, *([rest] * (2 * nblk)), rest_ctx, bias)


def _s5_window_scan(sr, si, tabs, cr, ci, reverse):
    n_tiles = sr.shape[0] // SUBLANES
    edge = 0 if reverse else SUBLANES - 1
    out_r, out_i = [None] * n_tiles, [None] * n_tiles
    for step in range(n_tiles):
        tile = (n_tiles - 1 - step) if reverse else step
        xr = sr[tile * SUBLANES:(tile + 1) * SUBLANES]
        xi = si[tile * SUBLANES:(tile + 1) * SUBLANES]
        for lvl, k in enumerate((1, 2, 4)):
            shift = (SUBLANES - k) if reverse else k
            rr, ri = pltpu.roll(xr, shift, 0), pltpu.roll(xi, shift, 0)
            mr, mi = tabs[2 * lvl], tabs[2 * lvl + 1]
            xr, xi = xr + mr * rr - mi * ri, xi + mr * ri + mi * rr
        pr, pi = tabs[6], tabs[7]
        xr, xi = xr + pr * cr - pi * ci, xi + pr * ci + pi * cr
        cr, ci = jnp.broadcast_to(xr[edge:edge + 1], xr.shape), jnp.broadcast_to(xi[edge:edge + 1], xi.shape)
        out_r[tile], out_i[tile] = xr, xi
    return jnp.concatenate(out_r, axis=0), jnp.concatenate(out_i, axis=0), cr, ci


def _s5_kernel(uz_ref, ul_ref, wc_ref, kc_ref, cc_ref, ew_ref, ek_ref, ec_ref, tab_ref, d_ref, *rest, reverse, final):
    if final:
        yfz_ref, yfl_ref, gw_ref, gb_ref, oz_ref, ol_ref, carry_ref, uf_ref, yc_ref, b_ref, k_ref, c_ref = rest
    else:
        oz_ref, ol_ref, carry_ref, uf_ref, yc_ref, b_ref, k_ref, c_ref = rest
    s = pl.program_id(0)
    nblk = b_ref.shape[0]
    ps = b_ref.shape[2] // 2
    gq = LANES // S5_GROUP_CH

    @pl.when(s == 0)
    def _():
        carry_ref[...] = jnp.zeros_like(carry_ref)

        def expand(compact, spread, row_group, per):
            full = jnp.dot(compact, spread, preferred_element_type=F32)
            col_group = (lax.broadcasted_iota(jnp.int32, (1, full.shape[1]), 1) // per) % gq
            return jnp.where(row_group == col_group, full, 0.0).astype(BF16)

        rows = lax.broadcasted_iota(jnp.int32, (wc_ref.shape[1], 1), 0)
        lag_rows_group = (rows // S5_GROUP_CH) % gq
        state_rows_group = (rows // S5_STATE) % gq
        for q in range(nblk):
            b_ref[q] = expand(wc_ref[q], ew_ref[...], lag_rows_group, S5_STATE)
            k_ref[q] = expand(kc_ref[q], ek_ref[...], lag_rows_group, S5_GROUP_CH)
            c_ref[q] = expand(cc_ref[q], ec_ref[...], state_rows_group, S5_GROUP_CH)

    def chunk(u_ref, yf_ref, o_ref):
        t = u_ref.shape[0]
        nw = t // SUBLANES
        row = lax.broadcasted_iota(jnp.int32, (t, 1), 0) & (SUBLANES - 1)
        wrow = lax.broadcasted_iota(jnp.int32, (nw, 1), 0)
        u = u_ref[...]
        uf = u.astype(F32)
        for q in range(nblk):
            uf_ref[q, 0:t] = uf[:, q * LANES:(q + 1) * LANES]
        lagged = [u]
        for k in range(1, SUBLANES):
            if reverse:
                lag = jnp.where(row < SUBLANES - k, pltpu.roll(uf, t - k, 0), 0.0)
            else:
                lag = jnp.where(row >= k, pltpu.roll(uf, k, 0), 0.0)
            lagged.append(lag.astype(BF16))
        outs = []
        for q in range(nblk):
            qs = slice(q * ps, (q + 1) * ps)
            lhs = jnp.concatenate([x[:, q * LANES:(q + 1) * LANES] for x in lagged], axis=1)
            y_loc = jnp.dot(lhs, k_ref[q], preferred_element_type=F32)
            ends = [uf_ref[q, pl.ds(k if reverse else SUBLANES - 1 - k, nw, stride=SUBLANES), :].astype(BF16)
                    for k in range(SUBLANES)]
            sw = jnp.dot(jnp.concatenate(ends, axis=1), b_ref[q], preferred_element_type=F32)
            tabs = [tab_ref[i, :, qs] for i in range(8)]
            cin_r, cin_i = carry_ref[0, :, qs], carry_ref[1, :, qs]
            c_r, c_i, cr, ci = _s5_window_scan(sw[:, :ps], sw[:, ps:], tabs, cin_r, cin_i, reverse)
            carry_ref[0, :, qs] = cr
            carry_ref[1, :, qs] = ci
            if reverse:
                e_r = jnp.where(wrow == nw - 1, cin_r[0:1], pltpu.roll(c_r, nw - 1, 0))
                e_i = jnp.where(wrow == nw - 1, cin_i[0:1], pltpu.roll(c_i, nw - 1, 0))
            else:
                e_r = jnp.where(wrow == 0, cin_r[0:1], pltpu.roll(c_r, 1, 0))
                e_i = jnp.where(wrow == 0, cin_i[0:1], pltpu.roll(c_i, 1, 0))
            y_in = jnp.dot(jnp.concatenate([e_r, e_i], axis=1).astype(BF16), c_ref[q],
                           preferred_element_type=F32)
            for j in range(SUBLANES):
                yc_ref[q, pl.ds(j, nw, stride=SUBLANES), :] = y_in[:, j * LANES:(j + 1) * LANES]
            outs.append(y_loc + yc_ref[q, 0:t])
        out = jnp.concatenate(outs, axis=1)
        if not final:
            o_ref[...] = u.astype(F32) * d_ref[...] + out
        else:
            g = _gelu_tanh(yf_ref[...] + out)
            z = jnp.dot(g.astype(BF16), gw_ref[...], preferred_element_type=F32) + gb_ref[...]
            o_ref[...] = (g * _sigmoid(z)).astype(o_ref.dtype)

    @pl.when(s == 0)
    def _():
        chunk(uz_ref, yfz_ref if final else None, oz_ref)

    @pl.when(s > 0)
    def _():
        chunk(ul_ref, yfl_ref if final else None, ol_ref)


def s5_direction(rest, rest_ctx, u_col, params, which, dskip, reverse, fwd=None, glu=None, t=1024):
    seq = rest.shape[0]
    tz = rest_ctx.shape[0]
    t = min(t, seq)
    nl = seq // t
    w = 4 * LANES
    wc, kc, cc, tabs = params
    _, nq, kq, _ = wc.shape
    gq = LANES // S5_GROUP_CH
    ps2 = 2 * gq * S5_STATE
    p2 = nq * ps2
    e_w = np.zeros((LANES, ps2), np.float32)
    for c in range(2):
        for gl in range(gq):
            for p in range(S5_STATE):
                e_w[c * S5_STATE + p, c * gq * S5_STATE + gl * S5_STATE + p] = 1.0
    e_k = np.zeros((LANES, LANES), np.float32)
    e_c = np.zeros((LANES, SUBLANES * LANES), np.float32)
    for gl in range(gq):
        for o in range(S5_GROUP_CH):
            e_k[o, gl * S5_GROUP_CH + o] = 1.0
            for j in range(SUBLANES):
                e_c[j * S5_GROUP_CH + o, j * LANES + gl * S5_GROUP_CH + o] = 1.0
    spreads = [jnp.asarray(e, BF16) for e in (e_w, e_k, e_c)]
    final = fwd is not None
    if reverse:
        lat = lambda s: (jnp.where(s == 0, nl - 1, nl - s), 0)
    else:
        lat = lambda s: (jnp.maximum(s - 1, 0), 0)
    lat_u = lambda s: (lat(s)[0], u_col)
    const = lambda s: (0, 0)
    table = lambda shape: pl.BlockSpec((None,) + shape, lambda s: (which,) + (0,) * len(shape),
                                       pipeline_mode=pl.Buffered(1))
    in_specs = [pl.BlockSpec((tz, w), lambda s: (0, u_col)),
                pl.BlockSpec((t, w), lat_u),
                table((nq, kq, LANES)),
                table((nq, kq, LANES)),
                table((nq, ps2, LANES)),
                pl.BlockSpec(e_w.shape, const),
                pl.BlockSpec(e_k.shape, const),
                pl.BlockSpec(e_c.shape, const),
                table((8, SUBLANES, p2 // 2)),
                pl.BlockSpec((1, w), const)]
    args = [rest_ctx, rest, wc, kc, cc, *spreads, tabs, dskip.reshape(1, w)]
    if final:
        in_specs += [pl.BlockSpec((tz, w), const), pl.BlockSpec((t, w), lat),
                     pl.BlockSpec((w, w), const), pl.BlockSpec((1, w), const)]
        args += [fwd[0], fwd[1], glu[0], glu[1].reshape(1, w)]
    odt = BF16 if final else F32
    return pl.pallas_call(
        functools.partial(_s5_kernel, reverse=reverse, final=final),
        grid=(nl + 1,),
        in_specs=in_specs,
        out_specs=[pl.BlockSpec((tz, w), const), pl.BlockSpec((t, w), lat)],
        out_shape=[jax.ShapeDtypeStruct((tz, w), odt), jax.ShapeDtypeStruct((seq, w), odt)],
        scratch_shapes=[pltpu.VMEM((2, SUBLANES, p2 // 2), F32),
                        pltpu.VMEM((nq, max(t, tz), LANES), F32),
                        pltpu.VMEM((nq, max(t, tz), LANES), F32),
                        pltpu.VMEM((nq, kq, ps2), BF16),
                        pltpu.VMEM((nq, kq, LANES), BF16),
                        pltpu.VMEM((nq, ps2, SUBLANES * LANES), BF16)],
        compiler_params=_cparams("arbitrary"),
        name="s5_direction",
    )(*args)


def _s5_params(a_re, a_im, log_step, b_re, b_im, c_re, c_im, rev):
    ar, ai = a_re.astype(F32), a_im.astype(F32)
    step = jnp.exp(log_step.astype(F32))[:, None]
    mag = jnp.exp(ar * step)
    ab_re, ab_im = mag * jnp.cos(ai * step), mag * jnp.sin(ai * step)
    den = ar * ar + ai * ai
    nr = ab_re - 1.0
    f_re = (nr * ar + ab_im * ai) / den
    f_im = (ab_im * ar - nr * ai) / den
    br, bi = b_re.astype(F32), b_im.astype(F32)
    bb_re = f_re[..., None] * br - f_im[..., None] * bi
    bb_im = f_re[..., None] * bi + f_im[..., None] * br
    cr, ci = c_re.astype(F32), c_im.astype(F32)
    g, p, h = bb_re.shape
    gq = LANES // h
    nq = g // gq
    nl = SUBLANES

    def cmul(xr, xi, yr, yi):
        return xr * yr - xi * yi, xr * yi + xi * yr

    def powers(xr, xi, n):
        out = [(jnp.ones_like(xr), jnp.zeros_like(xi)), (xr, xi)]
        for _ in range(n - 1):
            out.append(cmul(*out[-1], xr, xi))
        return jnp.stack([o[0] for o in out]), jnp.stack([o[1] for o in out])

    pr, pi = powers(ab_re, ab_im, nl)
    w_re = pr[:nl, ..., None] * bb_re - pi[:nl, ..., None] * bb_im
    w_im = pr[:nl, ..., None] * bb_im + pi[:nl, ..., None] * bb_re

    def rows_lag_group_channel(x):
        x = x.reshape(nl, nq, gq, h, x.shape[-1])
        return jnp.transpose(x, (1, 0, 2, 3, 4)).reshape(nq, nl * gq * h, x.shape[-1])

    wc = jnp.concatenate([rows_lag_group_channel(jnp.swapaxes(w, 2, 3)) for w in (w_re, w_im)], axis=2)
    hp = lax.Precision.HIGHEST
    kk = (jnp.einsum('kgph,gop->kgho', w_re, cr, precision=hp)
          - jnp.einsum('kgph,gop->kgho', w_im, ci, precision=hp))
    kc = jnp.pad(rows_lag_group_channel(kk), ((0, 0), (0, 0), (0, LANES - h)))
    dr = jnp.where(rev, pr[nl:0:-1], pr[1:nl + 1])
    di = jnp.where(rev, pi[nl:0:-1], pi[1:nl + 1])
    cd_re = cr[None] * dr[:, :, None, :] - ci[None] * di[:, :, None, :]
    cd_im = cr[None] * di[:, :, None, :] + ci[None] * dr[:, :, None, :]

    def rows_group_state(cd):
        return jnp.transpose(cd.reshape(nl, nq, gq, h, p), (1, 2, 4, 0, 3)).reshape(nq, gq * p, nl * h)

    cc = jnp.concatenate([rows_group_state(cd_re), -rows_group_state(cd_im)], axis=1)
    qr, qi = powers(pr[nl].reshape(-1), pi[nl].reshape(-1), SUBLANES)
    row = np.arange(SUBLANES)[:, None]
    tabs = []
    for k in (1, 2, 4):
        keep = jnp.where(rev, jnp.asarray(row < SUBLANES - k), jnp.asarray(row >= k))
        tabs += [jnp.where(keep, qr[k][None], 0.0), jnp.where(keep, qi[k][None], 0.0)]
    tabs += [jnp.where(rev, qr[SUBLANES:0:-1], qr[1:SUBLANES + 1]),
             jnp.where(rev, qi[SUBLANES:0:-1], qi[1:SUBLANES + 1])]
    return wc.astype(BF16), kc.astype(BF16), cc.astype(BF16), jnp.stack(tabs)


def _conv_kernel(val_ref, gate_ref, vp_ref, gp_ref, vn_ref, gn_ref, w_ref, b_ref, g_ref, beta_ref, o_ref, hs_ref,
                 *, sub):
    i = pl.program_id(0)
    n = pl.num_programs(0)
    t = val_ref.shape[0]

    def glu(v, g):
        return v[...].astype(F32) * _sigmoid(g[...].astype(F32))

    hs_ref[0:CONV_HALO] = jnp.where(i > 0, glu(vp_ref, gp_ref), 0.0)
    hs_ref[CONV_HALO:CONV_HALO + t] = glu(val_ref, gate_ref)
    hs_ref[CONV_HALO + t:2 * CONV_HALO + t] = jnp.where(i < n - 1, glu(vn_ref, gn_ref), 0.0)
    base = CONV_HALO - CONV_K // 2
    for r0 in range(t // sub):
        acc = jnp.zeros((sub, val_ref.shape[1]), F32) + b_ref[...]
        for r in range(SUBLANES):
            part = None
            for m in range((base + CONV_K - 1) // SUBLANES + 1):
                k = SUBLANES * m + r - base
                if 0 <= k < CONV_K:
                    lo_row = r0 * sub + SUBLANES * m
                    term = hs_ref[lo_row:lo_row + sub + SUBLANES] * w_ref[k:k + 1]
                    part = term if part is None else part + term
            acc = acc + part[r:r + sub]
        mu = jnp.mean(acc, axis=-1, keepdims=True)
        cen = acc - mu
        var = jnp.mean(cen * cen, axis=-1, keepdims=True)
        yn = cen * lax.rsqrt(var + EPS) * g_ref[...] + beta_ref[...]
        o_ref[r0 * sub:(r0 + 1) * sub] = _silu(yn).astype(o_ref.dtype)


def conv_module(rest, val_col, gate_col, w, b, g, beta, t=512, sub=64):
    seq = rest.shape[0]
    t = min(t, seq)
    cw = 4 * LANES
    hb = t // CONV_HALO
    nh = seq // CONV_HALO
    prev = lambda i: jnp.maximum(i * hb - 1, 0)
    nxt = lambda i: jnp.minimum((i + 1) * hb, nh - 1)
    vec = pl.BlockSpec((1, cw), lambda i: (0, 0))
    return pl.pallas_call(
        functools.partial(_conv_kernel, sub=sub),
        grid=(seq // t,),
        in_specs=[pl.BlockSpec((t, cw), lambda i: (i, val_col)),
                  pl.BlockSpec((t, cw), lambda i: (i, gate_col)),
                  pl.BlockSpec((CONV_HALO, cw), lambda i: (prev(i), val_col)),
                  pl.BlockSpec((CONV_HALO, cw), lambda i: (prev(i), gate_col)),
                  pl.BlockSpec((CONV_HALO, cw), lambda i: (nxt(i), val_col)),
                  pl.BlockSpec((CONV_HALO, cw), lambda i: (nxt(i), gate_col)),
                  pl.BlockSpec((CONV_K, cw), lambda i: (0, 0)), vec, vec, vec],
        out_specs=pl.BlockSpec((t, cw), lambda i: (i, 0)),
        out_shape=jax.ShapeDtypeStruct((seq, cw), BF16),
        scratch_shapes=[pltpu.VMEM((t + 2 * CONV_HALO, cw), F32)],
        compiler_params=_cparams("arbitrary"),
        name="conv_module",
    )(rest, rest, rest, rest, rest, rest, w.astype(F32), b.reshape(1, cw).astype(F32),
      g.reshape(1, cw).astype(F32), beta.reshape(1, cw).astype(F32))


def _merge_kernel(*refs, n_parts, cast_mlp):
    y_refs = refs[:N_BRANCH]
    gl_refs = refs[N_BRANCH:N_BRANCH * (1 + n_parts)]
    rest = refs[N_BRANCH * (1 + n_parts):]
    if cast_mlp:
        wb_ref, wo_ref, x_ref, gate_ref, g_ref, sc_ref, sh_ref, w1_ref, w2_ref, o_ref, h_ref, w1b_ref, w2b_ref = rest
        w1b_ref[...] = w1_ref[...].astype(BF16)
        w2b_ref[...] = w2_ref[...].astype(BF16)
    else:
        wb_ref, wo_ref, x_ref, gate_ref, g_ref, sc_ref, sh_ref, o_ref, h_ref = rest
    pw = gl_refs[0].shape[1]
    parts = []
    for part in range(n_parts):
        cols = slice(part * pw, (part + 1) * pw)
        m = None
        for k in range(N_BRANCH):
            proj = jnp.dot(y_refs[k][...], wb_ref[k, :, cols], preferred_element_type=F32)
            term = _sigmoid(gl_refs[k * n_parts + part][...].astype(F32)) * proj
            m = term if m is None else m + term
        parts.append(m.astype(BF16))
    merged = jnp.concatenate(parts, axis=1)
    x = x_ref[...] + gate_ref[...] * jnp.dot(merged, wo_ref[...], preferred_element_type=F32)
    o_ref[...] = x
    y = x * lax.rsqrt(jnp.mean(x * x, axis=-1, keepdims=True) + EPS) * g_ref[...]
    h_ref[...] = (y * (1.0 + sc_ref[...]) + sh_ref[...]).astype(h_ref.dtype)


def gated_merge_residual(ys, rest, logit_col, part_w, wb, wo, layer, x, gate, g2, scale2, shift2, mlp_w=None, tm=256):
    m, yw = ys[0].shape
    n = wb.shape[3]
    n_parts = n // part_w
    tm = min(tm, m)
    steps = m // tm
    resident = lambda shape: pl.BlockSpec((None,) + shape, lambda i: (layer,) + (0,) * len(shape),
                                          pipeline_mode=pl.Buffered(1))
    yspec = pl.BlockSpec((tm, yw), lambda i: (i, 0))
    gspecs = [pl.BlockSpec((tm, part_w), functools.partial(lambda i, c: (i, c), c=logit_col + k * n_parts + part))
              for k in range(N_BRANCH) for part in range(n_parts)]
    vec = pl.BlockSpec((1, n), lambda i: (0, 0))
    rows = pl.BlockSpec((tm, n), lambda i: (i, 0))
    in_specs = [yspec] * N_BRANCH + gspecs + [resident((N_BRANCH, yw, n)), resident((n, n)), rows,
                                              vec, vec, vec, vec]
    args = [*ys, *([rest] * (N_BRANCH * n_parts)), wb, wo, x, gate.reshape(1, n), g2.reshape(1, n),
            scale2.reshape(1, n), shift2.reshape(1, n)]
    out_specs = [rows, rows]
    out_shape = [jax.ShapeDtypeStruct((m, n), F32), jax.ShapeDtypeStruct((m, n), BF16)]
    if mlp_w is not None:
        w1, w2 = mlp_w
        ff = w1.shape[2]
        slab = ff // steps
        assert ff % steps == 0 and slab % LANES == 0
        in_specs += [pl.BlockSpec((None, n, slab), lambda i: (layer, 0, i)),
                     pl.BlockSpec((None, slab, n), lambda i: (layer, i, 0))]
        args += [w1, w2]
        out_specs += [pl.BlockSpec((n, slab), lambda i: (0, i)), pl.BlockSpec((slab, n), lambda i: (i, 0))]
        out_shape += [jax.ShapeDtypeStruct((n, ff), BF16), jax.ShapeDtypeStruct((ff, n), BF16)]
    return pl.pallas_call(
        functools.partial(_merge_kernel, n_parts=n_parts, cast_mlp=mlp_w is not None),
        grid=(steps,),
        in_specs=in_specs,
        out_specs=out_specs,
        out_shape=out_shape,
        compiler_params=_cparams("arbitrary"),
        name="gated_merge_residual",
    )(*args)


def _mlp_kernel(x_ref, h_ref, gate_ref, w1_ref, w2_ref, *rest, final_norm):
    if final_norm:
        gf_ref, o_ref, acc_ref = rest
    else:
        o_ref, acc_ref = rest
    f = pl.program_id(1)

    @pl.when(f == 0)
    def _():
        acc_ref[...] = jnp.zeros_like(acc_ref)

    t = jnp.dot(h_ref[...], w1_ref[...], preferred_element_type=F32)
    t = jnp.square(jnp.maximum(t, 0.0)).astype(BF16)
    acc_ref[...] += jnp.dot(t, w2_ref[...], preferred_element_type=F32)

    @pl.when(f == pl.num_programs(1) - 1)
    def _():
        y = x_ref[...] + gate_ref[...] * acc_ref[...]
        if final_norm:
            y = y * lax.rsqrt(jnp.mean(y * y, axis=-1, keepdims=True) + EPS) * gf_ref[...]
        o_ref[...] = y


def mlp_block(x, h, gate, w1, w2, final_g=None, tm=512, tf=1024):
    m, d = x.shape
    ff = w1.shape[1]
    tm = min(tm, m)
    vec = pl.BlockSpec((1, d), lambda i, f: (0, 0))
    rows = pl.BlockSpec((tm, d), lambda i, f: (i, 0))
    final_norm = final_g is not None
    args = [x, h, gate.reshape(1, d), w1, w2]
    in_specs = [rows, rows, vec,
                pl.BlockSpec((d, tf), lambda i, f: (0, f)),
                pl.BlockSpec((tf, d), lambda i, f: (f, 0))]
    if final_norm:
        args.append(final_g.reshape(1, d))
        in_specs.append(vec)
    return pl.pallas_call(
        functools.partial(_mlp_kernel, final_norm=final_norm),
        grid=(m // tm, ff // tf),
        in_specs=in_specs,
        out_specs=pl.BlockSpec((tm, d), lambda i, f: (i, 0)),
        out_shape=jax.ShapeDtypeStruct((m, d), F32),
        scratch_shapes=[pltpu.VMEM((tm, d), F32)],
        compiler_params=_cparams("arbitrary", "arbitrary"),
        name="mlp_block",
    )(*args)


_A_COL_K, _A_COL_V = 2, 3
_REST_COL0 = (A_HEADS + 2 * A_KV_HEADS) * HEAD_DIM
_COL_U, _COL_QC, _COL_KC, _COL_VC, _COL_VAL, _COL_GATE = 0, 1, 2, 3, 4, 5
_MERGE_PART = 1024
_COL_MERGE = 3


def _rope_tables(seq):
    freqs = ROPE_BASE ** (-jnp.arange(ROPE_PAIRS, dtype=F32) / ROPE_PAIRS)
    rows = seq // GRID_W
    ang_r = jnp.arange(rows).astype(F32)[:, None] * freqs
    ang_c = jnp.arange(GRID_W).astype(F32)[:, None] * freqs

    def expand(tab_r, tab_c, first, second):
        r = jnp.repeat(tab_r, GRID_W, axis=0)
        cc = jnp.tile(tab_c, (rows, 1))
        head = jnp.concatenate([first * r, second * r, first * cc, second * cc], axis=1)
        return jnp.tile(head, (1, LANES // HEAD_DIM))

    cos = expand(jnp.cos(ang_r), jnp.cos(ang_c), 1.0, 1.0)
    sa = expand(jnp.sin(ang_r), jnp.sin(ang_c), 0.0, 1.0)
    sb = expand(jnp.sin(ang_r), jnp.sin(ang_c), -1.0, 0.0)
    return cos, sa, sb


def kernel(x, c, ctx, c_ctx, ada_w, ada_b, norm1_g, norm2_g, w_in, a_sink, s5_a_re, s5_a_im, s5_log_step, s5_b_re, s5_b_im, s5_c_re, s5_c_im, s5_d, s5_glu_w, s5_glu_b, c_rpb, d_conv_w, d_conv_b, d_ln_g, d_ln_b, w_branch, w_out, mlp_w1, mlp_w2, final_g):
    depth = ada_w.shape[0]
    _, seq, d = x.shape
    lc = ctx.shape[1]
    xs = x.reshape(seq, d)
    zs = ctx.reshape(lc, d)
    rope = _rope_tables(seq)

    cc = jnp.zeros((SUBLANES, d), F32).at[0].set(c[0]).at[1].set(c_ctx)
    mods = ada_modulation(cc, ada_w, ada_b)
    n_rest = w_in.shape[2] - _REST_COL0
    wb, wo = w_branch.astype(BF16), w_out.astype(BF16)
    nbr_bias = jax.vmap(lambda rpb: _nbr_bias(rpb, seq // GRID_W))(c_rpb)
    s5_params = jax.vmap(_s5_params)(*(a.reshape((2 * depth,) + a.shape[2:]) for a in
                                       (s5_a_re, s5_a_im, s5_log_step, s5_b_re, s5_b_im, s5_c_re, s5_c_im)),
                                     jnp.arange(2 * depth) % 2 == 1)

    for l in range(depth):
        need_ctx = l < depth - 1
        mx = [mods[l, 0, k * d:(k + 1) * d] for k in range(6)]
        mz = [mods[l, 1, k * d:(k + 1) * d] for k in range(6)]

        hx, qk_x = norm_head_projection(xs, norm1_g[l], mx[1], mx[0], w_in, l, rope=rope)
        hz, qk_z = norm_head_projection(zs, norm1_g[l], mz[1], mz[0], w_in, l)
        r_x = wide_projection(hx, w_in, l, _REST_COL0, n_rest)
        r_z = wide_projection(hz, w_in, l, _REST_COL0, n_rest if need_ctx else (_COL_VC + 1) * 4 * LANES)

        ya_x = window_attention(a_sink[l], qk_x, qk_z)
        yc_x = neighborhood_attention(r_x, r_z, nbr_bias, l, _COL_QC, _COL_KC)
        yd_x = conv_module(r_x, _COL_VAL, _COL_GATE, d_conv_w[l], d_conv_b[l], d_ln_g[l], d_ln_b[l])

        yf = s5_direction(r_x, r_z, _COL_U, s5_params, 2 * l, s5_d[l].astype(F32), reverse=False)
        yb_z, yb_x = s5_direction(r_x, r_z, _COL_U, s5_params, 2 * l + 1, s5_d[l].astype(F32), reverse=True, fwd=yf,
                                  glu=(s5_glu_w[l].astype(BF16), s5_glu_b[l].astype(F32)))

        xs, h2, w1, w2 = gated_merge_residual((ya_x, yb_x, yc_x, yd_x), r_x, _COL_MERGE, _MERGE_PART, wb, wo, l, xs,
                                              mx[2], norm2_g[l], mx[4], mx[3], mlp_w=(mlp_w1, mlp_w2))
        xs = mlp_block(xs, h2, mx[5], w1, w2, final_g=None if need_ctx else final_g)

        if need_ctx:
            ya_z = context_attention(a_sink[l], qk_z, 0, qk_z, _A_COL_K, 2 * LANES, qk_z, _A_COL_V, True, True)
            yc_z = context_attention(a_sink[l], r_z, _COL_QC, r_z, _COL_KC, 4 * LANES, r_z, _COL_VC, False, False)
            yd_z = conv_module(r_z, _COL_VAL, _COL_GATE, d_conv_w[l], d_conv_b[l], d_ln_g[l], d_ln_b[l])
            zs, h2 = gated_merge_residual((ya_z, yb_z, yc_z, yd_z), r_z, _COL_MERGE, _MERGE_PART, wb, wo, l, zs,
                                          mz[2], norm2_g[l], mz[4], mz[3])
            zs = mlp_block(zs, h2, mz[5], w1, w2)

    return xs.reshape(x.shape)
```

```python
import functools
import math

import numpy as np
import jax
import jax.numpy as jnp
from jax import lax
from jax.experimental import pallas as pl
from jax.experimental.pallas import tpu as pltpu

F32 = jnp.float32
BF16 = jnp.bfloat16

LANES = 128
SUBLANES = 8
VMEM_LIMIT = 56 * 1024 * 1024

GRID_W = 64
HEAD_DIM = 64
N_BRANCH = 4
A_HEADS = 8
A_KV_HEADS = 2
A_WINDOW = 128
A_BLOCK = 128
ROPE_BASE = 10000.0
ROPE_PAIRS = HEAD_DIM // 4
S5_GROUP_CH = 16
S5_STATE = 64
C_HEADS = 8
C_WIN_R = 8
C_WIN_C = 16
C_QROWS = 2
C_KROWS = C_QROWS + C_WIN_R
CONV_K = 31
CONV_HALO = 16
EPS = 1e-6
NEG_INF = -1e30


def _cparams(*sem):
    return pltpu.CompilerParams(dimension_semantics=sem, vmem_limit_bytes=VMEM_LIMIT)


def _sigmoid(x):
    return 0.5 * jnp.tanh(0.5 * x) + 0.5


def _silu(x):
    return x * _sigmoid(x)


def _gelu_tanh(x):
    return 0.5 * x * (1.0 + jnp.tanh(math.sqrt(2.0 / math.pi) * (x + 0.044715 * (x * x * x))))


def _ada_kernel(c_ref, w_ref, b_ref, o_ref):
    a = _silu(c_ref[...]).astype(BF16)
    o_ref[...] = jnp.dot(a, w_ref[...].astype(BF16), preferred_element_type=F32) + b_ref[...]


def ada_modulation(cc, ada_w, ada_b, tn=2048):
    depth, d, n = ada_w.shape
    return pl.pallas_call(
        _ada_kernel,
        grid=(depth, n // tn),
        in_specs=[pl.BlockSpec((SUBLANES, d), lambda l, j: (0, 0)),
                  pl.BlockSpec((None, d, tn), lambda l, j: (l, 0, j)),
                  pl.BlockSpec((None, 1, tn), lambda l, j: (l, 0, j))],
        out_specs=pl.BlockSpec((None, SUBLANES, tn), lambda l, j: (l, 0, j)),
        out_shape=jax.ShapeDtypeStruct((depth, SUBLANES, n), F32),
        compiler_params=_cparams("arbitrary", "arbitrary"),
        name="ada_modulation",
    )(cc, ada_w, ada_b.reshape(depth, 1, n))


def _rope_block(y, cos, sa, sb):
    return y * cos + pltpu.roll(y, 16, 1) * sa + pltpu.roll(y, LANES - 16, 1) * sb


def _head_proj_kernel(x_ref, g_ref, sc_ref, sh_ref, w_ref, *rest, n_q, with_rope):
    if with_rope:
        cos_ref, sa_ref, sb_ref, h_ref, o_ref, wb_ref = rest
        cos, sa, sb = cos_ref[...], sa_ref[...], sb_ref[...]
    else:
        h_ref, o_ref, wb_ref = rest

    @pl.when(pl.program_id(0) == 0)
    def _():
        wb_ref[...] = w_ref[0].astype(BF16)

    x = x_ref[...]
    y = x * lax.rsqrt(jnp.mean(x * x, axis=-1, keepdims=True) + EPS) * g_ref[...]
    h = (y * (1.0 + sc_ref[...]) + sh_ref[...]).astype(BF16)
    h_ref[...] = h
    acc = jnp.dot(h, wb_ref[...], preferred_element_type=F32)
    lo = _lane_lo()
    out = 0
    for b in range(acc.shape[1] // LANES):
        y = acc[:, b * LANES:(b + 1) * LANES]
        if with_rope and b <= n_q:
            y = _rope_block(y, cos, sa, sb)
        if b < n_q:
            blocks = [y]
        else:
            swapped = pltpu.roll(y, HEAD_DIM, 1)
            blocks = [jnp.where(lo, y, swapped), jnp.where(lo, swapped, y)]
        for blk in blocks:
            o_ref[:, out * LANES:(out + 1) * LANES] = blk.astype(o_ref.dtype)
            out += 1


def norm_head_projection(x, g, scale, shift, w_in, layer, rope=None, tm=1024):
    m, k = x.shape
    n_in = (A_HEADS + 2 * A_KV_HEADS) * HEAD_DIM
    n_q = A_HEADS * HEAD_DIM // LANES
    assert A_KV_HEADS * HEAD_DIM == LANES
    n_out = n_in + 2 * LANES
    tm = min(tm, m)
    vec = pl.BlockSpec((1, k), lambda i: (0, 0))
    in_specs = [pl.BlockSpec((tm, k), lambda i: (i, 0)), vec, vec, vec,
                pl.BlockSpec((pl.Element(1), pl.Element(k), pl.Element(n_in)), lambda i: (layer, 0, 0))]
    args = [x, g.reshape(1, k), scale.reshape(1, k), shift.reshape(1, k), w_in]
    if rope is not None:
        in_specs += [pl.BlockSpec((tm, LANES), lambda i: (i, 0))] * 3
        args += list(rope)
    return pl.pallas_call(
        functools.partial(_head_proj_kernel, n_q=n_q, with_rope=rope is not None),
        grid=(m // tm,),
        in_specs=in_specs,
        out_specs=[pl.BlockSpec((tm, k), lambda i: (i, 0)), pl.BlockSpec((tm, n_out), lambda i: (i, 0))],
        out_shape=[jax.ShapeDtypeStruct((m, k), BF16), jax.ShapeDtypeStruct((m, n_out), BF16)],
        scratch_shapes=[pltpu.VMEM((k, n_in), BF16)],
        compiler_params=_cparams("arbitrary"),
        name="norm_head_projection",
    )(*args)


def _wide_proj_kernel(a_ref, w_ref, o_ref, wb_ref):
    @pl.when(pl.program_id(1) == 0)
    def _():
        wb_ref[...] = w_ref[0].astype(BF16)

    o_ref[...] = jnp.dot(a_ref[...], wb_ref[...], preferred_element_type=F32).astype(o_ref.dtype)


def wide_projection(a, w, layer, col0, n, tm=2048, tn=1024):
    m, k = a.shape
    tm = min(tm, m)
    assert n % tn == 0 and col0 % LANES == 0 and tn % LANES == 0
    return pl.pallas_call(
        _wide_proj_kernel,
        grid=(n // tn, m // tm),
        in_specs=[pl.BlockSpec((tm, k), lambda j, i: (i, 0)),
                  pl.BlockSpec((pl.Element(1), pl.Element(k), pl.Element(tn)),
                               lambda j, i: (layer, 0, pl.multiple_of(col0 + j * tn, LANES)))],
        out_specs=pl.BlockSpec((tm, tn), lambda j, i: (i, j)),
        out_shape=jax.ShapeDtypeStruct((m, n), BF16),
        scratch_shapes=[pltpu.VMEM((k, tn), BF16)],
        compiler_params=_cparams("arbitrary", "arbitrary"),
        name="wide_projection",
    )(a, w)


def _lane_lo():
    return lax.broadcasted_iota(jnp.int32, (1, LANES), 1) < HEAD_DIM


def _half_select(x, lo, half):
    zero = jnp.zeros_like(x)
    return jnp.where(lo, x, zero) if half == 0 else jnp.where(lo, zero, x)


def _dot_nt(a, b):
    return lax.dot_general(a, b, (((1,), (1,)), ((), ())), preferred_element_type=F32)


def _softmax_pv(parts, values, sink=None):
    m = parts[0].max(axis=-1, keepdims=True)
    for s in parts[1:]:
        m = jnp.maximum(m, s.max(axis=-1, keepdims=True))
    if sink is not None:
        m = jnp.maximum(m, sink)
    den = jnp.exp(sink - m) if sink is not None else 0.0
    out = 0.0
    for s, v in zip(parts, values):
        p = jnp.exp(s - m)
        den = den + p.sum(axis=-1, keepdims=True)
        out = out + jnp.dot(p.astype(BF16), v, preferred_element_type=F32)
    return out / den


def _attn_a_window_start(i, seq):
    return jnp.clip((i - 1) * A_BLOCK, 0, seq - 3 * A_BLOCK)


def _attn_a_kernel(sink_ref, q_ref, kv0_ref, kv1_ref, kv2_ref, kvz_ref, o_ref, *, seq):
    i = pl.program_id(0)
    lo = _lane_lo()
    nq = 4 * A_BLOCK
    r = lax.broadcasted_iota(jnp.int32, (nq, 3 * A_BLOCK), 0) & (A_BLOCK - 1)
    c = lax.broadcasted_iota(jnp.int32, (nq, 3 * A_BLOCK), 1)
    dist = c - r + (_attn_a_window_start(i, seq) - i * A_BLOCK)
    valid = (dist >= -A_WINDOW) & (dist <= A_WINDOW)
    rowblk = lax.broadcasted_iota(jnp.int32, (nq, 1), 0) // A_BLOCK
    vcol = A_KV_HEADS * LANES
    for g in range(A_KV_HEADS):
        gs = slice(g * LANES, (g + 1) * LANES)
        vs = slice(vcol + g * LANES, vcol + (g + 1) * LANES)
        kw = jnp.concatenate([ref[:, gs] for ref in (kv0_ref, kv1_ref, kv2_ref)], axis=0)
        vw = jnp.concatenate([ref[:, vs] for ref in (kv0_ref, kv1_ref, kv2_ref)], axis=0)
        kz, vz = kvz_ref[:, gs], kvz_ref[:, vs]
        blocks = [slice((2 * g + bb) * LANES, (2 * g + bb + 1) * LANES) for bb in range(2)]
        ql = jnp.concatenate([_half_select(q_ref[:, bs], lo, half) * (HEAD_DIM ** -0.5)
                              for half in range(2) for bs in blocks], axis=0)
        heads = [2 * (2 * g + bb) + half for half in range(2) for bb in range(2)]
        sink = jnp.full((nq, 1), sink_ref[heads[3]], F32)
        for k in range(3):
            sink = jnp.where(rowblk == k, sink_ref[heads[k]], sink)
        s = jnp.where(valid, _dot_nt(ql, kw), NEG_INF)
        sz = _dot_nt(ql, kz)
        m = jnp.maximum(jnp.maximum(s.max(axis=-1, keepdims=True), sz.max(axis=-1, keepdims=True)), sink)
        p, pz = jnp.exp(s - m), jnp.exp(sz - m)
        inv = 1.0 / (p.sum(axis=-1, keepdims=True) + pz.sum(axis=-1, keepdims=True) + jnp.exp(sink - m))
        p, pz = p.astype(BF16), pz.astype(BF16)
        halves = []
        for half in range(2):
            hs = slice(half * 2 * A_BLOCK, (half + 1) * 2 * A_BLOCK)
            halves.append((jnp.dot(p[hs], vw, preferred_element_type=F32)
                           + jnp.dot(pz[hs], vz, preferred_element_type=F32)) * inv[hs])
        out = jnp.where(lo, halves[0], halves[1])
        for bb in range(2):
            o_ref[:, blocks[bb]] = out[bb * A_BLOCK:(bb + 1) * A_BLOCK].astype(o_ref.dtype)


def window_attention(sink, qk, qk_ctx):
    seq = qk.shape[0]
    lc = qk_ctx.shape[0]
    nb = seq // A_BLOCK
    qw = A_HEADS * HEAD_DIM
    kvw = 4 * A_KV_HEADS * HEAD_DIM
    assert nb >= 3 and qw == kvw
    kv_specs = [pl.BlockSpec((A_BLOCK, kvw), functools.partial(
        lambda i, t: (_attn_a_window_start(i, seq) // A_BLOCK + t, 1), t=t)) for t in range(3)]
    return pl.pallas_call(
        functools.partial(_attn_a_kernel, seq=seq),
        grid=(nb,),
        in_specs=[pl.BlockSpec(memory_space=pltpu.SMEM),
                  pl.BlockSpec((A_BLOCK, qw), lambda i: (i, 0))] + kv_specs + [
                  pl.BlockSpec((lc, kvw), lambda i: (0, 1))],
        out_specs=pl.BlockSpec((A_BLOCK, qw), lambda i: (i, 0)),
        out_shape=jax.ShapeDtypeStruct((seq, qw), BF16),
        compiler_params=_cparams("arbitrary"),
        name="window_attention",
    )(sink, qk, qk, qk, qk, qk_ctx)


def _ctx_attn_kernel(sink_ref, q_ref, k_ref, v_ref, o_ref, *, kv_shared, has_sink):
    lo = _lane_lo()
    for b in range(q_ref.shape[1] // LANES):
        bs = slice(b * LANES, (b + 1) * LANES)
        kb = b // 2 if kv_shared else b
        ks = slice(kb * LANES, (kb + 1) * LANES)
        qb, k, v = q_ref[:, bs], k_ref[:, ks], v_ref[:, ks]
        acc = jnp.zeros((q_ref.shape[0], LANES), F32)
        for half in range(2):
            ql = _half_select(qb, lo, half) * (HEAD_DIM ** -0.5)
            sink = sink_ref[2 * b + half] if has_sink else None
            acc = acc + _softmax_pv([_dot_nt(ql, k)], [_half_select(v, lo, half)], sink=sink)
        o_ref[:, bs] = acc.astype(o_ref.dtype)


def context_attention(sink, q_arr, q_col, k_arr, k_col, k_w, v_arr, v_col, kv_shared, has_sink):
    lc = q_arr.shape[0]
    return pl.pallas_call(
        functools.partial(_ctx_attn_kernel, kv_shared=kv_shared, has_sink=has_sink),
        grid=(1,),
        in_specs=[pl.BlockSpec(memory_space=pltpu.SMEM),
                  pl.BlockSpec((lc, 4 * LANES), lambda i: (0, q_col)),
                  pl.BlockSpec((lc, k_w), lambda i: (0, k_col)),
                  pl.BlockSpec((lc, k_w), lambda i: (0, v_col))],
        out_specs=pl.BlockSpec((lc, 4 * LANES), lambda i: (0, 0)),
        out_shape=jax.ShapeDtypeStruct((lc, 4 * LANES), BF16),
        compiler_params=_cparams("arbitrary"),
        name="context_attention",
    )(sink, q_arr, k_arr, v_arr)


def _nbr_row_offsets(rows):
    nsteps = rows // C_QROWS
    nblk = C_KROWS * GRID_W // LANES
    wr = min(C_WIN_R, rows)
    reps = [0, 1, 2, nsteps - 2, nsteps - 1]
    dr = -np.ones((len(reps), C_QROWS, C_KROWS), np.int32)
    for v, i in enumerate(reps):
        w0 = C_QROWS * int(np.clip(i - 2, 0, nsteps - nblk))
        for a in range(C_QROWS):
            rq = C_QROWS * i + a
            k0 = int(np.clip(rq - C_WIN_R // 2, 0, rows - wr))
            for j in range(C_KROWS):
                if k0 <= w0 + j < k0 + wr:
                    dr[v, a, j] = w0 + j - rq + C_WIN_R - 1
    return dr


def _nbr_bias(rpb, rows):
    h, n_dr, n_dc = rpb.shape
    qc = np.arange(GRID_W)[:, None]
    kc = np.arange(GRID_W)[None, :]
    cstart = np.clip(qc - C_WIN_C // 2, 0, GRID_W - C_WIN_C)
    col_ok = (kc >= cstart) & (kc < cstart + C_WIN_C)
    dc = np.clip(kc - qc + C_WIN_C - 1, 0, 2 * C_WIN_C - 2)
    dr = _nbr_row_offsets(rows)
    sel_c = (dc[None] == np.arange(n_dc)[:, None, None]).astype(np.float32)
    sel_r = (dr[..., None] == np.arange(n_dr)).astype(np.float32)
    hp = lax.Precision.HIGHEST
    toep = jnp.einsum('hdc,cqk->hdqk', rpb.astype(F32), jnp.asarray(sel_c), precision=hp)
    bias = jnp.einsum('vajd,hdqk->vhaqjk', jnp.asarray(sel_r), toep, precision=hp)
    inside = (dr >= 0)[:, None, :, None, :, None] & col_ok[None, None, None, :, None, :]
    bias = jnp.where(jnp.asarray(inside), bias, NEG_INF)
    return bias.reshape(dr.shape[0], h * C_QROWS * GRID_W, C_KROWS * GRID_W)


def _attn_c_kernel(q_ref, *refs):
    nblk = (len(refs) - 3) // 2
    k_refs, v_refs = refs[:nblk], refs[nblk:2 * nblk]
    kvz_ref, bias_ref, o_ref = refs[-3:]
    lo = _lane_lo()
    nq = q_ref.shape[0]
    nb = C_HEADS // 2
    vcol = C_HEADS * HEAD_DIM
    blocks = [slice(b * LANES, (b + 1) * LANES) for b in range(nb)]
    s_parts, sz_parts = [], []
    for bs in blocks:
        ql = jnp.concatenate([_half_select(q_ref[:, bs], lo, half) * (HEAD_DIM ** -0.5) for half in range(2)],
                             axis=0)
        s_parts.append(_dot_nt(ql, jnp.concatenate([ref[:, bs] for ref in k_refs], axis=0)))
        sz_parts.append(_dot_nt(ql, kvz_ref[:, bs]))
    s = jnp.concatenate(s_parts, axis=0) + bias_ref[...]
    sz = jnp.concatenate(sz_parts, axis=0)
    m = jnp.maximum(s.max(axis=-1, keepdims=True), sz.max(axis=-1, keepdims=True))
    p, pz = jnp.exp(s - m), jnp.exp(sz - m)
    inv = 1.0 / (p.sum(axis=-1, keepdims=True) + pz.sum(axis=-1, keepdims=True))
    p, pz = p.astype(BF16), pz.astype(BF16)
    for b, bs in enumerate(blocks):
        vs = slice(vcol + b * LANES, vcol + (b + 1) * LANES)
        vw, vz = jnp.concatenate([ref[:, bs] for ref in v_refs], axis=0), kvz_ref[:, vs]
        halves = []
        for half in range(2):
            hs = slice((2 * b + half) * nq, (2 * b + half + 1) * nq)
            halves.append((jnp.dot(p[hs], vw, preferred_element_type=F32)
                           + jnp.dot(pz[hs], vz, preferred_element_type=F32)) * inv[hs])
        o_ref[:, bs] = jnp.where(lo, halves[0], halves[1]).astype(o_ref.dtype)


def neighborhood_attention(rest, rest_ctx, bias, layer, q_col, k_col):
    seq = rest.shape[0]
    lc = rest_ctx.shape[0]
    rows = seq // GRID_W
    nsteps = rows // C_QROWS
    nblk = C_KROWS * GRID_W // LANES
    assert nsteps > nblk and C_QROWS * GRID_W == LANES and k_col % 2 == 0

    def first_blk(i):
        return jnp.clip(i - 2, 0, nsteps - nblk)

    def variant(i):
        return jnp.where(i < 2, i, jnp.where(i >= nsteps - 2, i - (nsteps - 2) + 3, 2))

    w = 4 * LANES
    kv_specs = [pl.BlockSpec((LANES, w), functools.partial(lambda i, t, c: (first_blk(i) + t, c), t=t, c=c))
                for c in (k_col, k_col + 1) for t in range(nblk)]
    return pl.pallas_call(
        _attn_c_kernel,
        grid=(nsteps,),
        in_specs=[pl.BlockSpec((LANES, w), lambda i: (i, q_col))] + kv_specs + [
                  pl.BlockSpec((lc, 2 * w), lambda i: (0, k_col // 2)),
                  pl.BlockSpec((None, None, C_HEADS * LANES, nblk * LANES),
                               lambda i: (layer, variant(i), 0, 0))],
        out_specs=pl.BlockSpec((LANES, w), lambda i: (i, 0)),
        out_shape=jax.ShapeDtypeStruct((seq, w), BF16),
        compiler_params=_cparams("arbitrary"),
        name="neighborhood_attention",
    )(rest, *([rest] * (2 * nblk)), rest_ctx, bias)


def _s5_window_scan(sr, si, tabs, cr, ci, reverse):
    n_tiles = sr.shape[0] // SUBLANES
    edge = 0 if reverse else SUBLANES - 1
    out_r, out_i = [None] * n_tiles, [None] * n_tiles
    for step in range(n_tiles):
        tile = (n_tiles - 1 - step) if reverse else step
        xr = sr[tile * SUBLANES:(tile + 1) * SUBLANES]
        xi = si[tile * SUBLANES:(tile + 1) * SUBLANES]
        for lvl, k in enumerate((1, 2, 4)):
            shift = (SUBLANES - k) if reverse else k
            rr, ri = pltpu.roll(xr, shift, 0), pltpu.roll(xi, shift, 0)
            mr, mi = tabs[2 * lvl], tabs[2 * lvl + 1]
            xr, xi = xr + mr * rr - mi * ri, xi + mr * ri + mi * rr
        pr, pi = tabs[6], tabs[7]
        xr, xi = xr + pr * cr - pi * ci, xi + pr * ci + pi * cr
        cr, ci = jnp.broadcast_to(xr[edge:edge + 1], xr.shape), jnp.broadcast_to(xi[edge:edge + 1], xi.shape)
        out_r[tile], out_i[tile] = xr, xi
    return jnp.concatenate(out_r, axis=0), jnp.concatenate(out_i, axis=0), cr, ci


def _s5_kernel(uz_ref, ul_ref, wc_ref, kc_ref, cc_ref, ew_ref, ek_ref, ec_ref, tab_ref, d_ref, *rest, reverse, final):
    if final:
        yfz_ref, yfl_ref, gw_ref, gb_ref, oz_ref, ol_ref, carry_ref, uf_ref, yc_ref, b_ref, k_ref, c_ref = rest
    else:
        oz_ref, ol_ref, carry_ref, uf_ref, yc_ref, b_ref, k_ref, c_ref = rest
    s = pl.program_id(0)
    nblk = b_ref.shape[0]
    ps = b_ref.shape[2] // 2
    gq = LANES // S5_GROUP_CH

    @pl.when(s == 0)
    def _():
        carry_ref[...] = jnp.zeros_like(carry_ref)

        def expand(compact, spread, row_group, per):
            full = jnp.dot(compact, spread, preferred_element_type=F32)
            col_group = (lax.broadcasted_iota(jnp.int32, (1, full.shape[1]), 1) // per) % gq
            return jnp.where(row_group == col_group, full, 0.0).astype(BF16)

        rows = lax.broadcasted_iota(jnp.int32, (wc_ref.shape[1], 1), 0)
        lag_rows_group = (rows // S5_GROUP_CH) % gq
        state_rows_group = (rows // S5_STATE) % gq
        for q in range(nblk):
            b_ref[q] = expand(wc_ref[q], ew_ref[...], lag_rows_group, S5_STATE)
            k_ref[q] = expand(kc_ref[q], ek_ref[...], lag_rows_group, S5_GROUP_CH)
            c_ref[q] = expand(cc_ref[q], ec_ref[...], state_rows_group, S5_GROUP_CH)

    def chunk(u_ref, yf_ref, o_ref):
        t = u_ref.shape[0]
        nw = t // SUBLANES
        row = lax.broadcasted_iota(jnp.int32, (t, 1), 0) & (SUBLANES - 1)
        wrow = lax.broadcasted_iota(jnp.int32, (nw, 1), 0)
        u = u_ref[...]
        uf = u.astype(F32)
        for q in range(nblk):
            uf_ref[q, 0:t] = uf[:, q * LANES:(q + 1) * LANES]
        lagged = [u]
        for k in range(1, SUBLANES):
            if reverse:
                lag = jnp.where(row < SUBLANES - k, pltpu.roll(uf, t - k, 0), 0.0)
            else:
                lag = jnp.where(row >= k, pltpu.roll(uf, k, 0), 0.0)
            lagged.append(lag.astype(BF16))
        outs = []
        for q in range(nblk):
            qs = slice(q * ps, (q + 1) * ps)
            lhs = jnp.concatenate([x[:, q * LANES:(q + 1) * LANES] for x in lagged], axis=1)
            y_loc = jnp.dot(lhs, k_ref[q], preferred_element_type=F32)
            ends = [uf_ref[q, pl.ds(k if reverse else SUBLANES - 1 - k, nw, stride=SUBLANES), :].astype(BF16)
                    for k in range(SUBLANES)]
            sw = jnp.dot(jnp.concatenate(ends, axis=1), b_ref[q], preferred_element_type=F32)
            tabs = [tab_ref[i, :, qs] for i in range(8)]
            cin_r, cin_i = carry_ref[0, :, qs], carry_ref[1, :, qs]
            c_r, c_i, cr, ci = _s5_window_scan(sw[:, :ps], sw[:, ps:], tabs, cin_r, cin_i, reverse)
            carry_ref[0, :, qs] = cr
            carry_ref[1, :, qs] = ci
            if reverse:
                e_r = jnp.where(wrow == nw - 1, cin_r[0:1], pltpu.roll(c_r, nw - 1, 0))
                e_i = jnp.where(wrow == nw - 1, cin_i[0:1], pltpu.roll(c_i, nw - 1, 0))
            else:
                e_r = jnp.where(wrow == 0, cin_r[0:1], pltpu.roll(c_r, 1, 0))
                e_i = jnp.where(wrow == 0, cin_i[0:1], pltpu.roll(c_i, 1, 0))
            y_in = jnp.dot(jnp.concatenate([e_r, e_i], axis=1).astype(BF16), c_ref[q],
                           preferred_element_type=F32)
            for j in range(SUBLANES):
                yc_ref[q, pl.ds(j, nw, stride=SUBLANES), :] = y_in[:, j * LANES:(j + 1) * LANES]
            outs.append(y_loc + yc_ref[q, 0:t])
        out = jnp.concatenate(outs, axis=1)
        if not final:
            o_ref[...] = u.astype(F32) * d_ref[...] + out
        else:
            g = _gelu_tanh(yf_ref[...] + out)
            z = jnp.dot(g.astype(BF16), gw_ref[...], preferred_element_type=F32) + gb_ref[...]
            o_ref[...] = (g * _sigmoid(z)).astype(o_ref.dtype)

    @pl.when(s == 0)
    def _():
        chunk(uz_ref, yfz_ref if final else None, oz_ref)

    @pl.when(s > 0)
    def _():
        chunk(ul_ref, yfl_ref if final else None, ol_ref)


def s5_direction(rest, rest_ctx, u_col, params, which, dskip, reverse, fwd=None, glu=None, t=1024):
    seq = rest.shape[0]
    tz = rest_ctx.shape[0]
    t = min(t, seq)
    nl = seq // t
    w = 4 * LANES
    wc, kc, cc, tabs = params
    _, nq, kq, _ = wc.shape
    gq = LANES // S5_GROUP_CH
    ps2 = 2 * gq * S5_STATE
    p2 = nq * ps2
    e_w = np.zeros((LANES, ps2), np.float32)
    for c in range(2):
        for gl in range(gq):
            for p in range(S5_STATE):
                e_w[c * S5_STATE + p, c * gq * S5_STATE + gl * S5_STATE + p] = 1.0
    e_k = np.zeros((LANES, LANES), np.float32)
    e_c = np.zeros((LANES, SUBLANES * LANES), np.float32)
    for gl in range(gq):
        for o in range(S5_GROUP_CH):
            e_k[o, gl * S5_GROUP_CH + o] = 1.0
            for j in range(SUBLANES):
                e_c[j * S5_GROUP_CH + o, j * LANES + gl * S5_GROUP_CH + o] = 1.0
    spreads = [jnp.asarray(e, BF16) for e in (e_w, e_k, e_c)]
    final = fwd is not None
    if reverse:
        lat = lambda s: (jnp.where(s == 0, nl - 1, nl - s), 0)
    else:
        lat = lambda s: (jnp.maximum(s - 1, 0), 0)
    lat_u = lambda s: (lat(s)[0], u_col)
    const = lambda s: (0, 0)
    table = lambda shape: pl.BlockSpec((None,) + shape, lambda s: (which,) + (0,) * len(shape),
                                       pipeline_mode=pl.Buffered(1))
    in_specs = [pl.BlockSpec((tz, w), lambda s: (0, u_col)),
                pl.BlockSpec((t, w), lat_u),
                table((nq, kq, LANES)),
                table((nq, kq, LANES)),
                table((nq, ps2, LANES)),
                pl.BlockSpec(e_w.shape, const),
                pl.BlockSpec(e_k.shape, const),
                pl.BlockSpec(e_c.shape, const),
                table((8, SUBLANES, p2 // 2)),
                pl.BlockSpec((1, w), const)]
    args = [rest_ctx, rest, wc, kc, cc, *spreads, tabs, dskip.reshape(1, w)]
    if final:
        in_specs += [pl.BlockSpec((tz, w), const), pl.BlockSpec((t, w), lat),
                     pl.BlockSpec((w, w), const), pl.BlockSpec((1, w), const)]
        args += [fwd[0], fwd[1], glu[0], glu[1].reshape(1, w)]
    odt = BF16 if final else F32
    return pl.pallas_call(
        functools.partial(_s5_kernel, reverse=reverse, final=final),
        grid=(nl + 1,),
        in_specs=in_specs,
        out_specs=[pl.BlockSpec((tz, w), const), pl.BlockSpec((t, w), lat)],
        out_shape=[jax.ShapeDtypeStruct((tz, w), odt), jax.ShapeDtypeStruct((seq, w), odt)],
        scratch_shapes=[pltpu.VMEM((2, SUBLANES, p2 // 2), F32),
                        pltpu.VMEM((nq, max(t, tz), LANES), F32),
                        pltpu.VMEM((nq, max(t, tz), LANES), F32),
                        pltpu.VMEM((nq, kq, ps2), BF16),
                        pltpu.VMEM((nq, kq, LANES), BF16),
                        pltpu.VMEM((nq, ps2, SUBLANES * LANES), BF16)],
        compiler_params=_cparams("arbitrary"),
        name="s5_direction",
    )(*args)


def _s5_params(a_re, a_im, log_step, b_re, b_im, c_re, c_im, rev):
    ar, ai = a_re.astype(F32), a_im.astype(F32)
    step = jnp.exp(log_step.astype(F32))[:, None]
    mag = jnp.exp(ar * step)
    ab_re, ab_im = mag * jnp.cos(ai * step), mag * jnp.sin(ai * step)
    den = ar * ar + ai * ai
    nr = ab_re - 1.0
    f_re = (nr * ar + ab_im * ai) / den
    f_im = (ab_im * ar - nr * ai) / den
    br, bi = b_re.astype(F32), b_im.astype(F32)
    bb_re = f_re[..., None] * br - f_im[..., None] * bi
    bb_im = f_re[..., None] * bi + f_im[..., None] * br
    cr, ci = c_re.astype(F32), c_im.astype(F32)
    g, p, h = bb_re.shape
    gq = LANES // h
    nq = g // gq
    nl = SUBLANES

    def cmul(xr, xi, yr, yi):
        return xr * yr - xi * yi, xr * yi + xi * yr

    def powers(xr, xi, n):
        out = [(jnp.ones_like(xr), jnp.zeros_like(xi)), (xr, xi)]
        for _ in range(n - 1):
            out.append(cmul(*out[-1], xr, xi))
        return jnp.stack([o[0] for o in out]), jnp.stack([o[1] for o in out])

    pr, pi = powers(ab_re, ab_im, nl)
    w_re = pr[:nl, ..., None] * bb_re - pi[:nl, ..., None] * bb_im
    w_im = pr[:nl, ..., None] * bb_im + pi[:nl, ..., None] * bb_re

    def rows_lag_group_channel(x):
        x = x.reshape(nl, nq, gq, h, x.shape[-1])
        return jnp.transpose(x, (1, 0, 2, 3, 4)).reshape(nq, nl * gq * h, x.shape[-1])

    wc = jnp.concatenate([rows_lag_group_channel(jnp.swapaxes(w, 2, 3)) for w in (w_re, w_im)], axis=2)
    hp = lax.Precision.HIGHEST
    kk = (jnp.einsum('kgph,gop->kgho', w_re, cr, precision=hp)
          - jnp.einsum('kgph,gop->kgho', w_im, ci, precision=hp))
    kc = jnp.pad(rows_lag_group_channel(kk), ((0, 0), (0, 0), (0, LANES - h)))
    dr = jnp.where(rev, pr[nl:0:-1], pr[1:nl + 1])
    di = jnp.where(rev, pi[nl:0:-1], pi[1:nl + 1])
    cd_re = cr[None] * dr[:, :, None, :] - ci[None] * di[:, :, None, :]
    cd_im = cr[None] * di[:, :, None, :] + ci[None] * dr[:, :, None, :]

    def rows_group_state(cd):
        return jnp.transpose(cd.reshape(nl, nq, gq, h, p), (1, 2, 4, 0, 3)).reshape(nq, gq * p, nl * h)

    cc = jnp.concatenate([rows_group_state(cd_re), -rows_group_state(cd_im)], axis=1)
    qr, qi = powers(pr[nl].reshape(-1), pi[nl].reshape(-1), SUBLANES)
    row = np.arange(SUBLANES)[:, None]
    tabs = []
    for k in (1, 2, 4):
        keep = jnp.where(rev, jnp.asarray(row < SUBLANES - k), jnp.asarray(row >= k))
        tabs += [jnp.where(keep, qr[k][None], 0.0), jnp.where(keep, qi[k][None], 0.0)]
    tabs += [jnp.where(rev, qr[SUBLANES:0:-1], qr[1:SUBLANES + 1]),
             jnp.where(rev, qi[SUBLANES:0:-1], qi[1:SUBLANES + 1])]
    return wc.astype(BF16), kc.astype(BF16), cc.astype(BF16), jnp.stack(tabs)


def _conv_kernel(val_ref, gate_ref, vp_ref, gp_ref, vn_ref, gn_ref, w_ref, b_ref, g_ref, beta_ref, o_ref, hs_ref,
                 *, sub):
    i = pl.program_id(0)
    n = pl.num_programs(0)
    t = val_ref.shape[0]

    def glu(v, g):
        return v[...].astype(F32) * _sigmoid(g[...].astype(F32))

    hs_ref[0:CONV_HALO] = jnp.where(i > 0, glu(vp_ref, gp_ref), 0.0)
    hs_ref[CONV_HALO:CONV_HALO + t] = glu(val_ref, gate_ref)
    hs_ref[CONV_HALO + t:2 * CONV_HALO + t] = jnp.where(i < n - 1, glu(vn_ref, gn_ref), 0.0)
    base = CONV_HALO - CONV_K // 2
    for r0 in range(t // sub):
        acc = jnp.zeros((sub, val_ref.shape[1]), F32) + b_ref[...]
        for r in range(SUBLANES):
            part = None
            for m in range((base + CONV_K - 1) // SUBLANES + 1):
                k = SUBLANES * m + r - base
                if 0 <= k < CONV_K:
                    lo_row = r0 * sub + SUBLANES * m
                    term = hs_ref[lo_row:lo_row + sub + SUBLANES] * w_ref[k:k + 1]
                    part = term if part is None else part + term
            acc = acc + part[r:r + sub]
        mu = jnp.mean(acc, axis=-1, keepdims=True)
        cen = acc - mu
        var = jnp.mean(cen * cen, axis=-1, keepdims=True)
        yn = cen * lax.rsqrt(var + EPS) * g_ref[...] + beta_ref[...]
        o_ref[r0 * sub:(r0 + 1) * sub] = _silu(yn).astype(o_ref.dtype)


def conv_module(rest, val_col, gate_col, w, b, g, beta, t=512, sub=64):
    seq = rest.shape[0]
    t = min(t, seq)
    cw = 4 * LANES
    hb = t // CONV_HALO
    nh = seq // CONV_HALO
    prev = lambda i: jnp.maximum(i * hb - 1, 0)
    nxt = lambda i: jnp.minimum((i + 1) * hb, nh - 1)
    vec = pl.BlockSpec((1, cw), lambda i: (0, 0))
    return pl.pallas_call(
        functools.partial(_conv_kernel, sub=sub),
        grid=(seq // t,),
        in_specs=[pl.BlockSpec((t, cw), lambda i: (i, val_col)),
                  pl.BlockSpec((t, cw), lambda i: (i, gate_col)),
                  pl.BlockSpec((CONV_HALO, cw), lambda i: (prev(i), val_col)),
                  pl.BlockSpec((CONV_HALO, cw), lambda i: (prev(i), gate_col)),
                  pl.BlockSpec((CONV_HALO, cw), lambda i: (nxt(i), val_col)),
                  pl.BlockSpec((CONV_HALO, cw), lambda i: (nxt(i), gate_col)),
                  pl.BlockSpec((CONV_K, cw), lambda i: (0, 0)), vec, vec, vec],
        out_specs=pl.BlockSpec((t, cw), lambda i: (i, 0)),
        out_shape=jax.ShapeDtypeStruct((seq, cw), BF16),
        scratch_shapes=[pltpu.VMEM((t + 2 * CONV_HALO, cw), F32)],
        compiler_params=_cparams("arbitrary"),
        name="conv_module",
    )(rest, rest, rest, rest, rest, rest, w.astype(F32), b.reshape(1, cw).astype(F32),
      g.reshape(1, cw).astype(F32), beta.reshape(1, cw).astype(F32))


def _merge_kernel(*refs, n_parts, cast_mlp):
    y_refs = refs[:N_BRANCH]
    gl_refs = refs[N_BRANCH:N_BRANCH * (1 + n_parts)]
    rest = refs[N_BRANCH * (1 + n_parts):]
    if cast_mlp:
        wb_ref, wo_ref, x_ref, gate_ref, g_ref, sc_ref, sh_ref, w1_ref, w2_ref, o_ref, h_ref, w1b_ref, w2b_ref = rest
        w1b_ref[...] = w1_ref[...].astype(BF16)
        w2b_ref[...] = w2_ref[...].astype(BF16)
    else:
        wb_ref, wo_ref, x_ref, gate_ref, g_ref, sc_ref, sh_ref, o_ref, h_ref = rest
    pw = gl_refs[0].shape[1]
    parts = []
    for part in range(n_parts):
        cols = slice(part * pw, (part + 1) * pw)
        m = None
        for k in range(N_BRANCH):
            proj = jnp.dot(y_refs[k][...], wb_ref[k, :, cols], preferred_element_type=F32)
            term = _sigmoid(gl_refs[k * n_parts + part][...].astype(F32)) * proj
            m = term if m is None else m + term
        parts.append(m.astype(BF16))
    merged = jnp.concatenate(parts, axis=1)
    x = x_ref[...] + gate_ref[...] * jnp.dot(merged, wo_ref[...], preferred_element_type=F32)
    o_ref[...] = x
    y = x * lax.rsqrt(jnp.mean(x * x, axis=-1, keepdims=True) + EPS) * g_ref[...]
    h_ref[...] = (y * (1.0 + sc_ref[...]) + sh_ref[...]).astype(h_ref.dtype)


def gated_merge_residual(ys, rest, logit_col, part_w, wb, wo, layer, x, gate, g2, scale2, shift2, mlp_w=None, tm=256):
    m, yw = ys[0].shape
    n = wb.shape[3]
    n_parts = n // part_w
    tm = min(tm, m)
    steps = m // tm
    resident = lambda shape: pl.BlockSpec((None,) + shape, lambda i: (layer,) + (0,) * len(shape),
                                          pipeline_mode=pl.Buffered(1))
    yspec = pl.BlockSpec((tm, yw), lambda i: (i, 0))
    gspecs = [pl.BlockSpec((tm, part_w), functools.partial(lambda i, c: (i, c), c=logit_col + k * n_parts + part))
              for k in range(N_BRANCH) for part in range(n_parts)]
    vec = pl.BlockSpec((1, n), lambda i: (0, 0))
    rows = pl.BlockSpec((tm, n), lambda i: (i, 0))
    in_specs = [yspec] * N_BRANCH + gspecs + [resident((N_BRANCH, yw, n)), resident((n, n)), rows,
                                              vec, vec, vec, vec]
    args = [*ys, *([rest] * (N_BRANCH * n_parts)), wb, wo, x, gate.reshape(1, n), g2.reshape(1, n),
            scale2.reshape(1, n), shift2.reshape(1, n)]
    out_specs = [rows, rows]
    out_shape = [jax.ShapeDtypeStruct((m, n), F32), jax.ShapeDtypeStruct((m, n), BF16)]
    if mlp_w is not None:
        w1, w2 = mlp_w
        ff = w1.shape[2]
        slab = ff // steps
        assert ff % steps == 0 and slab % LANES == 0
        in_specs += [pl.BlockSpec((None, n, slab), lambda i: (layer, 0, i)),
                     pl.BlockSpec((None, slab, n), lambda i: (layer, i, 0))]
        args += [w1, w2]
        out_specs += [pl.BlockSpec((n, slab), lambda i: (0, i)), pl.BlockSpec((slab, n), lambda i: (i, 0))]
        out_shape += [jax.ShapeDtypeStruct((n, ff), BF16), jax.ShapeDtypeStruct((ff, n), BF16)]
    return pl.pallas_call(
        functools.partial(_merge_kernel, n_parts=n_parts, cast_mlp=mlp_w is not None),
        grid=(steps,),
        in_specs=in_specs,
        out_specs=out_specs,
        out_shape=out_shape,
        compiler_params=_cparams("arbitrary"),
        name="gated_merge_residual",
    )(*args)


def _mlp_kernel(x_ref, h_ref, gate_ref, w1_ref, w2_ref, *rest, final_norm):
    if final_norm:
        gf_ref, o_ref, acc_ref = rest
    else:
        o_ref, acc_ref = rest
    f = pl.program_id(1)

    @pl.when(f == 0)
    def _():
        acc_ref[...] = jnp.zeros_like(acc_ref)

    t = jnp.dot(h_ref[...], w1_ref[...], preferred_element_type=F32)
    t = jnp.square(jnp.maximum(t, 0.0)).astype(BF16)
    acc_ref[...] += jnp.dot(t, w2_ref[...], preferred_element_type=F32)

    @pl.when(f == pl.num_programs(1) - 1)
    def _():
        y = x_ref[...] + gate_ref[...] * acc_ref[...]
        if final_norm:
            y = y * lax.rsqrt(jnp.mean(y * y, axis=-1, keepdims=True) + EPS) * gf_ref[...]
        o_ref[...] = y


def mlp_block(x, h, gate, w1, w2, final_g=None, tm=512, tf=1024):
    m, d = x.shape
    ff = w1.shape[1]
    tm = min(tm, m)
    vec = pl.BlockSpec((1, d), lambda i, f: (0, 0))
    rows = pl.BlockSpec((tm, d), lambda i, f: (i, 0))
    final_norm = final_g is not None
    args = [x, h, gate.reshape(1, d), w1, w2]
    in_specs = [rows, rows, vec,
                pl.BlockSpec((d, tf), lambda i, f: (0, f)),
                pl.BlockSpec((tf, d), lambda i, f: (f, 0))]
    if final_norm:
        args.append(final_g.reshape(1, d))
        in_specs.append(vec)
    return pl.pallas_call(
        functools.partial(_mlp_kernel, final_norm=final_norm),
        grid=(m // tm, ff // tf),
        in_specs=in_specs,
        out_specs=pl.BlockSpec((tm, d), lambda i, f: (i, 0)),
        out_shape=jax.ShapeDtypeStruct((m, d), F32),
        scratch_shapes=[pltpu.VMEM((tm, d), F32)],
        compiler_params=_cparams("arbitrary", "arbitrary"),
        name="mlp_block",
    )(*args)


_A_COL_K, _A_COL_V = 2, 3
_REST_COL0 = (A_HEADS + 2 * A_KV_HEADS) * HEAD_DIM
_COL_U, _COL_QC, _COL_KC, _COL_VC, _COL_VAL, _COL_GATE = 0, 1, 2, 3, 4, 5
_MERGE_PART = 1024
_COL_MERGE = 3


def _rope_tables(seq):
    freqs = ROPE_BASE ** (-jnp.arange(ROPE_PAIRS, dtype=F32) / ROPE_PAIRS)
    rows = seq // GRID_W
    ang_r = jnp.arange(rows).astype(F32)[:, None] * freqs
    ang_c = jnp.arange(GRID_W).astype(F32)[:, None] * freqs

    def expand(tab_r, tab_c, first, second):
        r = jnp.repeat(tab_r, GRID_W, axis=0)
        cc = jnp.tile(tab_c, (rows, 1))
        head = jnp.concatenate([first * r, second * r, first * cc, second * cc], axis=1)
        return jnp.tile(head, (1, LANES // HEAD_DIM))

    cos = expand(jnp.cos(ang_r), jnp.cos(ang_c), 1.0, 1.0)
    sa = expand(jnp.sin(ang_r), jnp.sin(ang_c), 0.0, 1.0)
    sb = expand(jnp.sin(ang_r), jnp.sin(ang_c), -1.0, 0.0)
    return cos, sa, sb


def kernel(x, c, ctx, c_ctx, ada_w, ada_b, norm1_g, norm2_g, w_in, a_sink, s5_a_re, s5_a_im, s5_log_step, s5_b_re, s5_b_im, s5_c_re, s5_c_im, s5_d, s5_glu_w, s5_glu_b, c_rpb, d_conv_w, d_conv_b, d_ln_g, d_ln_b, w_branch, w_out, mlp_w1, mlp_w2, final_g):
    depth = ada_w.shape[0]
    _, seq, d = x.shape
    lc = ctx.shape[1]
    xs = x.reshape(seq, d)
    zs = ctx.reshape(lc, d)
    rope = _rope_tables(seq)

    cc = jnp.zeros((SUBLANES, d), F32).at[0].set(c[0]).at[1].set(c_ctx)
    mods = ada_modulation(cc, ada_w, ada_b)
    n_rest = w_in.shape[2] - _REST_COL0
    wb, wo = w_branch.astype(BF16), w_out.astype(BF16)
    nbr_bias = jax.vmap(lambda rpb: _nbr_bias(rpb, seq // GRID_W))(c_rpb)
    s5_params = jax.vmap(_s5_params)(*(a.reshape((2 * depth,) + a.shape[2:]) for a in
                                       (s5_a_re, s5_a_im, s5_log_step, s5_b_re, s5_b_im, s5_c_re, s5_c_im)),
                                     jnp.arange(2 * depth) % 2 == 1)

    for l in range(depth):
        need_ctx = l < depth - 1
        mx = [mods[l, 0, k * d:(k + 1) * d] for k in range(6)]
        mz = [mods[l, 1, k * d:(k + 1) * d] for k in range(6)]

        hx, qk_x = norm_head_projection(xs, norm1_g[l], mx[1], mx[0], w_in, l, rope=rope)
        hz, qk_z = norm_head_projection(zs, norm1_g[l], mz[1], mz[0], w_in, l)
        r_x = wide_projection(hx, w_in, l, _REST_COL0, n_rest)
        r_z = wide_projection(hz, w_in, l, _REST_COL0, n_rest if need_ctx else (_COL_VC + 1) * 4 * LANES)

        ya_x = window_attention(a_sink[l], qk_x, qk_z)
        yc_x = neighborhood_attention(r_x, r_z, nbr_bias, l, _COL_QC, _COL_KC)
        yd_x = conv_module(r_x, _COL_VAL, _COL_GATE, d_conv_w[l], d_conv_b[l], d_ln_g[l], d_ln_b[l])

        yf = s5_direction(r_x, r_z, _COL_U, s5_params, 2 * l, s5_d[l].astype(F32), reverse=False)
        yb_z, yb_x = s5_direction(r_x, r_z, _COL_U, s5_params, 2 * l + 1, s5_d[l].astype(F32), reverse=True, fwd=yf,
                                  glu=(s5_glu_w[l].astype(BF16), s5_glu_b[l].astype(F32)))

        xs, h2, w1, w2 = gated_merge_residual((ya_x, yb_x, yc_x, yd_x), r_x, _COL_MERGE, _MERGE_PART, wb, wo, l, xs,
                                              mx[2], norm2_g[l], mx[4], mx[3], mlp_w=(mlp_w1, mlp_w2))
        xs = mlp_block(xs, h2, mx[5], w1, w2, final_g=None if need_ctx else final_g)

        if need_ctx:
            ya_z = context_attention(a_sink[l], qk_z, 0, qk_z, _A_COL_K, 2 * LANES, qk_z, _A_COL_V, True, True)
            yc_z = context_attention(a_sink[l], r_z, _COL_QC, r_z, _COL_KC, 4 * LANES, r_z, _COL_VC, False, False)
            yd_z = conv_module(r_z, _COL_VAL, _COL_GATE, d_conv_w[l], d_conv_b[l], d_ln_g[l], d_ln_b[l])
            zs, h2 = gated_merge_residual((ya_z, yb_z, yc_z, yd_z), r_z, _COL_MERGE, _MERGE_PART, wb, wo, l, zs,
                                          mz[2], norm2_g[l], mz[4], mz[3])
            zs = mlp_block(zs, h2, mz[5], w1, w2)

    return xs.reshape(x.shape)
```

```python
import functools
import math

import numpy as np
import jax
import jax.numpy as jnp
from jax import lax
from jax.experimental import pallas as pl
from jax.experimental.pallas import tpu as pltpu

F32 = jnp.float32
BF16 = jnp.bfloat16

LANES = 128
SUBLANES = 8
VMEM_LIMIT = 56 * 1024 * 1024

GRID_W = 64
HEAD_DIM = 64
N_BRANCH = 4
A_HEADS = 8
A_KV_HEADS = 2
A_WINDOW = 128
A_BLOCK = 128
ROPE_BASE = 10000.0
ROPE_PAIRS = HEAD_DIM // 4
S5_GROUP_CH = 16
S5_STATE = 64
C_HEADS = 8
C_WIN_R = 8
C_WIN_C = 16
C_QROWS = 2
C_KROWS = C_QROWS + C_WIN_R
CONV_K = 31
CONV_HALO = 16
EPS = 1e-6
NEG_INF = -1e30


def _cparams(*sem):
    return pltpu.CompilerParams(dimension_semantics=sem, vmem_limit_bytes=VMEM_LIMIT)


def _sigmoid(x):
    return 0.5 * jnp.tanh(0.5 * x) + 0.5


def _silu(x):
    return x * _sigmoid(x)


def _gelu_tanh(x):
    return 0.5 * x * (1.0 + jnp.tanh(math.sqrt(2.0 / math.pi) * (x + 0.044715 * (x * x * x))))


def _ada_kernel(c_ref, w_ref, b_ref, o_ref):
    a = _silu(c_ref[...]).astype(BF16)
    o_ref[...] = jnp.dot(a, w_ref[...].astype(BF16), preferred_element_type=F32) + b_ref[...]


def ada_modulation(cc, ada_w, ada_b, tn=2048):
    depth, d, n = ada_w.shape
    return pl.pallas_call(
        _ada_kernel,
        grid=(depth, n // tn),
        in_specs=[pl.BlockSpec((SUBLANES, d), lambda l, j: (0, 0)),
                  pl.BlockSpec((None, d, tn), lambda l, j: (l, 0, j)),
                  pl.BlockSpec((None, 1, tn), lambda l, j: (l, 0, j))],
        out_specs=pl.BlockSpec((None, SUBLANES, tn), lambda l, j: (l, 0, j)),
        out_shape=jax.ShapeDtypeStruct((depth, SUBLANES, n), F32),
        compiler_params=_cparams("arbitrary", "arbitrary"),
        name="ada_modulation",
    )(cc, ada_w, ada_b.reshape(depth, 1, n))


def _rope_block(y, cos, sa, sb):
    return y * cos + pltpu.roll(y, 16, 1) * sa + pltpu.roll(y, LANES - 16, 1) * sb


def _head_proj_kernel(x_ref, g_ref, sc_ref, sh_ref, w_ref, *rest, n_q, with_rope):
    if with_rope:
        cos_ref, sa_ref, sb_ref, h_ref, o_ref, wb_ref = rest
        cos, sa, sb = cos_ref[...], sa_ref[...], sb_ref[...]
    else:
        h_ref, o_ref, wb_ref = rest

    @pl.when(pl.program_id(0) == 0)
    def _():
        wb_ref[...] = w_ref[0].astype(BF16)

    x = x_ref[...]
    y = x * lax.rsqrt(jnp.mean(x * x, axis=-1, keepdims=True) + EPS) * g_ref[...]
    h = (y * (1.0 + sc_ref[...]) + sh_ref[...]).astype(BF16)
    h_ref[...] = h
    acc = jnp.dot(h, wb_ref[...], preferred_element_type=F32)
    lo = _lane_lo()
    out = 0
    for b in range(acc.shape[1] // LANES):
        y = acc[:, b * LANES:(b + 1) * LANES]
        if with_rope and b <= n_q:
            y = _rope_block(y, cos, sa, sb)
        if b < n_q:
            blocks = [y]
        else:
            swapped = pltpu.roll(y, HEAD_DIM, 1)
            blocks = [jnp.where(lo, y, swapped), jnp.where(lo, swapped, y)]
        for blk in blocks:
            o_ref[:, out * LANES:(out + 1) * LANES] = blk.astype(o_ref.dtype)
            out += 1


def norm_head_projection(x, g, scale, shift, w_in, layer, rope=None, tm=1024):
    m, k = x.shape
    n_in = (A_HEADS + 2 * A_KV_HEADS) * HEAD_DIM
    n_q = A_HEADS * HEAD_DIM // LANES
    assert A_KV_HEADS * HEAD_DIM == LANES
    n_out = n_in + 2 * LANES
    tm = min(tm, m)
    vec = pl.BlockSpec((1, k), lambda i: (0, 0))
    in_specs = [pl.BlockSpec((tm, k), lambda i: (i, 0)), vec, vec, vec,
                pl.BlockSpec((pl.Element(1), pl.Element(k), pl.Element(n_in)), lambda i: (layer, 0, 0))]
    args = [x, g.reshape(1, k), scale.reshape(1, k), shift.reshape(1, k), w_in]
    if rope is not None:
        in_specs += [pl.BlockSpec((tm, LANES), lambda i: (i, 0))] * 3
        args += list(rope)
    return pl.pallas_call(
        functools.partial(_head_proj_kernel, n_q=n_q, with_rope=rope is not None),
        grid=(m // tm,),
        in_specs=in_specs,
        out_specs=[pl.BlockSpec((tm, k), lambda i: (i, 0)), pl.BlockSpec((tm, n_out), lambda i: (i, 0))],
        out_shape=[jax.ShapeDtypeStruct((m, k), BF16), jax.ShapeDtypeStruct((m, n_out), BF16)],
        scratch_shapes=[pltpu.VMEM((k, n_in), BF16)],
        compiler_params=_cparams("arbitrary"),
        name="norm_head_projection",
    )(*args)


def _wide_proj_kernel(a_ref, w_ref, o_ref, wb_ref):
    @pl.when(pl.program_id(1) == 0)
    def _():
        wb_ref[...] = w_ref[0].astype(BF16)

    o_ref[...] = jnp.dot(a_ref[...], wb_ref[...], preferred_element_type=F32).astype(o_ref.dtype)


def wide_projection(a, w, layer, col0, n, tm=2048, tn=1024):
    m, k = a.shape
    tm = min(tm, m)
    assert n % tn == 0 and col0 % LANES == 0 and tn % LANES == 0
    return pl.pallas_call(
        _wide_proj_kernel,
        grid=(n // tn, m // tm),
        in_specs=[pl.BlockSpec((tm, k), lambda j, i: (i, 0)),
                  pl.BlockSpec((pl.Element(1), pl.Element(k), pl.Element(tn)),
                               lambda j, i: (layer, 0, pl.multiple_of(col0 + j * tn, LANES)))],
        out_specs=pl.BlockSpec((tm, tn), lambda j, i: (i, j)),
        out_shape=jax.ShapeDtypeStruct((m, n), BF16),
        scratch_shapes=[pltpu.VMEM((k, tn), BF16)],
        compiler_params=_cparams("arbitrary", "arbitrary"),
        name="wide_projection",
    )(a, w)


def _lane_lo():
    return lax.broadcasted_iota(jnp.int32, (1, LANES), 1) < HEAD_DIM


def _half_select(x, lo, half):
    zero = jnp.zeros_like(x)
    return jnp.where(lo, x, zero) if half == 0 else jnp.where(lo, zero, x)


def _dot_nt(a, b):
    return lax.dot_general(a, b, (((1,), (1,)), ((), ())), preferred_element_type=F32)


def _softmax_pv(parts, values, sink=None):
    m = parts[0].max(axis=-1, keepdims=True)
    for s in parts[1:]:
        m = jnp.maximum(m, s.max(axis=-1, keepdims=True))
    if sink is not None:
        m = jnp.maximum(m, sink)
    den = jnp.exp(sink - m) if sink is not None else 0.0
    out = 0.0
    for s, v in zip(parts, values):
        p = jnp.exp(s - m)
        den = den + p.sum(axis=-1, keepdims=True)
        out = out + jnp.dot(p.astype(BF16), v, preferred_element_type=F32)
    return out / den


def _attn_a_window_start(i, seq):
    return jnp.clip((i - 1) * A_BLOCK, 0, seq - 3 * A_BLOCK)


def _attn_a_masks(nb):
    nq = 4 * A_BLOCK
    r = np.arange(nq)[:, None] % A_BLOCK
    c = np.arange(3 * A_BLOCK)[None, :]
    out = np.zeros((3, nq, 3 * A_BLOCK), np.float32)
    for v, start in enumerate((0, -A_BLOCK, -2 * A_BLOCK)):
        dist = c - r + start
        out[v] = np.where((dist >= -A_WINDOW) & (dist <= A_WINDOW), 0.0, NEG_INF)
    return out


def _attn_a_kernel(sink_ref, q_ref, kv0_ref, kv1_ref, kv2_ref, kvz_ref, mask_ref, o_ref):
    lo = _lane_lo()
    nq = 4 * A_BLOCK
    rowblk = lax.broadcasted_iota(jnp.int32, (nq, 1), 0) // A_BLOCK
    vcol = A_KV_HEADS * LANES
    for g in range(A_KV_HEADS):
        gs = slice(g * LANES, (g + 1) * LANES)
        vs = slice(vcol + g * LANES, vcol + (g + 1) * LANES)
        kw = jnp.concatenate([ref[:, gs] for ref in (kv0_ref, kv1_ref, kv2_ref)], axis=0)
        vw = jnp.concatenate([ref[:, vs] for ref in (kv0_ref, kv1_ref, kv2_ref)], axis=0)
        kz, vz = kvz_ref[:, gs], kvz_ref[:, vs]
        blocks = [slice((2 * g + bb) * LANES, (2 * g + bb + 1) * LANES) for bb in range(2)]
        ql = jnp.concatenate([_half_select(q_ref[:, bs], lo, half) * (HEAD_DIM ** -0.5)
                              for half in range(2) for bs in blocks], axis=0)
        heads = [2 * (2 * g + bb) + half for half in range(2) for bb in range(2)]
        sink = jnp.full((nq, 1), sink_ref[heads[3]], F32)
        for k in range(3):
            sink = jnp.where(rowblk == k, sink_ref[heads[k]], sink)
        s = _dot_nt(ql, kw) + mask_ref[...]
        sz = _dot_nt(ql, kz)
        m = jnp.maximum(jnp.maximum(s.max(axis=-1, keepdims=True), sz.max(axis=-1, keepdims=True)), sink)
        p, pz = jnp.exp(s - m), jnp.exp(sz - m)
        inv = 1.0 / (p.sum(axis=-1, keepdims=True) + pz.sum(axis=-1, keepdims=True) + jnp.exp(sink - m))
        p, pz = p.astype(BF16), pz.astype(BF16)
        halves = []
        for half in range(2):
            hs = slice(half * 2 * A_BLOCK, (half + 1) * 2 * A_BLOCK)
            halves.append((jnp.dot(p[hs], vw, preferred_element_type=F32)
                           + jnp.dot(pz[hs], vz, preferred_element_type=F32)) * inv[hs])
        out = jnp.where(lo, halves[0], halves[1])
        for bb in range(2):
            o_ref[:, blocks[bb]] = out[bb * A_BLOCK:(bb + 1) * A_BLOCK].astype(o_ref.dtype)


def window_attention(sink, qk, qk_ctx):
    seq = qk.shape[0]
    lc = qk_ctx.shape[0]
    nb = seq // A_BLOCK
    qw = A_HEADS * HEAD_DIM
    kvw = 4 * A_KV_HEADS * HEAD_DIM
    assert nb >= 3 and qw == kvw
    kv_specs = [pl.BlockSpec((A_BLOCK, kvw), functools.partial(
        lambda i, t: (_attn_a_window_start(i, seq) // A_BLOCK + t, 1), t=t)) for t in range(3)]
    masks = jnp.asarray(_attn_a_masks(nb))
    placement = lambda i: jnp.where(i == 0, 0, jnp.where(i == nb - 1, 2, 1))
    return pl.pallas_call(
        _attn_a_kernel,
        grid=(nb,),
        in_specs=[pl.BlockSpec(memory_space=pltpu.SMEM),
                  pl.BlockSpec((A_BLOCK, qw), lambda i: (i, 0))] + kv_specs + [
                  pl.BlockSpec((lc, kvw), lambda i: (0, 1)),
                  pl.BlockSpec((None,) + masks.shape[1:], lambda i: (placement(i), 0, 0))],
        out_specs=pl.BlockSpec((A_BLOCK, qw), lambda i: (i, 0)),
        out_shape=jax.ShapeDtypeStruct((seq, qw), BF16),
        compiler_params=_cparams("arbitrary"),
        name="window_attention",
    )(sink, qk, qk, qk, qk, qk_ctx, masks)


def _ctx_attn_kernel(sink_ref, q_ref, k_ref, v_ref, o_ref, *, kv_shared, has_sink):
    lo = _lane_lo()
    for b in range(q_ref.shape[1] // LANES):
        bs = slice(b * LANES, (b + 1) * LANES)
        kb = b // 2 if kv_shared else b
        ks = slice(kb * LANES, (kb + 1) * LANES)
        qb, k, v = q_ref[:, bs], k_ref[:, ks], v_ref[:, ks]
        acc = jnp.zeros((q_ref.shape[0], LANES), F32)
        for half in range(2):
            ql = _half_select(qb, lo, half) * (HEAD_DIM ** -0.5)
            sink = sink_ref[2 * b + half] if has_sink else None
            acc = acc + _softmax_pv([_dot_nt(ql, k)], [_half_select(v, lo, half)], sink=sink)
        o_ref[:, bs] = acc.astype(o_ref.dtype)


def context_attention(sink, q_arr, q_col, k_arr, k_col, k_w, v_arr, v_col, kv_shared, has_sink):
    lc = q_arr.shape[0]
    return pl.pallas_call(
        functools.partial(_ctx_attn_kernel, kv_shared=kv_shared, has_sink=has_sink),
        grid=(1,),
        in_specs=[pl.BlockSpec(memory_space=pltpu.SMEM),
                  pl.BlockSpec((lc, 4 * LANES), lambda i: (0, q_col)),
                  pl.BlockSpec((lc, k_w), lambda i: (0, k_col)),
                  pl.BlockSpec((lc, k_w), lambda i: (0, v_col))],
        out_specs=pl.BlockSpec((lc, 4 * LANES), lambda i: (0, 0)),
        out_shape=jax.ShapeDtypeStruct((lc, 4 * LANES), BF16),
        compiler_params=_cparams("arbitrary"),
        name="context_attention",
    )(sink, q_arr, k_arr, v_arr)


def _nbr_row_offsets(rows):
    nsteps = rows // C_QROWS
    nblk = C_KROWS * GRID_W // LANES
    wr = min(C_WIN_R, rows)
    reps = [0, 1, 2, nsteps - 2, nsteps - 1]
    dr = -np.ones((len(reps), C_QROWS, C_KROWS), np.int32)
    for v, i in enumerate(reps):
        w0 = C_QROWS * int(np.clip(i - 2, 0, nsteps - nblk))
        for a in range(C_QROWS):
            rq = C_QROWS * i + a
            k0 = int(np.clip(rq - C_WIN_R // 2, 0, rows - wr))
            for j in range(C_KROWS):
                if k0 <= w0 + j < k0 + wr:
                    dr[v, a, j] = w0 + j - rq + C_WIN_R - 1
    return dr


def _nbr_bias(rpb, rows):
    h = rpb.shape[0]
    qc = np.arange(GRID_W)[:, None]
    kc = np.arange(GRID_W)[None, :]
    cstart = np.clip(qc - C_WIN_C // 2, 0, GRID_W - C_WIN_C)
    col_ok = jnp.asarray((kc >= cstart) & (kc < cstart + C_WIN_C))
    pad = GRID_W - C_WIN_C
    ext = jnp.pad(rpb.astype(F32), ((0, 0), (0, 0), (pad, pad)), mode='edge')
    toep = jnp.stack([ext[:, :, GRID_W - 1 - q:2 * GRID_W - 1 - q] for q in range(GRID_W)], axis=2)
    toep = jnp.where(col_ok, toep, NEG_INF)
    masked = jnp.full((h, GRID_W, GRID_W), NEG_INF, F32)
    dr = _nbr_row_offsets(rows)
    variants = []
    for v in range(dr.shape[0]):
        qrows = [jnp.concatenate([toep[:, dr[v, a, j]] if dr[v, a, j] >= 0 else masked for j in range(C_KROWS)],
                                 axis=-1) for a in range(C_QROWS)]
        variants.append(jnp.concatenate(qrows, axis=-2).reshape(h * C_QROWS * GRID_W, C_KROWS * GRID_W))
    return jnp.stack(variants)


def _attn_c_kernel(q_ref, *refs):
    nblk = (len(refs) - 3) // 2
    k_refs, v_refs = refs[:nblk], refs[nblk:2 * nblk]
    kvz_ref, bias_ref, o_ref = refs[-3:]
    lo = _lane_lo()
    nq = q_ref.shape[0]
    nb = C_HEADS // 2
    vcol = C_HEADS * HEAD_DIM
    blocks = [slice(b * LANES, (b + 1) * LANES) for b in range(nb)]
    s_parts, sz_parts = [], []
    for bs in blocks:
        ql = jnp.concatenate([_half_select(q_ref[:, bs], lo, half) * (HEAD_DIM ** -0.5) for half in range(2)],
                             axis=0)
        s_parts.append(_dot_nt(ql, jnp.concatenate([ref[:, bs] for ref in k_refs], axis=0)))
        sz_parts.append(_dot_nt(ql, kvz_ref[:, bs]))
    s = jnp.concatenate(s_parts, axis=0) + bias_ref[...]
    sz = jnp.concatenate(sz_parts, axis=0)
    m = jnp.maximum(s.max(axis=-1, keepdims=True), sz.max(axis=-1, keepdims=True))
    p, pz = jnp.exp(s - m), jnp.exp(sz - m)
    inv = 1.0 / (p.sum(axis=-1, keepdims=True) + pz.sum(axis=-1, keepdims=True))
    p, pz = p.astype(BF16), pz.astype(BF16)
    for b, bs in enumerate(blocks):
        vs = slice(vcol + b * LANES, vcol + (b + 1) * LANES)
        vw, vz = jnp.concatenate([ref[:, bs] for ref in v_refs], axis=0), kvz_ref[:, vs]
        halves = []
        for half in range(2):
            hs = slice((2 * b + half) * nq, (2 * b + half + 1) * nq)
            halves.append((jnp.dot(p[hs], vw, preferred_element_type=F32)
                           + jnp.dot(pz[hs], vz, preferred_element_type=F32)) * inv[hs])
        o_ref[:, bs] = jnp.where(lo, halves[0], halves[1]).astype(o_ref.dtype)


def neighborhood_attention(rest, rest_ctx, bias, layer, q_col, k_col):
    seq = rest.shape[0]
    lc = rest_ctx.shape[0]
    rows = seq // GRID_W
    nsteps = rows // C_QROWS
    nblk = C_KROWS * GRID_W // LANES
    assert nsteps > nblk and C_QROWS * GRID_W == LANES and k_col % 2 == 0

    def first_blk(i):
        return jnp.clip(i - 2, 0, nsteps - nblk)

    def variant(i):
        return jnp.where(i < 2, i, jnp.where(i >= nsteps - 2, i - (nsteps - 2) + 3, 2))

    w = 4 * LANES
    kv_specs = [pl.BlockSpec((LANES, w), functools.partial(lambda i, t, c: (first_blk(i) + t, c), t=t, c=c))
                for c in (k_col, k_col + 1) for t in range(nblk)]
    return pl.pallas_call(
        _attn_c_kernel,
        grid=(nsteps,),
        in_specs=[pl.BlockSpec((LANES, w), lambda i: (i, q_col))] + kv_specs + [
                  pl.BlockSpec((lc, 2 * w), lambda i: (0, k_col // 2)),
                  pl.BlockSpec((None, None, C_HEADS * LANES, nblk * LANES),
                               lambda i: (layer, variant(i), 0, 0))],
        out_specs=pl.BlockSpec((LANES, w), lambda i: (i, 0)),
        out_shape=jax.ShapeDtypeStruct((seq, w), BF16),
        compiler_params=_cparams("arbitrary"),
        name="neighborhood_attention",
    )(rest, *([rest] * (2 * nblk)), rest_ctx, bias)


def _s5_window_scan(sr, si, tabs, cr, ci, reverse):
    n_tiles = sr.shape[0] // SUBLANES
    edge = 0 if reverse else SUBLANES - 1
    out_r, out_i = [None] * n_tiles, [None] * n_tiles
    for step in range(n_tiles):
        tile = (n_tiles - 1 - step) if reverse else step
        xr = sr[tile * SUBLANES:(tile + 1) * SUBLANES]
        xi = si[tile * SUBLANES:(tile + 1) * SUBLANES]
        for lvl, k in enumerate((1, 2, 4)):
            shift = (SUBLANES - k) if reverse else k
            rr, ri = pltpu.roll(xr, shift, 0), pltpu.roll(xi, shift, 0)
            mr, mi = tabs[2 * lvl], tabs[2 * lvl + 1]
            xr, xi = xr + mr * rr - mi * ri, xi + mr * ri + mi * rr
        pr, pi = tabs[6], tabs[7]
        xr, xi = xr + pr * cr - pi * ci, xi + pr * ci + pi * cr
        cr, ci = jnp.broadcast_to(xr[edge:edge + 1], xr.shape), jnp.broadcast_to(xi[edge:edge + 1], xi.shape)
        out_r[tile], out_i[tile] = xr, xi
    return jnp.concatenate(out_r, axis=0), jnp.concatenate(out_i, axis=0), cr, ci


def _s5_kernel(uz_ref, ul_ref, wc_ref, kc_ref, cc_ref, ew_ref, ek_ref, ec_ref, tab_ref, d_ref, *rest, reverse, final):
    if final:
        yfz_ref, yfl_ref, gw_ref, gb_ref, oz_ref, ol_ref, carry_ref, uf_ref, yc_ref, b_ref, k_ref, c_ref = rest
    else:
        oz_ref, ol_ref, carry_ref, uf_ref, yc_ref, b_ref, k_ref, c_ref = rest
    s = pl.program_id(0)
    nblk = b_ref.shape[0]
    ps = b_ref.shape[2] // 2
    gq = LANES // S5_GROUP_CH

    @pl.when(s == 0)
    def _():
        carry_ref[...] = jnp.zeros_like(carry_ref)

        def expand(compact, spread, row_group, per):
            full = jnp.dot(compact, spread, preferred_element_type=F32)
            col_group = (lax.broadcasted_iota(jnp.int32, (1, full.shape[1]), 1) // per) % gq
            return jnp.where(row_group == col_group, full, 0.0).astype(BF16)

        rows = lax.broadcasted_iota(jnp.int32, (wc_ref.shape[1], 1), 0)
        lag_rows_group = (rows // S5_GROUP_CH) % gq
        state_rows_group = (rows // S5_STATE) % gq
        for q in range(nblk):
            b_ref[q] = expand(wc_ref[q], ew_ref[...], lag_rows_group, S5_STATE)
            k_ref[q] = expand(kc_ref[q], ek_ref[...], lag_rows_group, S5_GROUP_CH)
            c_ref[q] = expand(cc_ref[q], ec_ref[...], state_rows_group, S5_GROUP_CH)

    def chunk(u_ref, yf_ref, o_ref):
        t = u_ref.shape[0]
        nw = t // SUBLANES
        row = lax.broadcasted_iota(jnp.int32, (t, 1), 0) & (SUBLANES - 1)
        wrow = lax.broadcasted_iota(jnp.int32, (nw, 1), 0)
        u = u_ref[...]
        uf = u.astype(F32)
        for q in range(nblk):
            uf_ref[q, 0:t] = uf[:, q * LANES:(q + 1) * LANES]
        lagged = [u]
        for k in range(1, SUBLANES):
            if reverse:
                lag = jnp.where(row < SUBLANES - k, pltpu.roll(uf, t - k, 0), 0.0)
            else:
                lag = jnp.where(row >= k, pltpu.roll(uf, k, 0), 0.0)
            lagged.append(lag.astype(BF16))
        outs = []
        for q in range(nblk):
            qs = slice(q * ps, (q + 1) * ps)
            lhs = jnp.concatenate([x[:, q * LANES:(q + 1) * LANES] for x in lagged], axis=1)
            y_loc = jnp.dot(lhs, k_ref[q], preferred_element_type=F32)
            ends = [uf_ref[q, pl.ds(k if reverse else SUBLANES - 1 - k, nw, stride=SUBLANES), :].astype(BF16)
                    for k in range(SUBLANES)]
            sw = jnp.dot(jnp.concatenate(ends, axis=1), b_ref[q], preferred_element_type=F32)
            tabs = [tab_ref[i, :, qs] for i in range(8)]
            cin_r, cin_i = carry_ref[0, :, qs], carry_ref[1, :, qs]
            c_r, c_i, cr, ci = _s5_window_scan(sw[:, :ps], sw[:, ps:], tabs, cin_r, cin_i, reverse)
            carry_ref[0, :, qs] = cr
            carry_ref[1, :, qs] = ci
            if reverse:
                e_r = jnp.where(wrow == nw - 1, cin_r[0:1], pltpu.roll(c_r, nw - 1, 0))
                e_i = jnp.where(wrow == nw - 1, cin_i[0:1], pltpu.roll(c_i, nw - 1, 0))
            else:
                e_r = jnp.where(wrow == 0, cin_r[0:1], pltpu.roll(c_r, 1, 0))
                e_i = jnp.where(wrow == 0, cin_i[0:1], pltpu.roll(c_i, 1, 0))
            y_in = jnp.dot(jnp.concatenate([e_r, e_i], axis=1).astype(BF16), c_ref[q],
                           preferred_element_type=F32)
            for j in range(SUBLANES):
                yc_ref[q, pl.ds(j, nw, stride=SUBLANES), :] = y_in[:, j * LANES:(j + 1) * LANES]
            outs.append(y_loc + yc_ref[q, 0:t])
        out = jnp.concatenate(outs, axis=1)
        if not final:
            o_ref[...] = u.astype(F32) * d_ref[...] + out
        else:
            g = _gelu_tanh(yf_ref[...] + out)
            z = jnp.dot(g.astype(BF16), gw_ref[...], preferred_element_type=F32) + gb_ref[...]
            o_ref[...] = (g * _sigmoid(z)).astype(o_ref.dtype)

    @pl.when(s == 0)
    def _():
        chunk(uz_ref, yfz_ref if final else None, oz_ref)

    @pl.when(s > 0)
    def _():
        chunk(ul_ref, yfl_ref if final else None, ol_ref)


def s5_direction(rest, rest_ctx, u_col, params, which, dskip, reverse, fwd=None, glu=None, t=1024):
    seq = rest.shape[0]
    tz = rest_ctx.shape[0]
    t = min(t, seq)
    nl = seq // t
    w = 4 * LANES
    wc, kc, cc, tabs = params
    _, nq, kq, _ = wc.shape
    gq = LANES // S5_GROUP_CH
    ps2 = 2 * gq * S5_STATE
    p2 = nq * ps2
    e_w = np.zeros((LANES, ps2), np.float32)
    for c in range(2):
        for gl in range(gq):
            for p in range(S5_STATE):
                e_w[c * S5_STATE + p, c * gq * S5_STATE + gl * S5_STATE + p] = 1.0
    e_k = np.zeros((LANES, LANES), np.float32)
    e_c = np.zeros((LANES, SUBLANES * LANES), np.float32)
    for gl in range(gq):
        for o in range(S5_GROUP_CH):
            e_k[o, gl * S5_GROUP_CH + o] = 1.0
            for j in range(SUBLANES):
                e_c[j * S5_GROUP_CH + o, j * LANES + gl * S5_GROUP_CH + o] = 1.0
    spreads = [jnp.asarray(e, BF16) for e in (e_w, e_k, e_c)]
    final = fwd is not None
    if reverse:
        lat = lambda s: (jnp.where(s == 0, nl - 1, nl - s), 0)
    else:
        lat = lambda s: (jnp.maximum(s - 1, 0), 0)
    lat_u = lambda s: (lat(s)[0], u_col)
    const = lambda s: (0, 0)
    table = lambda shape: pl.BlockSpec((None,) + shape, lambda s: (which,) + (0,) * len(shape),
                                       pipeline_mode=pl.Buffered(1))
    in_specs = [pl.BlockSpec((tz, w), lambda s: (0, u_col)),
                pl.BlockSpec((t, w), lat_u),
                table((nq, kq, LANES)),
                table((nq, kq, LANES)),
                table((nq, ps2, LANES)),
                pl.BlockSpec(e_w.shape, const),
                pl.BlockSpec(e_k.shape, const),
                pl.BlockSpec(e_c.shape, const),
                table((8, SUBLANES, p2 // 2)),
                pl.BlockSpec((1, w), const)]
    args = [rest_ctx, rest, wc, kc, cc, *spreads, tabs, dskip.reshape(1, w)]
    if final:
        in_specs += [pl.BlockSpec((tz, w), const), pl.BlockSpec((t, w), lat),
                     pl.BlockSpec((w, w), const), pl.BlockSpec((1, w), const)]
        args += [fwd[0], fwd[1], glu[0], glu[1].reshape(1, w)]
    odt = BF16 if final else F32
    return pl.pallas_call(
        functools.partial(_s5_kernel, reverse=reverse, final=final),
        grid=(nl + 1,),
        in_specs=in_specs,
        out_specs=[pl.BlockSpec((tz, w), const), pl.BlockSpec((t, w), lat)],
        out_shape=[jax.ShapeDtypeStruct((tz, w), odt), jax.ShapeDtypeStruct((seq, w), odt)],
        scratch_shapes=[pltpu.VMEM((2, SUBLANES, p2 // 2), F32),
                        pltpu.VMEM((nq, max(t, tz), LANES), F32),
                        pltpu.VMEM((nq, max(t, tz), LANES), F32),
                        pltpu.VMEM((nq, kq, ps2), BF16),
                        pltpu.VMEM((nq, kq, LANES), BF16),
                        pltpu.VMEM((nq, ps2, SUBLANES * LANES), BF16)],
        compiler_params=_cparams("arbitrary"),
        name="s5_direction",
    )(*args)


def _s5_params(a_re, a_im, log_step, b_re, b_im, c_re, c_im, rev):
    ar, ai = a_re.astype(F32), a_im.astype(F32)
    step = jnp.exp(log_step.astype(F32))[:, None]
    mag = jnp.exp(ar * step)
    ab_re, ab_im = mag * jnp.cos(ai * step), mag * jnp.sin(ai * step)
    den = ar * ar + ai * ai
    nr = ab_re - 1.0
    f_re = (nr * ar + ab_im * ai) / den
    f_im = (ab_im * ar - nr * ai) / den
    br, bi = b_re.astype(F32), b_im.astype(F32)
    bb_re = f_re[..., None] * br - f_im[..., None] * bi
    bb_im = f_re[..., None] * bi + f_im[..., None] * br
    cr, ci = c_re.astype(F32), c_im.astype(F32)
    g, p, h = bb_re.shape
    gq = LANES // h
    nq = g // gq
    nl = SUBLANES

    def cmul(xr, xi, yr, yi):
        return xr * yr - xi * yi, xr * yi + xi * yr

    def powers(xr, xi, n):
        out = [(jnp.ones_like(xr), jnp.zeros_like(xi)), (xr, xi)]
        for _ in range(n - 1):
            out.append(cmul(*out[-1], xr, xi))
        return jnp.stack([o[0] for o in out]), jnp.stack([o[1] for o in out])

    pr, pi = powers(ab_re, ab_im, nl)
    w_re = pr[:nl, ..., None] * bb_re - pi[:nl, ..., None] * bb_im
    w_im = pr[:nl, ..., None] * bb_im + pi[:nl, ..., None] * bb_re

    def rows_lag_group_channel(x):
        x = x.reshape(nl, nq, gq, h, x.shape[-1])
        return jnp.transpose(x, (1, 0, 2, 3, 4)).reshape(nq, nl * gq * h, x.shape[-1])

    wc = jnp.concatenate([rows_lag_group_channel(jnp.swapaxes(w, 2, 3)) for w in (w_re, w_im)], axis=2)
    hp = lax.Precision.HIGHEST
    kk = (jnp.einsum('kgph,gop->kgho', w_re, cr, precision=hp)
          - jnp.einsum('kgph,gop->kgho', w_im, ci, precision=hp))
    kc = jnp.pad(rows_lag_group_channel(kk), ((0, 0), (0, 0), (0, LANES - h)))
    dr = jnp.where(rev, pr[nl:0:-1], pr[1:nl + 1])
    di = jnp.where(rev, pi[nl:0:-1], pi[1:nl + 1])
    cd_re = cr[None] * dr[:, :, None, :] - ci[None] * di[:, :, None, :]
    cd_im = cr[None] * di[:, :, None, :] + ci[None] * dr[:, :, None, :]

    def rows_group_state(cd):
        return jnp.transpose(cd.reshape(nl, nq, gq, h, p), (1, 2, 4, 0, 3)).reshape(nq, gq * p, nl * h)

    cc = jnp.concatenate([rows_group_state(cd_re), -rows_group_state(cd_im)], axis=1)
    qr, qi = powers(pr[nl].reshape(-1), pi[nl].reshape(-1), SUBLANES)
    row = np.arange(SUBLANES)[:, None]
    tabs = []
    for k in (1, 2, 4):
        keep = jnp.where(rev, jnp.asarray(row < SUBLANES - k), jnp.asarray(row >= k))
        tabs += [jnp.where(keep, qr[k][None], 0.0), jnp.where(keep, qi[k][None], 0.0)]
    tabs += [jnp.where(rev, qr[SUBLANES:0:-1], qr[1:SUBLANES + 1]),
             jnp.where(rev, qi[SUBLANES:0:-1], qi[1:SUBLANES + 1])]
    return wc.astype(BF16), kc.astype(BF16), cc.astype(BF16), jnp.stack(tabs)


def _conv_kernel(val_ref, gate_ref, vp_ref, gp_ref, vn_ref, gn_ref, w_ref, b_ref, g_ref, beta_ref, o_ref, hs_ref,
                 *, sub):
    i = pl.program_id(0)
    n = pl.num_programs(0)
    t = val_ref.shape[0]

    def glu(v, g):
        return v[...].astype(F32) * _sigmoid(g[...].astype(F32))

    hs_ref[0:CONV_HALO] = jnp.where(i > 0, glu(vp_ref, gp_ref), 0.0)
    hs_ref[CONV_HALO:CONV_HALO + t] = glu(val_ref, gate_ref)
    hs_ref[CONV_HALO + t:2 * CONV_HALO + t] = jnp.where(i < n - 1, glu(vn_ref, gn_ref), 0.0)
    base = CONV_HALO - CONV_K // 2
    for r0 in range(t // sub):
        acc = jnp.zeros((sub, val_ref.shape[1]), F32) + b_ref[...]
        for r in range(SUBLANES):
            part = None
            for m in range((base + CONV_K - 1) // SUBLANES + 1):
                k = SUBLANES * m + r - base
                if 0 <= k < CONV_K:
                    lo_row = r0 * sub + SUBLANES * m
                    term = hs_ref[lo_row:lo_row + sub + SUBLANES] * w_ref[k:k + 1]
                    part = term if part is None else part + term
            acc = acc + part[r:r + sub]
        mu = jnp.mean(acc, axis=-1, keepdims=True)
        cen = acc - mu
        var = jnp.mean(cen * cen, axis=-1, keepdims=True)
        yn = cen * lax.rsqrt(var + EPS) * g_ref[...] + beta_ref[...]
        o_ref[r0 * sub:(r0 + 1) * sub] = _silu(yn).astype(o_ref.dtype)


def conv_module(rest, val_col, gate_col, w, b, g, beta, t=512, sub=64):
    seq = rest.shape[0]
    t = min(t, seq)
    cw = 4 * LANES
    hb = t // CONV_HALO
    nh = seq // CONV_HALO
    prev = lambda i: jnp.maximum(i * hb - 1, 0)
    nxt = lambda i: jnp.minimum((i + 1) * hb, nh - 1)
    vec = pl.BlockSpec((1, cw), lambda i: (0, 0))
    return pl.pallas_call(
        functools.partial(_conv_kernel, sub=sub),
        grid=(seq // t,),
        in_specs=[pl.BlockSpec((t, cw), lambda i: (i, val_col)),
                  pl.BlockSpec((t, cw), lambda i: (i, gate_col)),
                  pl.BlockSpec((CONV_HALO, cw), lambda i: (prev(i), val_col)),
                  pl.BlockSpec((CONV_HALO, cw), lambda i: (prev(i), gate_col)),
                  pl.BlockSpec((CONV_HALO, cw), lambda i: (nxt(i), val_col)),
                  pl.BlockSpec((CONV_HALO, cw), lambda i: (nxt(i), gate_col)),
                  pl.BlockSpec((CONV_K, cw), lambda i: (0, 0)), vec, vec, vec],
        out_specs=pl.BlockSpec((t, cw), lambda i: (i, 0)),
        out_shape=jax.ShapeDtypeStruct((seq, cw), BF16),
        scratch_shapes=[pltpu.VMEM((t + 2 * CONV_HALO, cw), F32)],
        compiler_params=_cparams("arbitrary"),
        name="conv_module",
    )(rest, rest, rest, rest, rest, rest, w.astype(F32), b.reshape(1, cw).astype(F32),
      g.reshape(1, cw).astype(F32), beta.reshape(1, cw).astype(F32))


def _merge_kernel(*refs, n_parts, cast_mlp):
    y_refs = refs[:N_BRANCH]
    gl_refs = refs[N_BRANCH:N_BRANCH * (1 + n_parts)]
    rest = refs[N_BRANCH * (1 + n_parts):]
    if cast_mlp:
        wb_ref, wo_ref, x_ref, gate_ref, g_ref, sc_ref, sh_ref, w1_ref, w2_ref, o_ref, h_ref, w1b_ref, w2b_ref = rest
        w1b_ref[...] = w1_ref[...].astype(BF16)
        w2b_ref[...] = w2_ref[...].astype(BF16)
    else:
        wb_ref, wo_ref, x_ref, gate_ref, g_ref, sc_ref, sh_ref, o_ref, h_ref = rest
    pw = gl_refs[0].shape[1]
    parts = []
    for part in range(n_parts):
        cols = slice(part * pw, (part + 1) * pw)
        m = None
        for k in range(N_BRANCH):
            proj = jnp.dot(y_refs[k][...], wb_ref[k, :, cols], preferred_element_type=F32)
            term = _sigmoid(gl_refs[k * n_parts + part][...].astype(F32)) * proj
            m = term if m is None else m + term
        parts.append(m.astype(BF16))
    merged = jnp.concatenate(parts, axis=1)
    x = x_ref[...] + gate_ref[...] * jnp.dot(merged, wo_ref[...], preferred_element_type=F32)
    o_ref[...] = x
    y = x * lax.rsqrt(jnp.mean(x * x, axis=-1, keepdims=True) + EPS) * g_ref[...]
    h_ref[...] = (y * (1.0 + sc_ref[...]) + sh_ref[...]).astype(h_ref.dtype)


def gated_merge_residual(ys, rest, logit_col, part_w, wb, wo, layer, x, gate, g2, scale2, shift2, mlp_w=None, tm=256):
    m, yw = ys[0].shape
    n = wb.shape[3]
    n_parts = n // part_w
    tm = min(tm, m)
    steps = m // tm
    resident = lambda shape: pl.BlockSpec((None,) + shape, lambda i: (layer,) + (0,) * len(shape),
                                          pipeline_mode=pl.Buffered(1))
    yspec = pl.BlockSpec((tm, yw), lambda i: (i, 0))
    gspecs = [pl.BlockSpec((tm, part_w), functools.partial(lambda i, c: (i, c), c=logit_col + k * n_parts + part))
              for k in range(N_BRANCH) for part in range(n_parts)]
    vec = pl.BlockSpec((1, n), lambda i: (0, 0))
    rows = pl.BlockSpec((tm, n), lambda i: (i, 0))
    in_specs = [yspec] * N_BRANCH + gspecs + [resident((N_BRANCH, yw, n)), resident((n, n)), rows,
                                              vec, vec, vec, vec]
    args = [*ys, *([rest] * (N_BRANCH * n_parts)), wb, wo, x, gate.reshape(1, n), g2.reshape(1, n),
            scale2.reshape(1, n), shift2.reshape(1, n)]
    out_specs = [rows, rows]
    out_shape = [jax.ShapeDtypeStruct((m, n), F32), jax.ShapeDtypeStruct((m, n), BF16)]
    if mlp_w is not None:
        w1, w2 = mlp_w
        ff = w1.shape[2]
        slab = ff // steps
        assert ff % steps == 0 and slab % LANES == 0
        in_specs += [pl.BlockSpec((None, n, slab), lambda i: (layer, 0, i)),
                     pl.BlockSpec((None, slab, n), lambda i: (layer, i, 0))]
        args += [w1, w2]
        out_specs += [pl.BlockSpec((n, slab), lambda i: (0, i)), pl.BlockSpec((slab, n), lambda i: (i, 0))]
        out_shape += [jax.ShapeDtypeStruct((n, ff), BF16), jax.ShapeDtypeStruct((ff, n), BF16)]
    return pl.pallas_call(
        functools.partial(_merge_kernel, n_parts=n_parts, cast_mlp=mlp_w is not None),
        grid=(steps,),
        in_specs=in_specs,
        out_specs=out_specs,
        out_shape=out_shape,
        compiler_params=_cparams("arbitrary"),
        name="gated_merge_residual",
    )(*args)


def _mlp_kernel(x_ref, h_ref, gate_ref, w1_ref, w2_ref, *rest, final_norm):
    if final_norm:
        gf_ref, o_ref, acc_ref = rest
    else:
        o_ref, acc_ref = rest
    f = pl.program_id(1)

    @pl.when(f == 0)
    def _():
        acc_ref[...] = jnp.zeros_like(acc_ref)

    t = jnp.dot(h_ref[...], w1_ref[...], preferred_element_type=F32)
    t = jnp.square(jnp.maximum(t, 0.0)).astype(BF16)
    acc_ref[...] += jnp.dot(t, w2_ref[...], preferred_element_type=F32)

    @pl.when(f == pl.num_programs(1) - 1)
    def _():
        y = x_ref[...] + gate_ref[...] * acc_ref[...]
        if final_norm:
            y = y * lax.rsqrt(jnp.mean(y * y, axis=-1, keepdims=True) + EPS) * gf_ref[...]
        o_ref[...] = y


def mlp_block(x, h, gate, w1, w2, final_g=None, tm=512, tf=1024):
    m, d = x.shape
    ff = w1.shape[1]
    tm = min(tm, m)
    vec = pl.BlockSpec((1, d), lambda i, f: (0, 0))
    rows = pl.BlockSpec((tm, d), lambda i, f: (i, 0))
    final_norm = final_g is not None
    args = [x, h, gate.reshape(1, d), w1, w2]
    in_specs = [rows, rows, vec,
                pl.BlockSpec((d, tf), lambda i, f: (0, f)),
                pl.BlockSpec((tf, d), lambda i, f: (f, 0))]
    if final_norm:
        args.append(final_g.reshape(1, d))
        in_specs.append(vec)
    return pl.pallas_call(
        functools.partial(_mlp_kernel, final_norm=final_norm),
        grid=(m // tm, ff // tf),
        in_specs=in_specs,
        out_specs=pl.BlockSpec((tm, d), lambda i, f: (i, 0)),
        out_shape=jax.ShapeDtypeStruct((m, d), F32),
        scratch_shapes=[pltpu.VMEM((tm, d), F32)],
        compiler_params=_cparams("arbitrary", "arbitrary"),
        name="mlp_block",
    )(*args)


_A_COL_K, _A_COL_V = 2, 3
_REST_COL0 = (A_HEADS + 2 * A_KV_HEADS) * HEAD_DIM
_COL_U, _COL_QC, _COL_KC, _COL_VC, _COL_VAL, _COL_GATE = 0, 1, 2, 3, 4, 5
_MERGE_PART = 1024
_COL_MERGE = 3


def _rope_tables(seq):
    freqs = ROPE_BASE ** (-jnp.arange(ROPE_PAIRS, dtype=F32) / ROPE_PAIRS)
    rows = seq // GRID_W
    ang_r = jnp.arange(rows).astype(F32)[:, None] * freqs
    ang_c = jnp.arange(GRID_W).astype(F32)[:, None] * freqs

    def expand(tab_r, tab_c, first, second):
        r = jnp.repeat(tab_r, GRID_W, axis=0)
        cc = jnp.tile(tab_c, (rows, 1))
        head = jnp.concatenate([first * r, second * r, first * cc, second * cc], axis=1)
        return jnp.tile(head, (1, LANES // HEAD_DIM))

    cos = expand(jnp.cos(ang_r), jnp.cos(ang_c), 1.0, 1.0)
    sa = expand(jnp.sin(ang_r), jnp.sin(ang_c), 0.0, 1.0)
    sb = expand(jnp.sin(ang_r), jnp.sin(ang_c), -1.0, 0.0)
    return cos, sa, sb


def kernel(x, c, ctx, c_ctx, ada_w, ada_b, norm1_g, norm2_g, w_in, a_sink, s5_a_re, s5_a_im, s5_log_step, s5_b_re, s5_b_im, s5_c_re, s5_c_im, s5_d, s5_glu_w, s5_glu_b, c_rpb, d_conv_w, d_conv_b, d_ln_g, d_ln_b, w_branch, w_out, mlp_w1, mlp_w2, final_g):
    depth = ada_w.shape[0]
    _, seq, d = x.shape
    lc = ctx.shape[1]
    xs = x.reshape(seq, d)
    zs = ctx.reshape(lc, d)
    rope = _rope_tables(seq)

    cc = jnp.zeros((SUBLANES, d), F32).at[0].set(c[0]).at[1].set(c_ctx)
    mods = ada_modulation(cc, ada_w, ada_b)
    n_rest = w_in.shape[2] - _REST_COL0
    wb, wo = w_branch.astype(BF16), w_out.astype(BF16)
    nbr_bias = jax.vmap(lambda rpb: _nbr_bias(rpb, seq // GRID_W))(c_rpb)
    s5_params = jax.vmap(_s5_params)(*(a.reshape((2 * depth,) + a.shape[2:]) for a in
                                       (s5_a_re, s5_a_im, s5_log_step, s5_b_re, s5_b_im, s5_c_re, s5_c_im)),
                                     jnp.arange(2 * depth) % 2 == 1)

    for l in range(depth):
        need_ctx = l < depth - 1
        mx = [mods[l, 0, k * d:(k + 1) * d] for k in range(6)]
        mz = [mods[l, 1, k * d:(k + 1) * d] for k in range(6)]

        hx, qk_x = norm_head_projection(xs, norm1_g[l], mx[1], mx[0], w_in, l, rope=rope)
        hz, qk_z = norm_head_projection(zs, norm1_g[l], mz[1], mz[0], w_in, l)
        r_x = wide_projection(hx, w_in, l, _REST_COL0, n_rest)
        r_z = wide_projection(hz, w_in, l, _REST_COL0, n_rest if need_ctx else (_COL_VC + 1) * 4 * LANES)

        ya_x = window_attention(a_sink[l], qk_x, qk_z)
        yc_x = neighborhood_attention(r_x, r_z, nbr_bias, l, _COL_QC, _COL_KC)
        yd_x = conv_module(r_x, _COL_VAL, _COL_GATE, d_conv_w[l], d_conv_b[l], d_ln_g[l], d_ln_b[l])

        yf = s5_direction(r_x, r_z, _COL_U, s5_params, 2 * l, s5_d[l].astype(F32), reverse=False)
        yb_z, yb_x = s5_direction(r_x, r_z, _COL_U, s5_params, 2 * l + 1, s5_d[l].astype(F32), reverse=True, fwd=yf,
                                  glu=(s5_glu_w[l].astype(BF16), s5_glu_b[l].astype(F32)))

        xs, h2, w1, w2 = gated_merge_residual((ya_x, yb_x, yc_x, yd_x), r_x, _COL_MERGE, _MERGE_PART, wb, wo, l, xs,
                                              mx[2], norm2_g[l], mx[4], mx[3], mlp_w=(mlp_w1, mlp_w2))
        xs = mlp_block(xs, h2, mx[5], w1, w2, final_g=None if need_ctx else final_g)

        if need_ctx:
            ya_z = context_attention(a_sink[l], qk_z, 0, qk_z, _A_COL_K, 2 * LANES, qk_z, _A_COL_V, True, True)
            yc_z = context_attention(a_sink[l], r_z, _COL_QC, r_z, _COL_KC, 4 * LANES, r_z, _COL_VC, False, False)
            yd_z = conv_module(r_z, _COL_VAL, _COL_GATE, d_conv_w[l], d_conv_b[l], d_ln_g[l], d_ln_b[l])
            zs, h2 = gated_merge_residual((ya_z, yb_z, yc_z, yd_z), r_z, _COL_MERGE, _MERGE_PART, wb, wo, l, zs,
                                          mz[2], norm2_g[l], mz[4], mz[3])
            zs = mlp_block(zs, h2, mz[5], w1, w2)

    return xs.reshape(x.shape)
```
